```python
import math
import jax, jax.numpy as jnp
from jax import lax
import numpy as np

D_MODEL = 1024
BATCH = 1
SEQ = 16384
DEPTH = 2

D_MIX = D_MODEL
N_MIXERS = 4
GROUP_WIDTH = D_MIX // N_MIXERS

S5_CH = 16
S5_GROUPS = GROUP_WIDTH // S5_CH
S5_STATE = 64
S5_DT_MIN = 1e-3
S5_DT_MAX = 1e-1

RET_HEADS = 4
RET_HEAD_DIM = GROUP_WIDTH // RET_HEADS
RET_CHUNK = 128
ROPE_BASE = 10000.0

M2_HEADS = 4
M2_HEAD_DIM = GROUP_WIDTH // M2_HEADS
M2_NGROUPS = 2
M2_STATE = 128
M2_CONV = 4
M2_CHUNK = 128
M2_CONV_DIM = GROUP_WIDTH + 2 * M2_NGROUPS * M2_STATE

HG_HEADS = 4
HG_KEY_DIM = GROUP_WIDTH // HG_HEADS
HG_VAL_DIM = GROUP_WIDTH // HG_HEADS
HG_CHUNK = 16

D_IN = 10 * GROUP_WIDTH + M2_CONV_DIM + M2_HEADS

MOE_GROUPS = 4
MOE_PER_GROUP = 4
MOE_EXPERTS = MOE_GROUPS * MOE_PER_GROUP
MOE_TOPK = 2
MOE_FF = 512

NORM_EPS = 1e-6

kernel_name = 'hybrid_s5_retnet_ssd_hgrn2_hmoe'


def rms_norm(x, g):
    x32 = x.astype(jnp.float32)
    y = x32 * lax.rsqrt(jnp.mean(x32 * x32, axis=-1, keepdims=True) + NORM_EPS)
    return (y * g.astype(jnp.float32)).astype(x.dtype)


def head_layer_norm(o):
    o32 = o.astype(jnp.float32)
    c = o32 - jnp.mean(o32, axis=-1, keepdims=True)
    return (c * lax.rsqrt(jnp.mean(c * c, axis=-1, keepdims=True) + NORM_EPS)).astype(o.dtype)


def rotary(t, positions):
    half = t.shape[-1] // 2
    inv_freq = ROPE_BASE ** (-jnp.arange(half, dtype=jnp.float32) / half)
    ang = positions.astype(jnp.float32)[..., None] * inv_freq
    cos = jnp.cos(ang)[:, :, None, :].astype(t.dtype)
    sin = jnp.sin(ang)[:, :, None, :].astype(t.dtype)
    t1, t2 = t[..., :half], t[..., half:]
    return jnp.concatenate([t1 * cos - t2 * sin, t2 * cos + t1 * sin], axis=-1)


def causal_mask(n):
    i = jnp.arange(n)
    return i[:, None] >= i[None, :]


def chunk_state_scan(decay, inc):
    dtype = jnp.result_type(decay, inc)
    inc_t = jnp.moveaxis(inc.astype(dtype), 1, 0)
    dec_t = jnp.moveaxis(jnp.broadcast_to(decay, inc.shape).astype(dtype), 1, 0)

    def step(s, xs):
        d, u = xs
        return d * s + u, s

    _, prev = lax.scan(step, jnp.zeros_like(inc_t[0]), (dec_t, inc_t))
    return jnp.moveaxis(prev, 0, 1)


def causal_depthwise_conv(x, w, b):
    out = lax.conv_general_dilated(
        x, w[:, None, :], window_strides=(1,), padding=[(M2_CONV - 1, 0)],
        dimension_numbers=('NWC', 'WIO', 'NWC'), feature_group_count=x.shape[-1])
    return out + b


def _complex_affine_combine(e1, e2):
    a1r, a1i, b1r, b1i = e1
    a2r, a2i, b2r, b2i = e2
    return (a2r * a1r - a2i * a1i, a2r * a1i + a2i * a1r,
            a2r * b1r - a2i * b1i + b2r, a2r * b1i + a2i * b1r + b2i)


def s5_mixer(u, a_re, a_im, log_dt, b_re, b_im, c_re, c_im, d_skip, w_glu, b_glu):
    bsz, seqlen, _ = u.shape
    ug = u.reshape(bsz, seqlen, S5_GROUPS, S5_CH)
    a_re = jnp.minimum(a_re, -1e-4)
    dt = jnp.exp(log_dt)[:, None]
    mag = jnp.exp(a_re * dt)
    ab_re = mag * jnp.cos(a_im * dt)
    ab_im = mag * jnp.sin(a_im * dt)
    den = a_re * a_re + a_im * a_im
    k_re = ((ab_re - 1.0) * a_re + ab_im * a_im) / den
    k_im = (ab_im * a_re - (ab_re - 1.0) * a_im) / den
    bb_re = k_re[..., None] * b_re - k_im[..., None] * b_im
    bb_im = k_re[..., None] * b_im + k_im[..., None] * b_re
    bu_re = jnp.einsum('blgp,gnp->blgn', ug, bb_re)
    bu_im = jnp.einsum('blgp,gnp->blgn', ug, bb_im)
    dtype = bu_re.dtype
    elems = (jnp.broadcast_to(ab_re, bu_re.shape).astype(dtype),
             jnp.broadcast_to(ab_im, bu_re.shape).astype(dtype),
             bu_re, bu_im.astype(dtype))
    _, _, s_re, s_im = lax.associative_scan(_complex_affine_combine, elems, axis=1)
    y = (jnp.einsum('blgn,gpn->blgp', s_re, c_re)
         - jnp.einsum('blgn,gpn->blgp', s_im, c_im) + d_skip * ug)
    y = jax.nn.gelu(y.reshape(bsz, seqlen, GROUP_WIDTH))
    return y * jax.nn.sigmoid(y @ w_glu + b_glu)


def retention_mixer(q, k, v, g, positions):
    bsz, seqlen, _ = q.shape
    nc = seqlen // RET_CHUNK
    hs = (bsz, seqlen, RET_HEADS, RET_HEAD_DIM)
    cs = (bsz, nc, RET_CHUNK, RET_HEADS, RET_HEAD_DIM)
    qh = rotary(q.reshape(hs), positions).reshape(cs)
    kh = (rotary(k.reshape(hs), positions) * (RET_HEAD_DIM ** -0.5)).reshape(cs)
    vh = v.reshape(cs)
    log_gamma = jnp.log1p(-jnp.exp2(-5.0 - jnp.arange(RET_HEADS, dtype=jnp.float32)))
    idx = jnp.arange(RET_CHUNK, dtype=jnp.float32)
    mask = causal_mask(RET_CHUNK)
    rel = jnp.where(mask, idx[:, None] - idx[None, :], 0.0)
    decay = jnp.where(mask[None], jnp.exp(rel[None] * log_gamma[:, None, None]), 0.0)
    xi = jnp.exp((idx + 1.0)[None] * log_gamma[:, None])
    zeta = jnp.exp((RET_CHUNK - 1.0 - idx)[None] * log_gamma[:, None])
    scores = jnp.einsum('bnthd,bnshd->bnhts', qh, kh) * decay
    inner = jnp.einsum('bnhts,bnshe->bnthe', scores, vh)
    kv = jnp.einsum('bnshd,hs,bnshe->bnhde', kh, zeta, vh)
    r_prev = chunk_state_scan(jnp.exp(RET_CHUNK * log_gamma)[:, None, None], kv)
    cross = jnp.einsum('bnthd,ht,bnhde->bnthe', qh, xi, r_prev)
    o = head_layer_norm((inner + cross).reshape(hs))
    return o.reshape(bsz, seqlen, GROUP_WIDTH) * jax.nn.silu(g)


def mamba2_mixer(z, xbc, dt_raw, conv_w, conv_b, dt_bias, a_log, d_skip, norm_g):
    bsz, seqlen, _ = z.shape
    nc = seqlen // M2_CHUNK
    xbc = jax.nn.silu(causal_depthwise_conv(xbc, conv_w, conv_b))
    xs, bm, cm = jnp.split(xbc, [GROUP_WIDTH, GROUP_WIDTH + M2_NGROUPS * M2_STATE], axis=-1)
    xs = xs.reshape(bsz, seqlen, M2_HEADS, M2_HEAD_DIM)
    rep = M2_HEADS // M2_NGROUPS
    bm = jnp.repeat(bm.reshape(bsz, seqlen, M2_NGROUPS, M2_STATE), rep, axis=2)
    cm = jnp.repeat(cm.reshape(bsz, seqlen, M2_NGROUPS, M2_STATE), rep, axis=2)
    dt = jax.nn.softplus((dt_raw + dt_bias).astype(jnp.float32))
    da = dt * (-jnp.exp(a_log.astype(jnp.float32)))
    xc = (xs * dt[..., None]).reshape(bsz, nc, M2_CHUNK, M2_HEADS, M2_HEAD_DIM)
    bc = bm.reshape(bsz, nc, M2_CHUNK, M2_HEADS, M2_STATE)
    cc = cm.reshape(bsz, nc, M2_CHUNK, M2_HEADS, M2_STATE)
    acum = jnp.cumsum(da.reshape(bsz, nc, M2_CHUNK, M2_HEADS), axis=2)
    acum_h = jnp.moveaxis(acum, 2, 3)
    mask = causal_mask(M2_CHUNK)
    seg = jnp.where(mask, acum_h[..., :, None] - acum_h[..., None, :], 0.0)
    lmat = jnp.where(mask, jnp.exp(seg), 0.0)
    scores = jnp.einsum('bnthN,bnshN->bnhts', cc, bc) * lmat
    y_diag = jnp.einsum('bnhts,bnshp->bnthp', scores, xc)
    decay_states = jnp.exp(acum[:, :, -1:, :] - acum)
    states = jnp.einsum('bnshN,bnsh,bnshp->bnhpN', bc, decay_states, xc)
    s_prev = chunk_state_scan(jnp.exp(acum[:, :, -1, :])[..., None, None], states)
    y_off = jnp.einsum('bnthN,bnhpN,bnth->bnthp', cc, s_prev, jnp.exp(acum))
    y = (y_diag + y_off).reshape(bsz, seqlen, M2_HEADS, M2_HEAD_DIM) + d_skip[:, None] * xs
    return rms_norm(y.reshape(bsz, seqlen, GROUP_WIDTH) * jax.nn.silu(z), norm_g)


def hgrn2_mixer(q, f, i, g, lower_bound, norm_g):
    bsz, seqlen, _ = q.shape
    nc = seqlen // HG_CHUNK
    ck = (bsz, nc, HG_CHUNK, HG_HEADS, HG_KEY_DIM)
    cv = (bsz, nc, HG_CHUNK, HG_HEADS, HG_VAL_DIM)
    forget = lower_bound + (1.0 - lower_bound) * jax.nn.sigmoid(f.astype(jnp.float32))
    qc = jax.nn.silu(q).reshape(ck)
    kc = (1.0 - forget).reshape(ck)
    vc = i.reshape(cv)
    bcum = jnp.cumsum(jnp.log(forget).reshape(ck), axis=2)
    mask = causal_mask(HG_CHUNK)[:, :, None, None]
    diff = bcum[:, :, :, None] - bcum[:, :, None, :]
    wdec = jnp.where(mask, jnp.exp(jnp.where(mask, diff, 0.0)), 0.0)
    attn = jnp.einsum('bnthd,bnshd,bntshd->bnhts', qc, kc, wdec)
    intra = jnp.einsum('bnhts,bnshe->bnthe', attn, vc)
    blast = bcum[:, :, -1]
    states = jnp.einsum('bnshd,bnshe->bnhde', kc * jnp.exp(blast[:, :, None] - bcum), vc)
    s_prev = chunk_state_scan(jnp.exp(blast)[..., None], states)
    cross = jnp.einsum('bnthd,bnhde->bnthe', qc * jnp.exp(bcum), s_prev)
    o32 = (intra + cross).reshape(bsz, seqlen, HG_HEADS, HG_VAL_DIM).astype(jnp.float32)
    o32 = o32 * lax.rsqrt(jnp.mean(o32 * o32, axis=-1, keepdims=True) + NORM_EPS)
    o = (o32 * norm_g.reshape(HG_HEADS, HG_VAL_DIM).astype(jnp.float32)).astype(q.dtype)
    return o.reshape(bsz, seqlen, GROUP_WIDTH) * jax.nn.silu(g)


def hier_moe(h, w_group, b_group, w_expert, b_expert, w_gate, w_up, w_down):
    bsz, seqlen, _ = h.shape
    group_prob = jax.nn.softmax((h @ w_group + b_group).astype(jnp.float32), axis=-1)
    p_g, g_idx = lax.top_k(group_prob, 1)
    expert_logits = (h @ w_expert + b_expert).astype(jnp.float32).reshape(
        bsz, seqlen, MOE_GROUPS, MOE_PER_GROUP)
    in_group = jnp.take_along_axis(expert_logits, g_idx[..., None], axis=2)[:, :, 0]
    top_p, top_i = lax.top_k(jax.nn.softmax(in_group, axis=-1), MOE_TOPK)
    weights = p_g * top_p / jnp.sum(top_p, axis=-1, keepdims=True)
    expert_id = g_idx * MOE_PER_GROUP + top_i
    combine = jnp.einsum('blk,blke->ble', weights,
                         jax.nn.one_hot(expert_id, MOE_EXPERTS, dtype=weights.dtype))
    out = jnp.zeros_like(h)
    for e in range(MOE_EXPERTS):
        act = jax.nn.silu(h @ w_gate[e]) * (h @ w_up[e])
        out = out + combine[..., e:e + 1].astype(h.dtype) * (act @ w_down[e])
    return out


def setup_inputs(seed: int = 0) -> dict:
    key = jax.random.key(seed)
    ks = jax.random.split(key, 32)
    f32 = jnp.float32

    def nrm(k, shape, scale):
        return scale * jax.random.normal(k, shape, f32)

    def gain(k, shape):
        return 1.0 + 0.1 * jax.random.normal(k, shape, f32)

    L, G, N, P = DEPTH, S5_GROUPS, S5_STATE, S5_CH
    m2_dt = jnp.exp(jax.random.uniform(ks[17], (L, M2_HEADS), f32, math.log(1e-3), math.log(1e-1)))
    return {
        'x': jax.random.normal(ks[0], (BATCH, SEQ, D_MODEL), f32),
        'positions': jnp.broadcast_to(jnp.arange(SEQ, dtype=jnp.int32)[None, :], (BATCH, SEQ)),
        'norm1_g': gain(ks[1], (L, D_MODEL)),
        'w_in': nrm(ks[2], (L, D_MODEL, D_IN), D_MODEL ** -0.5),
        'w_out': nrm(ks[3], (L, D_MIX, D_MODEL), D_MIX ** -0.5),
        's5_a_re': -0.5 + nrm(ks[4], (L, G, N), 0.01),
        's5_a_im': math.pi * jnp.arange(N, dtype=f32) + nrm(ks[5], (L, G, N), 0.01),
        's5_log_dt': jax.random.uniform(ks[6], (L, G), f32, math.log(S5_DT_MIN), math.log(S5_DT_MAX)),
        's5_b_re': nrm(ks[7], (L, G, N, P), (2.0 * P) ** -0.5),
        's5_b_im': nrm(ks[8], (L, G, N, P), (2.0 * P) ** -0.5),
        's5_c_re': nrm(ks[9], (L, G, P, N), (2.0 * N) ** -0.5),
        's5_c_im': nrm(ks[10], (L, G, P, N), (2.0 * N) ** -0.5),
        's5_d': nrm(ks[11], (L, G, P), 1.0),
        's5_w_glu': nrm(ks[12], (L, GROUP_WIDTH, GROUP_WIDTH), GROUP_WIDTH ** -0.5),
        's5_b_glu': nrm(ks[13], (L, GROUP_WIDTH), 0.01),
        'm2_conv_w': nrm(ks[14], (L, M2_CONV, M2_CONV_DIM), M2_CONV ** -0.5),
        'm2_conv_b': nrm(ks[15], (L, M2_CONV_DIM), 0.01),
        'm2_dt_bias': m2_dt + jnp.log(-jnp.expm1(-m2_dt)),
        'm2_a_log': jnp.log(jax.random.uniform(ks[16], (L, M2_HEADS), f32, 1.0, 16.0)),
        'm2_d': gain(ks[18], (L, M2_HEADS)),
        'm2_norm_g': gain(ks[19], (L, GROUP_WIDTH)),
        'hg_lb_logits': nrm(ks[20], (L, HG_HEADS * HG_KEY_DIM), 0.1),
        'hg_norm_g': gain(ks[21], (L, GROUP_WIDTH)),
        'norm2_g': gain(ks[22], (L, D_MODEL)),
        'moe_w_group': nrm(ks[23], (L, D_MODEL, MOE_GROUPS), D_MODEL ** -0.5),
        'moe_b_group': nrm(ks[24], (L, MOE_GROUPS), 0.01),
        'moe_w_expert': nrm(ks[25], (L, D_MODEL, MOE_EXPERTS), D_MODEL ** -0.5),
        'moe_b_expert': nrm(ks[26], (L, MOE_EXPERTS), 0.01),
        'moe_w_gate': nrm(ks[27], (L, MOE_EXPERTS, D_MODEL, MOE_FF), D_MODEL ** -0.5),
        'moe_w_up': nrm(ks[28], (L, MOE_EXPERTS, D_MODEL, MOE_FF), D_MODEL ** -0.5),
        'moe_w_down': nrm(ks[29], (L, MOE_EXPERTS, MOE_FF, D_MODEL), MOE_FF ** -0.5),
        'final_norm_g': gain(ks[30], (D_MODEL,)),
    }


def reference(x, positions, norm1_g, w_in, w_out,
              s5_a_re, s5_a_im, s5_log_dt, s5_b_re, s5_b_im, s5_c_re, s5_c_im, s5_d,
              s5_w_glu, s5_b_glu,
              m2_conv_w, m2_conv_b, m2_dt_bias, m2_a_log, m2_d, m2_norm_g,
              hg_lb_logits, hg_norm_g,
              norm2_g, moe_w_group, moe_b_group, moe_w_expert, moe_b_expert,
              moe_w_gate, moe_w_up, moe_w_down,
              final_norm_g):
    lb_probs = jax.nn.softmax(hg_lb_logits.astype(jnp.float32), axis=0)
    lower_bounds = jnp.cumsum(lb_probs, axis=0) - lb_probs[0]
    sizes = (GROUP_WIDTH,) * 6 + (M2_CONV_DIM, M2_HEADS) + (GROUP_WIDTH,) * 4
    cuts = [int(c) for c in np.cumsum(sizes)[:-1]]
    for l in range(DEPTH):
        h = rms_norm(x, norm1_g[l])
        (s5_u, r_q, r_k, r_v, r_g, m_z, m_xbc, m_dt,
         g_q, g_f, g_i, g_g) = jnp.split(h @ w_in[l], cuts, axis=-1)
        y_s5 = s5_mixer(s5_u, s5_a_re[l], s5_a_im[l], s5_log_dt[l], s5_b_re[l], s5_b_im[l],
                        s5_c_re[l], s5_c_im[l], s5_d[l], s5_w_glu[l], s5_b_glu[l])
        y_ret = retention_mixer(r_q, r_k, r_v, r_g, positions)
        y_m2 = mamba2_mixer(m_z, m_xbc, m_dt, m2_conv_w[l], m2_conv_b[l], m2_dt_bias[l],
                            m2_a_log[l], m2_d[l], m2_norm_g[l])
        y_hg = hgrn2_mixer(g_q, g_f, g_i, g_g, lower_bounds[l], hg_norm_g[l])
        x = x + jnp.concatenate([y_s5, y_ret, y_m2, y_hg], axis=-1) @ w_out[l]
        x = x + hier_moe(rms_norm(x, norm2_g[l]), moe_w_group[l], moe_b_group[l],
                         moe_w_expert[l], moe_b_expert[l], moe_w_gate[l], moe_w_up[l],
                         moe_w_down[l])
    return rms_norm(x, final_norm_g)
```

```python
import functools
import math

import numpy as np
import jax
import jax.numpy as jnp
from jax import lax
from jax.experimental import pallas as pl
from jax.experimental.pallas import tpu as pltpu

F32 = jnp.float32
BF16 = jnp.bfloat16
NORM_EPS = 1e-6

GROUP_WIDTH = 256
HEAD_DIM = 64
N_HEADS = GROUP_WIDTH // HEAD_DIM
S5_GROUPS = 16
S5_CH = 16
S5_STATE = 64
S5_DT_CLAMP = -1e-4
M2_STATE = 128
M2_CONV = 4
M2_CONV_DIM = 768
ROPE_BASE = 10000.0
MOE_GROUPS = 4
MOE_PER_GROUP = 4
MOE_EXPERTS = 16
ROUTE_LANES = 128

SEQ_TILE = 512
CHUNK = 128
S5_CHUNK = 32
MOE_TILE = 1024
VMEM_LIMIT = 56 * 1024 * 1024


def _cparams(*sem):
    return pltpu.CompilerParams(dimension_semantics=sem, vmem_limit_bytes=VMEM_LIMIT)


def _dot(a, b):
    return jnp.dot(a, b, preferred_element_type=F32)


def _dot_nt(a, b):
    return lax.dot_general(a, b, (((1,), (1,)), ((), ())), preferred_element_type=F32)


def _dot_tn(a, b):
    return lax.dot_general(a, b, (((0,), (0,)), ((), ())), preferred_element_type=F32)


def _split2(x):
    hi = x.astype(BF16)
    return hi, (x - hi.astype(F32)).astype(BF16)


def _split3(x):
    hi = x.astype(BF16)
    r = x - hi.astype(F32)
    mid = r.astype(BF16)
    return hi, mid, (r - mid.astype(F32)).astype(BF16)


def _dot_exact_lhs(m, x):
    hi, mid, lo = _split3(x)
    return _dot(m, hi) + _dot(m, mid) + _dot(m, lo)


def _dot_exact_rhs(x, m):
    hi, lo = _split2(x)
    return _dot(hi, m) + _dot(lo, m)


def _sigmoid(x):
    return 1.0 / (1.0 + jnp.exp(-x))


def _silu(x):
    return x * _sigmoid(x)


def _rms(x, g):
    return x * lax.rsqrt(jnp.mean(x * x, axis=-1, keepdims=True) + NORM_EPS) * g


def _full(shape):
    return pl.BlockSpec(shape, lambda *_: (0,) * len(shape))


def _rows(tile, width):
    return pl.BlockSpec((tile, width), lambda i: (i, 0))


IN_SEGMENTS = (256, 1024, 1280, 1024)


def _in_proj_kernel(x_ref, g_ref, w_ref, u_ref, ret_ref, m2_ref, hg_ref):
    hb = _rms(x_ref[...], g_ref[...]).astype(BF16)
    c0, c1, c2, c3 = np.cumsum(IN_SEGMENTS)
    u_ref[...] = _dot(hb, w_ref[:, 0:c0]).astype(BF16)
    ret_ref[...] = _dot(hb, w_ref[:, c0:c1])
    m2_ref[...] = _dot(hb, w_ref[:, c1:c2])
    hg_ref[...] = _dot(hb, w_ref[:, c2:c3])


def _in_proj(x, g, w):
    n, d = x.shape
    tile = min(SEQ_TILE, n)
    return pl.pallas_call(
        _in_proj_kernel,
        grid=(n // tile,),
        in_specs=[_rows(tile, d), _full((1, d)), _full(w.shape)],
        out_specs=[_rows(tile, s) for s in IN_SEGMENTS],
        out_shape=[jax.ShapeDtypeStruct((n, IN_SEGMENTS[0]), BF16)]
        + [jax.ShapeDtypeStruct((n, s), F32) for s in IN_SEGMENTS[1:]],
        compiler_params=_cparams("parallel"),
        name="in_proj",
    )(x, g, w)


def _rope_kernel(pos_ref, invf_ref, cos_ref, sin_ref):
    ang = pos_ref[...].astype(F32) * invf_ref[...]
    cos_ref[...] = jnp.cos(ang)
    sin_ref[...] = jnp.sin(ang)


def _rope_tables(positions):
    n = positions.shape[0]
    tile = min(SEQ_TILE, n)
    half = HEAD_DIM // 2
    inv_freq = ROPE_BASE ** (-jnp.arange(half, dtype=F32) / half)
    invf = jnp.tile(inv_freq, 128 // half).reshape(1, 128)
    return pl.pallas_call(
        _rope_kernel,
        grid=(n // tile,),
        in_specs=[_rows(tile, 1), _full((1, 128))],
        out_specs=[_rows(tile, 128), _rows(tile, 128)],
        out_shape=[jax.ShapeDtypeStruct((n, 128), F32)] * 2,
        compiler_params=_cparams("parallel"),
        name="rope_tables",
    )(positions, invf)


def _head_mean_matrix():
    h = np.arange(GROUP_WIDTH) // HEAD_DIM
    return jnp.asarray((h[:, None] == h[None, :]) / HEAD_DIM, dtype=BF16)


def _head_block_mask():
    h = np.arange(GROUP_WIDTH) // HEAD_DIM
    return jnp.asarray(h[:, None] == h[None, :], dtype=F32)


def _ret_constants(chunk):
    lg = np.log1p(-np.exp2(-5.0 - np.arange(N_HEADS, dtype=np.float64)))
    idx = np.arange(chunk, dtype=np.float64)
    rel = idx[:, None] - idx[None, :]
    decay = np.where(rel >= 0, np.exp(np.maximum(rel, 0.0)[None] * lg[:, None, None]), 0.0)
    lane_lg = np.repeat(lg, HEAD_DIM)
    xi = np.exp((idx + 1.0)[:, None] * lane_lg[None, :])
    zeta = np.exp((chunk - 1.0 - idx)[:, None] * lane_lg[None, :])
    h = np.arange(GROUP_WIDTH) // HEAD_DIM
    gc = np.where(h[:, None] == h[None, :], np.exp(chunk * lane_lg)[:, None], 0.0)
    f = lambda a: jnp.asarray(a, dtype=F32)
    return f(decay), f(xi), f(zeta), f(gc)


def _ret_kernel(p_ref, cos_ref, sin_ref, dec_ref, xi_ref, zeta_ref, gc_ref, bm_ref, gm_ref,
                o_ref, r_ref, *, chunk, n_chunks):
    @pl.when(pl.program_id(0) == 0)
    def _():
        r_ref[...] = jnp.zeros_like(r_ref)

    lane = lax.broadcasted_iota(jnp.int32, (chunk, GROUP_WIDTH), 1)
    first_half = (lane % HEAD_DIM) < (HEAD_DIM // 2)
    head = lane // HEAD_DIM
    gm = gm_ref[...]

    def rope(t, cos2, sin2):
        rot = jnp.where(first_half, -pltpu.roll(t, GROUP_WIDTH - HEAD_DIM // 2, 1),
                        pltpu.roll(t, HEAD_DIM // 2, 1))
        return t * cos2 + rot * sin2

    def body(c, carry):
        rows = pl.ds(pl.multiple_of(c * chunk, chunk), chunk)
        cs = cos_ref[rows, :]
        sn = sin_ref[rows, :]
        cos2 = jnp.concatenate([cs, cs], axis=1)
        sin2 = jnp.concatenate([sn, sn], axis=1)
        q = rope(p_ref[rows, 0:256], cos2, sin2)
        k = rope(p_ref[rows, 256:512], cos2, sin2) * (HEAD_DIM ** -0.5)
        v = p_ref[rows, 512:768]
        g = p_ref[rows, 768:1024]
        kb = k.astype(BF16)
        inner = jnp.zeros((chunk, GROUP_WIDTH), F32)
        for h in range(N_HEADS):
            qm = jnp.where(head == h, q, 0.0).astype(BF16)
            vm = jnp.where(head == h, v, 0.0).astype(BF16)
            scores = _dot_nt(qm, kb) * dec_ref[h]
            inner = inner + _dot(scores.astype(BF16), vm)
        r_prev = r_ref[...]
        cross = _dot((q * xi_ref[...]).astype(BF16), r_prev.astype(BF16))
        r_ref[...] = gc_ref[...] * r_prev + bm_ref[...] * _dot_tn(kb, (zeta_ref[...] * v).astype(BF16))
        o = inner + cross
        cen = o - _dot_exact_rhs(o, gm)
        var = _dot_exact_rhs(cen * cen, gm)
        o_ref[rows, :] = (cen * lax.rsqrt(var + NORM_EPS) * _silu(g)).astype(BF16)
        return carry

    lax.fori_loop(0, n_chunks, body, 0)


def _retention(proj, cos_t, sin_t):
    n = proj.shape[0]
    tile = min(SEQ_TILE, n)
    chunk = min(CHUNK, tile)
    decay, xi, zeta, gc = _ret_constants(chunk)
    kern = functools.partial(_ret_kernel, chunk=chunk, n_chunks=tile // chunk)
    return pl.pallas_call(
        kern,
        grid=(n // tile,),
        in_specs=[_rows(tile, 1024), _rows(tile, 128), _rows(tile, 128),
                  _full(decay.shape), _full(xi.shape), _full(zeta.shape), _full(gc.shape),
                  _full((256, 256)), _full((256, 256))],
        out_specs=_rows(tile, GROUP_WIDTH),
        out_shape=jax.ShapeDtypeStruct((n, GROUP_WIDTH), BF16),
        scratch_shapes=[pltpu.VMEM((GROUP_WIDTH, GROUP_WIDTH), F32)],
        compiler_params=_cparams("arbitrary"),
        name="retention",
    )(proj, cos_t, sin_t, decay, xi, zeta, gc, _head_block_mask(), _head_mean_matrix())


def _tri_matrix(chunk):
    i = np.arange(chunk)
    return jnp.asarray(i[:, None] >= i[None, :], dtype=BF16)


def _m2_kernel(p_ref, cw_ref, cb_ref, dtb_ref, alog_ref, d_ref, ng_ref, tri_ref,
               o_ref, tail_ref, ext_ref, act_ref, st_ref, *, tile, chunk, n_chunks):
    @pl.when(pl.program_id(0) == 0)
    def _():
        tail_ref[...] = jnp.zeros_like(tail_ref)
        st_ref[...] = jnp.zeros_like(st_ref)

    ext_ref[0:8, :] = tail_ref[...]
    ext_ref[8:tile + 8, :] = p_ref[:, 256:1024]
    tail_ref[...] = p_ref[tile - 8:tile, 256:1024]
    conv = cb_ref[...]
    for j in range(M2_CONV):
        lo = 8 - (M2_CONV - 1) + j
        conv = conv + cw_ref[j:j + 1, :] * ext_ref[lo:lo + tile, :]
    act_ref[...] = _silu(conv)
    a_lane = -jnp.exp(alog_ref[...])
    tri = tri_ref[...]
    ti = lax.broadcasted_iota(jnp.int32, (chunk, chunk), 0)
    si = lax.broadcasted_iota(jnp.int32, (chunk, chunk), 1)
    causal = ti >= si
    lane = lax.broadcasted_iota(jnp.int32, (chunk, 128), 1)

    def body(c, carry):
        start = pl.multiple_of(c * chunk, chunk)
        rows = pl.ds(start, chunk)
        xbc = act_ref[rows, :]
        xs = xbc[:, 0:256]
        z = p_ref[rows, 0:256]
        x_dt = p_ref[rows, 1024:1280] + dtb_ref[...]
        dt = jnp.maximum(x_dt, 0.0) + jnp.log1p(jnp.exp(-jnp.abs(x_dt)))
        acum = _dot_exact_lhs(tri, dt * a_lane)
        acum_t = acum.T
        a_last = acum[chunk - 1:chunk, :]
        e_acum = jnp.exp(acum)
        decs = jnp.exp(a_last - acum)
        d_chunk = jnp.exp(a_last)
        xc = xs * dt
        ys = []
        for g in range(2):
            sl = slice(128 * g, 128 * (g + 1))
            bmg = xbc[:, 256 + 128 * g:256 + 128 * (g + 1)].astype(BF16)
            cmg = xbc[:, 512 + 128 * g:512 + 128 * (g + 1)].astype(BF16)
            cb = _dot_nt(cmg, bmg)
            xcg = xc[:, sl]
            yd = jnp.zeros((chunk, 128), F32)
            for hh in range(2):
                col0 = 128 * g + HEAD_DIM * hh
                diff = acum[:, col0:col0 + 1] - acum_t[col0:col0 + 1, :]
                lm = jnp.where(causal, jnp.exp(jnp.where(causal, diff, 0.0)), 0.0)
                xm = jnp.where((lane // HEAD_DIM) == hh, xcg, 0.0).astype(BF16)
                yd = yd + _dot((cb * lm).astype(BF16), xm)
            st = st_ref[:, sl]
            y_off = _dot(cmg, st.astype(BF16)) * e_acum[:, sl]
            st_ref[:, sl] = d_chunk[:, sl] * st + _dot_tn(bmg, (xcg * decs[:, sl]).astype(BF16))
            ys.append(yd + y_off + d_ref[:, sl] * xs[:, sl])
        y = jnp.concatenate(ys, axis=1) * _silu(z)
        o_ref[rows, :] = _rms(y, ng_ref[...]).astype(BF16)
        return carry

    lax.fori_loop(0, n_chunks, body, 0)


def _mamba2(proj, conv_w, conv_b, dt_bias, a_log, d_skip, norm_g):
    n = proj.shape[0]
    tile = min(SEQ_TILE, n)
    chunk = min(CHUNK, tile)
    lanes = lambda v: jnp.repeat(v, HEAD_DIM).reshape(1, GROUP_WIDTH)
    kern = functools.partial(_m2_kernel, tile=tile, chunk=chunk, n_chunks=tile // chunk)
    return pl.pallas_call(
        kern,
        grid=(n // tile,),
        in_specs=[_rows(tile, 1280), _full((M2_CONV, M2_CONV_DIM)), _full((1, M2_CONV_DIM)),
                  _full((1, 256)), _full((1, 256)), _full((1, 256)), _full((1, 256)),
                  _full((chunk, chunk))],
        out_specs=_rows(tile, GROUP_WIDTH),
        out_shape=jax.ShapeDtypeStruct((n, GROUP_WIDTH), BF16),
        scratch_shapes=[pltpu.VMEM((8, M2_CONV_DIM), F32),
                        pltpu.VMEM((tile + 8, M2_CONV_DIM), F32),
                        pltpu.VMEM((tile, M2_CONV_DIM), F32),
                        pltpu.VMEM((M2_STATE, GROUP_WIDTH), F32)],
        compiler_params=_cparams("arbitrary"),
        name="mamba2_ssd",
    )(proj, conv_w, conv_b.reshape(1, -1), lanes(dt_bias), lanes(a_log), lanes(d_skip),
      norm_g.reshape(1, -1), _tri_matrix(chunk))


def _hg_exponent_matrix(chunk):
    levels = int(math.log2(chunk))
    t = np.arange(chunk)[:, None]
    r = np.arange(chunk)[None, :]
    blocks = []
    for lvl in range(levels):
        b = 1 << lvl
        blk = t // b
        odd = (blk % 2) == 1
        q_rows = odd & (r >= blk * b) & (r <= t)
        k_rows = (~odd) & (r > t) & (r <= (blk + 1) * b - 1)
        blocks.append(q_rows | k_rows)
    blocks.append(r <= t)
    blocks.append(r > t)
    return jnp.asarray(np.concatenate(blocks, axis=0), dtype=BF16)


def _hg_kernel(p_ref, lb_ref, ng_ref, gexp_ref, bm_ref, gm_ref, o_ref, st_ref, *, chunk, n_chunks):
    @pl.when(pl.program_id(0) == 0)
    def _():
        st_ref[...] = jnp.zeros_like(st_ref)

    levels = int(math.log2(chunk))
    lane = lax.broadcasted_iota(jnp.int32, (chunk, GROUP_WIDTH), 1)
    row = lax.broadcasted_iota(jnp.int32, (chunk, GROUP_WIDTH), 0)
    head = lane // HEAD_DIM
    ti = lax.broadcasted_iota(jnp.int32, (chunk, chunk), 0)
    si = lax.broadcasted_iota(jnp.int32, (chunk, chunk), 1)
    lb = lb_ref[...]
    gm = gm_ref[...]

    def body(c, carry):
        rows = pl.ds(pl.multiple_of(c * chunk, chunk), chunk)
        q = _silu(p_ref[rows, 0:256])
        forget = lb + (1.0 - lb) * _sigmoid(p_ref[rows, 256:512])
        k = 1.0 - forget
        v = p_ref[rows, 512:768]
        g = p_ref[rows, 768:1024]
        expo = _dot_exact_lhs(gexp_ref[...], jnp.log(forget))

        def head_scores(qe, ke):
            return [_dot_nt(jnp.where(head == h, qe, 0.0).astype(BF16), ke) for h in range(N_HEADS)]

        attn = [jnp.where(ti == si, a, 0.0) for a in head_scores(q, k.astype(BF16))]
        for lvl in range(levels):
            e = jnp.exp(expo[lvl * chunk:(lvl + 1) * chunk, :])
            odd = ((row >> lvl) & 1) == 1
            w = e * jnp.where(odd, q, k)
            qe = jnp.where(odd, w, 0.0)
            ke = jnp.where(odd, 0.0, w).astype(BF16)
            same = (ti >> (lvl + 1)) == (si >> (lvl + 1))
            attn = [a + jnp.where(same, s, 0.0) for a, s in zip(attn, head_scores(qe, ke))]
        intra = jnp.zeros((chunk, GROUP_WIDTH), F32)
        for h in range(N_HEADS):
            intra = intra + _dot(attn[h].astype(BF16), jnp.where(head == h, v, 0.0).astype(BF16))

        bcum = expo[levels * chunk:(levels + 1) * chunk, :]
        suffix = expo[(levels + 1) * chunk:(levels + 2) * chunk, :]
        b_last = bcum[chunk - 1:chunk, :]
        st = st_ref[...]
        cross = _dot_nt((q * jnp.exp(bcum)).astype(BF16), st.astype(BF16))
        st_ref[...] = jnp.exp(b_last) * st + bm_ref[...] * _dot_tn(
            v.astype(BF16), (k * jnp.exp(suffix)).astype(BF16))
        o = intra + cross
        o = o * lax.rsqrt(_dot_exact_rhs(o * o, gm) + NORM_EPS) * ng_ref[...]
        o_ref[rows, :] = (o * _silu(g)).astype(BF16)
        return carry

    lax.fori_loop(0, n_chunks, body, 0)


def _hgrn2(proj, lower_bound, norm_g):
    n = proj.shape[0]
    tile = min(SEQ_TILE, n)
    chunk = min(CHUNK, tile)
    gexp = _hg_exponent_matrix(chunk)
    kern = functools.partial(_hg_kernel, chunk=chunk, n_chunks=tile // chunk)
    return pl.pallas_call(
        kern,
        grid=(n // tile,),
        in_specs=[_rows(tile, 1024), _full((1, 256)), _full((1, 256)), _full(gexp.shape),
                  _full((256, 256)), _full((256, 256))],
        out_specs=_rows(tile, GROUP_WIDTH),
        out_shape=jax.ShapeDtypeStruct((n, GROUP_WIDTH), BF16),
        scratch_shapes=[pltpu.VMEM((GROUP_WIDTH, GROUP_WIDTH), F32)],
        compiler_params=_cparams("arbitrary"),
        name="hgrn2",
    )(proj, lower_bound.reshape(1, -1), norm_g.reshape(1, -1), gexp, _head_block_mask(),
      _head_mean_matrix())


def _s5_operators(a_re, a_im, log_dt, b_re, b_im, c_re, c_im, d_skip, chunk, n_chunks):
    hp = lax.Precision.HIGHEST
    a_re = jnp.minimum(a_re, S5_DT_CLAMP)
    dt = jnp.exp(log_dt)[:, None]
    mag = jnp.exp(a_re * dt)
    ab_re = mag * jnp.cos(a_im * dt)
    ab_im = mag * jnp.sin(a_im * dt)
    den = a_re * a_re + a_im * a_im
    k_re = ((ab_re - 1.0) * a_re + ab_im * a_im) / den
    k_im = (ab_im * a_re - (ab_re - 1.0) * a_im) / den
    bb_re = k_re[..., None] * b_re - k_im[..., None] * b_im
    bb_im = k_re[..., None] * b_im + k_im[..., None] * b_re

    def power(n):
        n = n.astype(F32)[:, None, None]
        m = jnp.exp(n * (a_re * dt))
        return m * jnp.cos(n * (a_im * dt)), m * jnp.sin(n * (a_im * dt))

    steps = jnp.arange(chunk)
    p_re, p_im = power(steps)
    cp_re = c_re[None] * p_re[:, :, None, :] - c_im[None] * p_im[:, :, None, :]
    cp_im = c_re[None] * p_im[:, :, None, :] + c_im[None] * p_re[:, :, None, :]
    taps = (jnp.einsum('tgpn,gnq->gtpq', cp_re, bb_re, precision=hp)
            - jnp.einsum('tgpn,gnq->gtpq', cp_im, bb_im, precision=hp))
    s_idx = steps[:, None]
    t_idx = steps[None, :]
    toe = jnp.where((t_idx >= s_idx)[None, :, :, None, None],
                    taps[:, jnp.clip(t_idx - s_idx, 0, chunk - 1)], 0.0)
    width = chunk * S5_CH
    m_intra = toe.transpose(0, 1, 4, 2, 3).reshape(S5_GROUPS, width, width)
    m_intra = m_intra + jnp.eye(width, dtype=F32)[None] * jnp.tile(d_skip, (1, chunk))[:, None, :]

    r_re, r_im = power(chunk - 1 - steps)
    ws_re = r_re[..., None] * bb_re[None] - r_im[..., None] * bb_im[None]
    ws_im = r_re[..., None] * bb_im[None] + r_im[..., None] * bb_re[None]
    w_state = jnp.concatenate([ws_re, ws_im], axis=2).transpose(1, 0, 3, 2).reshape(
        S5_GROUPS, width, 2 * S5_STATE)

    o_re, o_im = power(steps + 1)
    co_re = c_re[None] * o_re[:, :, None, :] - c_im[None] * o_im[:, :, None, :]
    co_im = c_re[None] * o_im[:, :, None, :] + c_im[None] * o_re[:, :, None, :]
    w_out = jnp.concatenate([co_re, -co_im], axis=3).transpose(1, 3, 0, 2).reshape(
        S5_GROUPS, 2 * S5_STATE, width)

    n_steps = max(1, int(math.ceil(math.log2(n_chunks))))
    s_re, s_im = power(chunk * (2 ** jnp.arange(n_steps)))
    scan = jnp.stack([jnp.concatenate([s_re, s_re], axis=-1),
                      jnp.concatenate([-s_im, s_im], axis=-1)], axis=2).transpose(1, 0, 2, 3)
    return m_intra.astype(BF16), w_state.astype(BF16), w_out.astype(BF16), scan


def _s5_kernel(u_ref, m_ref, ws_ref, wo_ref, a_ref, o_ref, *, n_chunks, n_steps):
    u = u_ref[0]
    y = _dot(u, m_ref[0])
    x = _dot(u, ws_ref[0])
    row = lax.broadcasted_iota(jnp.int32, x.shape, 0)
    for k in range(n_steps):
        sh = 1 << k
        if sh >= n_chunks:
            break
        s = jnp.where(row >= sh, pltpu.roll(x, sh, 0), 0.0)
        x = x + a_ref[0, k, 0:1, :] * s + a_ref[0, k, 1:2, :] * pltpu.roll(s, S5_STATE, 1)
    prev = jnp.where(row >= 1, pltpu.roll(x, 1, 0), 0.0)
    o_ref[0] = y + _dot(prev.astype(BF16), wo_ref[0])


def _s5(u, m_intra, w_state, w_out, scan):
    n = u.shape[0]
    chunk = m_intra.shape[1] // S5_CH
    n_chunks = n // chunk
    width = chunk * S5_CH
    u_g = u.reshape(n_chunks, chunk, S5_GROUPS, S5_CH).transpose(2, 0, 1, 3).reshape(
        S5_GROUPS, n_chunks, width)
    n_steps = scan.shape[1]
    blk = lambda *s: pl.BlockSpec((1,) + s, lambda g: (g,) + (0,) * len(s))
    y_g = pl.pallas_call(
        functools.partial(_s5_kernel, n_chunks=n_chunks, n_steps=n_steps),
        grid=(S5_GROUPS,),
        in_specs=[blk(n_chunks, width), blk(width, width), blk(width, 2 * S5_STATE),
                  blk(2 * S5_STATE, width), blk(n_steps, 2, 2 * S5_STATE)],
        out_specs=blk(n_chunks, width),
        out_shape=jax.ShapeDtypeStruct((S5_GROUPS, n_chunks, width), F32),
        compiler_params=_cparams("parallel"),
        name="s5_ssm",
    )(u_g, m_intra, w_state, w_out, scan)
    return y_g.reshape(S5_GROUPS, n_chunks, chunk, S5_CH).transpose(1, 2, 0, 3).reshape(n, GROUP_WIDTH)


def _route(logits):
    lane = lax.broadcasted_iota(jnp.int32, logits.shape, 1)
    big = jnp.int32(1 << 20)
    neg = jnp.float32(-jnp.inf)
    is_group = lane < MOE_GROUPS
    gl = jnp.where(is_group, logits, neg)
    ge = jnp.where(is_group, jnp.exp(gl - jnp.max(gl, axis=-1, keepdims=True)), 0.0)
    gp = ge / jnp.sum(ge, axis=-1, keepdims=True)
    p_g = jnp.max(gp, axis=-1, keepdims=True)
    g_idx = jnp.min(jnp.where(is_group & (gp == p_g), lane, big), axis=-1, keepdims=True)
    e_lane = lane - MOE_GROUPS
    in_group = (e_lane >= 0) & (e_lane < MOE_EXPERTS) & ((e_lane // MOE_PER_GROUP) == g_idx)
    el = jnp.where(in_group, logits, neg)
    ee = jnp.where(in_group, jnp.exp(el - jnp.max(el, axis=-1, keepdims=True)), 0.0)
    ep = ee / jnp.sum(ee, axis=-1, keepdims=True)
    p1 = jnp.max(jnp.where(in_group, ep, -1.0), axis=-1, keepdims=True)
    i1 = jnp.min(jnp.where(in_group & (ep == p1), lane, big), axis=-1, keepdims=True)
    rest = in_group & (lane != i1)
    p2 = jnp.max(jnp.where(rest, ep, -1.0), axis=-1, keepdims=True)
    i2 = jnp.min(jnp.where(rest & (ep == p2), lane, big), axis=-1, keepdims=True)
    tot = p1 + p2
    src = lane + MOE_GROUPS
    return jnp.where(src == i1, p_g * p1 / tot, 0.0) + jnp.where(src == i2, p_g * p2 / tot, 0.0)


def _out_kernel(x_ref, s5_ref, ret_ref, m2_ref, hg_ref, wglu_ref, bglu_ref, wo_ref, g2_ref,
                wrh_ref, wrl_ref, br_ref, x1_ref, h2_ref, comb_ref):
    y = s5_ref[...]
    y = y * (0.5 * (1.0 + jnp.tanh(math.sqrt(2.0 / math.pi) * (y + 0.044715 * (y * y * y)))))
    y = y * _sigmoid(_dot(y.astype(BF16), wglu_ref[...]) + bglu_ref[...])
    acc = x_ref[...] + _dot(y.astype(BF16), wo_ref[0:256, :])
    acc = acc + _dot(ret_ref[...], wo_ref[256:512, :])
    acc = acc + _dot(m2_ref[...], wo_ref[512:768, :])
    acc = acc + _dot(hg_ref[...], wo_ref[768:1024, :])
    x1_ref[...] = acc
    h2 = _rms(acc, g2_ref[...])
    h2_ref[...] = h2.astype(BF16)
    hi, lo = _split2(h2)
    logits = _dot(hi, wrh_ref[...]) + _dot(hi, wrl_ref[...]) + _dot(lo, wrh_ref[...]) + br_ref[...]
    comb_ref[...] = _route(logits)


def _out_proj(x, y_s5, y_ret, y_m2, y_hg, w_glu, b_glu, w_out, g2, w_route, b_route):
    n, d = x.shape
    tile = min(SEQ_TILE, n)
    wr_hi, wr_lo = _split2(w_route)
    return pl.pallas_call(
        _out_kernel,
        grid=(n // tile,),
        in_specs=[_rows(tile, d)] + [_rows(tile, GROUP_WIDTH)] * 4
        + [_full((256, 256)), _full((1, 256)), _full((d, d)), _full((1, d)),
           _full((d, ROUTE_LANES)), _full((d, ROUTE_LANES)), _full((1, ROUTE_LANES))],
        out_specs=[_rows(tile, d), _rows(tile, d), _rows(tile, ROUTE_LANES)],
        out_shape=[jax.ShapeDtypeStruct((n, d), F32), jax.ShapeDtypeStruct((n, d), BF16),
                   jax.ShapeDtypeStruct((n, ROUTE_LANES), F32)],
        compiler_params=_cparams("parallel"),
        name="out_proj_router",
    )(x, y_s5, y_ret, y_m2, y_hg, w_glu, b_glu, w_out, g2, wr_hi, wr_lo, b_route)


def _moe_kernel(h_ref, comb_ref, x1_ref, wg_ref, wu_ref, wd_ref, gf_ref, o_ref, *, final_norm):
    e = pl.program_id(1)

    @pl.when(e == 0)
    def _():
        o_ref[...] = x1_ref[...]

    h = h_ref[...]
    act = _silu(_dot(h, wg_ref[0])) * _dot(h, wu_ref[0])
    lane = lax.broadcasted_iota(jnp.int32, comb_ref.shape, 1)
    ce = jnp.sum(jnp.where(lane == e, comb_ref[...], 0.0), axis=-1, keepdims=True)
    o_ref[...] += ce * _dot(act.astype(BF16), wd_ref[0])

    if final_norm:
        @pl.when(e == MOE_EXPERTS - 1)
        def _():
            o_ref[...] = _rms(o_ref[...], gf_ref[...])


def _moe(h2, comb, x1, w_gate, w_up, w_down, g_final, final_norm):
    n, d = x1.shape
    ff = w_gate.shape[-1]
    tile = min(MOE_TILE, n)
    return pl.pallas_call(
        functools.partial(_moe_kernel, final_norm=final_norm),
        grid=(n // tile, MOE_EXPERTS),
        in_specs=[pl.BlockSpec((tile, d), lambda i, e: (i, 0)),
                  pl.BlockSpec((tile, ROUTE_LANES), lambda i, e: (i, 0)),
                  pl.BlockSpec((tile, d), lambda i, e: (i, 0)),
                  pl.BlockSpec((1, d, ff), lambda i, e: (e, 0, 0)),
                  pl.BlockSpec((1, d, ff), lambda i, e: (e, 0, 0)),
                  pl.BlockSpec((1, ff, d), lambda i, e: (e, 0, 0)),
                  pl.BlockSpec((1, d), lambda i, e: (0, 0))],
        out_specs=pl.BlockSpec((tile, d), lambda i, e: (i, 0)),
        out_shape=jax.ShapeDtypeStruct((n, d), F32),
        compiler_params=_cparams("parallel", "arbitrary"),
        name="moe_experts",
    )(h2, comb, x1, w_gate, w_up, w_down, g_final)


def kernel(x, positions, norm1_g, w_in, w_out, s5_a_re, s5_a_im, s5_log_dt, s5_b_re, s5_b_im, s5_c_re, s5_c_im, s5_d, s5_w_glu, s5_b_glu, m2_conv_w, m2_conv_b, m2_dt_bias, m2_a_log, m2_d, m2_norm_g, hg_lb_logits, hg_norm_g, norm2_g, moe_w_group, moe_b_group, moe_w_expert, moe_b_expert, moe_w_gate, moe_w_up, moe_w_down, final_norm_g):
    bsz, seqlen, d = x.shape
    assert bsz == 1 and seqlen % SEQ_TILE == 0 and seqlen % MOE_TILE == 0
    depth = w_in.shape[0]
    n_chunks = seqlen // S5_CHUNK

    lb_probs = jax.nn.softmax(hg_lb_logits.astype(F32), axis=0)
    lower_bounds = jnp.cumsum(lb_probs, axis=0) - lb_probs[0]
    cos_t, sin_t = _rope_tables(positions.reshape(seqlen, 1))
    dt_col = 9 * GROUP_WIDTH

    xc = x.reshape(seqlen, d)
    for l in range(depth):
        w = w_in[l]
        w_p = jnp.concatenate([w[:, :dt_col], jnp.repeat(w[:, dt_col:dt_col + N_HEADS], HEAD_DIM, axis=1),
                               w[:, dt_col + N_HEADS:]], axis=1).astype(BF16)
        u, p_ret, p_m2, p_hg = _in_proj(xc, norm1_g[l].reshape(1, d), w_p)
        ops = _s5_operators(s5_a_re[l], s5_a_im[l], s5_log_dt[l], s5_b_re[l], s5_b_im[l],
                            s5_c_re[l], s5_c_im[l], s5_d[l], S5_CHUNK, n_chunks)
        y_s5 = _s5(u, *ops)
        y_ret = _retention(p_ret, cos_t, sin_t)
        y_m2 = _mamba2(p_m2, m2_conv_w[l], m2_conv_b[l], m2_dt_bias[l], m2_a_log[l], m2_d[l], m2_norm_g[l])
        y_hg = _hgrn2(p_hg, lower_bounds[l], hg_norm_g[l])
        w_route = jnp.zeros((d, ROUTE_LANES), F32)
        w_route = w_route.at[:, :MOE_GROUPS].set(moe_w_group[l])
        w_route = w_route.at[:, MOE_GROUPS:MOE_GROUPS + MOE_EXPERTS].set(moe_w_expert[l])
        b_route = jnp.zeros((1, ROUTE_LANES), F32)
        b_route = b_route.at[0, :MOE_GROUPS].set(moe_b_group[l])
        b_route = b_route.at[0, MOE_GROUPS:MOE_GROUPS + MOE_EXPERTS].set(moe_b_expert[l])
        x1, h2, comb = _out_proj(xc, y_s5, y_ret, y_m2, y_hg, s5_w_glu[l].astype(BF16),
                                 s5_b_glu[l].reshape(1, -1), w_out[l].astype(BF16),
                                 norm2_g[l].reshape(1, d), w_route, b_route)
        xc = _moe(h2, comb, x1, moe_w_gate[l].astype(BF16), moe_w_up[l].astype(BF16),
                  moe_w_down[l].astype(BF16), final_norm_g.reshape(1, d), final_norm=(l == depth - 1))
    return xc.reshape(bsz, seqlen, d)
```

```python
import functools
import math

import numpy as np
import jax
import jax.numpy as jnp
from jax import lax
from jax.experimental import pallas as pl
from jax.experimental.pallas import tpu as pltpu

F32 = jnp.float32
BF16 = jnp.bfloat16
NORM_EPS = 1e-6

GROUP_WIDTH = 256
HEAD_DIM = 64
N_HEADS = GROUP_WIDTH // HEAD_DIM
S5_GROUPS = 16
S5_CH = 16
S5_STATE = 64
S5_DT_CLAMP = -1e-4
M2_STATE = 128
M2_CONV = 4
M2_CONV_DIM = 768
ROPE_BASE = 10000.0
MOE_GROUPS = 4
MOE_PER_GROUP = 4
MOE_EXPERTS = 16
ROUTE_LANES = 128

SEQ_TILE = 512
CHUNK = 128
S5_CHUNK = 16
MOE_TILE = 1024
VMEM_LIMIT = 56 * 1024 * 1024


def _cparams(*sem):
    return pltpu.CompilerParams(dimension_semantics=sem, vmem_limit_bytes=VMEM_LIMIT)


def _dot(a, b):
    return jnp.dot(a, b, preferred_element_type=F32)


def _dot_nt(a, b):
    return lax.dot_general(a, b, (((1,), (1,)), ((), ())), preferred_element_type=F32)


def _dot_tn(a, b):
    return lax.dot_general(a, b, (((0,), (0,)), ((), ())), preferred_element_type=F32)


def _split2(x):
    hi = x.astype(BF16)
    return hi, (x - hi.astype(F32)).astype(BF16)


def _split3(x):
    hi = x.astype(BF16)
    r = x - hi.astype(F32)
    mid = r.astype(BF16)
    return hi, mid, (r - mid.astype(F32)).astype(BF16)


def _dot_exact_lhs(m, x):
    hi, mid, lo = _split3(x)
    return _dot(m, hi) + _dot(m, mid) + _dot(m, lo)


def _dot_exact_rhs(x, m):
    hi, lo = _split2(x)
    return _dot(hi, m) + _dot(lo, m)


def _sigmoid(x):
    return 1.0 / (1.0 + jnp.exp(-x))


def _silu(x):
    return x * _sigmoid(x)


def _rms(x, g):
    return x * lax.rsqrt(jnp.mean(x * x, axis=-1, keepdims=True) + NORM_EPS) * g


def _full(shape):
    return pl.BlockSpec(shape, lambda *_: (0,) * len(shape))


def _rows(tile, width):
    return pl.BlockSpec((tile, width), lambda i: (i, 0))


IN_SEGMENTS = (256, 1024, 1280, 1024)


def _in_proj_kernel(x_ref, g_ref, w_ref, ub_ref, uf_ref, ret_ref, m2_ref, hg_ref, u_tmp, *, tile):
    hb = _rms(x_ref[...], g_ref[...]).astype(BF16)
    c0, c1, c2, c3 = np.cumsum(IN_SEGMENTS)
    ret_ref[...] = _dot(hb, w_ref[:, c0:c1])
    m2_ref[...] = _dot(hb, w_ref[:, c1:c2])
    hg_ref[...] = _dot(hb, w_ref[:, c2:c3])
    u = _dot(hb, w_ref[:, 0:c0])
    for j in range(GROUP_WIDTH // 128):
        u_tmp[j] = u[:, 128 * j:128 * (j + 1)]
    for s in range(S5_CHUNK):
        for j in range(GROUP_WIDTH // 128):
            v = u_tmp[j, pl.ds(s, tile // S5_CHUNK, stride=S5_CHUNK), :]
            lanes = slice(GROUP_WIDTH * s + 128 * j, GROUP_WIDTH * s + 128 * (j + 1))
            uf_ref[:, lanes] = v
            ub_ref[:, lanes] = v.astype(BF16)


def _in_proj(x, g, w):
    n, d = x.shape
    tile = min(SEQ_TILE, n)
    fold = S5_CHUNK * GROUP_WIDTH
    return pl.pallas_call(
        functools.partial(_in_proj_kernel, tile=tile),
        grid=(n // tile,),
        in_specs=[_rows(tile, d), _full((1, d)), _full(w.shape)],
        out_specs=[_rows(tile // S5_CHUNK, fold)] * 2 + [_rows(tile, s) for s in IN_SEGMENTS[1:]],
        out_shape=[jax.ShapeDtypeStruct((n // S5_CHUNK, fold), BF16),
                   jax.ShapeDtypeStruct((n // S5_CHUNK, fold), F32)]
        + [jax.ShapeDtypeStruct((n, s), F32) for s in IN_SEGMENTS[1:]],
        scratch_shapes=[pltpu.VMEM((GROUP_WIDTH // 128, tile, 128), F32)],
        compiler_params=_cparams("parallel"),
        name="in_proj",
    )(x, g, w)


def _rope_kernel(pos_ref, invf_ref, cos_ref, sin_ref):
    ang = pos_ref[...].astype(F32) * invf_ref[...]
    cos_ref[...] = jnp.cos(ang)
    sin_ref[...] = jnp.sin(ang)


def _rope_tables(positions):
    n = positions.shape[0]
    tile = min(SEQ_TILE, n)
    half = HEAD_DIM // 2
    inv_freq = ROPE_BASE ** (-jnp.arange(half, dtype=F32) / half)
    invf = jnp.tile(inv_freq, 128 // half).reshape(1, 128)
    return pl.pallas_call(
        _rope_kernel,
        grid=(n // tile,),
        in_specs=[_rows(tile, 1), _full((1, 128))],
        out_specs=[_rows(tile, 128), _rows(tile, 128)],
        out_shape=[jax.ShapeDtypeStruct((n, 128), F32)] * 2,
        compiler_params=_cparams("parallel"),
        name="rope_tables",
    )(positions, invf)


def _head_mean_matrix():
    h = np.arange(GROUP_WIDTH) // HEAD_DIM
    return jnp.asarray((h[:, None] == h[None, :]) / HEAD_DIM, dtype=BF16)


def _head_block_mask():
    h = np.arange(GROUP_WIDTH) // HEAD_DIM
    return jnp.asarray(h[:, None] == h[None, :], dtype=F32)


def _ret_constants(chunk):
    lg = np.log1p(-np.exp2(-5.0 - np.arange(N_HEADS, dtype=np.float64)))
    idx = np.arange(chunk, dtype=np.float64)
    rel = idx[:, None] - idx[None, :]
    decay = np.where(rel >= 0, np.exp(np.maximum(rel, 0.0)[None] * lg[:, None, None]), 0.0)
    lane_lg = np.repeat(lg, HEAD_DIM)
    xi = np.exp((idx + 1.0)[:, None] * lane_lg[None, :])
    zeta = np.exp((chunk - 1.0 - idx)[:, None] * lane_lg[None, :])
    h = np.arange(GROUP_WIDTH) // HEAD_DIM
    gc = np.where(h[:, None] == h[None, :], np.exp(chunk * lane_lg)[:, None], 0.0)
    f = lambda a: jnp.asarray(a, dtype=F32)
    return f(decay), f(xi), f(zeta), f(gc)


def _ret_kernel(p_ref, cos_ref, sin_ref, dec_ref, xi_ref, zeta_ref, gc_ref, bm_ref, gm_ref,
                o_ref, r_ref, *, chunk, n_chunks):
    @pl.when(pl.program_id(0) == 0)
    def _():
        r_ref[...] = jnp.zeros_like(r_ref)

    lane = lax.broadcasted_iota(jnp.int32, (chunk, GROUP_WIDTH), 1)
    first_half = (lane % HEAD_DIM) < (HEAD_DIM // 2)
    head = lane // HEAD_DIM
    gm = gm_ref[...]

    def rope(t, cos2, sin2):
        rot = jnp.where(first_half, -pltpu.roll(t, GROUP_WIDTH - HEAD_DIM // 2, 1),
                        pltpu.roll(t, HEAD_DIM // 2, 1))
        return t * cos2 + rot * sin2

    def body(c, carry):
        rows = pl.ds(pl.multiple_of(c * chunk, chunk), chunk)
        cs = cos_ref[rows, :]
        sn = sin_ref[rows, :]
        cos2 = jnp.concatenate([cs, cs], axis=1)
        sin2 = jnp.concatenate([sn, sn], axis=1)
        q = rope(p_ref[rows, 0:256], cos2, sin2)
        k = rope(p_ref[rows, 256:512], cos2, sin2) * (HEAD_DIM ** -0.5)
        v = p_ref[rows, 512:768]
        g = p_ref[rows, 768:1024]
        kb = k.astype(BF16)
        inner = jnp.zeros((chunk, GROUP_WIDTH), F32)
        for h in range(N_HEADS):
            qm = jnp.where(head == h, q, 0.0).astype(BF16)
            vm = jnp.where(head == h, v, 0.0).astype(BF16)
            scores = _dot_nt(qm, kb) * dec_ref[h]
            inner = inner + _dot(scores.astype(BF16), vm)
        r_prev = r_ref[...]
        cross = _dot((q * xi_ref[...]).astype(BF16), r_prev.astype(BF16))
        r_ref[...] = gc_ref[...] * r_prev + bm_ref[...] * _dot_tn(kb, (zeta_ref[...] * v).astype(BF16))
        o = inner + cross
        cen = o - _dot_exact_rhs(o, gm)
        var = _dot_exact_rhs(cen * cen, gm)
        o_ref[rows, :] = (cen * lax.rsqrt(var + NORM_EPS) * _silu(g)).astype(BF16)
        return carry

    lax.fori_loop(0, n_chunks, body, 0)


def _retention(proj, cos_t, sin_t):
    n = proj.shape[0]
    tile = min(SEQ_TILE, n)
    chunk = min(CHUNK, tile)
    decay, xi, zeta, gc = _ret_constants(chunk)
    kern = functools.partial(_ret_kernel, chunk=chunk, n_chunks=tile // chunk)
    return pl.pallas_call(
        kern,
        grid=(n // tile,),
        in_specs=[_rows(tile, 1024), _rows(tile, 128), _rows(tile, 128),
                  _full(decay.shape), _full(xi.shape), _full(zeta.shape), _full(gc.shape),
                  _full((256, 256)), _full((256, 256))],
        out_specs=_rows(tile, GROUP_WIDTH),
        out_shape=jax.ShapeDtypeStruct((n, GROUP_WIDTH), BF16),
        scratch_shapes=[pltpu.VMEM((GROUP_WIDTH, GROUP_WIDTH), F32)],
        compiler_params=_cparams("arbitrary"),
        name="retention",
    )(proj, cos_t, sin_t, decay, xi, zeta, gc, _head_block_mask(), _head_mean_matrix())


def _tri_matrix(chunk):
    i = np.arange(chunk)
    return jnp.asarray(i[:, None] >= i[None, :], dtype=BF16)


def _m2_kernel(p_ref, cw_ref, cb_ref, dtb_ref, alog_ref, d_ref, ng_ref, tri_ref,
               o_ref, tail_ref, ext_ref, act_ref, st_ref, *, tile, chunk, n_chunks):
    @pl.when(pl.program_id(0) == 0)
    def _():
        tail_ref[...] = jnp.zeros_like(tail_ref)
        st_ref[...] = jnp.zeros_like(st_ref)

    ext_ref[0:8, :] = tail_ref[...]
    ext_ref[8:tile + 8, :] = p_ref[:, 256:1024]
    tail_ref[...] = p_ref[tile - 8:tile, 256:1024]
    conv = cb_ref[...]
    for j in range(M2_CONV):
        lo = 8 - (M2_CONV - 1) + j
        conv = conv + cw_ref[j:j + 1, :] * ext_ref[lo:lo + tile, :]
    act_ref[...] = _silu(conv)
    a_lane = -jnp.exp(alog_ref[...])
    tri = tri_ref[...]
    ti = lax.broadcasted_iota(jnp.int32, (chunk, chunk), 0)
    si = lax.broadcasted_iota(jnp.int32, (chunk, chunk), 1)
    causal = ti >= si
    lane = lax.broadcasted_iota(jnp.int32, (chunk, 128), 1)

    def body(c, carry):
        start = pl.multiple_of(c * chunk, chunk)
        rows = pl.ds(start, chunk)
        xbc = act_ref[rows, :]
        xs = xbc[:, 0:256]
        z = p_ref[rows, 0:256]
        x_dt = p_ref[rows, 1024:1280] + dtb_ref[...]
        dt = jnp.maximum(x_dt, 0.0) + jnp.log1p(jnp.exp(-jnp.abs(x_dt)))
        acum = _dot_exact_lhs(tri, dt * a_lane)
        acum_t = acum.T
        a_last = acum[chunk - 1:chunk, :]
        e_acum = jnp.exp(acum)
        decs = jnp.exp(a_last - acum)
        d_chunk = jnp.exp(a_last)
        xc = xs * dt
        ys = []
        for g in range(2):
            sl = slice(128 * g, 128 * (g + 1))
            bmg = xbc[:, 256 + 128 * g:256 + 128 * (g + 1)].astype(BF16)
            cmg = xbc[:, 512 + 128 * g:512 + 128 * (g + 1)].astype(BF16)
            cb = _dot_nt(cmg, bmg)
            xcg = xc[:, sl]
            yd = jnp.zeros((chunk, 128), F32)
            for hh in range(2):
                col0 = 128 * g + HEAD_DIM * hh
                diff = acum[:, col0:col0 + 1] - acum_t[col0:col0 + 1, :]
                lm = jnp.where(causal, jnp.exp(jnp.where(causal, diff, 0.0)), 0.0)
                xm = jnp.where((lane // HEAD_DIM) == hh, xcg, 0.0).astype(BF16)
                yd = yd + _dot((cb * lm).astype(BF16), xm)
            st = st_ref[:, sl]
            y_off = _dot(cmg, st.astype(BF16)) * e_acum[:, sl]
            st_ref[:, sl] = d_chunk[:, sl] * st + _dot_tn(bmg, (xcg * decs[:, sl]).astype(BF16))
            ys.append(yd + y_off + d_ref[:, sl] * xs[:, sl])
        y = jnp.concatenate(ys, axis=1) * _silu(z)
        o_ref[rows, :] = _rms(y, ng_ref[...]).astype(BF16)
        return carry

    lax.fori_loop(0, n_chunks, body, 0)


def _mamba2(proj, conv_w, conv_b, dt_bias, a_log, d_skip, norm_g):
    n = proj.shape[0]
    tile = min(SEQ_TILE, n)
    chunk = min(CHUNK, tile)
    lanes = lambda v: jnp.repeat(v, HEAD_DIM).reshape(1, GROUP_WIDTH)
    kern = functools.partial(_m2_kernel, tile=tile, chunk=chunk, n_chunks=tile // chunk)
    return pl.pallas_call(
        kern,
        grid=(n // tile,),
        in_specs=[_rows(tile, 1280), _full((M2_CONV, M2_CONV_DIM)), _full((1, M2_CONV_DIM)),
                  _full((1, 256)), _full((1, 256)), _full((1, 256)), _full((1, 256)),
                  _full((chunk, chunk))],
        out_specs=_rows(tile, GROUP_WIDTH),
        out_shape=jax.ShapeDtypeStruct((n, GROUP_WIDTH), BF16),
        scratch_shapes=[pltpu.VMEM((8, M2_CONV_DIM), F32),
                        pltpu.VMEM((tile + 8, M2_CONV_DIM), F32),
                        pltpu.VMEM((tile, M2_CONV_DIM), F32),
                        pltpu.VMEM((M2_STATE, GROUP_WIDTH), F32)],
        compiler_params=_cparams("arbitrary"),
        name="mamba2_ssd",
    )(proj, conv_w, conv_b.reshape(1, -1), lanes(dt_bias), lanes(a_log), lanes(d_skip),
      norm_g.reshape(1, -1), _tri_matrix(chunk))


def _hg_exponent_matrix(chunk):
    levels = int(math.log2(chunk))
    t = np.arange(chunk)[:, None]
    r = np.arange(chunk)[None, :]
    blocks = []
    for lvl in range(levels):
        b = 1 << lvl
        blk = t // b
        odd = (blk % 2) == 1
        q_rows = odd & (r >= blk * b) & (r <= t)
        k_rows = (~odd) & (r > t) & (r <= (blk + 1) * b - 1)
        blocks.append(q_rows | k_rows)
    blocks.append(r <= t)
    blocks.append(r > t)
    return jnp.asarray(np.concatenate(blocks, axis=0), dtype=BF16)


def _hg_kernel(p_ref, lb_ref, ng_ref, gexp_ref, bm_ref, gm_ref, o_ref, st_ref, *, chunk, n_chunks):
    @pl.when(pl.program_id(0) == 0)
    def _():
        st_ref[...] = jnp.zeros_like(st_ref)

    levels = int(math.log2(chunk))
    lane = lax.broadcasted_iota(jnp.int32, (chunk, GROUP_WIDTH), 1)
    row = lax.broadcasted_iota(jnp.int32, (chunk, GROUP_WIDTH), 0)
    head = lane // HEAD_DIM
    ti = lax.broadcasted_iota(jnp.int32, (chunk, chunk), 0)
    si = lax.broadcasted_iota(jnp.int32, (chunk, chunk), 1)
    lb = lb_ref[...]
    gm = gm_ref[...]

    def body(c, carry):
        rows = pl.ds(pl.multiple_of(c * chunk, chunk), chunk)
        q = _silu(p_ref[rows, 0:256])
        forget = lb + (1.0 - lb) * _sigmoid(p_ref[rows, 256:512])
        k = 1.0 - forget
        v = p_ref[rows, 512:768]
        g = p_ref[rows, 768:1024]
        expo = _dot_exact_lhs(gexp_ref[...], jnp.log(forget))

        def head_scores(qe, ke):
            return [_dot_nt(jnp.where(head == h, qe, 0.0).astype(BF16), ke) for h in range(N_HEADS)]

        attn = [jnp.where(ti == si, a, 0.0) for a in head_scores(q, k.astype(BF16))]
        for lvl in range(levels):
            e = jnp.exp(expo[lvl * chunk:(lvl + 1) * chunk, :])
            odd = ((row >> lvl) & 1) == 1
            w = e * jnp.where(odd, q, k)
            qe = jnp.where(odd, w, 0.0)
            ke = jnp.where(odd, 0.0, w).astype(BF16)
            same = (ti >> (lvl + 1)) == (si >> (lvl + 1))
            attn = [a + jnp.where(same, s, 0.0) for a, s in zip(attn, head_scores(qe, ke))]
        intra = jnp.zeros((chunk, GROUP_WIDTH), F32)
        for h in range(N_HEADS):
            intra = intra + _dot(attn[h].astype(BF16), jnp.where(head == h, v, 0.0).astype(BF16))

        bcum = expo[levels * chunk:(levels + 1) * chunk, :]
        suffix = expo[(levels + 1) * chunk:(levels + 2) * chunk, :]
        b_last = bcum[chunk - 1:chunk, :]
        st = st_ref[...]
        cross = _dot_nt((q * jnp.exp(bcum)).astype(BF16), st.astype(BF16))
        st_ref[...] = jnp.exp(b_last) * st + bm_ref[...] * _dot_tn(
            v.astype(BF16), (k * jnp.exp(suffix)).astype(BF16))
        o = intra + cross
        o = o * lax.rsqrt(_dot_exact_rhs(o * o, gm) + NORM_EPS) * ng_ref[...]
        o_ref[rows, :] = (o * _silu(g)).astype(BF16)
        return carry

    lax.fori_loop(0, n_chunks, body, 0)


def _hgrn2(proj, lower_bound, norm_g):
    n = proj.shape[0]
    tile = min(SEQ_TILE, n)
    chunk = min(CHUNK, tile)
    gexp = _hg_exponent_matrix(chunk)
    kern = functools.partial(_hg_kernel, chunk=chunk, n_chunks=tile // chunk)
    return pl.pallas_call(
        kern,
        grid=(n // tile,),
        in_specs=[_rows(tile, 1024), _full((1, 256)), _full((1, 256)), _full(gexp.shape),
                  _full((256, 256)), _full((256, 256))],
        out_specs=_rows(tile, GROUP_WIDTH),
        out_shape=jax.ShapeDtypeStruct((n, GROUP_WIDTH), BF16),
        scratch_shapes=[pltpu.VMEM((GROUP_WIDTH, GROUP_WIDTH), F32)],
        compiler_params=_cparams("arbitrary"),
        name="hgrn2",
    )(proj, lower_bound.reshape(1, -1), norm_g.reshape(1, -1), gexp, _head_block_mask(),
      _head_mean_matrix())


S5_LANES = S5_GROUPS * 2 * S5_STATE


def _s5_rows(a_re, a_im, log_dt):
    are = jnp.minimum(a_re, S5_DT_CLAMP)
    dt = jnp.exp(log_dt)
    lam_re = are * dt
    lam_im = a_im * dt
    mag = jnp.exp(lam_re)
    ab_re = mag * jnp.cos(lam_im)
    ab_im = mag * jnp.sin(lam_im)
    den = are * are + a_im * a_im
    k_re = ((ab_re - 1.0) * are + ab_im * a_im) / den
    k_im = (ab_im * are - (ab_re - 1.0) * a_im) / den
    return lam_re, lam_im, k_re, k_im


def _s5_power(lam_re, lam_im, e):
    m = jnp.exp(e * lam_re)
    return m * jnp.cos(e * lam_im), m * jnp.sin(e * lam_im)


def _s5_state_kernel(u_ref, are_ref, aim_ref, ldt_ref, b1_ref, b2_ref, ca_ref, k_ref, sp_ref,
                     bb1_ref, bb2_ref, inc_ref, *, n_chunks):
    s = pl.program_id(0)
    lam_re, lam_im, k_re, k_im = _s5_rows(are_ref[...], aim_ref[...], ldt_ref[...])

    @pl.when(s == 0)
    def _():
        bb1_ref[...] = k_re * b1_ref[...] + k_im * b2_ref[...]
        bb2_ref[...] = k_re * b2_ref[...] - k_im * b1_ref[...]

    p_re, p_im = _s5_power(lam_re, lam_im, (S5_CHUNK - 1 - s).astype(F32))
    w = p_re * bb1_ref[...] + p_im * bb2_ref[...]
    w_hi, w_lo = _split2(w)
    c_hi, c_lo = _split2(ca_ref[...])
    k_ref[0] = (_dot_nt(w_hi, c_hi) + _dot_nt(w_hi, c_lo) + _dot_nt(w_lo, c_hi)).astype(BF16)
    contrib = _dot(u_ref[...], w_hi)

    @pl.when(s == 0)
    def _():
        inc_ref[...] = contrib

    @pl.when(s > 0)
    def _():
        inc_ref[...] += contrib

    @pl.when(s == S5_CHUNK - 1)
    def _():
        n_steps = max(1, int(math.ceil(math.log2(n_chunks))))
        step = lax.broadcasted_iota(jnp.int32, (16, S5_LANES), 0)
        e = (jnp.left_shift(1, step) * S5_CHUNK).astype(F32)
        a_re_all, a_im_all = _s5_power(lam_re, lam_im, e)
        row = lax.broadcasted_iota(jnp.int32, (n_chunks, 2 * S5_STATE), 0)
        lane = lax.broadcasted_iota(jnp.int32, (1, 2 * S5_STATE), 1)
        sign = jnp.where(lane < S5_STATE, -1.0, 1.0)
        for g in range(S5_GROUPS):
            sl = slice(2 * S5_STATE * g, 2 * S5_STATE * (g + 1))
            x = inc_ref[:, sl]
            for k in range(n_steps):
                sh = 1 << k
                if sh >= n_chunks:
                    break
                prev = jnp.where(row >= sh, pltpu.roll(x, sh, 0), 0.0)
                x = (x + a_re_all[k:k + 1, sl] * prev
                     + (a_im_all[k:k + 1, sl] * sign) * pltpu.roll(prev, S5_STATE, 1))
            sp_ref[:, sl] = jnp.where(row >= 1, pltpu.roll(x, 1, 0), 0.0).astype(BF16)


def _s5_out_kernel(ub_ref, uf_ref, k_ref, sp_ref, are_ref, aim_ref, ldt_ref, ca_ref, cb_ref, d_ref,
                   y_ref, acc_ref):
    t = pl.program_id(0)
    lam_re, lam_im, _, _ = _s5_rows(are_ref[...], aim_ref[...], ldt_ref[...])
    p_re, p_im = _s5_power(lam_re, lam_im, (t + 1).astype(F32))
    w_out = (p_re * ca_ref[...] + p_im * cb_ref[...]).astype(BF16)
    acc_ref[...] = _dot_nt(sp_ref[...], w_out) + d_ref[...] * uf_ref[...]
    for s in range(S5_CHUNK):
        @pl.when(s <= t)
        def _():
            acc_ref[...] += _dot(ub_ref[:, GROUP_WIDTH * s:GROUP_WIDTH * (s + 1)], k_ref[t - s])
    y_ref[...] = acc_ref[...]


def _s5_embed(re, im):
    pack = jnp.concatenate([re, im], axis=2)
    eye = jnp.eye(S5_GROUPS, dtype=F32)
    return (eye[:, None, :, None] * pack[:, :, None, :]).reshape(S5_GROUPS * S5_CH, S5_LANES)


def _s5(u_b, u_f, a_re, a_im, log_dt, b_re, b_im, c_re, c_im, d_skip):
    n_chunks, fold = u_b.shape
    row = lambda v: jnp.concatenate([v, v], axis=1).reshape(1, S5_LANES)
    are, aim = row(a_re), row(a_im)
    ldt = jnp.repeat(log_dt, 2 * S5_STATE).reshape(1, S5_LANES)
    bt_re, bt_im = b_re.transpose(0, 2, 1), b_im.transpose(0, 2, 1)
    b1, b2 = _s5_embed(bt_re, bt_im), _s5_embed(-bt_im, bt_re)
    ca, cb = _s5_embed(c_re, -c_im), _s5_embed(-c_im, -c_re)
    col = lambda: pl.BlockSpec((n_chunks, GROUP_WIDTH), lambda s: (0, s))
    taps, s_prev = pl.pallas_call(
        functools.partial(_s5_state_kernel, n_chunks=n_chunks),
        grid=(S5_CHUNK,),
        in_specs=[col()] + [_full((1, S5_LANES))] * 3 + [_full((GROUP_WIDTH, S5_LANES))] * 3,
        out_specs=[pl.BlockSpec((1, GROUP_WIDTH, GROUP_WIDTH), lambda s: (S5_CHUNK - 1 - s, 0, 0)),
                   _full((n_chunks, S5_LANES))],
        out_shape=[jax.ShapeDtypeStruct((S5_CHUNK, GROUP_WIDTH, GROUP_WIDTH), BF16),
                   jax.ShapeDtypeStruct((n_chunks, S5_LANES), BF16)],
        scratch_shapes=[pltpu.VMEM((GROUP_WIDTH, S5_LANES), F32), pltpu.VMEM((GROUP_WIDTH, S5_LANES), F32),
                        pltpu.VMEM((n_chunks, S5_LANES), F32)],
        compiler_params=_cparams("arbitrary"),
        name="s5_state",
    )(u_b, are, aim, ldt, b1, b2, ca)
    return pl.pallas_call(
        _s5_out_kernel,
        grid=(S5_CHUNK,),
        in_specs=[_full((n_chunks, fold)), col(), _full(taps.shape), _full(s_prev.shape)]
        + [_full((1, S5_LANES))] * 3 + [_full((GROUP_WIDTH, S5_LANES))] * 2 + [_full((1, GROUP_WIDTH))],
        out_specs=col(),
        out_shape=jax.ShapeDtypeStruct((n_chunks, fold), F32),
        scratch_shapes=[pltpu.VMEM((n_chunks, GROUP_WIDTH), F32)],
        compiler_params=_cparams("arbitrary"),
        name="s5_out",
    )(u_b, u_f, taps, s_prev, are, aim, ldt, ca, cb, d_skip.reshape(1, GROUP_WIDTH))


def _route(logits):
    lane = lax.broadcasted_iota(jnp.int32, logits.shape, 1)
    big = jnp.int32(1 << 20)
    neg = jnp.float32(-jnp.inf)
    is_group = lane < MOE_GROUPS
    gl = jnp.where(is_group, logits, neg)
    ge = jnp.where(is_group, jnp.exp(gl - jnp.max(gl, axis=-1, keepdims=True)), 0.0)
    gp = ge / jnp.sum(ge, axis=-1, keepdims=True)
    p_g = jnp.max(gp, axis=-1, keepdims=True)
    g_idx = jnp.min(jnp.where(is_group & (gp == p_g), lane, big), axis=-1, keepdims=True)
    e_lane = lane - MOE_GROUPS
    in_group = (e_lane >= 0) & (e_lane < MOE_EXPERTS) & ((e_lane // MOE_PER_GROUP) == g_idx)
    el = jnp.where(in_group, logits, neg)
    ee = jnp.where(in_group, jnp.exp(el - jnp.max(el, axis=-1, keepdims=True)), 0.0)
    ep = ee / jnp.sum(ee, axis=-1, keepdims=True)
    p1 = jnp.max(jnp.where(in_group, ep, -1.0), axis=-1, keepdims=True)
    i1 = jnp.min(jnp.where(in_group & (ep == p1), lane, big), axis=-1, keepdims=True)
    rest = in_group & (lane != i1)
    p2 = jnp.max(jnp.where(rest, ep, -1.0), axis=-1, keepdims=True)
    i2 = jnp.min(jnp.where(rest & (ep == p2), lane, big), axis=-1, keepdims=True)
    tot = p1 + p2
    src = lane + MOE_GROUPS
    return jnp.where(src == i1, p_g * p1 / tot, 0.0) + jnp.where(src == i2, p_g * p2 / tot, 0.0)


def _out_kernel(x_ref, s5_ref, ret_ref, m2_ref, hg_ref, wglu_ref, bglu_ref, wo_ref, g2_ref,
                wrh_ref, wrl_ref, br_ref, x1_ref, h2_ref, comb_ref, s5_tmp, *, tile):
    for s in range(S5_CHUNK):
        for j in range(GROUP_WIDTH // 128):
            lanes = slice(GROUP_WIDTH * s + 128 * j, GROUP_WIDTH * s + 128 * (j + 1))
            s5_tmp[j, pl.ds(s, tile // S5_CHUNK, stride=S5_CHUNK), :] = s5_ref[:, lanes]
    y = jnp.concatenate([s5_tmp[j] for j in range(GROUP_WIDTH // 128)], axis=1)
    y = y * (0.5 * (1.0 + jnp.tanh(math.sqrt(2.0 / math.pi) * (y + 0.044715 * (y * y * y)))))
    y = y * _sigmoid(_dot(y.astype(BF16), wglu_ref[...]) + bglu_ref[...])
    acc = x_ref[...] + _dot(y.astype(BF16), wo_ref[0:256, :])
    acc = acc + _dot(ret_ref[...], wo_ref[256:512, :])
    acc = acc + _dot(m2_ref[...], wo_ref[512:768, :])
    acc = acc + _dot(hg_ref[...], wo_ref[768:1024, :])
    x1_ref[...] = acc
    h2 = _rms(acc, g2_ref[...])
    h2_ref[...] = h2.astype(BF16)
    hi, lo = _split2(h2)
    logits = _dot(hi, wrh_ref[...]) + _dot(hi, wrl_ref[...]) + _dot(lo, wrh_ref[...]) + br_ref[...]
    comb_ref[...] = _route(logits)


def _out_proj(x, y_s5, y_ret, y_m2, y_hg, w_glu, b_glu, w_out, g2, w_route, b_route):
    n, d = x.shape
    tile = min(SEQ_TILE, n)
    wr_hi, wr_lo = _split2(w_route)
    return pl.pallas_call(
        functools.partial(_out_kernel, tile=tile),
        grid=(n // tile,),
        in_specs=[_rows(tile, d), _rows(tile // S5_CHUNK, S5_CHUNK * GROUP_WIDTH)] + [_rows(tile, GROUP_WIDTH)] * 3
        + [_full((256, 256)), _full((1, 256)), _full((d, d)), _full((1, d)),
           _full((d, ROUTE_LANES)), _full((d, ROUTE_LANES)), _full((1, ROUTE_LANES))],
        out_specs=[_rows(tile, d), _rows(tile, d), _rows(tile, ROUTE_LANES)],
        out_shape=[jax.ShapeDtypeStruct((n, d), F32), jax.ShapeDtypeStruct((n, d), BF16),
                   jax.ShapeDtypeStruct((n, ROUTE_LANES), F32)],
        scratch_shapes=[pltpu.VMEM((GROUP_WIDTH // 128, tile, 128), F32)],
        compiler_params=_cparams("parallel"),
        name="out_proj_router",
    )(x, y_s5, y_ret, y_m2, y_hg, w_glu, b_glu, w_out, g2, wr_hi, wr_lo, b_route)


def _moe_kernel(h_ref, comb_ref, x1_ref, wg_ref, wu_ref, wd_ref, gf_ref, o_ref, *, final_norm):
    e = pl.program_id(1)

    @pl.when(e == 0)
    def _():
        o_ref[...] = x1_ref[...]

    h = h_ref[...]
    act = _silu(_dot(h, wg_ref[0])) * _dot(h, wu_ref[0])
    lane = lax.broadcasted_iota(jnp.int32, comb_ref.shape, 1)
    ce = jnp.sum(jnp.where(lane == e, comb_ref[...], 0.0), axis=-1, keepdims=True)
    o_ref[...] += ce * _dot(act.astype(BF16), wd_ref[0])

    if final_norm:
        @pl.when(e == MOE_EXPERTS - 1)
        def _():
            o_ref[...] = _rms(o_ref[...], gf_ref[...])


def _moe(h2, comb, x1, w_gate, w_up, w_down, g_final, final_norm):
    n, d = x1.shape
    ff = w_gate.shape[-1]
    tile = min(MOE_TILE, n)
    return pl.pallas_call(
        functools.partial(_moe_kernel, final_norm=final_norm),
        grid=(n // tile, MOE_EXPERTS),
        in_specs=[pl.BlockSpec((tile, d), lambda i, e: (i, 0)),
                  pl.BlockSpec((tile, ROUTE_LANES), lambda i, e: (i, 0)),
                  pl.BlockSpec((tile, d), lambda i, e: (i, 0)),
                  pl.BlockSpec((1, d, ff), lambda i, e: (e, 0, 0)),
                  pl.BlockSpec((1, d, ff), lambda i, e: (e, 0, 0)),
                  pl.BlockSpec((1, ff, d), lambda i, e: (e, 0, 0)),
                  pl.BlockSpec((1, d), lambda i, e: (0, 0))],
        out_specs=pl.BlockSpec((tile, d), lambda i, e: (i, 0)),
        out_shape=jax.ShapeDtypeStruct((n, d), F32),
        compiler_params=_cparams("parallel", "arbitrary"),
        name="moe_experts",
    )(h2, comb, x1, w_gate, w_up, w_down, g_final)


def kernel(x, positions, norm1_g, w_in, w_out, s5_a_re, s5_a_im, s5_log_dt, s5_b_re, s5_b_im, s5_c_re, s5_c_im, s5_d, s5_w_glu, s5_b_glu, m2_conv_w, m2_conv_b, m2_dt_bias, m2_a_log, m2_d, m2_norm_g, hg_lb_logits, hg_norm_g, norm2_g, moe_w_group, moe_b_group, moe_w_expert, moe_b_expert, moe_w_gate, moe_w_up, moe_w_down, final_norm_g):
    bsz, seqlen, d = x.shape
    assert bsz == 1 and seqlen % SEQ_TILE == 0 and seqlen % MOE_TILE == 0
    depth = w_in.shape[0]

    lb_probs = jax.nn.softmax(hg_lb_logits.astype(F32), axis=0)
    lower_bounds = jnp.cumsum(lb_probs, axis=0) - lb_probs[0]
    cos_t, sin_t = _rope_tables(positions.reshape(seqlen, 1))
    dt_col = 9 * GROUP_WIDTH

    xc = x.reshape(seqlen, d)
    for l in range(depth):
        w = w_in[l]
        w_p = jnp.concatenate([w[:, :dt_col], jnp.repeat(w[:, dt_col:dt_col + N_HEADS], HEAD_DIM, axis=1),
                               w[:, dt_col + N_HEADS:]], axis=1).astype(BF16)
        u_b, u_f, p_ret, p_m2, p_hg = _in_proj(xc, norm1_g[l].reshape(1, d), w_p)
        y_s5 = _s5(u_b, u_f, s5_a_re[l], s5_a_im[l], s5_log_dt[l], s5_b_re[l], s5_b_im[l],
                   s5_c_re[l], s5_c_im[l], s5_d[l])
        y_ret = _retention(p_ret, cos_t, sin_t)
        y_m2 = _mamba2(p_m2, m2_conv_w[l], m2_conv_b[l], m2_dt_bias[l], m2_a_log[l], m2_d[l], m2_norm_g[l])
        y_hg = _hgrn2(p_hg, lower_bounds[l], hg_norm_g[l])
        w_route = jnp.zeros((d, ROUTE_LANES), F32)
        w_route = w_route.at[:, :MOE_GROUPS].set(moe_w_group[l])
        w_route = w_route.at[:, MOE_GROUPS:MOE_GROUPS + MOE_EXPERTS].set(moe_w_expert[l])
        b_route = jnp.zeros((1, ROUTE_LANES), F32)
        b_route = b_route.at[0, :MOE_GROUPS].set(moe_b_group[l])
        b_route = b_route.at[0, MOE_GROUPS:MOE_GROUPS + MOE_EXPERTS].set(moe_b_expert[l])
        x1, h2, comb = _out_proj(xc, y_s5, y_ret, y_m2, y_hg, s5_w_glu[l].astype(BF16),
                                 s5_b_glu[l].reshape(1, -1), w_out[l].astype(BF16),
                                 norm2_g[l].reshape(1, d), w_route, b_route)
        xc = _moe(h2, comb, x1, moe_w_gate[l].astype(BF16), moe_w_up[l].astype(BF16),
                  moe_w_down[l].astype(BF16), final_norm_g.reshape(1, d), final_norm=(l == depth - 1))
    return xc.reshape(bsz, seqlen, d)
```

```python
import functools
import math

import numpy as np
import jax
import jax.numpy as jnp
from jax import lax
from jax.experimental import pallas as pl
from jax.experimental.pallas import tpu as pltpu
from jax.experimental.pallas import tpu_sc as plsc

F32 = jnp.float32
BF16 = jnp.bfloat16
NORM_EPS = 1e-6

GROUP_WIDTH = 256
HEAD_DIM = 64
N_HEADS = GROUP_WIDTH // HEAD_DIM
S5_GROUPS = 16
S5_CH = 16
S5_STATE = 64
S5_DT_CLAMP = -1e-4
M2_STATE = 128
M2_CONV = 4
M2_CONV_DIM = 768
ROPE_BASE = 10000.0
MOE_GROUPS = 4
MOE_PER_GROUP = 4
MOE_EXPERTS = 16
ROUTE_LANES = 128

SEQ_TILE = 512
CHUNK = 128
S5_CHUNK = 16
MOE_TOPK = 2
EXPERT_TILE = 256
SC_WINDOW = 128
VMEM_LIMIT = 56 * 1024 * 1024


def _cparams(*sem):
    return pltpu.CompilerParams(dimension_semantics=sem, vmem_limit_bytes=VMEM_LIMIT)


def _dot(a, b):
    return jnp.dot(a, b, preferred_element_type=F32)


def _dot_nt(a, b):
    return lax.dot_general(a, b, (((1,), (1,)), ((), ())), preferred_element_type=F32)


def _dot_tn(a, b):
    return lax.dot_general(a, b, (((0,), (0,)), ((), ())), preferred_element_type=F32)


def _split2(x):
    hi = x.astype(BF16)
    return hi, (x - hi.astype(F32)).astype(BF16)


def _split3(x):
    hi = x.astype(BF16)
    r = x - hi.astype(F32)
    mid = r.astype(BF16)
    return hi, mid, (r - mid.astype(F32)).astype(BF16)


def _dot_exact_lhs(m, x):
    hi, mid, lo = _split3(x)
    return _dot(m, hi) + _dot(m, mid) + _dot(m, lo)


def _dot_exact_rhs(x, m):
    hi, lo = _split2(x)
    return _dot(hi, m) + _dot(lo, m)


def _sigmoid(x):
    return 1.0 / (1.0 + jnp.exp(-x))


def _silu(x):
    return x * _sigmoid(x)


def _rms(x, g):
    return x * lax.rsqrt(jnp.mean(x * x, axis=-1, keepdims=True) + NORM_EPS) * g


def _full(shape):
    return pl.BlockSpec(shape, lambda *_: (0,) * len(shape))


def _rows(tile, width):
    return pl.BlockSpec((tile, width), lambda i: (i, 0))


IN_SEGMENTS = (256, 1024, 1280, 1024)


DT_COL = 9 * GROUP_WIDTH
W_PREP_ROWS = 128


def _w_prep_kernel(w_ref, o_ref):
    o_ref[:, 0:DT_COL] = w_ref[0, :, 0:DT_COL].astype(BF16)
    head = lax.broadcasted_iota(jnp.int32, (W_PREP_ROWS, GROUP_WIDTH), 1) // HEAD_DIM
    rep = jnp.zeros((W_PREP_ROWS, GROUP_WIDTH), F32)
    for h in range(N_HEADS):
        rep = jnp.where(head == h, w_ref[0, :, DT_COL + h:DT_COL + h + 1], rep)
    o_ref[:, DT_COL:DT_COL + GROUP_WIDTH] = rep.astype(BF16)
    o_ref[:, DT_COL + GROUP_WIDTH:] = w_ref[0, :, DT_COL + N_HEADS:].astype(BF16)


def _w_prep(w_in, layer):
    _, d, n_in = w_in.shape
    return pl.pallas_call(
        _w_prep_kernel,
        grid=(d // W_PREP_ROWS,),
        in_specs=[pl.BlockSpec((1, W_PREP_ROWS, n_in), lambda i: (layer, i, 0))],
        out_specs=pl.BlockSpec((W_PREP_ROWS, sum(IN_SEGMENTS)), lambda i: (i, 0)),
        out_shape=jax.ShapeDtypeStruct((d, sum(IN_SEGMENTS)), BF16),
        compiler_params=_cparams("parallel"),
        name="w_in_prep",
    )(w_in)


def _in_proj_kernel(x_ref, g_ref, w_ref, ub_ref, uf_ref, ret_ref, m2_ref, hg_ref, u_tmp, *, tile):
    hb = _rms(x_ref[...], g_ref[...]).astype(BF16)
    c0, c1, c2, c3 = np.cumsum(IN_SEGMENTS)
    ret_ref[...] = _dot(hb, w_ref[:, c0:c1])
    m2_ref[...] = _dot(hb, w_ref[:, c1:c2])
    hg_ref[...] = _dot(hb, w_ref[:, c2:c3])
    u = _dot(hb, w_ref[:, 0:c0])
    for j in range(GROUP_WIDTH // 128):
        u_tmp[j] = u[:, 128 * j:128 * (j + 1)]
    for s in range(S5_CHUNK):
        for j in range(GROUP_WIDTH // 128):
            v = u_tmp[j, pl.ds(s, tile // S5_CHUNK, stride=S5_CHUNK), :]
            lanes = slice(GROUP_WIDTH * s + 128 * j, GROUP_WIDTH * s + 128 * (j + 1))
            uf_ref[:, lanes] = v
            ub_ref[:, lanes] = v.astype(BF16)


def _in_proj(x, g, w):
    n, d = x.shape
    tile = min(SEQ_TILE, n)
    fold = S5_CHUNK * GROUP_WIDTH
    return pl.pallas_call(
        functools.partial(_in_proj_kernel, tile=tile),
        grid=(n // tile,),
        in_specs=[_rows(tile, d), _full((1, d)), _full(w.shape)],
        out_specs=[_rows(tile // S5_CHUNK, fold)] * 2 + [_rows(tile, s) for s in IN_SEGMENTS[1:]],
        out_shape=[jax.ShapeDtypeStruct((n // S5_CHUNK, fold), BF16),
                   jax.ShapeDtypeStruct((n // S5_CHUNK, fold), F32)]
        + [jax.ShapeDtypeStruct((n, s), F32) for s in IN_SEGMENTS[1:]],
        scratch_shapes=[pltpu.VMEM((GROUP_WIDTH // 128, tile, 128), F32)],
        compiler_params=_cparams("parallel"),
        name="in_proj",
    )(x, g, w)


def _rope_kernel(pos_ref, invf_ref, cos_ref, sin_ref):
    ang = pos_ref[...].astype(F32) * invf_ref[...]
    cos_ref[...] = jnp.cos(ang)
    sin_ref[...] = jnp.sin(ang)


def _rope_tables(positions):
    n = positions.shape[0]
    tile = min(SEQ_TILE, n)
    half = HEAD_DIM // 2
    inv_freq = ROPE_BASE ** (-jnp.arange(half, dtype=F32) / half)
    invf = jnp.tile(inv_freq, 128 // half).reshape(1, 128)
    return pl.pallas_call(
        _rope_kernel,
        grid=(n // tile,),
        in_specs=[_rows(tile, 1), _full((1, 128))],
        out_specs=[_rows(tile, 128), _rows(tile, 128)],
        out_shape=[jax.ShapeDtypeStruct((n, 128), F32)] * 2,
        compiler_params=_cparams("parallel"),
        name="rope_tables",
    )(positions, invf)


def _head_mean_matrix():
    h = np.arange(GROUP_WIDTH) // HEAD_DIM
    return jnp.asarray((h[:, None] == h[None, :]) / HEAD_DIM, dtype=BF16)


def _head_block_mask():
    h = np.arange(GROUP_WIDTH) // HEAD_DIM
    return jnp.asarray(h[:, None] == h[None, :], dtype=F32)


def _ret_constants(chunk):
    lg = np.log1p(-np.exp2(-5.0 - np.arange(N_HEADS, dtype=np.float64)))
    idx = np.arange(chunk, dtype=np.float64)
    rel = idx[:, None] - idx[None, :]
    decay = np.where(rel >= 0, np.exp(np.maximum(rel, 0.0)[None] * lg[:, None, None]), 0.0)
    lane_lg = np.repeat(lg, HEAD_DIM)
    xi = np.exp((idx + 1.0)[:, None] * lane_lg[None, :])
    zeta = np.exp((chunk - 1.0 - idx)[:, None] * lane_lg[None, :])
    h = np.arange(GROUP_WIDTH) // HEAD_DIM
    gc = np.where(h[:, None] == h[None, :], np.exp(chunk * lane_lg)[:, None], 0.0)
    f = lambda a: jnp.asarray(a, dtype=F32)
    return f(decay), f(xi), f(zeta), f(gc)


def _ret_kernel(p_ref, cos_ref, sin_ref, dec_ref, xi_ref, zeta_ref, gc_ref, bm_ref, gm_ref,
                o_ref, r_ref, *, chunk, n_chunks):
    @pl.when(pl.program_id(0) == 0)
    def _():
        r_ref[...] = jnp.zeros_like(r_ref)

    lane = lax.broadcasted_iota(jnp.int32, (chunk, GROUP_WIDTH), 1)
    first_half = (lane % HEAD_DIM) < (HEAD_DIM // 2)
    head = lane // HEAD_DIM
    gm = gm_ref[...]

    def rope(t, cos2, sin2):
        rot = jnp.where(first_half, -pltpu.roll(t, GROUP_WIDTH - HEAD_DIM // 2, 1),
                        pltpu.roll(t, HEAD_DIM // 2, 1))
        return t * cos2 + rot * sin2

    def body(c, carry):
        rows = pl.ds(pl.multiple_of(c * chunk, chunk), chunk)
        cs = cos_ref[rows, :]
        sn = sin_ref[rows, :]
        cos2 = jnp.concatenate([cs, cs], axis=1)
        sin2 = jnp.concatenate([sn, sn], axis=1)
        q = rope(p_ref[rows, 0:256], cos2, sin2)
        k = rope(p_ref[rows, 256:512], cos2, sin2) * (HEAD_DIM ** -0.5)
        v = p_ref[rows, 512:768]
        g = p_ref[rows, 768:1024]
        kb = k.astype(BF16)
        inner = jnp.zeros((chunk, GROUP_WIDTH), F32)
        for h in range(N_HEADS):
            qm = jnp.where(head == h, q, 0.0).astype(BF16)
            vm = jnp.where(head == h, v, 0.0).astype(BF16)
            scores = _dot_nt(qm, kb) * dec_ref[h]
            inner = inner + _dot(scores.astype(BF16), vm)
        r_prev = r_ref[...]
        cross = _dot((q * xi_ref[...]).astype(BF16), r_prev.astype(BF16))
        r_ref[...] = gc_ref[...] * r_prev + bm_ref[...] * _dot_tn(kb, (zeta_ref[...] * v).astype(BF16))
        o = inner + cross
        cen = o - _dot_exact_rhs(o, gm)
        var = _dot_exact_rhs(cen * cen, gm)
        o_ref[rows, :] = (cen * lax.rsqrt(var + NORM_EPS) * _silu(g)).astype(BF16)
        return carry

    lax.fori_loop(0, n_chunks, body, 0)


def _retention(proj, cos_t, sin_t):
    n = proj.shape[0]
    tile = min(SEQ_TILE, n)
    chunk = min(CHUNK, tile)
    decay, xi, zeta, gc = _ret_constants(chunk)
    kern = functools.partial(_ret_kernel, chunk=chunk, n_chunks=tile // chunk)
    return pl.pallas_call(
        kern,
        grid=(n // tile,),
        in_specs=[_rows(tile, 1024), _rows(tile, 128), _rows(tile, 128),
                  _full(decay.shape), _full(xi.shape), _full(zeta.shape), _full(gc.shape),
                  _full((256, 256)), _full((256, 256))],
        out_specs=_rows(tile, GROUP_WIDTH),
        out_shape=jax.ShapeDtypeStruct((n, GROUP_WIDTH), BF16),
        scratch_shapes=[pltpu.VMEM((GROUP_WIDTH, GROUP_WIDTH), F32)],
        compiler_params=_cparams("arbitrary"),
        name="retention",
    )(proj, cos_t, sin_t, decay, xi, zeta, gc, _head_block_mask(), _head_mean_matrix())


def _tri_matrix(chunk):
    i = np.arange(chunk)
    return jnp.asarray(i[:, None] >= i[None, :], dtype=BF16)


def _m2_kernel(p_ref, cw_ref, cb_ref, dtb_ref, alog_ref, d_ref, ng_ref, tri_ref,
               o_ref, tail_ref, ext_ref, act_ref, st_ref, *, tile, chunk, n_chunks):
    @pl.when(pl.program_id(0) == 0)
    def _():
        tail_ref[...] = jnp.zeros_like(tail_ref)
        st_ref[...] = jnp.zeros_like(st_ref)

    ext_ref[0:8, :] = tail_ref[...]
    ext_ref[8:tile + 8, :] = p_ref[:, 256:1024]
    tail_ref[...] = p_ref[tile - 8:tile, 256:1024]
    conv = cb_ref[...]
    for j in range(M2_CONV):
        lo = 8 - (M2_CONV - 1) + j
        conv = conv + cw_ref[j:j + 1, :] * ext_ref[lo:lo + tile, :]
    act_ref[...] = _silu(conv)
    a_lane = -jnp.exp(alog_ref[...])
    tri = tri_ref[...]
    ti = lax.broadcasted_iota(jnp.int32, (chunk, chunk), 0)
    si = lax.broadcasted_iota(jnp.int32, (chunk, chunk), 1)
    causal = ti >= si
    lane = lax.broadcasted_iota(jnp.int32, (chunk, 128), 1)

    def body(c, carry):
        start = pl.multiple_of(c * chunk, chunk)
        rows = pl.ds(start, chunk)
        xbc = act_ref[rows, :]
        xs = xbc[:, 0:256]
        z = p_ref[rows, 0:256]
        x_dt = p_ref[rows, 1024:1280] + dtb_ref[...]
        dt = jnp.maximum(x_dt, 0.0) + jnp.log1p(jnp.exp(-jnp.abs(x_dt)))
        acum = _dot_exact_lhs(tri, dt * a_lane)
        acum_t = acum.T
        a_last = acum[chunk - 1:chunk, :]
        e_acum = jnp.exp(acum)
        decs = jnp.exp(a_last - acum)
        d_chunk = jnp.exp(a_last)
        xc = xs * dt
        ys = []
        for g in range(2):
            sl = slice(128 * g, 128 * (g + 1))
            bmg = xbc[:, 256 + 128 * g:256 + 128 * (g + 1)].astype(BF16)
            cmg = xbc[:, 512 + 128 * g:512 + 128 * (g + 1)].astype(BF16)
            cb = _dot_nt(cmg, bmg)
            xcg = xc[:, sl]
            yd = jnp.zeros((chunk, 128), F32)
            for hh in range(2):
                col0 = 128 * g + HEAD_DIM * hh
                diff = acum[:, col0:col0 + 1] - acum_t[col0:col0 + 1, :]
                lm = jnp.where(causal, jnp.exp(jnp.where(causal, diff, 0.0)), 0.0)
                xm = jnp.where((lane // HEAD_DIM) == hh, xcg, 0.0).astype(BF16)
                yd = yd + _dot((cb * lm).astype(BF16), xm)
            st = st_ref[:, sl]
            y_off = _dot(cmg, st.astype(BF16)) * e_acum[:, sl]
            st_ref[:, sl] = d_chunk[:, sl] * st + _dot_tn(bmg, (xcg * decs[:, sl]).astype(BF16))
            ys.append(yd + y_off + d_ref[:, sl] * xs[:, sl])
        y = jnp.concatenate(ys, axis=1) * _silu(z)
        o_ref[rows, :] = _rms(y, ng_ref[...]).astype(BF16)
        return carry

    lax.fori_loop(0, n_chunks, body, 0)


def _mamba2(proj, conv_w, conv_b, dt_bias, a_log, d_skip, norm_g):
    n = proj.shape[0]
    tile = min(SEQ_TILE, n)
    chunk = min(CHUNK, tile)
    lanes = lambda v: jnp.repeat(v, HEAD_DIM).reshape(1, GROUP_WIDTH)
    kern = functools.partial(_m2_kernel, tile=tile, chunk=chunk, n_chunks=tile // chunk)
    return pl.pallas_call(
        kern,
        grid=(n // tile,),
        in_specs=[_rows(tile, 1280), _full((M2_CONV, M2_CONV_DIM)), _full((1, M2_CONV_DIM)),
                  _full((1, 256)), _full((1, 256)), _full((1, 256)), _full((1, 256)),
                  _full((chunk, chunk))],
        out_specs=_rows(tile, GROUP_WIDTH),
        out_shape=jax.ShapeDtypeStruct((n, GROUP_WIDTH), BF16),
        scratch_shapes=[pltpu.VMEM((8, M2_CONV_DIM), F32),
                        pltpu.VMEM((tile + 8, M2_CONV_DIM), F32),
                        pltpu.VMEM((tile, M2_CONV_DIM), F32),
                        pltpu.VMEM((M2_STATE, GROUP_WIDTH), F32)],
        compiler_params=_cparams("arbitrary"),
        name="mamba2_ssd",
    )(proj, conv_w, conv_b.reshape(1, -1), lanes(dt_bias), lanes(a_log), lanes(d_skip),
      norm_g.reshape(1, -1), _tri_matrix(chunk))


def _hg_exponent_matrix(chunk):
    levels = int(math.log2(chunk))
    t = np.arange(chunk)[:, None]
    r = np.arange(chunk)[None, :]
    blocks = []
    for lvl in range(levels):
        b = 1 << lvl
        blk = t // b
        odd = (blk % 2) == 1
        q_rows = odd & (r >= blk * b) & (r <= t)
        k_rows = (~odd) & (r > t) & (r <= (blk + 1) * b - 1)
        blocks.append(q_rows | k_rows)
    blocks.append(r <= t)
    blocks.append(r > t)
    return jnp.asarray(np.concatenate(blocks, axis=0), dtype=BF16)


def _hg_kernel(p_ref, lb_ref, ng_ref, gexp_ref, bm_ref, gm_ref, o_ref, st_ref, *, chunk, n_chunks):
    @pl.when(pl.program_id(0) == 0)
    def _():
        st_ref[...] = jnp.zeros_like(st_ref)

    levels = int(math.log2(chunk))
    lane = lax.broadcasted_iota(jnp.int32, (chunk, GROUP_WIDTH), 1)
    row = lax.broadcasted_iota(jnp.int32, (chunk, GROUP_WIDTH), 0)
    head = lane // HEAD_DIM
    ti = lax.broadcasted_iota(jnp.int32, (chunk, chunk), 0)
    si = lax.broadcasted_iota(jnp.int32, (chunk, chunk), 1)
    lb = lb_ref[...]
    gm = gm_ref[...]

    def body(c, carry):
        rows = pl.ds(pl.multiple_of(c * chunk, chunk), chunk)
        q = _silu(p_ref[rows, 0:256])
        forget = lb + (1.0 - lb) * _sigmoid(p_ref[rows, 256:512])
        k = 1.0 - forget
        v = p_ref[rows, 512:768]
        g = p_ref[rows, 768:1024]
        expo = _dot_exact_lhs(gexp_ref[...], jnp.log(forget))

        def head_scores(qe, ke):
            return [_dot_nt(jnp.where(head == h, qe, 0.0).astype(BF16), ke) for h in range(N_HEADS)]

        attn = [jnp.where(ti == si, a, 0.0) for a in head_scores(q, k.astype(BF16))]
        for lvl in range(levels):
            e = jnp.exp(expo[lvl * chunk:(lvl + 1) * chunk, :])
            odd = ((row >> lvl) & 1) == 1
            w = e * jnp.where(odd, q, k)
            qe = jnp.where(odd, w, 0.0)
            ke = jnp.where(odd, 0.0, w).astype(BF16)
            same = (ti >> (lvl + 1)) == (si >> (lvl + 1))
            attn = [a + jnp.where(same, s, 0.0) for a, s in zip(attn, head_scores(qe, ke))]
        intra = jnp.zeros((chunk, GROUP_WIDTH), F32)
        for h in range(N_HEADS):
            intra = intra + _dot(attn[h].astype(BF16), jnp.where(head == h, v, 0.0).astype(BF16))

        bcum = expo[levels * chunk:(levels + 1) * chunk, :]
        suffix = expo[(levels + 1) * chunk:(levels + 2) * chunk, :]
        b_last = bcum[chunk - 1:chunk, :]
        st = st_ref[...]
        cross = _dot_nt((q * jnp.exp(bcum)).astype(BF16), st.astype(BF16))
        st_ref[...] = jnp.exp(b_last) * st + bm_ref[...] * _dot_tn(
            v.astype(BF16), (k * jnp.exp(suffix)).astype(BF16))
        o = intra + cross
        o = o * lax.rsqrt(_dot_exact_rhs(o * o, gm) + NORM_EPS) * ng_ref[...]
        o_ref[rows, :] = (o * _silu(g)).astype(BF16)
        return carry

    lax.fori_loop(0, n_chunks, body, 0)


def _hgrn2(proj, lower_bound, norm_g):
    n = proj.shape[0]
    tile = min(SEQ_TILE, n)
    chunk = min(CHUNK, tile)
    gexp = _hg_exponent_matrix(chunk)
    kern = functools.partial(_hg_kernel, chunk=chunk, n_chunks=tile // chunk)
    return pl.pallas_call(
        kern,
        grid=(n // tile,),
        in_specs=[_rows(tile, 1024), _full((1, 256)), _full((1, 256)), _full(gexp.shape),
                  _full((256, 256)), _full((256, 256))],
        out_specs=_rows(tile, GROUP_WIDTH),
        out_shape=jax.ShapeDtypeStruct((n, GROUP_WIDTH), BF16),
        scratch_shapes=[pltpu.VMEM((GROUP_WIDTH, GROUP_WIDTH), F32)],
        compiler_params=_cparams("arbitrary"),
        name="hgrn2",
    )(proj, lower_bound.reshape(1, -1), norm_g.reshape(1, -1), gexp, _head_block_mask(),
      _head_mean_matrix())


S5_LANES = S5_GROUPS * 2 * S5_STATE


def _s5_rows(a_re, a_im, log_dt):
    are = jnp.minimum(a_re, S5_DT_CLAMP)
    dt = jnp.exp(log_dt)
    lam_re = are * dt
    lam_im = a_im * dt
    mag = jnp.exp(lam_re)
    ab_re = mag * jnp.cos(lam_im)
    ab_im = mag * jnp.sin(lam_im)
    den = are * are + a_im * a_im
    k_re = ((ab_re - 1.0) * are + ab_im * a_im) / den
    k_im = (ab_im * are - (ab_re - 1.0) * a_im) / den
    return lam_re, lam_im, k_re, k_im


def _s5_power(lam_re, lam_im, e):
    m = jnp.exp(e * lam_re)
    return m * jnp.cos(e * lam_im), m * jnp.sin(e * lam_im)


def _s5_state_kernel(u_ref, are_ref, aim_ref, ldt_ref, b1_ref, b2_ref, ca_ref, k_ref, sp_ref,
                     bb1_ref, bb2_ref, inc_ref, *, n_chunks):
    s = pl.program_id(0)
    lam_re, lam_im, k_re, k_im = _s5_rows(are_ref[...], aim_ref[...], ldt_ref[...])

    @pl.when(s == 0)
    def _():
        bb1_ref[...] = k_re * b1_ref[...] + k_im * b2_ref[...]
        bb2_ref[...] = k_re * b2_ref[...] - k_im * b1_ref[...]

    p_re, p_im = _s5_power(lam_re, lam_im, (S5_CHUNK - 1 - s).astype(F32))
    w = p_re * bb1_ref[...] + p_im * bb2_ref[...]
    w_hi, w_lo = _split2(w)
    c_hi, c_lo = _split2(ca_ref[...])
    k_ref[0] = (_dot_nt(w_hi, c_hi) + _dot_nt(w_hi, c_lo) + _dot_nt(w_lo, c_hi)).astype(BF16)
    contrib = _dot(u_ref[...], w_hi)

    @pl.when(s == 0)
    def _():
        inc_ref[...] = contrib

    @pl.when(s > 0)
    def _():
        inc_ref[...] += contrib

    @pl.when(s == S5_CHUNK - 1)
    def _():
        n_steps = max(1, int(math.ceil(math.log2(n_chunks))))
        step = lax.broadcasted_iota(jnp.int32, (16, S5_LANES), 0)
        e = (jnp.left_shift(1, step) * S5_CHUNK).astype(F32)
        a_re_all, a_im_all = _s5_power(lam_re, lam_im, e)
        row = lax.broadcasted_iota(jnp.int32, (n_chunks, 2 * S5_STATE), 0)
        lane = lax.broadcasted_iota(jnp.int32, (1, 2 * S5_STATE), 1)
        sign = jnp.where(lane < S5_STATE, -1.0, 1.0)
        for g in range(S5_GROUPS):
            sl = slice(2 * S5_STATE * g, 2 * S5_STATE * (g + 1))
            x = inc_ref[:, sl]
            for k in range(n_steps):
                sh = 1 << k
                if sh >= n_chunks:
                    break
                prev = jnp.where(row >= sh, pltpu.roll(x, sh, 0), 0.0)
                x = (x + a_re_all[k:k + 1, sl] * prev
                     + (a_im_all[k:k + 1, sl] * sign) * pltpu.roll(prev, S5_STATE, 1))
            sp_ref[:, sl] = jnp.where(row >= 1, pltpu.roll(x, 1, 0), 0.0).astype(BF16)


def _s5_out_kernel(ub_ref, uf_ref, k_ref, sp_ref, are_ref, aim_ref, ldt_ref, ca_ref, cb_ref, d_ref,
                   y_ref, acc_ref):
    t = pl.program_id(0)
    lam_re, lam_im, _, _ = _s5_rows(are_ref[...], aim_ref[...], ldt_ref[...])
    p_re, p_im = _s5_power(lam_re, lam_im, (t + 1).astype(F32))
    w_out = (p_re * ca_ref[...] + p_im * cb_ref[...]).astype(BF16)
    acc_ref[...] = _dot_nt(sp_ref[...], w_out) + d_ref[...] * uf_ref[...]
    for s in range(S5_CHUNK):
        @pl.when(s <= t)
        def _():
            acc_ref[...] += _dot(ub_ref[:, GROUP_WIDTH * s:GROUP_WIDTH * (s + 1)], k_ref[t - s])
    y_ref[...] = acc_ref[...]


def _s5_embed(re, im):
    pack = jnp.concatenate([re, im], axis=2)
    eye = jnp.eye(S5_GROUPS, dtype=F32)
    return (eye[:, None, :, None] * pack[:, :, None, :]).reshape(S5_GROUPS * S5_CH, S5_LANES)


def _s5(u_b, u_f, a_re, a_im, log_dt, b_re, b_im, c_re, c_im, d_skip):
    n_chunks, fold = u_b.shape
    row = lambda v: jnp.concatenate([v, v], axis=1).reshape(1, S5_LANES)
    are, aim = row(a_re), row(a_im)
    ldt = jnp.repeat(log_dt, 2 * S5_STATE).reshape(1, S5_LANES)
    bt_re, bt_im = b_re.transpose(0, 2, 1), b_im.transpose(0, 2, 1)
    b1, b2 = _s5_embed(bt_re, bt_im), _s5_embed(-bt_im, bt_re)
    ca, cb = _s5_embed(c_re, -c_im), _s5_embed(-c_im, -c_re)
    col = lambda: pl.BlockSpec((n_chunks, GROUP_WIDTH), lambda s: (0, s))
    taps, s_prev = pl.pallas_call(
        functools.partial(_s5_state_kernel, n_chunks=n_chunks),
        grid=(S5_CHUNK,),
        in_specs=[col()] + [_full((1, S5_LANES))] * 3 + [_full((GROUP_WIDTH, S5_LANES))] * 3,
        out_specs=[pl.BlockSpec((1, GROUP_WIDTH, GROUP_WIDTH), lambda s: (S5_CHUNK - 1 - s, 0, 0)),
                   _full((n_chunks, S5_LANES))],
        out_shape=[jax.ShapeDtypeStruct((S5_CHUNK, GROUP_WIDTH, GROUP_WIDTH), BF16),
                   jax.ShapeDtypeStruct((n_chunks, S5_LANES), BF16)],
        scratch_shapes=[pltpu.VMEM((GROUP_WIDTH, S5_LANES), F32), pltpu.VMEM((GROUP_WIDTH, S5_LANES), F32),
                        pltpu.VMEM((n_chunks, S5_LANES), F32)],
        compiler_params=_cparams("arbitrary"),
        name="s5_state",
    )(u_b, are, aim, ldt, b1, b2, ca)
    return pl.pallas_call(
        _s5_out_kernel,
        grid=(S5_CHUNK,),
        in_specs=[_full((n_chunks, fold)), col(), _full(taps.shape), _full(s_prev.shape)]
        + [_full((1, S5_LANES))] * 3 + [_full((GROUP_WIDTH, S5_LANES))] * 2 + [_full((1, GROUP_WIDTH))],
        out_specs=col(),
        out_shape=jax.ShapeDtypeStruct((n_chunks, fold), F32),
        scratch_shapes=[pltpu.VMEM((n_chunks, GROUP_WIDTH), F32)],
        compiler_params=_cparams("arbitrary"),
        name="s5_out",
    )(u_b, u_f, taps, s_prev, are, aim, ldt, ca, cb, d_skip.reshape(1, GROUP_WIDTH))


def _route(logits):
    lane = lax.broadcasted_iota(jnp.int32, logits.shape, 1)
    big = jnp.int32(1 << 20)
    neg = jnp.float32(-jnp.inf)
    is_group = lane < MOE_GROUPS
    gl = jnp.where(is_group, logits, neg)
    ge = jnp.where(is_group, jnp.exp(gl - jnp.max(gl, axis=-1, keepdims=True)), 0.0)
    gp = ge / jnp.sum(ge, axis=-1, keepdims=True)
    p_g = jnp.max(gp, axis=-1, keepdims=True)
    g_idx = jnp.min(jnp.where(is_group & (gp == p_g), lane, big), axis=-1, keepdims=True)
    e_lane = lane - MOE_GROUPS
    in_group = (e_lane >= 0) & (e_lane < MOE_EXPERTS) & ((e_lane // MOE_PER_GROUP) == g_idx)
    el = jnp.where(in_group, logits, neg)
    ee = jnp.where(in_group, jnp.exp(el - jnp.max(el, axis=-1, keepdims=True)), 0.0)
    ep = ee / jnp.sum(ee, axis=-1, keepdims=True)
    p1 = jnp.max(jnp.where(in_group, ep, -1.0), axis=-1, keepdims=True)
    i1 = jnp.min(jnp.where(in_group & (ep == p1), lane, big), axis=-1, keepdims=True)
    rest = in_group & (lane != i1)
    p2 = jnp.max(jnp.where(rest, ep, -1.0), axis=-1, keepdims=True)
    i2 = jnp.min(jnp.where(rest & (ep == p2), lane, big), axis=-1, keepdims=True)
    tot = p1 + p2
    return i1 - MOE_GROUPS, i2 - MOE_GROUPS, p_g * p1 / tot, p_g * p2 / tot


ROW_SPLIT = 8
ROUTE_E1, ROUTE_E2, ROUTE_R1, ROUTE_R2, ROUTE_W1, ROUTE_W2 = range(6)


def _split_rows(ref, value, rows):
    for j in range(ROW_SPLIT):
        ref[pl.ds(j, rows, stride=ROW_SPLIT), :] = value[:, 128 * j:128 * (j + 1)]


def _merge_rows(ref, rows):
    return jnp.concatenate([ref[pl.ds(j, rows, stride=ROW_SPLIT), :] for j in range(ROW_SPLIT)], axis=1)


def _out_kernel(x_ref, s5_ref, ret_ref, m2_ref, hg_ref, wglu_ref, bglu_ref, wo_ref, g2_ref,
                wrh_ref, wrl_ref, br_ref, stri_ref, x1_ref, h2_ref, route_ref, cnt_ref, s5_tmp, carry_ref,
                *, tile):
    @pl.when(pl.program_id(0) == 0)
    def _():
        carry_ref[...] = jnp.zeros_like(carry_ref)

    for s in range(S5_CHUNK):
        for j in range(GROUP_WIDTH // 128):
            lanes = slice(GROUP_WIDTH * s + 128 * j, GROUP_WIDTH * s + 128 * (j + 1))
            s5_tmp[j, pl.ds(s, tile // S5_CHUNK, stride=S5_CHUNK), :] = s5_ref[:, lanes]
    y = jnp.concatenate([s5_tmp[j] for j in range(GROUP_WIDTH // 128)], axis=1)
    y = y * (0.5 * (1.0 + jnp.tanh(math.sqrt(2.0 / math.pi) * (y + 0.044715 * (y * y * y)))))
    y = y * _sigmoid(_dot(y.astype(BF16), wglu_ref[...]) + bglu_ref[...])
    acc = x_ref[...] + _dot(y.astype(BF16), wo_ref[0:256, :])
    acc = acc + _dot(ret_ref[...], wo_ref[256:512, :])
    acc = acc + _dot(m2_ref[...], wo_ref[512:768, :])
    acc = acc + _dot(hg_ref[...], wo_ref[768:1024, :])
    x1_ref[...] = acc
    h2 = _rms(acc, g2_ref[...])
    _split_rows(h2_ref, h2, tile)
    hi, lo = _split2(h2)
    logits = _dot(hi, wrh_ref[...]) + _dot(hi, wrl_ref[...]) + _dot(lo, wrh_ref[...]) + br_ref[...]
    e1, e2, w1, w2 = _route(logits)
    lane = lax.broadcasted_iota(jnp.int32, (tile, ROUTE_LANES), 1)
    picked = jnp.where((lane == e1) | (lane == e2), 1.0, 0.0)
    rank = carry_ref[...] + _dot(stri_ref[...], picked.astype(BF16))
    r1 = jnp.sum(jnp.where(lane == e1, rank, 0.0), axis=-1, keepdims=True)
    r2 = jnp.sum(jnp.where(lane == e2, rank, 0.0), axis=-1, keepdims=True)
    carry_ref[...] += jnp.sum(picked, axis=0, keepdims=True)
    cnt_ref[...] = carry_ref[...]
    rec = jnp.zeros((tile, ROUTE_LANES), F32)
    for col, val in ((ROUTE_E1, e1.astype(F32)), (ROUTE_E2, e2.astype(F32)), (ROUTE_R1, r1), (ROUTE_R2, r2),
                     (ROUTE_W1, w1), (ROUTE_W2, w2)):
        rec = jnp.where(lane == col, val, rec)
    route_ref[...] = rec


def _out_proj(x, y_s5, y_ret, y_m2, y_hg, w_glu, b_glu, w_out, g2, w_route, b_route):
    n, d = x.shape
    tile = min(SEQ_TILE, n)
    wr_hi, wr_lo = _split2(w_route)
    i = np.arange(tile)
    strict_lower = jnp.asarray(i[:, None] > i[None, :], dtype=BF16)
    return pl.pallas_call(
        functools.partial(_out_kernel, tile=tile),
        grid=(n // tile,),
        in_specs=[_rows(tile, d), _rows(tile // S5_CHUNK, S5_CHUNK * GROUP_WIDTH)] + [_rows(tile, GROUP_WIDTH)] * 3
        + [_full((256, 256)), _full((1, 256)), _full((d, d)), _full((1, d)),
           _full((d, ROUTE_LANES)), _full((d, ROUTE_LANES)), _full((1, ROUTE_LANES)), _full((tile, tile))],
        out_specs=[_rows(tile, d), _rows(ROW_SPLIT * tile, 128), _rows(tile, ROUTE_LANES), _full((1, ROUTE_LANES))],
        out_shape=[jax.ShapeDtypeStruct((n, d), F32), jax.ShapeDtypeStruct((ROW_SPLIT * n, 128), F32),
                   jax.ShapeDtypeStruct((n, ROUTE_LANES), F32), jax.ShapeDtypeStruct((1, ROUTE_LANES), F32)],
        scratch_shapes=[pltpu.VMEM((GROUP_WIDTH // 128, tile, 128), F32), pltpu.VMEM((1, ROUTE_LANES), F32)],
        compiler_params=_cparams("arbitrary"),
        name="out_proj_router",
    )(x, y_s5, y_ret, y_m2, y_hg, w_glu, b_glu, w_out, g2, wr_hi, wr_lo, b_route, strict_lower)


def _sc_mesh():
    return plsc.VectorSubcoreMesh(core_axis_name="core", subcore_axis_name="subcore")


def _sc_scatter2(src, idx_a, idx_b, n_out):
    n = src.shape[0]

    @functools.partial(pl.kernel, out_type=jax.ShapeDtypeStruct((n_out, 128), src.dtype), mesh=_sc_mesh(),
                       scratch_types=[])
    def scatter_kernel(x_hbm, ia_hbm, ib_hbm, o_hbm):
        def body(x_vmem, ia_vmem, ib_vmem):
            pltpu.sync_copy(x_vmem, o_hbm.at[ia_vmem.at[0]])
            pltpu.sync_copy(x_vmem, o_hbm.at[ib_vmem.at[0]])

        pltpu.emit_pipeline(
            body, grid=(n // SC_WINDOW,),
            in_specs=[pl.BlockSpec((SC_WINDOW, 128), index_map=lambda i: (i, 0)),
                      pl.BlockSpec((1, SC_WINDOW), index_map=lambda i: (0, i)),
                      pl.BlockSpec((1, SC_WINDOW), index_map=lambda i: (0, i))],
            out_specs=[],
            core_axis_name=("core", "subcore"), dimension_semantics=(pltpu.PARALLEL,),
        )(x_hbm, ia_hbm, ib_hbm)

    return scatter_kernel(src, idx_a.reshape(1, n), idx_b.reshape(1, n))


def _sc_gather2(table, idx_a, idx_b):
    n = idx_a.shape[0]
    sds = jax.ShapeDtypeStruct((n, 128), table.dtype)

    @functools.partial(pl.kernel, out_type=(sds, sds), mesh=_sc_mesh(), scratch_types=[])
    def gather_kernel(t_hbm, ia_hbm, ib_hbm, oa_hbm, ob_hbm):
        def body(ia_vmem, ib_vmem, oa_vmem, ob_vmem):
            pltpu.sync_copy(t_hbm.at[ia_vmem.at[0]], oa_vmem)
            pltpu.sync_copy(t_hbm.at[ib_vmem.at[0]], ob_vmem)

        pltpu.emit_pipeline(
            body, grid=(n // SC_WINDOW,),
            in_specs=[pl.BlockSpec((1, SC_WINDOW), index_map=lambda i: (0, i)),
                      pl.BlockSpec((1, SC_WINDOW), index_map=lambda i: (0, i))],
            out_specs=[pl.BlockSpec((SC_WINDOW, 128), index_map=lambda i: (i, 0)),
                       pl.BlockSpec((SC_WINDOW, 128), index_map=lambda i: (i, 0))],
            core_axis_name=("core", "subcore"), dimension_semantics=(pltpu.PARALLEL,),
        )(ia_hbm, ib_hbm, oa_hbm, ob_hbm)

    return gather_kernel(table, idx_a.reshape(1, n), idx_b.reshape(1, n))


def _dispatch_plan(route, counts, n_tiles):
    cnt = counts[0, :MOE_EXPERTS].astype(jnp.int32)
    padded = ((cnt + EXPERT_TILE - 1) // EXPERT_TILE) * EXPERT_TILE
    ends = jnp.cumsum(padded)
    off = ends - padded
    e1 = route[:, ROUTE_E1].astype(jnp.int32)
    e2 = route[:, ROUTE_E2].astype(jnp.int32)
    pos1 = off[e1] + route[:, ROUTE_R1].astype(jnp.int32)
    pos2 = off[e2] + route[:, ROUTE_R2].astype(jnp.int32)
    sub = jnp.arange(ROW_SPLIT, dtype=jnp.int32)[None, :]
    idx1 = (pos1[:, None] * ROW_SPLIT + sub).reshape(-1)
    idx2 = (pos2[:, None] * ROW_SPLIT + sub).reshape(-1)
    start = jnp.arange(n_tiles, dtype=jnp.int32) * EXPERT_TILE
    tile_expert = jnp.minimum(jnp.searchsorted(ends, start, side='right'), MOE_EXPERTS - 1).astype(jnp.int32)
    tile_rows = jnp.clip(off[tile_expert] + cnt[tile_expert] - start, 0, EXPERT_TILE).astype(jnp.int32)
    prev = jnp.concatenate([jnp.full((1,), -1, jnp.int32), tile_expert[:-1]])
    tile_first = ((tile_expert != prev) & (tile_rows > 0)).astype(jnp.int32)
    return idx1, idx2, tile_expert, tile_rows, tile_first


def _experts_kernel(te_ref, rows_ref, first_ref, xs_ref, wg_ref, wu_ref, wd_ref, y_ref, wgb, wub, wdb):
    i = pl.program_id(0)

    @pl.when(first_ref[i] == 1)
    def _():
        wgb[...] = wg_ref[0, 0].astype(BF16)
        wub[...] = wu_ref[0, 0].astype(BF16)
        wdb[...] = wd_ref[0, 0].astype(BF16)

    @pl.when(rows_ref[i] > 0)
    def _():
        x = _merge_rows(xs_ref, EXPERT_TILE)
        row = lax.broadcasted_iota(jnp.int32, x.shape, 0)
        x = jnp.where(row < rows_ref[i], x, 0.0).astype(BF16)
        act = _silu(_dot(x, wgb[...])) * _dot(x, wub[...])
        _split_rows(y_ref, _dot(act.astype(BF16), wdb[...]), EXPERT_TILE)


def _experts(xs, tile_expert, tile_rows, tile_first, w_gate, w_up, w_down, layer):
    n_tiles = tile_expert.shape[0]
    _, _, d, ff = w_gate.shape
    rows_blk = pl.BlockSpec((ROW_SPLIT * EXPERT_TILE, 128), lambda i, te, rows, first: (i, 0))
    return pl.pallas_call(
        _experts_kernel,
        grid_spec=pltpu.PrefetchScalarGridSpec(
            num_scalar_prefetch=3,
            grid=(n_tiles,),
            in_specs=[rows_blk,
                      pl.BlockSpec((1, 1, d, ff), lambda i, te, rows, first: (layer, te[i], 0, 0)),
                      pl.BlockSpec((1, 1, d, ff), lambda i, te, rows, first: (layer, te[i], 0, 0)),
                      pl.BlockSpec((1, 1, ff, d), lambda i, te, rows, first: (layer, te[i], 0, 0))],
            out_specs=rows_blk,
            scratch_shapes=[pltpu.VMEM((d, ff), BF16), pltpu.VMEM((d, ff), BF16), pltpu.VMEM((ff, d), BF16)],
        ),
        out_shape=jax.ShapeDtypeStruct(xs.shape, F32),
        compiler_params=_cparams("arbitrary"),
        name="moe_experts",
    )(tile_expert, tile_rows, tile_first, xs, w_gate, w_up, w_down)


def _combine_kernel(x1_ref, g1_ref, g2_ref, route_ref, gf_ref, o_ref, *, tile, final_norm):
    route = route_ref[...]
    out = (x1_ref[...] + route[:, ROUTE_W1:ROUTE_W1 + 1] * _merge_rows(g1_ref, tile)
           + route[:, ROUTE_W2:ROUTE_W2 + 1] * _merge_rows(g2_ref, tile))
    o_ref[...] = _rms(out, gf_ref[...]) if final_norm else out


def _combine(x1, g1, g2, route, g_final, final_norm):
    n, d = x1.shape
    tile = min(SEQ_TILE, n)
    return pl.pallas_call(
        functools.partial(_combine_kernel, tile=tile, final_norm=final_norm),
        grid=(n // tile,),
        in_specs=[_rows(tile, d), _rows(ROW_SPLIT * tile, 128), _rows(ROW_SPLIT * tile, 128),
                  _rows(tile, ROUTE_LANES), _full((1, d))],
        out_specs=_rows(tile, d),
        out_shape=jax.ShapeDtypeStruct((n, d), F32),
        compiler_params=_cparams("parallel"),
        name="moe_combine",
    )(x1, g1, g2, route, g_final)


def _moe(h2_rows, route, counts, x1, w_gate, w_up, w_down, layer, g_final, final_norm):
    n = x1.shape[0]
    n_tiles = (MOE_TOPK * n) // EXPERT_TILE + MOE_EXPERTS
    idx1, idx2, tile_expert, tile_rows, tile_first = _dispatch_plan(route, counts, n_tiles)
    xs = _sc_scatter2(h2_rows, idx1, idx2, ROW_SPLIT * n_tiles * EXPERT_TILE)
    ys = _experts(xs, tile_expert, tile_rows, tile_first, w_gate, w_up, w_down, layer)
    g1, g2 = _sc_gather2(ys, idx1, idx2)
    return _combine(x1, g1, g2, route, g_final, final_norm)


def kernel(x, positions, norm1_g, w_in, w_out, s5_a_re, s5_a_im, s5_log_dt, s5_b_re, s5_b_im, s5_c_re, s5_c_im, s5_d, s5_w_glu, s5_b_glu, m2_conv_w, m2_conv_b, m2_dt_bias, m2_a_log, m2_d, m2_norm_g, hg_lb_logits, hg_norm_g, norm2_g, moe_w_group, moe_b_group, moe_w_expert, moe_b_expert, moe_w_gate, moe_w_up, moe_w_down, final_norm_g):
    bsz, seqlen, d = x.shape
    assert bsz == 1 and seqlen % SEQ_TILE == 0 and (MOE_TOPK * seqlen) % EXPERT_TILE == 0
    depth = w_in.shape[0]

    lb_probs = jax.nn.softmax(hg_lb_logits.astype(F32), axis=0)
    lower_bounds = jnp.cumsum(lb_probs, axis=0) - lb_probs[0]
    cos_t, sin_t = _rope_tables(positions.reshape(seqlen, 1))

    xc = x.reshape(seqlen, d)
    for l in range(depth):
        u_b, u_f, p_ret, p_m2, p_hg = _in_proj(xc, norm1_g[l].reshape(1, d), _w_prep(w_in, l))
        y_s5 = _s5(u_b, u_f, s5_a_re[l], s5_a_im[l], s5_log_dt[l], s5_b_re[l], s5_b_im[l],
                   s5_c_re[l], s5_c_im[l], s5_d[l])
        y_ret = _retention(p_ret, cos_t, sin_t)
        y_m2 = _mamba2(p_m2, m2_conv_w[l], m2_conv_b[l], m2_dt_bias[l], m2_a_log[l], m2_d[l], m2_norm_g[l])
        y_hg = _hgrn2(p_hg, lower_bounds[l], hg_norm_g[l])
        w_route = jnp.zeros((d, ROUTE_LANES), F32)
        w_route = w_route.at[:, :MOE_GROUPS].set(moe_w_group[l])
        w_route = w_route.at[:, MOE_GROUPS:MOE_GROUPS + MOE_EXPERTS].set(moe_w_expert[l])
        b_route = jnp.zeros((1, ROUTE_LANES), F32)
        b_route = b_route.at[0, :MOE_GROUPS].set(moe_b_group[l])
        b_route = b_route.at[0, MOE_GROUPS:MOE_GROUPS + MOE_EXPERTS].set(moe_b_expert[l])
        x1, h2_rows, route, counts = _out_proj(xc, y_s5, y_ret, y_m2, y_hg, s5_w_glu[l].astype(BF16),
                                               s5_b_glu[l].reshape(1, -1), w_out[l].astype(BF16),
                                               norm2_g[l].reshape(1, d), w_route, b_route)
        xc = _moe(h2_rows, route, counts, x1, moe_w_gate, moe_w_up, moe_w_down, l,
                  final_norm_g.reshape(1, d), final_norm=(l == depth - 1))
    return xc.reshape(bsz, seqlen, d)
```

```python
import functools
import math

import numpy as np
import jax
import jax.numpy as jnp
from jax import lax
from jax.experimental import pallas as pl
from jax.experimental.pallas import tpu as pltpu
from jax.experimental.pallas import tpu_sc as plsc

F32 = jnp.float32
BF16 = jnp.bfloat16
NORM_EPS = 1e-6

GROUP_WIDTH = 256
HEAD_DIM = 64
N_HEADS = GROUP_WIDTH // HEAD_DIM
S5_GROUPS = 16
S5_CH = 16
S5_STATE = 64
S5_DT_CLAMP = -1e-4
M2_STATE = 128
M2_CONV = 4
M2_CONV_DIM = 768
ROPE_BASE = 10000.0
MOE_GROUPS = 4
MOE_PER_GROUP = 4
MOE_EXPERTS = 16
ROUTE_LANES = 128

SEQ_TILE = 512
CHUNK = 128
S5_CHUNK = 16
MOE_TOPK = 2
EXPERT_TILE = 512
SC_WINDOW = 128
VMEM_LIMIT = 56 * 1024 * 1024


def _cparams(*sem):
    return pltpu.CompilerParams(dimension_semantics=sem, vmem_limit_bytes=VMEM_LIMIT)


def _dot(a, b):
    return jnp.dot(a, b, preferred_element_type=F32)


def _dot_nt(a, b):
    return lax.dot_general(a, b, (((1,), (1,)), ((), ())), preferred_element_type=F32)


def _dot_tn(a, b):
    return lax.dot_general(a, b, (((0,), (0,)), ((), ())), preferred_element_type=F32)


def _split2(x):
    hi = x.astype(BF16)
    return hi, (x - hi.astype(F32)).astype(BF16)


def _split3(x):
    hi = x.astype(BF16)
    r = x - hi.astype(F32)
    mid = r.astype(BF16)
    return hi, mid, (r - mid.astype(F32)).astype(BF16)


def _dot_exact_lhs(m, x):
    hi, mid, lo = _split3(x)
    return _dot(m, hi) + _dot(m, mid) + _dot(m, lo)


def _dot_exact_rhs(x, m):
    hi, lo = _split2(x)
    return _dot(hi, m) + _dot(lo, m)


def _sigmoid(x):
    return 1.0 / (1.0 + jnp.exp(-x))


def _silu(x):
    return x * _sigmoid(x)


def _rms(x, g):
    return x * lax.rsqrt(jnp.mean(x * x, axis=-1, keepdims=True) + NORM_EPS) * g


def _full(shape):
    return pl.BlockSpec(shape, lambda *_: (0,) * len(shape))


def _rows(tile, width):
    return pl.BlockSpec((tile, width), lambda i: (i, 0))


IN_SEGMENTS = (256, 1024, 1280, 1024)


DT_COL = 9 * GROUP_WIDTH
W_PREP_ROWS = 128


def _w_prep_kernel(w_ref, o_ref):
    o_ref[:, 0:DT_COL] = w_ref[0, :, 0:DT_COL].astype(BF16)
    head = lax.broadcasted_iota(jnp.int32, (W_PREP_ROWS, GROUP_WIDTH), 1) // HEAD_DIM
    rep = jnp.zeros((W_PREP_ROWS, GROUP_WIDTH), F32)
    for h in range(N_HEADS):
        rep = jnp.where(head == h, w_ref[0, :, DT_COL + h:DT_COL + h + 1], rep)
    o_ref[:, DT_COL:DT_COL + GROUP_WIDTH] = rep.astype(BF16)
    o_ref[:, DT_COL + GROUP_WIDTH:] = w_ref[0, :, DT_COL + N_HEADS:].astype(BF16)


def _w_prep(w_in, layer):
    _, d, n_in = w_in.shape
    return pl.pallas_call(
        _w_prep_kernel,
        grid=(d // W_PREP_ROWS,),
        in_specs=[pl.BlockSpec((1, W_PREP_ROWS, n_in), lambda i: (layer, i, 0))],
        out_specs=pl.BlockSpec((W_PREP_ROWS, sum(IN_SEGMENTS)), lambda i: (i, 0)),
        out_shape=jax.ShapeDtypeStruct((d, sum(IN_SEGMENTS)), BF16),
        compiler_params=_cparams("parallel"),
        name="w_in_prep",
    )(w_in)


def _in_proj_kernel(x_ref, g_ref, w_ref, ub_ref, uf_ref, ret_ref, m2_ref, hg_ref, u_tmp, *, tile):
    hb = _rms(x_ref[...], g_ref[...]).astype(BF16)
    c0, c1, c2, c3 = np.cumsum(IN_SEGMENTS)
    ret_ref[...] = _dot(hb, w_ref[:, c0:c1])
    m2_ref[...] = _dot(hb, w_ref[:, c1:c2])
    hg_ref[...] = _dot(hb, w_ref[:, c2:c3])
    u = _dot(hb, w_ref[:, 0:c0])
    for j in range(GROUP_WIDTH // 128):
        u_tmp[j] = u[:, 128 * j:128 * (j + 1)]
    for s in range(S5_CHUNK):
        for j in range(GROUP_WIDTH // 128):
            v = u_tmp[j, pl.ds(s, tile // S5_CHUNK, stride=S5_CHUNK), :]
            lanes = slice(GROUP_WIDTH * s + 128 * j, GROUP_WIDTH * s + 128 * (j + 1))
            uf_ref[:, lanes] = v
            ub_ref[:, lanes] = v.astype(BF16)


def _in_proj(x, g, w):
    n, d = x.shape
    tile = min(SEQ_TILE, n)
    fold = S5_CHUNK * GROUP_WIDTH
    return pl.pallas_call(
        functools.partial(_in_proj_kernel, tile=tile),
        grid=(n // tile,),
        in_specs=[_rows(tile, d), _full((1, d)), _full(w.shape)],
        out_specs=[_rows(tile // S5_CHUNK, fold)] * 2 + [_rows(tile, s) for s in IN_SEGMENTS[1:]],
        out_shape=[jax.ShapeDtypeStruct((n // S5_CHUNK, fold), BF16),
                   jax.ShapeDtypeStruct((n // S5_CHUNK, fold), F32)]
        + [jax.ShapeDtypeStruct((n, s), F32) for s in IN_SEGMENTS[1:]],
        scratch_shapes=[pltpu.VMEM((GROUP_WIDTH // 128, tile, 128), F32)],
        compiler_params=_cparams("parallel"),
        name="in_proj",
    )(x, g, w)


def _rope_kernel(pos_ref, invf_ref, cos_ref, sin_ref):
    ang = pos_ref[...].astype(F32) * invf_ref[...]
    cos_ref[...] = jnp.cos(ang)
    sin_ref[...] = jnp.sin(ang)


def _rope_tables(positions):
    n = positions.shape[0]
    tile = min(SEQ_TILE, n)
    half = HEAD_DIM // 2
    inv_freq = ROPE_BASE ** (-jnp.arange(half, dtype=F32) / half)
    invf = jnp.tile(inv_freq, 128 // half).reshape(1, 128)
    return pl.pallas_call(
        _rope_kernel,
        grid=(n // tile,),
        in_specs=[_rows(tile, 1), _full((1, 128))],
        out_specs=[_rows(tile, 128), _rows(tile, 128)],
        out_shape=[jax.ShapeDtypeStruct((n, 128), F32)] * 2,
        compiler_params=_cparams("parallel"),
        name="rope_tables",
    )(positions, invf)


def _head_mean_matrix():
    h = np.arange(GROUP_WIDTH) // HEAD_DIM
    return jnp.asarray((h[:, None] == h[None, :]) / HEAD_DIM, dtype=BF16)


def _head_block_mask():
    h = np.arange(GROUP_WIDTH) // HEAD_DIM
    return jnp.asarray(h[:, None] == h[None, :], dtype=F32)


def _ret_constants(chunk):
    lg = np.log1p(-np.exp2(-5.0 - np.arange(N_HEADS, dtype=np.float64)))
    idx = np.arange(chunk, dtype=np.float64)
    rel = idx[:, None] - idx[None, :]
    decay = np.where(rel >= 0, np.exp(np.maximum(rel, 0.0)[None] * lg[:, None, None]), 0.0)
    lane_lg = np.repeat(lg, HEAD_DIM)
    xi = np.exp((idx + 1.0)[:, None] * lane_lg[None, :])
    zeta = np.exp((chunk - 1.0 - idx)[:, None] * lane_lg[None, :])
    h = np.arange(GROUP_WIDTH) // HEAD_DIM
    gc = np.where(h[:, None] == h[None, :], np.exp(chunk * lane_lg)[:, None], 0.0)
    f = lambda a: jnp.asarray(a, dtype=F32)
    return f(decay), f(xi), f(zeta), f(gc)


def _ret_kernel(p_ref, cos_ref, sin_ref, dec_ref, xi_ref, zeta_ref, gc_ref, bm_ref, gm_ref,
                o_ref, r_ref, *, chunk, n_chunks):
    @pl.when(pl.program_id(0) == 0)
    def _():
        r_ref[...] = jnp.zeros_like(r_ref)

    lane = lax.broadcasted_iota(jnp.int32, (chunk, GROUP_WIDTH), 1)
    first_half = (lane % HEAD_DIM) < (HEAD_DIM // 2)
    head = lane // HEAD_DIM
    gm = gm_ref[...]

    def rope(t, cos2, sin2):
        rot = jnp.where(first_half, -pltpu.roll(t, GROUP_WIDTH - HEAD_DIM // 2, 1),
                        pltpu.roll(t, HEAD_DIM // 2, 1))
        return t * cos2 + rot * sin2

    def body(c, carry):
        rows = pl.ds(pl.multiple_of(c * chunk, chunk), chunk)
        cs = cos_ref[rows, :]
        sn = sin_ref[rows, :]
        cos2 = jnp.concatenate([cs, cs], axis=1)
        sin2 = jnp.concatenate([sn, sn], axis=1)
        q = rope(p_ref[rows, 0:256], cos2, sin2)
        k = rope(p_ref[rows, 256:512], cos2, sin2) * (HEAD_DIM ** -0.5)
        v = p_ref[rows, 512:768]
        g = p_ref[rows, 768:1024]
        kb = k.astype(BF16)
        inner = jnp.zeros((chunk, GROUP_WIDTH), F32)
        for h in range(N_HEADS):
            qm = jnp.where(head == h, q, 0.0).astype(BF16)
            vm = jnp.where(head == h, v, 0.0).astype(BF16)
            scores = _dot_nt(qm, kb) * dec_ref[h]
            inner = inner + _dot(scores.astype(BF16), vm)
        r_prev = r_ref[...]
        cross = _dot((q * xi_ref[...]).astype(BF16), r_prev.astype(BF16))
        r_ref[...] = gc_ref[...] * r_prev + bm_ref[...] * _dot_tn(kb, (zeta_ref[...] * v).astype(BF16))
        o = inner + cross
        cen = o - _dot_exact_rhs(o, gm)
        var = _dot_exact_rhs(cen * cen, gm)
        o_ref[rows, :] = (cen * lax.rsqrt(var + NORM_EPS) * _silu(g)).astype(BF16)
        return carry

    lax.fori_loop(0, n_chunks, body, 0)


def _retention(proj, cos_t, sin_t):
    n = proj.shape[0]
    tile = min(SEQ_TILE, n)
    chunk = min(CHUNK, tile)
    decay, xi, zeta, gc = _ret_constants(chunk)
    kern = functools.partial(_ret_kernel, chunk=chunk, n_chunks=tile // chunk)
    return pl.pallas_call(
        kern,
        grid=(n // tile,),
        in_specs=[_rows(tile, 1024), _rows(tile, 128), _rows(tile, 128),
                  _full(decay.shape), _full(xi.shape), _full(zeta.shape), _full(gc.shape),
                  _full((256, 256)), _full((256, 256))],
        out_specs=_rows(tile, GROUP_WIDTH),
        out_shape=jax.ShapeDtypeStruct((n, GROUP_WIDTH), BF16),
        scratch_shapes=[pltpu.VMEM((GROUP_WIDTH, GROUP_WIDTH), F32)],
        compiler_params=_cparams("arbitrary"),
        name="retention",
    )(proj, cos_t, sin_t, decay, xi, zeta, gc, _head_block_mask(), _head_mean_matrix())


def _tri_matrix(chunk):
    i = np.arange(chunk)
    return jnp.asarray(i[:, None] >= i[None, :], dtype=BF16)


def _m2_kernel(p_ref, cw_ref, cb_ref, dtb_ref, alog_ref, d_ref, ng_ref, tri_ref,
               o_ref, tail_ref, ext_ref, act_ref, st_ref, *, tile, chunk, n_chunks):
    @pl.when(pl.program_id(0) == 0)
    def _():
        tail_ref[...] = jnp.zeros_like(tail_ref)
        st_ref[...] = jnp.zeros_like(st_ref)

    ext_ref[0:8, :] = tail_ref[...]
    ext_ref[8:tile + 8, :] = p_ref[:, 256:1024]
    tail_ref[...] = p_ref[tile - 8:tile, 256:1024]
    conv = cb_ref[...]
    for j in range(M2_CONV):
        lo = 8 - (M2_CONV - 1) + j
        conv = conv + cw_ref[j:j + 1, :] * ext_ref[lo:lo + tile, :]
    act_ref[...] = _silu(conv)
    a_lane = -jnp.exp(alog_ref[...])
    tri = tri_ref[...]
    ti = lax.broadcasted_iota(jnp.int32, (chunk, chunk), 0)
    si = lax.broadcasted_iota(jnp.int32, (chunk, chunk), 1)
    causal = ti >= si
    lane = lax.broadcasted_iota(jnp.int32, (chunk, 128), 1)

    def body(c, carry):
        start = pl.multiple_of(c * chunk, chunk)
        rows = pl.ds(start, chunk)
        xbc = act_ref[rows, :]
        xs = xbc[:, 0:256]
        z = p_ref[rows, 0:256]
        x_dt = p_ref[rows, 1024:1280] + dtb_ref[...]
        dt = jnp.maximum(x_dt, 0.0) + jnp.log1p(jnp.exp(-jnp.abs(x_dt)))
        acum = _dot_exact_lhs(tri, dt * a_lane)
        acum_t = acum.T
        a_last = acum[chunk - 1:chunk, :]
        e_acum = jnp.exp(acum)
        decs = jnp.exp(a_last - acum)
        d_chunk = jnp.exp(a_last)
        xc = xs * dt
        ys = []
        for g in range(2):
            sl = slice(128 * g, 128 * (g + 1))
            bmg = xbc[:, 256 + 128 * g:256 + 128 * (g + 1)].astype(BF16)
            cmg = xbc[:, 512 + 128 * g:512 + 128 * (g + 1)].astype(BF16)
            cb = _dot_nt(cmg, bmg)
            xcg = xc[:, sl]
            yd = jnp.zeros((chunk, 128), F32)
            for hh in range(2):
                col0 = 128 * g + HEAD_DIM * hh
                diff = acum[:, col0:col0 + 1] - acum_t[col0:col0 + 1, :]
                lm = jnp.where(causal, jnp.exp(jnp.where(causal, diff, 0.0)), 0.0)
                xm = jnp.where((lane // HEAD_DIM) == hh, xcg, 0.0).astype(BF16)
                yd = yd + _dot((cb * lm).astype(BF16), xm)
            st = st_ref[:, sl]
            y_off = _dot(cmg, st.astype(BF16)) * e_acum[:, sl]
            st_ref[:, sl] = d_chunk[:, sl] * st + _dot_tn(bmg, (xcg * decs[:, sl]).astype(BF16))
            ys.append(yd + y_off + d_ref[:, sl] * xs[:, sl])
        y = jnp.concatenate(ys, axis=1) * _silu(z)
        o_ref[rows, :] = _rms(y, ng_ref[...]).astype(BF16)
        return carry

    lax.fori_loop(0, n_chunks, body, 0)


def _mamba2(proj, conv_w, conv_b, dt_bias, a_log, d_skip, norm_g):
    n = proj.shape[0]
    tile = min(SEQ_TILE, n)
    chunk = min(CHUNK, tile)
    lanes = lambda v: jnp.repeat(v, HEAD_DIM).reshape(1, GROUP_WIDTH)
    kern = functools.partial(_m2_kernel, tile=tile, chunk=chunk, n_chunks=tile // chunk)
    return pl.pallas_call(
        kern,
        grid=(n // tile,),
        in_specs=[_rows(tile, 1280), _full((M2_CONV, M2_CONV_DIM)), _full((1, M2_CONV_DIM)),
                  _full((1, 256)), _full((1, 256)), _full((1, 256)), _full((1, 256)),
                  _full((chunk, chunk))],
        out_specs=_rows(tile, GROUP_WIDTH),
        out_shape=jax.ShapeDtypeStruct((n, GROUP_WIDTH), BF16),
        scratch_shapes=[pltpu.VMEM((8, M2_CONV_DIM), F32),
                        pltpu.VMEM((tile + 8, M2_CONV_DIM), F32),
                        pltpu.VMEM((tile, M2_CONV_DIM), F32),
                        pltpu.VMEM((M2_STATE, GROUP_WIDTH), F32)],
        compiler_params=_cparams("arbitrary"),
        name="mamba2_ssd",
    )(proj, conv_w, conv_b.reshape(1, -1), lanes(dt_bias), lanes(a_log), lanes(d_skip),
      norm_g.reshape(1, -1), _tri_matrix(chunk))


def _hg_exponent_matrix(chunk):
    levels = int(math.log2(chunk))
    t = np.arange(chunk)[:, None]
    r = np.arange(chunk)[None, :]
    blocks = []
    for lvl in range(levels):
        b = 1 << lvl
        blk = t // b
        odd = (blk % 2) == 1
        q_rows = odd & (r >= blk * b) & (r <= t)
        k_rows = (~odd) & (r > t) & (r <= (blk + 1) * b - 1)
        blocks.append(q_rows | k_rows)
    blocks.append(r <= t)
    blocks.append(r > t)
    return jnp.asarray(np.concatenate(blocks, axis=0), dtype=BF16)


def _hg_kernel(p_ref, lb_ref, ng_ref, gexp_ref, bm_ref, gm_ref, o_ref, st_ref, *, chunk, n_chunks):
    @pl.when(pl.program_id(0) == 0)
    def _():
        st_ref[...] = jnp.zeros_like(st_ref)

    levels = int(math.log2(chunk))
    lane = lax.broadcasted_iota(jnp.int32, (chunk, GROUP_WIDTH), 1)
    row = lax.broadcasted_iota(jnp.int32, (chunk, GROUP_WIDTH), 0)
    head = lane // HEAD_DIM
    ti = lax.broadcasted_iota(jnp.int32, (chunk, chunk), 0)
    si = lax.broadcasted_iota(jnp.int32, (chunk, chunk), 1)
    lb = lb_ref[...]
    gm = gm_ref[...]

    def body(c, carry):
        rows = pl.ds(pl.multiple_of(c * chunk, chunk), chunk)
        q = _silu(p_ref[rows, 0:256])
        forget = lb + (1.0 - lb) * _sigmoid(p_ref[rows, 256:512])
        k = 1.0 - forget
        v = p_ref[rows, 512:768]
        g = p_ref[rows, 768:1024]
        expo = _dot_exact_lhs(gexp_ref[...], jnp.log(forget))

        def head_scores(qe, ke):
            return [_dot_nt(jnp.where(head == h, qe, 0.0).astype(BF16), ke) for h in range(N_HEADS)]

        attn = [jnp.where(ti == si, a, 0.0) for a in head_scores(q, k.astype(BF16))]
        for lvl in range(levels):
            e = jnp.exp(expo[lvl * chunk:(lvl + 1) * chunk, :])
            odd = ((row >> lvl) & 1) == 1
            w = e * jnp.where(odd, q, k)
            qe = jnp.where(odd, w, 0.0)
            ke = jnp.where(odd, 0.0, w).astype(BF16)
            same = (ti >> (lvl + 1)) == (si >> (lvl + 1))
            attn = [a + jnp.where(same, s, 0.0) for a, s in zip(attn, head_scores(qe, ke))]
        intra = jnp.zeros((chunk, GROUP_WIDTH), F32)
        for h in range(N_HEADS):
            intra = intra + _dot(attn[h].astype(BF16), jnp.where(head == h, v, 0.0).astype(BF16))

        bcum = expo[levels * chunk:(levels + 1) * chunk, :]
        suffix = expo[(levels + 1) * chunk:(levels + 2) * chunk, :]
        b_last = bcum[chunk - 1:chunk, :]
        st = st_ref[...]
        cross = _dot_nt((q * jnp.exp(bcum)).astype(BF16), st.astype(BF16))
        st_ref[...] = jnp.exp(b_last) * st + bm_ref[...] * _dot_tn(
            v.astype(BF16), (k * jnp.exp(suffix)).astype(BF16))
        o = intra + cross
        o = o * lax.rsqrt(_dot_exact_rhs(o * o, gm) + NORM_EPS) * ng_ref[...]
        o_ref[rows, :] = (o * _silu(g)).astype(BF16)
        return carry

    lax.fori_loop(0, n_chunks, body, 0)


def _hgrn2(proj, lower_bound, norm_g):
    n = proj.shape[0]
    tile = min(SEQ_TILE, n)
    chunk = min(CHUNK, tile)
    gexp = _hg_exponent_matrix(chunk)
    kern = functools.partial(_hg_kernel, chunk=chunk, n_chunks=tile // chunk)
    return pl.pallas_call(
        kern,
        grid=(n // tile,),
        in_specs=[_rows(tile, 1024), _full((1, 256)), _full((1, 256)), _full(gexp.shape),
                  _full((256, 256)), _full((256, 256))],
        out_specs=_rows(tile, GROUP_WIDTH),
        out_shape=jax.ShapeDtypeStruct((n, GROUP_WIDTH), BF16),
        scratch_shapes=[pltpu.VMEM((GROUP_WIDTH, GROUP_WIDTH), F32)],
        compiler_params=_cparams("arbitrary"),
        name="hgrn2",
    )(proj, lower_bound.reshape(1, -1), norm_g.reshape(1, -1), gexp, _head_block_mask(),
      _head_mean_matrix())


S5_LANES = S5_GROUPS * 2 * S5_STATE


def _s5_rows(a_re, a_im, log_dt):
    are = jnp.minimum(a_re, S5_DT_CLAMP)
    dt = jnp.exp(log_dt)
    lam_re = are * dt
    lam_im = a_im * dt
    mag = jnp.exp(lam_re)
    ab_re = mag * jnp.cos(lam_im)
    ab_im = mag * jnp.sin(lam_im)
    den = are * are + a_im * a_im
    k_re = ((ab_re - 1.0) * are + ab_im * a_im) / den
    k_im = (ab_im * are - (ab_re - 1.0) * a_im) / den
    return lam_re, lam_im, k_re, k_im


def _s5_power(lam_re, lam_im, e):
    m = jnp.exp(e * lam_re)
    return m * jnp.cos(e * lam_im), m * jnp.sin(e * lam_im)


def _s5_state_kernel(u_ref, are_ref, aim_ref, ldt_ref, b1_ref, b2_ref, ca_ref, k_ref, sp_ref,
                     bb1_ref, bb2_ref, inc_ref, *, n_chunks):
    s = pl.program_id(0)
    lam_re, lam_im, k_re, k_im = _s5_rows(are_ref[...], aim_ref[...], ldt_ref[...])

    @pl.when(s == 0)
    def _():
        bb1_ref[...] = k_re * b1_ref[...] + k_im * b2_ref[...]
        bb2_ref[...] = k_re * b2_ref[...] - k_im * b1_ref[...]

    p_re, p_im = _s5_power(lam_re, lam_im, (S5_CHUNK - 1 - s).astype(F32))
    w = p_re * bb1_ref[...] + p_im * bb2_ref[...]
    w_hi, w_lo = _split2(w)
    c_hi, c_lo = _split2(ca_ref[...])
    k_ref[0] = (_dot_nt(w_hi, c_hi) + _dot_nt(w_hi, c_lo) + _dot_nt(w_lo, c_hi)).astype(BF16)
    contrib = _dot(u_ref[...], w_hi)

    @pl.when(s == 0)
    def _():
        inc_ref[...] = contrib

    @pl.when(s > 0)
    def _():
        inc_ref[...] += contrib

    @pl.when(s == S5_CHUNK - 1)
    def _():
        n_steps = max(1, int(math.ceil(math.log2(n_chunks))))
        step = lax.broadcasted_iota(jnp.int32, (16, S5_LANES), 0)
        e = (jnp.left_shift(1, step) * S5_CHUNK).astype(F32)
        a_re_all, a_im_all = _s5_power(lam_re, lam_im, e)
        row = lax.broadcasted_iota(jnp.int32, (n_chunks, 2 * S5_STATE), 0)
        lane = lax.broadcasted_iota(jnp.int32, (1, 2 * S5_STATE), 1)
        sign = jnp.where(lane < S5_STATE, -1.0, 1.0)
        for g in range(S5_GROUPS):
            sl = slice(2 * S5_STATE * g, 2 * S5_STATE * (g + 1))
            x = inc_ref[:, sl]
            for k in range(n_steps):
                sh = 1 << k
                if sh >= n_chunks:
                    break
                prev = jnp.where(row >= sh, pltpu.roll(x, sh, 0), 0.0)
                x = (x + a_re_all[k:k + 1, sl] * prev
                     + (a_im_all[k:k + 1, sl] * sign) * pltpu.roll(prev, S5_STATE, 1))
            sp_ref[:, sl] = jnp.where(row >= 1, pltpu.roll(x, 1, 0), 0.0).astype(BF16)


def _s5_out_kernel(ub_ref, uf_ref, k_ref, sp_ref, are_ref, aim_ref, ldt_ref, ca_ref, cb_ref, d_ref,
                   y_ref, acc_ref):
    t = pl.program_id(0)
    lam_re, lam_im, _, _ = _s5_rows(are_ref[...], aim_ref[...], ldt_ref[...])
    p_re, p_im = _s5_power(lam_re, lam_im, (t + 1).astype(F32))
    w_out = (p_re * ca_ref[...] + p_im * cb_ref[...]).astype(BF16)
    acc_ref[...] = _dot_nt(sp_ref[...], w_out) + d_ref[...] * uf_ref[...]
    for s in range(S5_CHUNK):
        @pl.when(s <= t)
        def _():
            acc_ref[...] += _dot(ub_ref[:, GROUP_WIDTH * s:GROUP_WIDTH * (s + 1)], k_ref[t - s])
    y_ref[...] = acc_ref[...]


def _s5_embed(re, im):
    pack = jnp.concatenate([re, im], axis=2)
    eye = jnp.eye(S5_GROUPS, dtype=F32)
    return (eye[:, None, :, None] * pack[:, :, None, :]).reshape(S5_GROUPS * S5_CH, S5_LANES)


def _s5(u_b, u_f, a_re, a_im, log_dt, b_re, b_im, c_re, c_im, d_skip):
    n_chunks, fold = u_b.shape
    row = lambda v: jnp.concatenate([v, v], axis=1).reshape(1, S5_LANES)
    are, aim = row(a_re), row(a_im)
    ldt = jnp.repeat(log_dt, 2 * S5_STATE).reshape(1, S5_LANES)
    bt_re, bt_im = b_re.transpose(0, 2, 1), b_im.transpose(0, 2, 1)
    b1, b2 = _s5_embed(bt_re, bt_im), _s5_embed(-bt_im, bt_re)
    ca, cb = _s5_embed(c_re, -c_im), _s5_embed(-c_im, -c_re)
    col = lambda: pl.BlockSpec((n_chunks, GROUP_WIDTH), lambda s: (0, s))
    taps, s_prev = pl.pallas_call(
        functools.partial(_s5_state_kernel, n_chunks=n_chunks),
        grid=(S5_CHUNK,),
        in_specs=[col()] + [_full((1, S5_LANES))] * 3 + [_full((GROUP_WIDTH, S5_LANES))] * 3,
        out_specs=[pl.BlockSpec((1, GROUP_WIDTH, GROUP_WIDTH), lambda s: (S5_CHUNK - 1 - s, 0, 0)),
                   _full((n_chunks, S5_LANES))],
        out_shape=[jax.ShapeDtypeStruct((S5_CHUNK, GROUP_WIDTH, GROUP_WIDTH), BF16),
                   jax.ShapeDtypeStruct((n_chunks, S5_LANES), BF16)],
        scratch_shapes=[pltpu.VMEM((GROUP_WIDTH, S5_LANES), F32), pltpu.VMEM((GROUP_WIDTH, S5_LANES), F32),
                        pltpu.VMEM((n_chunks, S5_LANES), F32)],
        compiler_params=_cparams("arbitrary"),
        name="s5_state",
    )(u_b, are, aim, ldt, b1, b2, ca)
    return pl.pallas_call(
        _s5_out_kernel,
        grid=(S5_CHUNK,),
        in_specs=[_full((n_chunks, fold)), col(), _full(taps.shape), _full(s_prev.shape)]
        + [_full((1, S5_LANES))] * 3 + [_full((GROUP_WIDTH, S5_LANES))] * 2 + [_full((1, GROUP_WIDTH))],
        out_specs=col(),
        out_shape=jax.ShapeDtypeStruct((n_chunks, fold), F32),
        scratch_shapes=[pltpu.VMEM((n_chunks, GROUP_WIDTH), F32)],
        compiler_params=_cparams("arbitrary"),
        name="s5_out",
    )(u_b, u_f, taps, s_prev, are, aim, ldt, ca, cb, d_skip.reshape(1, GROUP_WIDTH))


def _route(logits):
    lane = lax.broadcasted_iota(jnp.int32, logits.shape, 1)
    big = jnp.int32(1 << 20)
    neg = jnp.float32(-jnp.inf)
    is_group = lane < MOE_GROUPS
    gl = jnp.where(is_group, logits, neg)
    ge = jnp.where(is_group, jnp.exp(gl - jnp.max(gl, axis=-1, keepdims=True)), 0.0)
    gp = ge / jnp.sum(ge, axis=-1, keepdims=True)
    p_g = jnp.max(gp, axis=-1, keepdims=True)
    g_idx = jnp.min(jnp.where(is_group & (gp == p_g), lane, big), axis=-1, keepdims=True)
    e_lane = lane - MOE_GROUPS
    in_group = (e_lane >= 0) & (e_lane < MOE_EXPERTS) & ((e_lane // MOE_PER_GROUP) == g_idx)
    el = jnp.where(in_group, logits, neg)
    ee = jnp.where(in_group, jnp.exp(el - jnp.max(el, axis=-1, keepdims=True)), 0.0)
    ep = ee / jnp.sum(ee, axis=-1, keepdims=True)
    p1 = jnp.max(jnp.where(in_group, ep, -1.0), axis=-1, keepdims=True)
    i1 = jnp.min(jnp.where(in_group & (ep == p1), lane, big), axis=-1, keepdims=True)
    rest = in_group & (lane != i1)
    p2 = jnp.max(jnp.where(rest, ep, -1.0), axis=-1, keepdims=True)
    i2 = jnp.min(jnp.where(rest & (ep == p2), lane, big), axis=-1, keepdims=True)
    tot = p1 + p2
    return i1 - MOE_GROUPS, i2 - MOE_GROUPS, p_g * p1 / tot, p_g * p2 / tot


ROW_SPLIT = 8
ROUTE_E1, ROUTE_E2, ROUTE_R1, ROUTE_R2, ROUTE_W1, ROUTE_W2 = range(6)


def _split_rows(ref, value, rows):
    for j in range(ROW_SPLIT):
        ref[pl.ds(j, rows, stride=ROW_SPLIT), :] = value[:, 128 * j:128 * (j + 1)]


def _merge_rows(ref, rows):
    return jnp.concatenate([ref[pl.ds(j, rows, stride=ROW_SPLIT), :] for j in range(ROW_SPLIT)], axis=1)


def _out_kernel(x_ref, s5_ref, ret_ref, m2_ref, hg_ref, wglu_ref, bglu_ref, wo_ref, g2_ref,
                wrh_ref, br_ref, stri_ref, x1_ref, h2_ref, route_ref, cnt_ref, s5_tmp, carry_ref,
                *, tile):
    @pl.when(pl.program_id(0) == 0)
    def _():
        carry_ref[...] = jnp.zeros_like(carry_ref)

    for s in range(S5_CHUNK):
        for j in range(GROUP_WIDTH // 128):
            lanes = slice(GROUP_WIDTH * s + 128 * j, GROUP_WIDTH * s + 128 * (j + 1))
            s5_tmp[j, pl.ds(s, tile // S5_CHUNK, stride=S5_CHUNK), :] = s5_ref[:, lanes]
    y = jnp.concatenate([s5_tmp[j] for j in range(GROUP_WIDTH // 128)], axis=1)
    y = y * (0.5 * (1.0 + jnp.tanh(math.sqrt(2.0 / math.pi) * (y + 0.044715 * (y * y * y)))))
    y = y * _sigmoid(_dot(y.astype(BF16), wglu_ref[...]) + bglu_ref[...])
    acc = x_ref[...] + _dot(y.astype(BF16), wo_ref[0:256, :])
    acc = acc + _dot(ret_ref[...], wo_ref[256:512, :])
    acc = acc + _dot(m2_ref[...], wo_ref[512:768, :])
    acc = acc + _dot(hg_ref[...], wo_ref[768:1024, :])
    x1_ref[...] = acc
    h2 = _rms(acc, g2_ref[...])
    _split_rows(h2_ref, h2, tile)
    hi, lo = _split2(h2)
    hw = _dot(hi, wrh_ref[...])
    logits = (hw[:, :ROUTE_LANES] + hw[:, ROUTE_LANES:] + _dot(lo, wrh_ref[:, :ROUTE_LANES])) + br_ref[...]
    e1, e2, w1, w2 = _route(logits)
    lane = lax.broadcasted_iota(jnp.int32, (tile, ROUTE_LANES), 1)
    picked = jnp.where((lane == e1) | (lane == e2), 1.0, 0.0)
    rank = carry_ref[...] + _dot(stri_ref[...], picked.astype(BF16))
    r1 = jnp.sum(jnp.where(lane == e1, rank, 0.0), axis=-1, keepdims=True)
    r2 = jnp.sum(jnp.where(lane == e2, rank, 0.0), axis=-1, keepdims=True)
    carry_ref[...] += jnp.sum(picked, axis=0, keepdims=True)
    cnt_ref[...] = carry_ref[...]
    rec = jnp.zeros((tile, ROUTE_LANES), F32)
    for col, val in ((ROUTE_E1, e1.astype(F32)), (ROUTE_E2, e2.astype(F32)), (ROUTE_R1, r1), (ROUTE_R2, r2),
                     (ROUTE_W1, w1), (ROUTE_W2, w2)):
        rec = jnp.where(lane == col, val, rec)
    route_ref[...] = rec


def _out_proj(x, y_s5, y_ret, y_m2, y_hg, w_glu, b_glu, w_out, g2, w_route, b_route):
    n, d = x.shape
    tile = min(SEQ_TILE, n)
    wr_packed = jnp.concatenate(_split2(w_route), axis=1)
    i = np.arange(tile)
    strict_lower = jnp.asarray(i[:, None] > i[None, :], dtype=BF16)
    return pl.pallas_call(
        functools.partial(_out_kernel, tile=tile),
        grid=(n // tile,),
        in_specs=[_rows(tile, d), _rows(tile // S5_CHUNK, S5_CHUNK * GROUP_WIDTH)] + [_rows(tile, GROUP_WIDTH)] * 3
        + [_full((256, 256)), _full((1, 256)), _full((d, d)), _full((1, d)),
           _full((d, 2 * ROUTE_LANES)), _full((1, ROUTE_LANES)), _full((tile, tile))],
        out_specs=[_rows(tile, d), _rows(ROW_SPLIT * tile, 128), _rows(tile, ROUTE_LANES), _full((1, ROUTE_LANES))],
        out_shape=[jax.ShapeDtypeStruct((n, d), F32), jax.ShapeDtypeStruct((ROW_SPLIT * n, 128), F32),
                   jax.ShapeDtypeStruct((n, ROUTE_LANES), F32), jax.ShapeDtypeStruct((1, ROUTE_LANES), F32)],
        scratch_shapes=[pltpu.VMEM((GROUP_WIDTH // 128, tile, 128), F32), pltpu.VMEM((1, ROUTE_LANES), F32)],
        compiler_params=_cparams("arbitrary"),
        name="out_proj_router",
    )(x, y_s5, y_ret, y_m2, y_hg, w_glu, b_glu, w_out, g2, wr_packed, b_route, strict_lower)


def _sc_mesh():
    return plsc.VectorSubcoreMesh(core_axis_name="core", subcore_axis_name="subcore")


def _sc_scatter2(src, idx_a, idx_b, n_out):
    n = src.shape[0]

    @functools.partial(pl.kernel, out_type=jax.ShapeDtypeStruct((n_out, 128), src.dtype), mesh=_sc_mesh(),
                       scratch_types=[])
    def scatter_kernel(x_hbm, ia_hbm, ib_hbm, o_hbm):
        def body(x_vmem, ia_vmem, ib_vmem):
            pltpu.sync_copy(x_vmem, o_hbm.at[ia_vmem.at[0]])
            pltpu.sync_copy(x_vmem, o_hbm.at[ib_vmem.at[0]])

        pltpu.emit_pipeline(
            body, grid=(n // SC_WINDOW,),
            in_specs=[pl.BlockSpec((SC_WINDOW, 128), index_map=lambda i: (i, 0)),
                      pl.BlockSpec((1, SC_WINDOW), index_map=lambda i: (0, i)),
                      pl.BlockSpec((1, SC_WINDOW), index_map=lambda i: (0, i))],
            out_specs=[],
            core_axis_name=("core", "subcore"), dimension_semantics=(pltpu.PARALLEL,),
        )(x_hbm, ia_hbm, ib_hbm)

    return scatter_kernel(src, idx_a.reshape(1, n), idx_b.reshape(1, n))


def _sc_gather2(table, idx_a, idx_b):
    n = idx_a.size
    sds = jax.ShapeDtypeStruct((n, 128), table.dtype)

    @functools.partial(pl.kernel, out_type=(sds, sds), mesh=_sc_mesh(), scratch_types=[])
    def gather_kernel(t_hbm, ia_hbm, ib_hbm, oa_hbm, ob_hbm):
        def body(ia_vmem, ib_vmem, oa_vmem, ob_vmem):
            pltpu.sync_copy(t_hbm.at[ia_vmem.at[0]], oa_vmem)
            pltpu.sync_copy(t_hbm.at[ib_vmem.at[0]], ob_vmem)

        pltpu.emit_pipeline(
            body, grid=(n // SC_WINDOW,),
            in_specs=[pl.BlockSpec((1, SC_WINDOW), index_map=lambda i: (0, i)),
                      pl.BlockSpec((1, SC_WINDOW), index_map=lambda i: (0, i))],
            out_specs=[pl.BlockSpec((SC_WINDOW, 128), index_map=lambda i: (i, 0)),
                       pl.BlockSpec((SC_WINDOW, 128), index_map=lambda i: (i, 0))],
            core_axis_name=("core", "subcore"), dimension_semantics=(pltpu.PARALLEL,),
        )(ia_hbm, ib_hbm, oa_hbm, ob_hbm)

    return gather_kernel(table, idx_a.reshape(1, n), idx_b.reshape(1, n))


INDEX_TILE = 256


def _index_kernel(route_ref, off_ref, expand_ref, i1_ref, i2_ref):
    rec = route_ref[...]
    lane = lax.broadcasted_iota(jnp.int32, rec.shape, 1)
    off = off_ref[...]
    pos = []
    for e_col, r_col in ((ROUTE_E1, ROUTE_R1), (ROUTE_E2, ROUTE_R2)):
        e = rec[:, e_col:e_col + 1].astype(jnp.int32)
        pos.append(jnp.sum(jnp.where(lane == e, off, 0.0), axis=-1, keepdims=True) + rec[:, r_col:r_col + 1])
    cols = jnp.where(lane == 0, pos[0], jnp.where(lane == 1, pos[1], 0.0))
    p = cols.T[0:8, :]
    hi = jnp.floor(p * (1.0 / 256.0))
    lo = p - 256.0 * hi
    rep = 256.0 * _dot(hi.astype(BF16), expand_ref[...]) + _dot(lo.astype(BF16), expand_ref[...])
    sub = lax.broadcasted_iota(jnp.int32, rep.shape, 1) % ROW_SPLIT
    idx = rep.astype(jnp.int32) * ROW_SPLIT + sub
    i1_ref[...] = idx[0:1, :]
    i2_ref[...] = idx[1:2, :]


def _dispatch_plan(route, counts, n_tiles):
    n = route.shape[0]
    cnt = counts[0, :MOE_EXPERTS].astype(jnp.int32)
    padded = ((cnt + EXPERT_TILE - 1) // EXPERT_TILE) * EXPERT_TILE
    ends = jnp.cumsum(padded)
    off = ends - padded
    start = jnp.arange(n_tiles, dtype=jnp.int32) * EXPERT_TILE
    tile_expert = jnp.minimum(jnp.sum((start[:, None] >= ends[None, :]).astype(jnp.int32), axis=1), MOE_EXPERTS - 1)
    onehot = (tile_expert[:, None] == jnp.arange(MOE_EXPERTS, dtype=jnp.int32)[None, :]).astype(jnp.int32)
    seg_end = jnp.sum(onehot * (off + cnt)[None, :], axis=1)
    tile_rows = jnp.clip(seg_end - start, 0, EXPERT_TILE).astype(jnp.int32)
    prev = jnp.concatenate([jnp.full((1,), -1, jnp.int32), tile_expert[:-1]])
    tile_first = ((tile_expert != prev) & (tile_rows > 0)).astype(jnp.int32)

    tile = min(INDEX_TILE, n)
    off_row = jnp.zeros((1, ROUTE_LANES), F32).at[0, :MOE_EXPERTS].set(off.astype(F32))
    t = np.arange(tile)
    expand = jnp.asarray(t[:, None] == (np.arange(ROW_SPLIT * tile) // ROW_SPLIT)[None, :], dtype=BF16)
    blk = pl.BlockSpec((1, ROW_SPLIT * tile), lambda i: (0, i))
    idx1, idx2 = pl.pallas_call(
        _index_kernel,
        grid=(n // tile,),
        in_specs=[_rows(tile, ROUTE_LANES), _full((1, ROUTE_LANES)), _full(expand.shape)],
        out_specs=[blk, blk],
        out_shape=[jax.ShapeDtypeStruct((1, ROW_SPLIT * n), jnp.int32)] * 2,
        compiler_params=_cparams("parallel"),
        name="dispatch_index",
    )(route, off_row, expand)
    return idx1, idx2, tile_expert, tile_rows, tile_first


def _experts_kernel(te_ref, rows_ref, first_ref, xs_ref, wg_ref, wu_ref, wd_ref, y_ref, wgb, wub, wdb):
    i = pl.program_id(0)

    @pl.when(first_ref[i] == 1)
    def _():
        wgb[...] = wg_ref[0, 0].astype(BF16)
        wub[...] = wu_ref[0, 0].astype(BF16)
        wdb[...] = wd_ref[0, 0].astype(BF16)

    @pl.when(rows_ref[i] > 0)
    def _():
        x = _merge_rows(xs_ref, EXPERT_TILE)
        row = lax.broadcasted_iota(jnp.int32, x.shape, 0)
        x = jnp.where(row < rows_ref[i], x, 0.0).astype(BF16)
        act = _silu(_dot(x, wgb[...])) * _dot(x, wub[...])
        _split_rows(y_ref, _dot(act.astype(BF16), wdb[...]), EXPERT_TILE)


def _experts(xs, tile_expert, tile_rows, tile_first, w_gate, w_up, w_down, layer):
    n_tiles = tile_expert.shape[0]
    _, _, d, ff = w_gate.shape
    rows_blk = pl.BlockSpec((ROW_SPLIT * EXPERT_TILE, 128), lambda i, te, rows, first: (i, 0))
    return pl.pallas_call(
        _experts_kernel,
        grid_spec=pltpu.PrefetchScalarGridSpec(
            num_scalar_prefetch=3,
            grid=(n_tiles,),
            in_specs=[rows_blk,
                      pl.BlockSpec((1, 1, d, ff), lambda i, te, rows, first: (layer, te[i], 0, 0)),
                      pl.BlockSpec((1, 1, d, ff), lambda i, te, rows, first: (layer, te[i], 0, 0)),
                      pl.BlockSpec((1, 1, ff, d), lambda i, te, rows, first: (layer, te[i], 0, 0))],
            out_specs=rows_blk,
            scratch_shapes=[pltpu.VMEM((d, ff), BF16), pltpu.VMEM((d, ff), BF16), pltpu.VMEM((ff, d), BF16)],
        ),
        out_shape=jax.ShapeDtypeStruct(xs.shape, F32),
        compiler_params=_cparams("arbitrary"),
        name="moe_experts",
    )(tile_expert, tile_rows, tile_first, xs, w_gate, w_up, w_down)


def _combine_kernel(x1_ref, g1_ref, g2_ref, route_ref, gf_ref, o_ref, *, tile, final_norm):
    route = route_ref[...]
    out = (x1_ref[...] + route[:, ROUTE_W1:ROUTE_W1 + 1] * _merge_rows(g1_ref, tile)
           + route[:, ROUTE_W2:ROUTE_W2 + 1] * _merge_rows(g2_ref, tile))
    o_ref[...] = _rms(out, gf_ref[...]) if final_norm else out


def _combine(x1, g1, g2, route, g_final, final_norm):
    n, d = x1.shape
    tile = min(SEQ_TILE, n)
    return pl.pallas_call(
        functools.partial(_combine_kernel, tile=tile, final_norm=final_norm),
        grid=(n // tile,),
        in_specs=[_rows(tile, d), _rows(ROW_SPLIT * tile, 128), _rows(ROW_SPLIT * tile, 128),
                  _rows(tile, ROUTE_LANES), _full((1, d))],
        out_specs=_rows(tile, d),
        out_shape=jax.ShapeDtypeStruct((n, d), F32),
        compiler_params=_cparams("parallel"),
        name="moe_combine",
    )(x1, g1, g2, route, g_final)


def _moe(h2_rows, route, counts, x1, w_gate, w_up, w_down, layer, g_final, final_norm):
    n = x1.shape[0]
    n_tiles = (MOE_TOPK * n) // EXPERT_TILE + MOE_EXPERTS
    idx1, idx2, tile_expert, tile_rows, tile_first = _dispatch_plan(route, counts, n_tiles)
    xs = _sc_scatter2(h2_rows, idx1, idx2, ROW_SPLIT * n_tiles * EXPERT_TILE)
    ys = _experts(xs, tile_expert, tile_rows, tile_first, w_gate, w_up, w_down, layer)
    g1, g2 = _sc_gather2(ys, idx1, idx2)
    return _combine(x1, g1, g2, route, g_final, final_norm)


def kernel(x, positions, norm1_g, w_in, w_out, s5_a_re, s5_a_im, s5_log_dt, s5_b_re, s5_b_im, s5_c_re, s5_c_im, s5_d, s5_w_glu, s5_b_glu, m2_conv_w, m2_conv_b, m2_dt_bias, m2_a_log, m2_d, m2_norm_g, hg_lb_logits, hg_norm_g, norm2_g, moe_w_group, moe_b_group, moe_w_expert, moe_b_expert, moe_w_gate, moe_w_up, moe_w_down, final_norm_g):
    bsz, seqlen, d = x.shape
    assert bsz == 1 and seqlen % SEQ_TILE == 0 and (MOE_TOPK * seqlen) % EXPERT_TILE == 0
    depth = w_in.shape[0]

    lb_probs = jax.nn.softmax(hg_lb_logits.astype(F32), axis=0)
    lower_bounds = jnp.cumsum(lb_probs, axis=0) - lb_probs[0]
    cos_t, sin_t = _rope_tables(positions.reshape(seqlen, 1))

    xc = x.reshape(seqlen, d)
    for l in range(depth):
        u_b, u_f, p_ret, p_m2, p_hg = _in_proj(xc, norm1_g[l].reshape(1, d), _w_prep(w_in, l))
        y_s5 = _s5(u_b, u_f, s5_a_re[l], s5_a_im[l], s5_log_dt[l], s5_b_re[l], s5_b_im[l],
                   s5_c_re[l], s5_c_im[l], s5_d[l])
        y_ret = _retention(p_ret, cos_t, sin_t)
        y_m2 = _mamba2(p_m2, m2_conv_w[l], m2_conv_b[l], m2_dt_bias[l], m2_a_log[l], m2_d[l], m2_norm_g[l])
        y_hg = _hgrn2(p_hg, lower_bounds[l], hg_norm_g[l])
        w_route = jnp.zeros((d, ROUTE_LANES), F32)
        w_route = w_route.at[:, :MOE_GROUPS].set(moe_w_group[l])
        w_route = w_route.at[:, MOE_GROUPS:MOE_GROUPS + MOE_EXPERTS].set(moe_w_expert[l])
        b_route = jnp.zeros((1, ROUTE_LANES), F32)
        b_route = b_route.at[0, :MOE_GROUPS].set(moe_b_group[l])
        b_route = b_route.at[0, MOE_GROUPS:MOE_GROUPS + MOE_EXPERTS].set(moe_b_expert[l])
        x1, h2_rows, route, counts = _out_proj(xc, y_s5, y_ret, y_m2, y_hg, s5_w_glu[l].astype(BF16),
                                               s5_b_glu[l].reshape(1, -1), w_out[l].astype(BF16),
                                               norm2_g[l].reshape(1, d), w_route, b_route)
        xc = _moe(h2_rows, route, counts, x1, moe_w_gate, moe_w_up, moe_w_down, l,
                  final_norm_g.reshape(1, d), final_norm=(l == depth - 1))
    return xc.reshape(bsz, seqlen, d)
```

```python
import functools
import math

import numpy as np
import jax
import jax.numpy as jnp
from jax import lax
from jax.experimental import pallas as pl
from jax.experimental.pallas import tpu as pltpu
from jax.experimental.pallas import tpu_sc as plsc

F32 = jnp.float32
BF16 = jnp.bfloat16
NORM_EPS = 1e-6

GROUP_WIDTH = 256
HEAD_DIM = 64
N_HEADS = GROUP_WIDTH // HEAD_DIM
S5_GROUPS = 16
S5_CH = 16
S5_STATE = 64
S5_DT_CLAMP = -1e-4
M2_STATE = 128
M2_CONV = 4
M2_CONV_DIM = 768
ROPE_BASE = 10000.0
MOE_GROUPS = 4
MOE_PER_GROUP = 4
MOE_EXPERTS = 16
ROUTE_LANES = 128

SEQ_TILE = 512
CHUNK = 128
CHUNK_UNROLL = 4
S5_CHUNK = 16
MOE_TOPK = 2
EXPERT_TILE = 512
SC_WINDOW = 128
VMEM_LIMIT = 56 * 1024 * 1024


def _cparams(*sem):
    return pltpu.CompilerParams(dimension_semantics=sem, vmem_limit_bytes=VMEM_LIMIT)


def _dot(a, b):
    return jnp.dot(a, b, preferred_element_type=F32)


def _dot_nt(a, b):
    return lax.dot_general(a, b, (((1,), (1,)), ((), ())), preferred_element_type=F32)


def _dot_tn(a, b):
    return lax.dot_general(a, b, (((0,), (0,)), ((), ())), preferred_element_type=F32)


def _split2(x):
    hi = x.astype(BF16)
    return hi, (x - hi.astype(F32)).astype(BF16)


def _split3(x):
    hi = x.astype(BF16)
    r = x - hi.astype(F32)
    mid = r.astype(BF16)
    return hi, mid, (r - mid.astype(F32)).astype(BF16)


def _dot_exact_lhs(m, x):
    hi, mid, lo = _split3(x)
    return _dot(m, hi) + _dot(m, mid) + _dot(m, lo)


def _dot_exact_rhs(x, m):
    hi, lo = _split2(x)
    return _dot(hi, m) + _dot(lo, m)


def _sigmoid(x):
    return 1.0 / (1.0 + jnp.exp(-x))


def _silu(x):
    return x * _sigmoid(x)


def _rms(x, g):
    return x * lax.rsqrt(jnp.mean(x * x, axis=-1, keepdims=True) + NORM_EPS) * g


def _full(shape):
    return pl.BlockSpec(shape, lambda *_: (0,) * len(shape))


def _rows(tile, width):
    return pl.BlockSpec((tile, width), lambda i: (i, 0))


IN_SEGMENTS = (256, 1024, 1280, 1024)


DT_COL = 9 * GROUP_WIDTH
W_PREP_ROWS = 128


def _w_prep_kernel(w_ref, o_ref):
    o_ref[:, 0:DT_COL] = w_ref[0, :, 0:DT_COL].astype(BF16)
    head = lax.broadcasted_iota(jnp.int32, (W_PREP_ROWS, GROUP_WIDTH), 1) // HEAD_DIM
    rep = jnp.zeros((W_PREP_ROWS, GROUP_WIDTH), F32)
    for h in range(N_HEADS):
        rep = jnp.where(head == h, w_ref[0, :, DT_COL + h:DT_COL + h + 1], rep)
    o_ref[:, DT_COL:DT_COL + GROUP_WIDTH] = rep.astype(BF16)
    o_ref[:, DT_COL + GROUP_WIDTH:] = w_ref[0, :, DT_COL + N_HEADS:].astype(BF16)


def _w_prep(w_in, layer):
    _, d, n_in = w_in.shape
    return pl.pallas_call(
        _w_prep_kernel,
        grid=(d // W_PREP_ROWS,),
        in_specs=[pl.BlockSpec((1, W_PREP_ROWS, n_in), lambda i: (layer, i, 0))],
        out_specs=pl.BlockSpec((W_PREP_ROWS, sum(IN_SEGMENTS)), lambda i: (i, 0)),
        out_shape=jax.ShapeDtypeStruct((d, sum(IN_SEGMENTS)), BF16),
        compiler_params=_cparams("parallel"),
        name="w_in_prep",
    )(w_in)


def _in_proj_kernel(x_ref, g_ref, w_ref, ub_ref, uf_ref, ret_ref, m2_ref, hg_ref, u_tmp, *, tile):
    hb = _rms(x_ref[...], g_ref[...]).astype(BF16)
    c0, c1, c2, c3 = np.cumsum(IN_SEGMENTS)
    ret_ref[...] = _dot(hb, w_ref[:, c0:c1])
    m2_ref[...] = _dot(hb, w_ref[:, c1:c2])
    hg_ref[...] = _dot(hb, w_ref[:, c2:c3])
    u = _dot(hb, w_ref[:, 0:c0])
    for j in range(GROUP_WIDTH // 128):
        u_tmp[j] = u[:, 128 * j:128 * (j + 1)]
    for s in range(S5_CHUNK):
        for j in range(GROUP_WIDTH // 128):
            v = u_tmp[j, pl.ds(s, tile // S5_CHUNK, stride=S5_CHUNK), :]
            lanes = slice(GROUP_WIDTH * s + 128 * j, GROUP_WIDTH * s + 128 * (j + 1))
            uf_ref[:, lanes] = v
            ub_ref[:, lanes] = v.astype(BF16)


def _in_proj(x, g, w):
    n, d = x.shape
    tile = min(SEQ_TILE, n)
    fold = S5_CHUNK * GROUP_WIDTH
    return pl.pallas_call(
        functools.partial(_in_proj_kernel, tile=tile),
        grid=(n // tile,),
        in_specs=[_rows(tile, d), _full((1, d)), _full(w.shape)],
        out_specs=[_rows(tile // S5_CHUNK, fold)] * 2 + [_rows(tile, s) for s in IN_SEGMENTS[1:]],
        out_shape=[jax.ShapeDtypeStruct((n // S5_CHUNK, fold), BF16),
                   jax.ShapeDtypeStruct((n // S5_CHUNK, fold), F32)]
        + [jax.ShapeDtypeStruct((n, s), F32) for s in IN_SEGMENTS[1:]],
        scratch_shapes=[pltpu.VMEM((GROUP_WIDTH // 128, tile, 128), F32)],
        compiler_params=_cparams("parallel"),
        name="in_proj",
    )(x, g, w)


def _rope_kernel(pos_ref, invf_ref, cos_ref, sin_ref):
    ang = pos_ref[...].astype(F32) * invf_ref[...]
    cos_ref[...] = jnp.cos(ang)
    sin_ref[...] = jnp.sin(ang)


def _rope_tables(positions):
    n = positions.shape[0]
    tile = min(SEQ_TILE, n)
    half = HEAD_DIM // 2
    inv_freq = ROPE_BASE ** (-jnp.arange(half, dtype=F32) / half)
    invf = jnp.tile(inv_freq, 128 // half).reshape(1, 128)
    return pl.pallas_call(
        _rope_kernel,
        grid=(n // tile,),
        in_specs=[_rows(tile, 1), _full((1, 128))],
        out_specs=[_rows(tile, 128), _rows(tile, 128)],
        out_shape=[jax.ShapeDtypeStruct((n, 128), F32)] * 2,
        compiler_params=_cparams("parallel"),
        name="rope_tables",
    )(positions, invf)


def _head_mean_matrix():
    h = np.arange(GROUP_WIDTH) // HEAD_DIM
    return jnp.asarray((h[:, None] == h[None, :]) / HEAD_DIM, dtype=BF16)


def _head_block_mask():
    h = np.arange(GROUP_WIDTH) // HEAD_DIM
    return jnp.asarray(h[:, None] == h[None, :], dtype=F32)


HEAD_PAIRS = GROUP_WIDTH // 128


def _lanes(x, j):
    return x[:, 128 * j:128 * (j + 1)]


def _stack_pair(x):
    xb = x.astype(BF16)
    low = lax.broadcasted_iota(jnp.int32, x.shape, 1) < HEAD_DIM
    zero = jnp.zeros_like(xb)
    return jnp.concatenate([jnp.where(low, xb, zero), jnp.where(low, zero, xb)], axis=0)


def _pair_scores(q, kb):
    s = _dot_nt(_stack_pair(q), kb)
    return s[:q.shape[0]], s[q.shape[0]:]


def _pair_apply(a0, a1, v):
    return _dot(jnp.concatenate([a0.astype(BF16), a1.astype(BF16)], axis=1), _stack_pair(v))


def _ret_constants(chunk):
    lg = np.log1p(-np.exp2(-5.0 - np.arange(N_HEADS, dtype=np.float64)))
    idx = np.arange(chunk, dtype=np.float64)
    rel = idx[:, None] - idx[None, :]
    decay = np.where(rel >= 0, np.exp(np.maximum(rel, 0.0)[None] * lg[:, None, None]), 0.0)
    lane_lg = np.repeat(lg, HEAD_DIM)
    xi = np.exp((idx + 1.0)[:, None] * lane_lg[None, :])
    zeta = np.exp((chunk - 1.0 - idx)[:, None] * lane_lg[None, :])
    h = np.arange(GROUP_WIDTH) // HEAD_DIM
    gc = np.where(h[:, None] == h[None, :], np.exp(chunk * lane_lg)[:, None], 0.0)
    f = lambda a: jnp.asarray(a, dtype=F32)
    return f(decay), f(xi), f(zeta), f(gc)


def _ret_kernel(p_ref, cos_ref, sin_ref, dec_ref, xi_ref, zeta_ref, gc_ref, bm_ref, gm_ref,
                o_ref, r_ref, *, chunk, n_chunks):
    @pl.when(pl.program_id(0) == 0)
    def _():
        r_ref[...] = jnp.zeros_like(r_ref)

    lane = lax.broadcasted_iota(jnp.int32, (chunk, GROUP_WIDTH), 1)
    first_half = (lane % HEAD_DIM) < (HEAD_DIM // 2)
    gm = gm_ref[...]

    def rope(t, cos2, sin2):
        rot = jnp.where(first_half, -pltpu.roll(t, GROUP_WIDTH - HEAD_DIM // 2, 1),
                        pltpu.roll(t, HEAD_DIM // 2, 1))
        return t * cos2 + rot * sin2

    def body(c, carry):
        rows = pl.ds(pl.multiple_of(c * chunk, chunk), chunk)
        cs = cos_ref[rows, :]
        sn = sin_ref[rows, :]
        cos2 = jnp.concatenate([cs, cs], axis=1)
        sin2 = jnp.concatenate([sn, sn], axis=1)
        q = rope(p_ref[rows, 0:256], cos2, sin2)
        k = rope(p_ref[rows, 256:512], cos2, sin2) * (HEAD_DIM ** -0.5)
        v = p_ref[rows, 512:768]
        g = p_ref[rows, 768:1024]
        kb = k.astype(BF16)
        inner = []
        for j in range(HEAD_PAIRS):
            s0, s1 = _pair_scores(_lanes(q, j), _lanes(kb, j))
            inner.append(_pair_apply(s0 * dec_ref[2 * j], s1 * dec_ref[2 * j + 1], _lanes(v, j)))
        inner = jnp.concatenate(inner, axis=1)
        r_prev = r_ref[...]
        cross = _dot((q * xi_ref[...]).astype(BF16), r_prev.astype(BF16))
        r_ref[...] = gc_ref[...] * r_prev + bm_ref[...] * _dot_tn(kb, (zeta_ref[...] * v).astype(BF16))
        o = inner + cross
        cen = o - _dot_exact_rhs(o, gm)
        var = _dot_exact_rhs(cen * cen, gm)
        o_ref[rows, :] = (cen * lax.rsqrt(var + NORM_EPS) * _silu(g)).astype(BF16)
        return carry

    lax.fori_loop(0, n_chunks, body, 0, unroll=CHUNK_UNROLL)


def _retention(proj, cos_t, sin_t):
    n = proj.shape[0]
    tile = min(SEQ_TILE, n)
    chunk = min(CHUNK, tile)
    decay, xi, zeta, gc = _ret_constants(chunk)
    kern = functools.partial(_ret_kernel, chunk=chunk, n_chunks=tile // chunk)
    return pl.pallas_call(
        kern,
        grid=(n // tile,),
        in_specs=[_rows(tile, 1024), _rows(tile, 128), _rows(tile, 128),
                  _full(decay.shape), _full(xi.shape), _full(zeta.shape), _full(gc.shape),
                  _full((256, 256)), _full((256, 256))],
        out_specs=_rows(tile, GROUP_WIDTH),
        out_shape=jax.ShapeDtypeStruct((n, GROUP_WIDTH), BF16),
        scratch_shapes=[pltpu.VMEM((GROUP_WIDTH, GROUP_WIDTH), F32)],
        compiler_params=_cparams("arbitrary"),
        name="retention",
    )(proj, cos_t, sin_t, decay, xi, zeta, gc, _head_block_mask(), _head_mean_matrix())


def _tri_matrix(chunk):
    i = np.arange(chunk)
    return jnp.asarray(i[:, None] >= i[None, :], dtype=BF16)


def _m2_kernel(p_ref, cw_ref, cb_ref, dtb_ref, alog_ref, d_ref, ng_ref, tri_ref,
               o_ref, tail_ref, ext_ref, act_ref, st_ref, *, tile, chunk, n_chunks):
    @pl.when(pl.program_id(0) == 0)
    def _():
        tail_ref[...] = jnp.zeros_like(tail_ref)
        st_ref[...] = jnp.zeros_like(st_ref)

    ext_ref[0:8, :] = tail_ref[...]
    ext_ref[8:tile + 8, :] = p_ref[:, 256:1024]
    tail_ref[...] = p_ref[tile - 8:tile, 256:1024]
    conv = cb_ref[...]
    for j in range(M2_CONV):
        lo = 8 - (M2_CONV - 1) + j
        conv = conv + cw_ref[j:j + 1, :] * ext_ref[lo:lo + tile, :]
    act_ref[...] = _silu(conv)
    a_lane = -jnp.exp(alog_ref[...])
    tri = tri_ref[...]
    ti = lax.broadcasted_iota(jnp.int32, (chunk, chunk), 0)
    si = lax.broadcasted_iota(jnp.int32, (chunk, chunk), 1)
    causal = ti >= si
    lane = lax.broadcasted_iota(jnp.int32, (chunk, 128), 1)

    def body(c, carry):
        start = pl.multiple_of(c * chunk, chunk)
        rows = pl.ds(start, chunk)
        xbc = act_ref[rows, :]
        xs = xbc[:, 0:256]
        z = p_ref[rows, 0:256]
        x_dt = p_ref[rows, 1024:1280] + dtb_ref[...]
        dt = jnp.maximum(x_dt, 0.0) + jnp.log1p(jnp.exp(-jnp.abs(x_dt)))
        acum = _dot_exact_lhs(tri, dt * a_lane)
        acum_t = acum.T
        a_last = acum[chunk - 1:chunk, :]
        e_acum = jnp.exp(acum)
        decs = jnp.exp(a_last - acum)
        d_chunk = jnp.exp(a_last)
        xc = xs * dt
        ys = []
        for g in range(2):
            sl = slice(128 * g, 128 * (g + 1))
            bmg = xbc[:, 256 + 128 * g:256 + 128 * (g + 1)].astype(BF16)
            cmg = xbc[:, 512 + 128 * g:512 + 128 * (g + 1)].astype(BF16)
            cb = _dot_nt(cmg, bmg)
            xcg = xc[:, sl]
            yd = jnp.zeros((chunk, 128), F32)
            for hh in range(2):
                col0 = 128 * g + HEAD_DIM * hh
                diff = acum[:, col0:col0 + 1] - acum_t[col0:col0 + 1, :]
                lm = jnp.where(causal, jnp.exp(jnp.where(causal, diff, 0.0)), 0.0)
                xm = jnp.where((lane // HEAD_DIM) == hh, xcg, 0.0).astype(BF16)
                yd = yd + _dot((cb * lm).astype(BF16), xm)
            st = st_ref[:, sl]
            y_off = _dot(cmg, st.astype(BF16)) * e_acum[:, sl]
            st_ref[:, sl] = d_chunk[:, sl] * st + _dot_tn(bmg, (xcg * decs[:, sl]).astype(BF16))
            ys.append(yd + y_off + d_ref[:, sl] * xs[:, sl])
        y = jnp.concatenate(ys, axis=1) * _silu(z)
        o_ref[rows, :] = _rms(y, ng_ref[...]).astype(BF16)
        return carry

    lax.fori_loop(0, n_chunks, body, 0, unroll=CHUNK_UNROLL)


def _mamba2(proj, conv_w, conv_b, dt_bias, a_log, d_skip, norm_g):
    n = proj.shape[0]
    tile = min(SEQ_TILE, n)
    chunk = min(CHUNK, tile)
    lanes = lambda v: jnp.repeat(v, HEAD_DIM).reshape(1, GROUP_WIDTH)
    kern = functools.partial(_m2_kernel, tile=tile, chunk=chunk, n_chunks=tile // chunk)
    return pl.pallas_call(
        kern,
        grid=(n // tile,),
        in_specs=[_rows(tile, 1280), _full((M2_CONV, M2_CONV_DIM)), _full((1, M2_CONV_DIM)),
                  _full((1, 256)), _full((1, 256)), _full((1, 256)), _full((1, 256)),
                  _full((chunk, chunk))],
        out_specs=_rows(tile, GROUP_WIDTH),
        out_shape=jax.ShapeDtypeStruct((n, GROUP_WIDTH), BF16),
        scratch_shapes=[pltpu.VMEM((8, M2_CONV_DIM), F32),
                        pltpu.VMEM((tile + 8, M2_CONV_DIM), F32),
                        pltpu.VMEM((tile, M2_CONV_DIM), F32),
                        pltpu.VMEM((M2_STATE, GROUP_WIDTH), F32)],
        compiler_params=_cparams("arbitrary"),
        name="mamba2_ssd",
    )(proj, conv_w, conv_b.reshape(1, -1), lanes(dt_bias), lanes(a_log), lanes(d_skip),
      norm_g.reshape(1, -1), _tri_matrix(chunk))


def _hg_exponent_matrix(chunk):
    levels = int(math.log2(chunk))
    t = np.arange(chunk)[:, None]
    r = np.arange(chunk)[None, :]
    blocks = []
    for lvl in range(levels):
        b = 1 << lvl
        blk = t // b
        odd = (blk % 2) == 1
        q_rows = odd & (r >= blk * b) & (r <= t)
        k_rows = (~odd) & (r > t) & (r <= (blk + 1) * b - 1)
        blocks.append(q_rows | k_rows)
    blocks.append(r <= t)
    blocks.append(r > t)
    return jnp.asarray(np.concatenate(blocks, axis=0), dtype=BF16)


def _hg_kernel(p_ref, lb_ref, ng_ref, gexp_ref, bm_ref, gm_ref, o_ref, st_ref, *, chunk, n_chunks):
    @pl.when(pl.program_id(0) == 0)
    def _():
        st_ref[...] = jnp.zeros_like(st_ref)

    levels = int(math.log2(chunk))
    row = lax.broadcasted_iota(jnp.int32, (chunk, 128), 0)
    odd_rows = [((row >> lvl) & 1) == 1 for lvl in range(levels)]
    ti = lax.broadcasted_iota(jnp.int32, (chunk, chunk), 0)
    si = lax.broadcasted_iota(jnp.int32, (chunk, chunk), 1)
    pair_level = [((ti >> (lvl + 1)) == (si >> (lvl + 1))) & (((ti >> lvl) & 1) == 1) & (((si >> lvl) & 1) == 0)
                  for lvl in range(levels)]
    lb = lb_ref[...]
    gm = gm_ref[...]

    def body(c, carry):
        rows = pl.ds(pl.multiple_of(c * chunk, chunk), chunk)
        q = _silu(p_ref[rows, 0:256])
        forget = lb + (1.0 - lb) * _sigmoid(p_ref[rows, 256:512])
        k = 1.0 - forget
        v = p_ref[rows, 512:768]
        g = p_ref[rows, 768:1024]
        lf_hi, lf_lo = _split2(jnp.log(forget))
        expo = _dot(gexp_ref[...], lf_hi) + _dot(gexp_ref[...], lf_lo)

        intra = []
        for j in range(HEAD_PAIRS):
            qj, kj = _lanes(q, j), _lanes(k, j)
            a0, a1 = (jnp.where(ti == si, s, 0.0) for s in _pair_scores(qj, kj.astype(BF16)))
            for lvl in range(levels):
                w = jnp.exp(_lanes(expo[lvl * chunk:(lvl + 1) * chunk, :], j)) * jnp.where(odd_rows[lvl], qj, kj)
                s0, s1 = _pair_scores(w, w.astype(BF16))
                a0 = jnp.where(pair_level[lvl], s0, a0)
                a1 = jnp.where(pair_level[lvl], s1, a1)
            intra.append(_pair_apply(a0, a1, _lanes(v, j)))
        intra = jnp.concatenate(intra, axis=1)

        bcum = expo[levels * chunk:(levels + 1) * chunk, :]
        suffix = expo[(levels + 1) * chunk:(levels + 2) * chunk, :]
        b_last = bcum[chunk - 1:chunk, :]
        st = st_ref[...]
        cross = _dot_nt((q * jnp.exp(bcum)).astype(BF16), st.astype(BF16))
        st_ref[...] = jnp.exp(b_last) * st + bm_ref[...] * _dot_tn(
            v.astype(BF16), (k * jnp.exp(suffix)).astype(BF16))
        o = intra + cross
        o = o * lax.rsqrt(_dot_exact_rhs(o * o, gm) + NORM_EPS) * ng_ref[...]
        o_ref[rows, :] = (o * _silu(g)).astype(BF16)
        return carry

    lax.fori_loop(0, n_chunks, body, 0, unroll=CHUNK_UNROLL)


def _hgrn2(proj, lower_bound, norm_g):
    n = proj.shape[0]
    tile = min(SEQ_TILE, n)
    chunk = min(CHUNK, tile)
    gexp = _hg_exponent_matrix(chunk)
    kern = functools.partial(_hg_kernel, chunk=chunk, n_chunks=tile // chunk)
    return pl.pallas_call(
        kern,
        grid=(n // tile,),
        in_specs=[_rows(tile, 1024), _full((1, 256)), _full((1, 256)), _full(gexp.shape),
                  _full((256, 256)), _full((256, 256))],
        out_specs=_rows(tile, GROUP_WIDTH),
        out_shape=jax.ShapeDtypeStruct((n, GROUP_WIDTH), BF16),
        scratch_shapes=[pltpu.VMEM((GROUP_WIDTH, GROUP_WIDTH), F32)],
        compiler_params=_cparams("arbitrary"),
        name="hgrn2",
    )(proj, lower_bound.reshape(1, -1), norm_g.reshape(1, -1), gexp, _head_block_mask(),
      _head_mean_matrix())


S5_LANES = S5_GROUPS * 2 * S5_STATE


def _s5_rows(a_re, a_im, log_dt):
    are = jnp.minimum(a_re, S5_DT_CLAMP)
    dt = jnp.exp(log_dt)
    lam_re = are * dt
    lam_im = a_im * dt
    mag = jnp.exp(lam_re)
    ab_re = mag * jnp.cos(lam_im)
    ab_im = mag * jnp.sin(lam_im)
    den = are * are + a_im * a_im
    k_re = ((ab_re - 1.0) * are + ab_im * a_im) / den
    k_im = (ab_im * are - (ab_re - 1.0) * a_im) / den
    return lam_re, lam_im, k_re, k_im


def _s5_power(lam_re, lam_im, e):
    m = jnp.exp(e * lam_re)
    return m * jnp.cos(e * lam_im), m * jnp.sin(e * lam_im)


def _s5_state_kernel(u_ref, are_ref, aim_ref, ldt_ref, b1_ref, b2_ref, ca_ref, k_ref, sp_ref,
                     bb1_ref, bb2_ref, inc_ref, *, n_chunks):
    s = pl.program_id(0)
    lam_re, lam_im, k_re, k_im = _s5_rows(are_ref[...], aim_ref[...], ldt_ref[...])

    @pl.when(s == 0)
    def _():
        bb1_ref[...] = k_re * b1_ref[...] + k_im * b2_ref[...]
        bb2_ref[...] = k_re * b2_ref[...] - k_im * b1_ref[...]

    p_re, p_im = _s5_power(lam_re, lam_im, (S5_CHUNK - 1 - s).astype(F32))
    w = p_re * bb1_ref[...] + p_im * bb2_ref[...]
    w_hi, w_lo = _split2(w)
    c_hi, c_lo = _split2(ca_ref[...])
    k_ref[0] = (_dot_nt(w_hi, c_hi) + _dot_nt(w_hi, c_lo) + _dot_nt(w_lo, c_hi)).astype(BF16)
    contrib = _dot(u_ref[...], w_hi)

    @pl.when(s == 0)
    def _():
        inc_ref[...] = contrib

    @pl.when(s > 0)
    def _():
        inc_ref[...] += contrib

    @pl.when(s == S5_CHUNK - 1)
    def _():
        n_steps = max(1, int(math.ceil(math.log2(n_chunks))))
        step = lax.broadcasted_iota(jnp.int32, (16, S5_LANES), 0)
        e = (jnp.left_shift(1, step) * S5_CHUNK).astype(F32)
        a_re_all, a_im_all = _s5_power(lam_re, lam_im, e)
        row = lax.broadcasted_iota(jnp.int32, (n_chunks, 2 * S5_STATE), 0)
        lane = lax.broadcasted_iota(jnp.int32, (1, 2 * S5_STATE), 1)
        sign = jnp.where(lane < S5_STATE, -1.0, 1.0)
        for g in range(S5_GROUPS):
            sl = slice(2 * S5_STATE * g, 2 * S5_STATE * (g + 1))
            x = inc_ref[:, sl]
            for k in range(n_steps):
                sh = 1 << k
                if sh >= n_chunks:
                    break
                prev = jnp.where(row >= sh, pltpu.roll(x, sh, 0), 0.0)
                x = (x + a_re_all[k:k + 1, sl] * prev
                     + (a_im_all[k:k + 1, sl] * sign) * pltpu.roll(prev, S5_STATE, 1))
            sp_ref[:, sl] = jnp.where(row >= 1, pltpu.roll(x, 1, 0), 0.0).astype(BF16)


def _s5_out_kernel(ub_ref, uf_ref, k_ref, sp_ref, are_ref, aim_ref, ldt_ref, ca_ref, cb_ref, d_ref,
                   y_ref, acc_ref):
    t = pl.program_id(0)
    lam_re, lam_im, _, _ = _s5_rows(are_ref[...], aim_ref[...], ldt_ref[...])
    p_re, p_im = _s5_power(lam_re, lam_im, (t + 1).astype(F32))
    w_out = (p_re * ca_ref[...] + p_im * cb_ref[...]).astype(BF16)
    acc_ref[...] = _dot_nt(sp_ref[...], w_out) + d_ref[...] * uf_ref[...]
    for s in range(S5_CHUNK):
        @pl.when(s <= t)
        def _():
            acc_ref[...] += _dot(ub_ref[:, GROUP_WIDTH * s:GROUP_WIDTH * (s + 1)], k_ref[t - s])
    y_ref[...] = acc_ref[...]


def _s5_embed(re, im):
    pack = jnp.concatenate([re, im], axis=2)
    eye = jnp.eye(S5_GROUPS, dtype=F32)
    return (eye[:, None, :, None] * pack[:, :, None, :]).reshape(S5_GROUPS * S5_CH, S5_LANES)


def _s5(u_b, u_f, a_re, a_im, log_dt, b_re, b_im, c_re, c_im, d_skip):
    n_chunks, fold = u_b.shape
    row = lambda v: jnp.concatenate([v, v], axis=1).reshape(1, S5_LANES)
    are, aim = row(a_re), row(a_im)
    ldt = jnp.repeat(log_dt, 2 * S5_STATE).reshape(1, S5_LANES)
    bt_re, bt_im = b_re.transpose(0, 2, 1), b_im.transpose(0, 2, 1)
    b1, b2 = _s5_embed(bt_re, bt_im), _s5_embed(-bt_im, bt_re)
    ca, cb = _s5_embed(c_re, -c_im), _s5_embed(-c_im, -c_re)
    col = lambda: pl.BlockSpec((n_chunks, GROUP_WIDTH), lambda s: (0, s))
    taps, s_prev = pl.pallas_call(
        functools.partial(_s5_state_kernel, n_chunks=n_chunks),
        grid=(S5_CHUNK,),
        in_specs=[col()] + [_full((1, S5_LANES))] * 3 + [_full((GROUP_WIDTH, S5_LANES))] * 3,
        out_specs=[pl.BlockSpec((1, GROUP_WIDTH, GROUP_WIDTH), lambda s: (S5_CHUNK - 1 - s, 0, 0)),
                   _full((n_chunks, S5_LANES))],
        out_shape=[jax.ShapeDtypeStruct((S5_CHUNK, GROUP_WIDTH, GROUP_WIDTH), BF16),
                   jax.ShapeDtypeStruct((n_chunks, S5_LANES), BF16)],
        scratch_shapes=[pltpu.VMEM((GROUP_WIDTH, S5_LANES), F32), pltpu.VMEM((GROUP_WIDTH, S5_LANES), F32),
                        pltpu.VMEM((n_chunks, S5_LANES), F32)],
        compiler_params=_cparams("arbitrary"),
        name="s5_state",
    )(u_b, are, aim, ldt, b1, b2, ca)
    return pl.pallas_call(
        _s5_out_kernel,
        grid=(S5_CHUNK,),
        in_specs=[_full((n_chunks, fold)), col(), _full(taps.shape), _full(s_prev.shape)]
        + [_full((1, S5_LANES))] * 3 + [_full((GROUP_WIDTH, S5_LANES))] * 2 + [_full((1, GROUP_WIDTH))],
        out_specs=col(),
        out_shape=jax.ShapeDtypeStruct((n_chunks, fold), F32),
        scratch_shapes=[pltpu.VMEM((n_chunks, GROUP_WIDTH), F32)],
        compiler_params=_cparams("arbitrary"),
        name="s5_out",
    )(u_b, u_f, taps, s_prev, are, aim, ldt, ca, cb, d_skip.reshape(1, GROUP_WIDTH))


def _route(logits):
    lane = lax.broadcasted_iota(jnp.int32, logits.shape, 1)
    big = jnp.int32(1 << 20)
    neg = jnp.float32(-jnp.inf)
    is_group = lane < MOE_GROUPS
    gl = jnp.where(is_group, logits, neg)
    ge = jnp.where(is_group, jnp.exp(gl - jnp.max(gl, axis=-1, keepdims=True)), 0.0)
    gp = ge / jnp.sum(ge, axis=-1, keepdims=True)
    p_g = jnp.max(gp, axis=-1, keepdims=True)
    g_idx = jnp.min(jnp.where(is_group & (gp == p_g), lane, big), axis=-1, keepdims=True)
    e_lane = lane - MOE_GROUPS
    in_group = (e_lane >= 0) & (e_lane < MOE_EXPERTS) & ((e_lane // MOE_PER_GROUP) == g_idx)
    el = jnp.where(in_group, logits, neg)
    ee = jnp.where(in_group, jnp.exp(el - jnp.max(el, axis=-1, keepdims=True)), 0.0)
    ep = ee / jnp.sum(ee, axis=-1, keepdims=True)
    p1 = jnp.max(jnp.where(in_group, ep, -1.0), axis=-1, keepdims=True)
    i1 = jnp.min(jnp.where(in_group & (ep == p1), lane, big), axis=-1, keepdims=True)
    rest = in_group & (lane != i1)
    p2 = jnp.max(jnp.where(rest, ep, -1.0), axis=-1, keepdims=True)
    i2 = jnp.min(jnp.where(rest & (ep == p2), lane, big), axis=-1, keepdims=True)
    tot = p1 + p2
    return i1 - MOE_GROUPS, i2 - MOE_GROUPS, p_g * p1 / tot, p_g * p2 / tot


ROW_SPLIT = 4
ROUTE_E1, ROUTE_E2, ROUTE_R1, ROUTE_R2, ROUTE_W1, ROUTE_W2 = range(6)
HIGH_HALF = 0xFFFF0000


def _split_rows(ref, value, rows):
    half = value.shape[1] // 2
    lo = lax.bitcast_convert_type(value[:, :half].astype(jnp.bfloat16).astype(F32), jnp.uint32)
    hi = lax.bitcast_convert_type(value[:, half:].astype(jnp.bfloat16).astype(F32), jnp.uint32)
    words = lax.bitcast_convert_type((lo >> 16) | (hi & jnp.uint32(HIGH_HALF)), jnp.int32)
    for j in range(ROW_SPLIT):
        ref[pl.ds(j, rows, stride=ROW_SPLIT), :] = words[:, 128 * j:128 * (j + 1)]


def _merge_rows(ref, rows):
    words = jnp.concatenate([ref[pl.ds(j, rows, stride=ROW_SPLIT), :] for j in range(ROW_SPLIT)], axis=1)
    words = lax.bitcast_convert_type(words, jnp.uint32)
    lo = lax.bitcast_convert_type(words << 16, F32)
    hi = lax.bitcast_convert_type(words & jnp.uint32(HIGH_HALF), F32)
    return jnp.concatenate([lo, hi], axis=1)


def _out_kernel(x_ref, s5_ref, ret_ref, m2_ref, hg_ref, wglu_ref, bglu_ref, wo_ref, g2_ref,
                wrh_ref, br_ref, stri_ref, x1_ref, h2_ref, route_ref, cnt_ref, s5_tmp, carry_ref,
                *, tile):
    @pl.when(pl.program_id(0) == 0)
    def _():
        carry_ref[...] = jnp.zeros_like(carry_ref)

    for s in range(S5_CHUNK):
        for j in range(GROUP_WIDTH // 128):
            lanes = slice(GROUP_WIDTH * s + 128 * j, GROUP_WIDTH * s + 128 * (j + 1))
            s5_tmp[j, pl.ds(s, tile // S5_CHUNK, stride=S5_CHUNK), :] = s5_ref[:, lanes]
    y = jnp.concatenate([s5_tmp[j] for j in range(GROUP_WIDTH // 128)], axis=1)
    y = y * (0.5 * (1.0 + jnp.tanh(math.sqrt(2.0 / math.pi) * (y + 0.044715 * (y * y * y)))))
    y = y * _sigmoid(_dot(y.astype(BF16), wglu_ref[...]) + bglu_ref[...])
    acc = x_ref[...] + _dot(y.astype(BF16), wo_ref[0:256, :])
    acc = acc + _dot(ret_ref[...], wo_ref[256:512, :])
    acc = acc + _dot(m2_ref[...], wo_ref[512:768, :])
    acc = acc + _dot(hg_ref[...], wo_ref[768:1024, :])
    x1_ref[...] = acc
    h2 = _rms(acc, g2_ref[...])
    _split_rows(h2_ref, h2, tile)
    hi, lo = _split2(h2)
    hw = _dot(hi, wrh_ref[...])
    logits = (hw[:, :ROUTE_LANES] + hw[:, ROUTE_LANES:] + _dot(lo, wrh_ref[:, :ROUTE_LANES])) + br_ref[...]
    e1, e2, w1, w2 = _route(logits)
    lane = lax.broadcasted_iota(jnp.int32, (tile, ROUTE_LANES), 1)
    picked = jnp.where((lane == e1) | (lane == e2), 1.0, 0.0)
    rank = carry_ref[...] + _dot(stri_ref[...], picked.astype(BF16))
    r1 = jnp.sum(jnp.where(lane == e1, rank, 0.0), axis=-1, keepdims=True)
    r2 = jnp.sum(jnp.where(lane == e2, rank, 0.0), axis=-1, keepdims=True)
    carry_ref[...] += jnp.sum(picked, axis=0, keepdims=True)
    cnt_ref[...] = carry_ref[...]
    rec = jnp.zeros((tile, ROUTE_LANES), F32)
    for col, val in ((ROUTE_E1, e1.astype(F32)), (ROUTE_E2, e2.astype(F32)), (ROUTE_R1, r1), (ROUTE_R2, r2),
                     (ROUTE_W1, w1), (ROUTE_W2, w2)):
        rec = jnp.where(lane == col, val, rec)
    route_ref[...] = rec


def _out_proj(x, y_s5, y_ret, y_m2, y_hg, w_glu, b_glu, w_out, g2, w_route, b_route):
    n, d = x.shape
    tile = min(SEQ_TILE, n)
    wr_packed = jnp.concatenate(_split2(w_route), axis=1)
    i = np.arange(tile)
    strict_lower = jnp.asarray(i[:, None] > i[None, :], dtype=BF16)
    return pl.pallas_call(
        functools.partial(_out_kernel, tile=tile),
        grid=(n // tile,),
        in_specs=[_rows(tile, d), _rows(tile // S5_CHUNK, S5_CHUNK * GROUP_WIDTH)] + [_rows(tile, GROUP_WIDTH)] * 3
        + [_full((256, 256)), _full((1, 256)), _full((d, d)), _full((1, d)),
           _full((d, 2 * ROUTE_LANES)), _full((1, ROUTE_LANES)), _full((tile, tile))],
        out_specs=[_rows(tile, d), _rows(ROW_SPLIT * tile, 128), _rows(tile, ROUTE_LANES), _full((1, ROUTE_LANES))],
        out_shape=[jax.ShapeDtypeStruct((n, d), F32), jax.ShapeDtypeStruct((ROW_SPLIT * n, 128), jnp.int32),
                   jax.ShapeDtypeStruct((n, ROUTE_LANES), F32), jax.ShapeDtypeStruct((1, ROUTE_LANES), F32)],
        scratch_shapes=[pltpu.VMEM((GROUP_WIDTH // 128, tile, 128), F32), pltpu.VMEM((1, ROUTE_LANES), F32)],
        compiler_params=_cparams("arbitrary"),
        name="out_proj_router",
    )(x, y_s5, y_ret, y_m2, y_hg, w_glu, b_glu, w_out, g2, wr_packed, b_route, strict_lower)


def _sc_mesh():
    return plsc.VectorSubcoreMesh(core_axis_name="core", subcore_axis_name="subcore")


def _sc_scatter2(src, idx_a, idx_b, n_out):
    n = src.shape[0]

    @functools.partial(pl.kernel, out_type=jax.ShapeDtypeStruct((n_out, 128), src.dtype), mesh=_sc_mesh(),
                       scratch_types=[])
    def scatter_kernel(x_hbm, ia_hbm, ib_hbm, o_hbm):
        def body(x_vmem, ia_vmem, ib_vmem):
            pltpu.sync_copy(x_vmem, o_hbm.at[ia_vmem.at[0]])
            pltpu.sync_copy(x_vmem, o_hbm.at[ib_vmem.at[0]])

        pltpu.emit_pipeline(
            body, grid=(n // SC_WINDOW,),
            in_specs=[pl.BlockSpec((SC_WINDOW, 128), index_map=lambda i: (i, 0)),
                      pl.BlockSpec((1, SC_WINDOW), index_map=lambda i: (0, i)),
                      pl.BlockSpec((1, SC_WINDOW), index_map=lambda i: (0, i))],
            out_specs=[],
            core_axis_name=("core", "subcore"), dimension_semantics=(pltpu.PARALLEL,),
        )(x_hbm, ia_hbm, ib_hbm)

    return scatter_kernel(src, idx_a.reshape(1, n), idx_b.reshape(1, n))


def _sc_gather2(table, idx_a, idx_b):
    n = idx_a.size
    sds = jax.ShapeDtypeStruct((n, 128), table.dtype)

    @functools.partial(pl.kernel, out_type=(sds, sds), mesh=_sc_mesh(), scratch_types=[])
    def gather_kernel(t_hbm, ia_hbm, ib_hbm, oa_hbm, ob_hbm):
        def body(ia_vmem, ib_vmem, oa_vmem, ob_vmem):
            pltpu.sync_copy(t_hbm.at[ia_vmem.at[0]], oa_vmem)
            pltpu.sync_copy(t_hbm.at[ib_vmem.at[0]], ob_vmem)

        pltpu.emit_pipeline(
            body, grid=(n // SC_WINDOW,),
            in_specs=[pl.BlockSpec((1, SC_WINDOW), index_map=lambda i: (0, i)),
                      pl.BlockSpec((1, SC_WINDOW), index_map=lambda i: (0, i))],
            out_specs=[pl.BlockSpec((SC_WINDOW, 128), index_map=lambda i: (i, 0)),
                       pl.BlockSpec((SC_WINDOW, 128), index_map=lambda i: (i, 0))],
            core_axis_name=("core", "subcore"), dimension_semantics=(pltpu.PARALLEL,),
        )(ia_hbm, ib_hbm, oa_hbm, ob_hbm)

    return gather_kernel(table, idx_a.reshape(1, n), idx_b.reshape(1, n))


def _index_kernel(route_ref, off_ref, i1_ref, i2_ref):
    rec = route_ref[...]
    lane = lax.broadcasted_iota(jnp.int32, rec.shape, 1)
    off = off_ref[...]
    pos = []
    for e_col, r_col in ((ROUTE_E1, ROUTE_R1), (ROUTE_E2, ROUTE_R2)):
        e = rec[:, e_col:e_col + 1].astype(jnp.int32)
        pos.append(jnp.sum(jnp.where(lane == e, off, 0.0), axis=-1, keepdims=True) + rec[:, r_col:r_col + 1])
    i1_ref[...] = (pos[0].astype(jnp.int32) * ROW_SPLIT + lane)[:, 0:ROW_SPLIT]
    i2_ref[...] = (pos[1].astype(jnp.int32) * ROW_SPLIT + lane)[:, 0:ROW_SPLIT]


def _dispatch_plan(route, counts, n_tiles):
    n = route.shape[0]
    cnt = counts[0, :MOE_EXPERTS].astype(jnp.int32)
    padded = ((cnt + EXPERT_TILE - 1) // EXPERT_TILE) * EXPERT_TILE
    ends = jnp.cumsum(padded)
    off = ends - padded
    start = jnp.arange(n_tiles, dtype=jnp.int32) * EXPERT_TILE
    tile_expert = jnp.minimum(jnp.sum((start[:, None] >= ends[None, :]).astype(jnp.int32), axis=1), MOE_EXPERTS - 1)
    onehot = (tile_expert[:, None] == jnp.arange(MOE_EXPERTS, dtype=jnp.int32)[None, :]).astype(jnp.int32)
    seg_end = jnp.sum(onehot * (off + cnt)[None, :], axis=1)
    tile_rows = jnp.clip(seg_end - start, 0, EXPERT_TILE).astype(jnp.int32)
    prev = jnp.concatenate([jnp.full((1,), -1, jnp.int32), tile_expert[:-1]])
    tile_first = ((tile_expert != prev) & (tile_rows > 0)).astype(jnp.int32)

    tile = min(SEQ_TILE, n)
    off_row = jnp.zeros((1, ROUTE_LANES), F32).at[0, :MOE_EXPERTS].set(off.astype(F32))
    idx1, idx2 = pl.pallas_call(
        _index_kernel,
        grid=(n // tile,),
        in_specs=[_rows(tile, ROUTE_LANES), _full((1, ROUTE_LANES))],
        out_specs=[_rows(tile, ROW_SPLIT)] * 2,
        out_shape=[jax.ShapeDtypeStruct((n, ROW_SPLIT), jnp.int32)] * 2,
        compiler_params=_cparams("parallel"),
        name="dispatch_index",
    )(route, off_row)
    return idx1.reshape(1, -1), idx2.reshape(1, -1), tile_expert, tile_rows, tile_first


def _experts_kernel(te_ref, rows_ref, first_ref, xs_ref, wg_ref, wu_ref, wd_ref, y_ref, wgb, wub, wdb):
    i = pl.program_id(0)

    @pl.when(first_ref[i] == 1)
    def _():
        wgb[...] = wg_ref[0, 0].astype(BF16)
        wub[...] = wu_ref[0, 0].astype(BF16)
        wdb[...] = wd_ref[0, 0].astype(BF16)

    @pl.when(rows_ref[i] > 0)
    def _():
        x = _merge_rows(xs_ref, EXPERT_TILE)
        row = lax.broadcasted_iota(jnp.int32, x.shape, 0)
        x = jnp.where(row < rows_ref[i], x, 0.0).astype(BF16)
        act = _silu(_dot(x, wgb[...])) * _dot(x, wub[...])
        _split_rows(y_ref, _dot(act.astype(BF16), wdb[...]), EXPERT_TILE)


def _experts(xs, tile_expert, tile_rows, tile_first, w_gate, w_up, w_down, layer):
    n_tiles = tile_expert.shape[0]
    _, _, d, ff = w_gate.shape
    rows_blk = pl.BlockSpec((ROW_SPLIT * EXPERT_TILE, 128), lambda i, te, rows, first: (i, 0))
    return pl.pallas_call(
        _experts_kernel,
        grid_spec=pltpu.PrefetchScalarGridSpec(
            num_scalar_prefetch=3,
            grid=(n_tiles,),
            in_specs=[rows_blk,
                      pl.BlockSpec((1, 1, d, ff), lambda i, te, rows, first: (layer, te[i], 0, 0)),
                      pl.BlockSpec((1, 1, d, ff), lambda i, te, rows, first: (layer, te[i], 0, 0)),
                      pl.BlockSpec((1, 1, ff, d), lambda i, te, rows, first: (layer, te[i], 0, 0))],
            out_specs=rows_blk,
            scratch_shapes=[pltpu.VMEM((d, ff), BF16), pltpu.VMEM((d, ff), BF16), pltpu.VMEM((ff, d), BF16)],
        ),
        out_shape=jax.ShapeDtypeStruct(xs.shape, xs.dtype),
        compiler_params=_cparams("arbitrary"),
        name="moe_experts",
    )(tile_expert, tile_rows, tile_first, xs, w_gate, w_up, w_down)


def _combine_kernel(x1_ref, g1_ref, g2_ref, route_ref, gf_ref, o_ref, *, tile, final_norm):
    route = route_ref[...]
    out = (x1_ref[...] + route[:, ROUTE_W1:ROUTE_W1 + 1] * _merge_rows(g1_ref, tile)
           + route[:, ROUTE_W2:ROUTE_W2 + 1] * _merge_rows(g2_ref, tile))
    o_ref[...] = _rms(out, gf_ref[...]) if final_norm else out


def _combine(x1, g1, g2, route, g_final, final_norm):
    n, d = x1.shape
    tile = min(SEQ_TILE, n)
    return pl.pallas_call(
        functools.partial(_combine_kernel, tile=tile, final_norm=final_norm),
        grid=(n // tile,),
        in_specs=[_rows(tile, d), _rows(ROW_SPLIT * tile, 128), _rows(ROW_SPLIT * tile, 128),
                  _rows(tile, ROUTE_LANES), _full((1, d))],
        out_specs=_rows(tile, d),
        out_shape=jax.ShapeDtypeStruct((n, d), F32),
        compiler_params=_cparams("parallel"),
        name="moe_combine",
    )(x1, g1, g2, route, g_final)


def _moe(h2_rows, route, counts, x1, w_gate, w_up, w_down, layer, g_final, final_norm):
    n = x1.shape[0]
    n_tiles = (MOE_TOPK * n) // EXPERT_TILE + MOE_EXPERTS
    idx1, idx2, tile_expert, tile_rows, tile_first = _dispatch_plan(route, counts, n_tiles)
    xs = _sc_scatter2(h2_rows, idx1, idx2, ROW_SPLIT * n_tiles * EXPERT_TILE)
    ys = _experts(xs, tile_expert, tile_rows, tile_first, w_gate, w_up, w_down, layer)
    g1, g2 = _sc_gather2(ys, idx1, idx2)
    return _combine(x1, g1, g2, route, g_final, final_norm)


def kernel(x, positions, norm1_g, w_in, w_out, s5_a_re, s5_a_im, s5_log_dt, s5_b_re, s5_b_im, s5_c_re, s5_c_im, s5_d, s5_w_glu, s5_b_glu, m2_conv_w, m2_conv_b, m2_dt_bias, m2_a_log, m2_d, m2_norm_g, hg_lb_logits, hg_norm_g, norm2_g, moe_w_group, moe_b_group, moe_w_expert, moe_b_expert, moe_w_gate, moe_w_up, moe_w_down, final_norm_g):
    bsz, seqlen, d = x.shape
    assert bsz == 1 and seqlen % SEQ_TILE == 0 and (MOE_TOPK * seqlen) % EXPERT_TILE == 0
    depth = w_in.shape[0]

    lb_probs = jax.nn.softmax(hg_lb_logits.astype(F32), axis=0)
    lower_bounds = jnp.cumsum(lb_probs, axis=0) - lb_probs[0]
    cos_t, sin_t = _rope_tables(positions.reshape(seqlen, 1))

    xc = x.reshape(seqlen, d)
    for l in range(depth):
        u_b, u_f, p_ret, p_m2, p_hg = _in_proj(xc, norm1_g[l].reshape(1, d), _w_prep(w_in, l))
        y_s5 = _s5(u_b, u_f, s5_a_re[l], s5_a_im[l], s5_log_dt[l], s5_b_re[l], s5_b_im[l],
                   s5_c_re[l], s5_c_im[l], s5_d[l])
        y_ret = _retention(p_ret, cos_t, sin_t)
        y_m2 = _mamba2(p_m2, m2_conv_w[l], m2_conv_b[l], m2_dt_bias[l], m2_a_log[l], m2_d[l], m2_norm_g[l])
        y_hg = _hgrn2(p_hg, lower_bounds[l], hg_norm_g[l])
        w_route = jnp.zeros((d, ROUTE_LANES), F32)
        w_route = w_route.at[:, :MOE_GROUPS].set(moe_w_group[l])
        w_route = w_route.at[:, MOE_GROUPS:MOE_GROUPS + MOE_EXPERTS].set(moe_w_expert[l])
        b_route = jnp.zeros((1, ROUTE_LANES), F32)
        b_route = b_route.at[0, :MOE_GROUPS].set(moe_b_group[l])
        b_route = b_route.at[0, MOE_GROUPS:MOE_GROUPS + MOE_EXPERTS].set(moe_b_expert[l])
        x1, h2_rows, route, counts = _out_proj(xc, y_s5, y_ret, y_m2, y_hg, s5_w_glu[l].astype(BF16),
                                               s5_b_glu[l].reshape(1, -1), w_out[l].astype(BF16),
                                               norm2_g[l].reshape(1, d), w_route, b_route)
        xc = _moe(h2_rows, route, counts, x1, moe_w_gate, moe_w_up, moe_w_down, l,
                  final_norm_g.reshape(1, d), final_norm=(l == depth - 1))
    return xc.reshape(bsz, seqlen, d)
```

```python
import functools
import math

import numpy as np
import jax
import jax.numpy as jnp
from jax import lax
from jax.experimental import pallas as pl
from jax.experimental.pallas import tpu as pltpu
from jax.experimental.pallas import tpu_sc as plsc

F32 = jnp.float32
BF16 = jnp.bfloat16
NORM_EPS = 1e-6

GROUP_WIDTH = 256
HEAD_DIM = 64
N_HEADS = GROUP_WIDTH // HEAD_DIM
S5_GROUPS = 16
S5_CH = 16
S5_STATE = 64
S5_DT_CLAMP = -1e-4
M2_STATE = 128
M2_CONV = 4
M2_CONV_DIM = 768
ROPE_BASE = 10000.0
MOE_GROUPS = 4
MOE_PER_GROUP = 4
MOE_EXPERTS = 16
ROUTE_LANES = 128

SEQ_TILE = 512
CHUNK = 128
CHUNK_UNROLL = 4
S5_CHUNK = 16
MOE_TOPK = 2
EXPERT_TILE = 512
SC_WINDOW = 128
VMEM_LIMIT = 56 * 1024 * 1024


def _cparams(*sem):
    return pltpu.CompilerParams(dimension_semantics=sem, vmem_limit_bytes=VMEM_LIMIT)


def _dot(a, b):
    return jnp.dot(a, b, preferred_element_type=F32)


def _dot_nt(a, b):
    return lax.dot_general(a, b, (((1,), (1,)), ((), ())), preferred_element_type=F32)


def _dot_tn(a, b):
    return lax.dot_general(a, b, (((0,), (0,)), ((), ())), preferred_element_type=F32)


def _split2(x):
    hi = x.astype(BF16)
    return hi, (x - hi.astype(F32)).astype(BF16)


def _split3(x):
    hi = x.astype(BF16)
    r = x - hi.astype(F32)
    mid = r.astype(BF16)
    return hi, mid, (r - mid.astype(F32)).astype(BF16)


def _dot_exact_lhs(m, x):
    hi, mid, lo = _split3(x)
    return _dot(m, hi) + _dot(m, mid) + _dot(m, lo)


def _dot_exact_rhs(x, m):
    hi, lo = _split2(x)
    return _dot(hi, m) + _dot(lo, m)


def _sigmoid(x):
    return 1.0 / (1.0 + jnp.exp(-x))


def _silu(x):
    return x * _sigmoid(x)


def _rms(x, g):
    return x * lax.rsqrt(jnp.mean(x * x, axis=-1, keepdims=True) + NORM_EPS) * g


def _full(shape):
    return pl.BlockSpec(shape, lambda *_: (0,) * len(shape))


def _rows(tile, width):
    return pl.BlockSpec((tile, width), lambda i: (i, 0))


IN_SEGMENTS = (256, 1024, 1280, 1024)


DT_COL = 9 * GROUP_WIDTH
W_PREP_ROWS = 128


def _w_prep_kernel(w_ref, o_ref):
    o_ref[:, 0:DT_COL] = w_ref[0, :, 0:DT_COL].astype(BF16)
    head = lax.broadcasted_iota(jnp.int32, (W_PREP_ROWS, GROUP_WIDTH), 1) // HEAD_DIM
    rep = jnp.zeros((W_PREP_ROWS, GROUP_WIDTH), F32)
    for h in range(N_HEADS):
        rep = jnp.where(head == h, w_ref[0, :, DT_COL + h:DT_COL + h + 1], rep)
    o_ref[:, DT_COL:DT_COL + GROUP_WIDTH] = rep.astype(BF16)
    o_ref[:, DT_COL + GROUP_WIDTH:] = w_ref[0, :, DT_COL + N_HEADS:].astype(BF16)


def _w_prep(w_in, layer):
    _, d, n_in = w_in.shape
    return pl.pallas_call(
        _w_prep_kernel,
        grid=(d // W_PREP_ROWS,),
        in_specs=[pl.BlockSpec((1, W_PREP_ROWS, n_in), lambda i: (layer, i, 0))],
        out_specs=pl.BlockSpec((W_PREP_ROWS, sum(IN_SEGMENTS)), lambda i: (i, 0)),
        out_shape=jax.ShapeDtypeStruct((d, sum(IN_SEGMENTS)), BF16),
        compiler_params=_cparams("parallel"),
        name="w_in_prep",
    )(w_in)


def _in_proj_kernel(x_ref, g_ref, w_ref, ub_ref, uf_ref, ret_ref, m2_ref, hg_ref, u_tmp, *, tile):
    hb = _rms(x_ref[...], g_ref[...]).astype(BF16)
    c0, c1, c2, c3 = np.cumsum(IN_SEGMENTS)
    ret_ref[...] = _dot(hb, w_ref[:, c0:c1])
    m2_ref[...] = _dot(hb, w_ref[:, c1:c2])
    hg_ref[...] = _dot(hb, w_ref[:, c2:c3])
    u = _dot(hb, w_ref[:, 0:c0])
    for j in range(GROUP_WIDTH // 128):
        u_tmp[j] = u[:, 128 * j:128 * (j + 1)]
    for s in range(S5_CHUNK):
        for j in range(GROUP_WIDTH // 128):
            v = u_tmp[j, pl.ds(s, tile // S5_CHUNK, stride=S5_CHUNK), :]
            lanes = slice(GROUP_WIDTH * s + 128 * j, GROUP_WIDTH * s + 128 * (j + 1))
            uf_ref[:, lanes] = v
            ub_ref[:, lanes] = v.astype(BF16)


def _in_proj(x, g, w):
    n, d = x.shape
    tile = min(SEQ_TILE, n)
    fold = S5_CHUNK * GROUP_WIDTH
    return pl.pallas_call(
        functools.partial(_in_proj_kernel, tile=tile),
        grid=(n // tile,),
        in_specs=[_rows(tile, d), _full((1, d)), _full(w.shape)],
        out_specs=[_rows(tile // S5_CHUNK, fold)] * 2 + [_rows(tile, s) for s in IN_SEGMENTS[1:]],
        out_shape=[jax.ShapeDtypeStruct((n // S5_CHUNK, fold), BF16),
                   jax.ShapeDtypeStruct((n // S5_CHUNK, fold), F32)]
        + [jax.ShapeDtypeStruct((n, s), F32) for s in IN_SEGMENTS[1:]],
        scratch_shapes=[pltpu.VMEM((GROUP_WIDTH // 128, tile, 128), F32)],
        compiler_params=_cparams("parallel"),
        name="in_proj",
    )(x, g, w)


def _rope_kernel(pos_ref, invf_ref, cos_ref, sin_ref):
    ang = pos_ref[...].astype(F32) * invf_ref[...]
    cos_ref[...] = jnp.cos(ang)
    sin_ref[...] = jnp.sin(ang)


def _rope_tables(positions):
    n = positions.shape[0]
    tile = min(SEQ_TILE, n)
    half = HEAD_DIM // 2
    inv_freq = ROPE_BASE ** (-jnp.arange(half, dtype=F32) / half)
    invf = jnp.tile(inv_freq, 128 // half).reshape(1, 128)
    return pl.pallas_call(
        _rope_kernel,
        grid=(n // tile,),
        in_specs=[_rows(tile, 1), _full((1, 128))],
        out_specs=[_rows(tile, 128), _rows(tile, 128)],
        out_shape=[jax.ShapeDtypeStruct((n, 128), F32)] * 2,
        compiler_params=_cparams("parallel"),
        name="rope_tables",
    )(positions, invf)


def _head_mean_matrix():
    h = np.arange(GROUP_WIDTH) // HEAD_DIM
    return jnp.asarray((h[:, None] == h[None, :]) / HEAD_DIM, dtype=BF16)


def _head_block_mask():
    h = np.arange(GROUP_WIDTH) // HEAD_DIM
    return jnp.asarray(h[:, None] == h[None, :], dtype=F32)


HEAD_PAIRS = GROUP_WIDTH // 128


def _lanes(x, j):
    return x[:, 128 * j:128 * (j + 1)]


def _stack_pair(x):
    xb = x.astype(BF16)
    low = lax.broadcasted_iota(jnp.int32, x.shape, 1) < HEAD_DIM
    zero = jnp.zeros_like(xb)
    return jnp.concatenate([jnp.where(low, xb, zero), jnp.where(low, zero, xb)], axis=0)


def _pair_scores(q, kb):
    s = _dot_nt(_stack_pair(q), kb)
    return s[:q.shape[0]], s[q.shape[0]:]


def _pair_apply(a0, a1, v):
    return _dot(jnp.concatenate([a0.astype(BF16), a1.astype(BF16)], axis=1), _stack_pair(v))


def _ret_constants(chunk):
    lg = np.log1p(-np.exp2(-5.0 - np.arange(N_HEADS, dtype=np.float64)))
    idx = np.arange(chunk, dtype=np.float64)
    rel = idx[:, None] - idx[None, :]
    decay = np.where(rel >= 0, np.exp(np.maximum(rel, 0.0)[None] * lg[:, None, None]), 0.0)
    lane_lg = np.repeat(lg, HEAD_DIM)
    xi = np.exp((idx + 1.0)[:, None] * lane_lg[None, :])
    zeta = np.exp((chunk - 1.0 - idx)[:, None] * lane_lg[None, :])
    h = np.arange(GROUP_WIDTH) // HEAD_DIM
    gc = np.where(h[:, None] == h[None, :], np.exp(chunk * lane_lg)[:, None], 0.0)
    f = lambda a: jnp.asarray(a, dtype=F32)
    return f(decay), f(xi), f(zeta), f(gc)


def _ret_kernel(p_ref, cos_ref, sin_ref, dec_ref, xi_ref, zeta_ref, gc_ref, bm_ref, gm_ref,
                o_ref, r_ref, *, chunk, n_chunks):
    @pl.when(pl.program_id(0) == 0)
    def _():
        r_ref[...] = jnp.zeros_like(r_ref)

    lane = lax.broadcasted_iota(jnp.int32, (chunk, GROUP_WIDTH), 1)
    first_half = (lane % HEAD_DIM) < (HEAD_DIM // 2)
    gm = gm_ref[...]

    def rope(t, cos2, sin2):
        rot = jnp.where(first_half, -pltpu.roll(t, GROUP_WIDTH - HEAD_DIM // 2, 1),
                        pltpu.roll(t, HEAD_DIM // 2, 1))
        return t * cos2 + rot * sin2

    def body(c, carry):
        rows = pl.ds(pl.multiple_of(c * chunk, chunk), chunk)
        cs = cos_ref[rows, :]
        sn = sin_ref[rows, :]
        cos2 = jnp.concatenate([cs, cs], axis=1)
        sin2 = jnp.concatenate([sn, sn], axis=1)
        q = rope(p_ref[rows, 0:256], cos2, sin2)
        k = rope(p_ref[rows, 256:512], cos2, sin2) * (HEAD_DIM ** -0.5)
        v = p_ref[rows, 512:768]
        g = p_ref[rows, 768:1024]
        kb = k.astype(BF16)
        inner = []
        for j in range(HEAD_PAIRS):
            s0, s1 = _pair_scores(_lanes(q, j), _lanes(kb, j))
            inner.append(_pair_apply(s0 * dec_ref[2 * j], s1 * dec_ref[2 * j + 1], _lanes(v, j)))
        inner = jnp.concatenate(inner, axis=1)
        r_prev = r_ref[...]
        cross = _dot((q * xi_ref[...]).astype(BF16), r_prev.astype(BF16))
        r_ref[...] = gc_ref[...] * r_prev + bm_ref[...] * _dot_tn(kb, (zeta_ref[...] * v).astype(BF16))
        o = inner + cross
        cen = o - _dot_exact_rhs(o, gm)
        var = _dot_exact_rhs(cen * cen, gm)
        o_ref[rows, :] = (cen * lax.rsqrt(var + NORM_EPS) * _silu(g)).astype(BF16)
        return carry

    lax.fori_loop(0, n_chunks, body, 0, unroll=CHUNK_UNROLL)


def _retention(proj, cos_t, sin_t):
    n = proj.shape[0]
    tile = min(SEQ_TILE, n)
    chunk = min(CHUNK, tile)
    decay, xi, zeta, gc = _ret_constants(chunk)
    kern = functools.partial(_ret_kernel, chunk=chunk, n_chunks=tile // chunk)
    return pl.pallas_call(
        kern,
        grid=(n // tile,),
        in_specs=[_rows(tile, 1024), _rows(tile, 128), _rows(tile, 128),
                  _full(decay.shape), _full(xi.shape), _full(zeta.shape), _full(gc.shape),
                  _full((256, 256)), _full((256, 256))],
        out_specs=_rows(tile, GROUP_WIDTH),
        out_shape=jax.ShapeDtypeStruct((n, GROUP_WIDTH), BF16),
        scratch_shapes=[pltpu.VMEM((GROUP_WIDTH, GROUP_WIDTH), F32)],
        compiler_params=_cparams("arbitrary"),
        name="retention",
    )(proj, cos_t, sin_t, decay, xi, zeta, gc, _head_block_mask(), _head_mean_matrix())


def _tri_matrix(chunk):
    i = np.arange(chunk)
    return jnp.asarray(i[:, None] >= i[None, :], dtype=BF16)


def _m2_kernel(p_ref, cw_ref, cb_ref, dtb_ref, alog_ref, d_ref, ng_ref, tri_ref,
               o_ref, tail_ref, ext_ref, act_ref, st_ref, *, tile, chunk, n_chunks):
    @pl.when(pl.program_id(0) == 0)
    def _():
        tail_ref[...] = jnp.zeros_like(tail_ref)
        st_ref[...] = jnp.zeros_like(st_ref)

    ext_ref[0:8, :] = tail_ref[...]
    ext_ref[8:tile + 8, :] = p_ref[:, 256:1024]
    tail_ref[...] = p_ref[tile - 8:tile, 256:1024]
    conv = cb_ref[...]
    for j in range(M2_CONV):
        lo = 8 - (M2_CONV - 1) + j
        conv = conv + cw_ref[j:j + 1, :] * ext_ref[lo:lo + tile, :]
    act_ref[...] = _silu(conv)
    a_lane = -jnp.exp(alog_ref[...])
    tri = tri_ref[...]
    ti = lax.broadcasted_iota(jnp.int32, (chunk, chunk), 0)
    si = lax.broadcasted_iota(jnp.int32, (chunk, chunk), 1)
    causal = ti >= si
    lane = lax.broadcasted_iota(jnp.int32, (chunk, 128), 1)

    def body(c, carry):
        start = pl.multiple_of(c * chunk, chunk)
        rows = pl.ds(start, chunk)
        xbc = act_ref[rows, :]
        xs = xbc[:, 0:256]
        z = p_ref[rows, 0:256]
        x_dt = p_ref[rows, 1024:1280] + dtb_ref[...]
        dt = jnp.maximum(x_dt, 0.0) + jnp.log1p(jnp.exp(-jnp.abs(x_dt)))
        acum = _dot_exact_lhs(tri, dt * a_lane)
        acum_t = acum.T
        a_last = acum[chunk - 1:chunk, :]
        e_acum = jnp.exp(acum)
        decs = jnp.exp(a_last - acum)
        d_chunk = jnp.exp(a_last)
        xc = xs * dt
        ys = []
        for g in range(2):
            sl = slice(128 * g, 128 * (g + 1))
            bmg = xbc[:, 256 + 128 * g:256 + 128 * (g + 1)].astype(BF16)
            cmg = xbc[:, 512 + 128 * g:512 + 128 * (g + 1)].astype(BF16)
            cb = _dot_nt(cmg, bmg)
            xcg = xc[:, sl]
            yd = jnp.zeros((chunk, 128), F32)
            for hh in range(2):
                col0 = 128 * g + HEAD_DIM * hh
                diff = acum[:, col0:col0 + 1] - acum_t[col0:col0 + 1, :]
                lm = jnp.where(causal, jnp.exp(jnp.where(causal, diff, 0.0)), 0.0)
                xm = jnp.where((lane // HEAD_DIM) == hh, xcg, 0.0).astype(BF16)
                yd = yd + _dot((cb * lm).astype(BF16), xm)
            st = st_ref[:, sl]
            y_off = _dot(cmg, st.astype(BF16)) * e_acum[:, sl]
            st_ref[:, sl] = d_chunk[:, sl] * st + _dot_tn(bmg, (xcg * decs[:, sl]).astype(BF16))
            ys.append(yd + y_off + d_ref[:, sl] * xs[:, sl])
        y = jnp.concatenate(ys, axis=1) * _silu(z)
        o_ref[rows, :] = _rms(y, ng_ref[...]).astype(BF16)
        return carry

    lax.fori_loop(0, n_chunks, body, 0, unroll=CHUNK_UNROLL)


def _mamba2(proj, conv_w, conv_b, dt_bias, a_log, d_skip, norm_g):
    n = proj.shape[0]
    tile = min(SEQ_TILE, n)
    chunk = min(CHUNK, tile)
    lanes = lambda v: jnp.repeat(v, HEAD_DIM).reshape(1, GROUP_WIDTH)
    kern = functools.partial(_m2_kernel, tile=tile, chunk=chunk, n_chunks=tile // chunk)
    return pl.pallas_call(
        kern,
        grid=(n // tile,),
        in_specs=[_rows(tile, 1280), _full((M2_CONV, M2_CONV_DIM)), _full((1, M2_CONV_DIM)),
                  _full((1, 256)), _full((1, 256)), _full((1, 256)), _full((1, 256)),
                  _full((chunk, chunk))],
        out_specs=_rows(tile, GROUP_WIDTH),
        out_shape=jax.ShapeDtypeStruct((n, GROUP_WIDTH), BF16),
        scratch_shapes=[pltpu.VMEM((8, M2_CONV_DIM), F32),
                        pltpu.VMEM((tile + 8, M2_CONV_DIM), F32),
                        pltpu.VMEM((tile, M2_CONV_DIM), F32),
                        pltpu.VMEM((M2_STATE, GROUP_WIDTH), F32)],
        compiler_params=_cparams("arbitrary"),
        name="mamba2_ssd",
    )(proj, conv_w, conv_b.reshape(1, -1), lanes(dt_bias), lanes(a_log), lanes(d_skip),
      norm_g.reshape(1, -1), _tri_matrix(chunk))


HG_MATMUL_LEVELS = 3


def _hg_exponent_matrix(chunk):
    levels = HG_MATMUL_LEVELS
    t = np.arange(chunk)[:, None]
    r = np.arange(chunk)[None, :]
    blocks = []
    for lvl in range(levels):
        b = 1 << lvl
        blk = t // b
        odd = (blk % 2) == 1
        q_rows = odd & (r >= blk * b) & (r <= t)
        k_rows = (~odd) & (r > t) & (r <= (blk + 1) * b - 1)
        blocks.append(q_rows | k_rows)
    blocks.append(r <= t)
    return jnp.asarray(np.concatenate(blocks, axis=0), dtype=BF16)


def _hg_kernel(p_ref, lb_ref, ng_ref, gexp_ref, bm_ref, gm_ref, o_ref, st_ref, *, chunk, n_chunks):
    @pl.when(pl.program_id(0) == 0)
    def _():
        st_ref[...] = jnp.zeros_like(st_ref)

    levels = int(math.log2(chunk))
    row = lax.broadcasted_iota(jnp.int32, (chunk, 128), 0)
    odd_rows = [((row >> lvl) & 1) == 1 for lvl in range(levels)]
    row_wide = lax.broadcasted_iota(jnp.int32, (chunk, GROUP_WIDTH), 0)
    odd_rows_wide = [((row_wide >> lvl) & 1) == 1 for lvl in range(levels)]
    ti = lax.broadcasted_iota(jnp.int32, (chunk, chunk), 0)
    si = lax.broadcasted_iota(jnp.int32, (chunk, chunk), 1)
    pair_level = [((ti >> (lvl + 1)) == (si >> (lvl + 1))) & (((ti >> lvl) & 1) == 1) & (((si >> lvl) & 1) == 0)
                  for lvl in range(levels)]
    lb = lb_ref[...]
    gm = gm_ref[...]

    def body(c, carry):
        rows = pl.ds(pl.multiple_of(c * chunk, chunk), chunk)
        q = _silu(p_ref[rows, 0:256])
        forget = lb + (1.0 - lb) * _sigmoid(p_ref[rows, 256:512])
        k = 1.0 - forget
        v = p_ref[rows, 512:768]
        g = p_ref[rows, 768:1024]
        lf_hi, lf_lo = _split2(jnp.log(forget))
        expo = _dot(gexp_ref[...], lf_hi) + _dot(gexp_ref[...], lf_lo)
        bcum = expo[HG_MATMUL_LEVELS * chunk:(HG_MATMUL_LEVELS + 1) * chunk, :]

        def level_log_decay(lvl):
            if lvl < HG_MATMUL_LEVELS:
                return expo[lvl * chunk:(lvl + 1) * chunk, :]
            b = 1 << lvl
            ref = jnp.concatenate([jnp.broadcast_to(bcum[m + b - 1:m + b, :], (2 * b, GROUP_WIDTH))
                                   for m in range(0, chunk, 2 * b)], axis=0)
            return jnp.where(odd_rows_wide[lvl], bcum - ref, ref - bcum)


        log_decay = [level_log_decay(lvl) for lvl in range(levels)]
        intra = []
        for j in range(HEAD_PAIRS):
            qj, kj = _lanes(q, j), _lanes(k, j)
            a0, a1 = (jnp.where(ti == si, s, 0.0) for s in _pair_scores(qj, kj.astype(BF16)))
            for lvl in range(levels):
                w = jnp.exp(_lanes(log_decay[lvl], j)) * jnp.where(odd_rows[lvl], qj, kj)
                s0, s1 = _pair_scores(w, w.astype(BF16))
                a0 = jnp.where(pair_level[lvl], s0, a0)
                a1 = jnp.where(pair_level[lvl], s1, a1)
            intra.append(_pair_apply(a0, a1, _lanes(v, j)))
        intra = jnp.concatenate(intra, axis=1)

        b_last = bcum[chunk - 1:chunk, :]
        suffix = b_last - bcum
        st = st_ref[...]
        cross = _dot_nt((q * jnp.exp(bcum)).astype(BF16), st.astype(BF16))
        st_ref[...] = jnp.exp(b_last) * st + bm_ref[...] * _dot_tn(
            v.astype(BF16), (k * jnp.exp(suffix)).astype(BF16))
        o = intra + cross
        o = o * lax.rsqrt(_dot_exact_rhs(o * o, gm) + NORM_EPS) * ng_ref[...]
        o_ref[rows, :] = (o * _silu(g)).astype(BF16)
        return carry

    lax.fori_loop(0, n_chunks, body, 0, unroll=CHUNK_UNROLL)


def _hgrn2(proj, lower_bound, norm_g):
    n = proj.shape[0]
    tile = min(SEQ_TILE, n)
    chunk = min(CHUNK, tile)
    gexp = _hg_exponent_matrix(chunk)
    kern = functools.partial(_hg_kernel, chunk=chunk, n_chunks=tile // chunk)
    return pl.pallas_call(
        kern,
        grid=(n // tile,),
        in_specs=[_rows(tile, 1024), _full((1, 256)), _full((1, 256)), _full(gexp.shape),
                  _full((256, 256)), _full((256, 256))],
        out_specs=_rows(tile, GROUP_WIDTH),
        out_shape=jax.ShapeDtypeStruct((n, GROUP_WIDTH), BF16),
        scratch_shapes=[pltpu.VMEM((GROUP_WIDTH, GROUP_WIDTH), F32)],
        compiler_params=_cparams("arbitrary"),
        name="hgrn2",
    )(proj, lower_bound.reshape(1, -1), norm_g.reshape(1, -1), gexp, _head_block_mask(),
      _head_mean_matrix())


S5_LANES = S5_GROUPS * 2 * S5_STATE


def _s5_rows(a_re, a_im, log_dt):
    are = jnp.minimum(a_re, S5_DT_CLAMP)
    dt = jnp.exp(log_dt)
    lam_re = are * dt
    lam_im = a_im * dt
    mag = jnp.exp(lam_re)
    ab_re = mag * jnp.cos(lam_im)
    ab_im = mag * jnp.sin(lam_im)
    den = are * are + a_im * a_im
    k_re = ((ab_re - 1.0) * are + ab_im * a_im) / den
    k_im = (ab_im * are - (ab_re - 1.0) * a_im) / den
    return lam_re, lam_im, k_re, k_im


def _s5_power(lam_re, lam_im, e):
    m = jnp.exp(e * lam_re)
    return m * jnp.cos(e * lam_im), m * jnp.sin(e * lam_im)


def _s5_state_kernel(u_ref, are_ref, aim_ref, ldt_ref, b1_ref, b2_ref, ca_ref, k_ref, sp_ref,
                     bb1_ref, bb2_ref, inc_ref, *, n_chunks):
    s = pl.program_id(0)
    lam_re, lam_im, k_re, k_im = _s5_rows(are_ref[...], aim_ref[...], ldt_ref[...])

    @pl.when(s == 0)
    def _():
        bb1_ref[...] = k_re * b1_ref[...] + k_im * b2_ref[...]
        bb2_ref[...] = k_re * b2_ref[...] - k_im * b1_ref[...]

    p_re, p_im = _s5_power(lam_re, lam_im, (S5_CHUNK - 1 - s).astype(F32))
    w = (p_re * bb1_ref[...] + p_im * bb2_ref[...]).astype(BF16)
    k_ref[0] = _dot_nt(w, ca_ref[...].astype(BF16)).astype(BF16)
    contrib = _dot(u_ref[...], w)

    @pl.when(s == 0)
    def _():
        inc_ref[...] = contrib

    @pl.when(s > 0)
    def _():
        inc_ref[...] += contrib

    @pl.when(s == S5_CHUNK - 1)
    def _():
        n_steps = max(1, int(math.ceil(math.log2(n_chunks))))
        step = lax.broadcasted_iota(jnp.int32, (16, S5_LANES), 0)
        e = (jnp.left_shift(1, step) * S5_CHUNK).astype(F32)
        a_re_all, a_im_all = _s5_power(lam_re, lam_im, e)
        row = lax.broadcasted_iota(jnp.int32, (n_chunks, 128), 0)
        half = S5_LANES // 2

        def shifted(x, sh):
            return jnp.where(row >= sh, pltpu.roll(x, sh, 0), 0.0)

        for j in range(half // 128):
            re_l, im_l = slice(128 * j, 128 * (j + 1)), slice(half + 128 * j, half + 128 * (j + 1))
            x_re, x_im = inc_ref[:, re_l], inc_ref[:, im_l]
            for k in range(n_steps):
                sh = 1 << k
                if sh >= n_chunks:
                    break
                a_re, a_im = a_re_all[k:k + 1, re_l], a_im_all[k:k + 1, re_l]
                p_re, p_im = shifted(x_re, sh), shifted(x_im, sh)
                x_re, x_im = x_re + a_re * p_re - a_im * p_im, x_im + a_re * p_im + a_im * p_re
            sp_ref[:, re_l] = shifted(x_re, 1).astype(BF16)
            sp_ref[:, im_l] = shifted(x_im, 1).astype(BF16)


def _s5_out_kernel(ub_ref, uf_ref, k_ref, sp_ref, are_ref, aim_ref, ldt_ref, ca_ref, cb_ref, d_ref,
                   y_ref, taps_ref):
    t = pl.program_id(0)
    fold = S5_CHUNK * GROUP_WIDTH

    @pl.when(t == 0)
    def _():
        for j in range(S5_CHUNK):
            taps_ref[GROUP_WIDTH * j:GROUP_WIDTH * (j + 1), :] = k_ref[j]
        taps_ref[fold:, :] = jnp.zeros((fold - GROUP_WIDTH, GROUP_WIDTH), BF16)

    lam_re, lam_im, _, _ = _s5_rows(are_ref[...], aim_ref[...], ldt_ref[...])
    p_re, p_im = _s5_power(lam_re, lam_im, (t + 1).astype(F32))
    w_out = (p_re * ca_ref[...] + p_im * cb_ref[...]).astype(BF16)
    start = pl.multiple_of((S5_CHUNK - 1 - t) * GROUP_WIDTH, GROUP_WIDTH)
    y_ref[...] = (_dot(ub_ref[...], taps_ref[pl.ds(start, fold), :]) + _dot_nt(sp_ref[...], w_out)
                  + d_ref[...] * uf_ref[...])


def _s5_embed(re, im):
    eye = jnp.eye(S5_GROUPS, dtype=F32)
    blocks = [(eye[:, None, :, None] * x[:, :, None, :]).reshape(S5_GROUPS * S5_CH, S5_LANES // 2) for x in (re, im)]
    return jnp.concatenate(blocks, axis=1)


def _s5(u_b, u_f, a_re, a_im, log_dt, b_re, b_im, c_re, c_im, d_skip):
    n_chunks, fold = u_b.shape
    row = lambda v: jnp.tile(v.reshape(1, S5_LANES // 2), (1, 2))
    are, aim = row(a_re), row(a_im)
    ldt = row(jnp.repeat(log_dt, S5_STATE))
    bt_re, bt_im = b_re.transpose(0, 2, 1), b_im.transpose(0, 2, 1)
    b1, b2 = _s5_embed(bt_re, bt_im), _s5_embed(-bt_im, bt_re)
    ca, cb = _s5_embed(c_re, -c_im), _s5_embed(-c_im, -c_re)
    col = lambda: pl.BlockSpec((n_chunks, GROUP_WIDTH), lambda s: (0, s))
    taps, s_prev = pl.pallas_call(
        functools.partial(_s5_state_kernel, n_chunks=n_chunks),
        grid=(S5_CHUNK,),
        in_specs=[col()] + [_full((1, S5_LANES))] * 3 + [_full((GROUP_WIDTH, S5_LANES))] * 3,
        out_specs=[pl.BlockSpec((1, GROUP_WIDTH, GROUP_WIDTH), lambda s: (s, 0, 0)),
                   _full((n_chunks, S5_LANES))],
        out_shape=[jax.ShapeDtypeStruct((S5_CHUNK, GROUP_WIDTH, GROUP_WIDTH), BF16),
                   jax.ShapeDtypeStruct((n_chunks, S5_LANES), BF16)],
        scratch_shapes=[pltpu.VMEM((GROUP_WIDTH, S5_LANES), F32), pltpu.VMEM((GROUP_WIDTH, S5_LANES), F32),
                        pltpu.VMEM((n_chunks, S5_LANES), F32)],
        compiler_params=_cparams("arbitrary"),
        name="s5_state",
    )(u_b, are, aim, ldt, b1, b2, ca)
    return pl.pallas_call(
        _s5_out_kernel,
        grid=(S5_CHUNK,),
        in_specs=[_full((n_chunks, fold)), col(), _full(taps.shape), _full(s_prev.shape)]
        + [_full((1, S5_LANES))] * 3 + [_full((GROUP_WIDTH, S5_LANES))] * 2 + [_full((1, GROUP_WIDTH))],
        out_specs=col(),
        out_shape=jax.ShapeDtypeStruct((n_chunks, fold), F32),
        scratch_shapes=[pltpu.VMEM(((2 * S5_CHUNK - 1) * GROUP_WIDTH, GROUP_WIDTH), BF16)],
        compiler_params=_cparams("arbitrary"),
        name="s5_out",
    )(u_b, u_f, taps, s_prev, are, aim, ldt, ca, cb, d_skip.reshape(1, GROUP_WIDTH))


def _route(logits):
    lane = lax.broadcasted_iota(jnp.int32, logits.shape, 1)
    big = jnp.int32(1 << 20)
    neg = jnp.float32(-jnp.inf)
    is_group = lane < MOE_GROUPS
    gl = jnp.where(is_group, logits, neg)
    ge = jnp.where(is_group, jnp.exp(gl - jnp.max(gl, axis=-1, keepdims=True)), 0.0)
    gp = ge / jnp.sum(ge, axis=-1, keepdims=True)
    p_g = jnp.max(gp, axis=-1, keepdims=True)
    g_idx = jnp.min(jnp.where(is_group & (gp == p_g), lane, big), axis=-1, keepdims=True)
    e_lane = lane - MOE_GROUPS
    in_group = (e_lane >= 0) & (e_lane < MOE_EXPERTS) & ((e_lane // MOE_PER_GROUP) == g_idx)
    el = jnp.where(in_group, logits, neg)
    ee = jnp.where(in_group, jnp.exp(el - jnp.max(el, axis=-1, keepdims=True)), 0.0)
    ep = ee / jnp.sum(ee, axis=-1, keepdims=True)
    p1 = jnp.max(jnp.where(in_group, ep, -1.0), axis=-1, keepdims=True)
    i1 = jnp.min(jnp.where(in_group & (ep == p1), lane, big), axis=-1, keepdims=True)
    rest = in_group & (lane != i1)
    p2 = jnp.max(jnp.where(rest, ep, -1.0), axis=-1, keepdims=True)
    i2 = jnp.min(jnp.where(rest & (ep == p2), lane, big), axis=-1, keepdims=True)
    tot = p1 + p2
    return i1 - MOE_GROUPS, i2 - MOE_GROUPS, p_g * p1 / tot, p_g * p2 / tot


ROW_SPLIT = 4
ROUTE_E1, ROUTE_E2, ROUTE_R1, ROUTE_R2, ROUTE_W1, ROUTE_W2 = range(6)
HIGH_HALF = 0xFFFF0000


def _split_rows(ref, value, rows):
    half = value.shape[1] // 2
    lo = lax.bitcast_convert_type(value[:, :half].astype(jnp.bfloat16).astype(F32), jnp.uint32)
    hi = lax.bitcast_convert_type(value[:, half:].astype(jnp.bfloat16).astype(F32), jnp.uint32)
    words = lax.bitcast_convert_type((lo >> 16) | (hi & jnp.uint32(HIGH_HALF)), jnp.int32)
    for j in range(ROW_SPLIT):
        ref[pl.ds(j, rows, stride=ROW_SPLIT), :] = words[:, 128 * j:128 * (j + 1)]


def _merge_rows(ref, rows):
    words = jnp.concatenate([ref[pl.ds(j, rows, stride=ROW_SPLIT), :] for j in range(ROW_SPLIT)], axis=1)
    words = lax.bitcast_convert_type(words, jnp.uint32)
    lo = lax.bitcast_convert_type(words << 16, F32)
    hi = lax.bitcast_convert_type(words & jnp.uint32(HIGH_HALF), F32)
    return jnp.concatenate([lo, hi], axis=1)


def _out_kernel(x_ref, s5_ref, ret_ref, m2_ref, hg_ref, wglu_ref, bglu_ref, wo_ref, g2_ref,
                wrh_ref, br_ref, stri_ref, x1_ref, h2_ref, route_ref, cnt_ref, s5_tmp, carry_ref,
                *, tile):
    @pl.when(pl.program_id(0) == 0)
    def _():
        carry_ref[...] = jnp.zeros_like(carry_ref)

    for s in range(S5_CHUNK):
        for j in range(GROUP_WIDTH // 128):
            lanes = slice(GROUP_WIDTH * s + 128 * j, GROUP_WIDTH * s + 128 * (j + 1))
            s5_tmp[j, pl.ds(s, tile // S5_CHUNK, stride=S5_CHUNK), :] = s5_ref[:, lanes]
    y = jnp.concatenate([s5_tmp[j] for j in range(GROUP_WIDTH // 128)], axis=1)
    y = y * (0.5 * (1.0 + jnp.tanh(math.sqrt(2.0 / math.pi) * (y + 0.044715 * (y * y * y)))))
    y = y * _sigmoid(_dot(y.astype(BF16), wglu_ref[...]) + bglu_ref[...])
    acc = x_ref[...] + _dot(y.astype(BF16), wo_ref[0:256, :])
    acc = acc + _dot(ret_ref[...], wo_ref[256:512, :])
    acc = acc + _dot(m2_ref[...], wo_ref[512:768, :])
    acc = acc + _dot(hg_ref[...], wo_ref[768:1024, :])
    x1_ref[...] = acc
    h2 = _rms(acc, g2_ref[...])
    _split_rows(h2_ref, h2, tile)
    hi, lo = _split2(h2)
    hw = _dot(hi, wrh_ref[...])
    logits = (hw[:, :ROUTE_LANES] + hw[:, ROUTE_LANES:] + _dot(lo, wrh_ref[:, :ROUTE_LANES])) + br_ref[...]
    e1, e2, w1, w2 = _route(logits)
    lane = lax.broadcasted_iota(jnp.int32, (tile, ROUTE_LANES), 1)
    picked = jnp.where((lane == e1) | (lane == e2), 1.0, 0.0)
    rank = carry_ref[...] + _dot(stri_ref[...], picked.astype(BF16))
    r1 = jnp.sum(jnp.where(lane == e1, rank, 0.0), axis=-1, keepdims=True)
    r2 = jnp.sum(jnp.where(lane == e2, rank, 0.0), axis=-1, keepdims=True)
    carry_ref[...] += jnp.sum(picked, axis=0, keepdims=True)
    cnt_ref[...] = carry_ref[...]
    rec = jnp.zeros((tile, ROUTE_LANES), F32)
    for col, val in ((ROUTE_E1, e1.astype(F32)), (ROUTE_E2, e2.astype(F32)), (ROUTE_R1, r1), (ROUTE_R2, r2),
                     (ROUTE_W1, w1), (ROUTE_W2, w2)):
        rec = jnp.where(lane == col, val, rec)
    route_ref[...] = rec


def _out_proj(x, y_s5, y_ret, y_m2, y_hg, w_glu, b_glu, w_out, g2, w_route, b_route):
    n, d = x.shape
    tile = min(SEQ_TILE, n)
    wr_packed = jnp.concatenate(_split2(w_route), axis=1)
    i = np.arange(tile)
    strict_lower = jnp.asarray(i[:, None] > i[None, :], dtype=BF16)
    return pl.pallas_call(
        functools.partial(_out_kernel, tile=tile),
        grid=(n // tile,),
        in_specs=[_rows(tile, d), _rows(tile // S5_CHUNK, S5_CHUNK * GROUP_WIDTH)] + [_rows(tile, GROUP_WIDTH)] * 3
        + [_full((256, 256)), _full((1, 256)), _full((d, d)), _full((1, d)),
           _full((d, 2 * ROUTE_LANES)), _full((1, ROUTE_LANES)), _full((tile, tile))],
        out_specs=[_rows(tile, d), _rows(ROW_SPLIT * tile, 128), _rows(tile, ROUTE_LANES), _full((1, ROUTE_LANES))],
        out_shape=[jax.ShapeDtypeStruct((n, d), F32), jax.ShapeDtypeStruct((ROW_SPLIT * n, 128), jnp.int32),
                   jax.ShapeDtypeStruct((n, ROUTE_LANES), F32), jax.ShapeDtypeStruct((1, ROUTE_LANES), F32)],
        scratch_shapes=[pltpu.VMEM((GROUP_WIDTH // 128, tile, 128), F32), pltpu.VMEM((1, ROUTE_LANES), F32)],
        compiler_params=_cparams("arbitrary"),
        name="out_proj_router",
    )(x, y_s5, y_ret, y_m2, y_hg, w_glu, b_glu, w_out, g2, wr_packed, b_route, strict_lower)


def _sc_mesh():
    return plsc.VectorSubcoreMesh(core_axis_name="core", subcore_axis_name="subcore")


def _sc_scatter2(src, idx_a, idx_b, n_out):
    n = src.shape[0]

    @functools.partial(pl.kernel, out_type=jax.ShapeDtypeStruct((n_out, 128), src.dtype), mesh=_sc_mesh(),
                       scratch_types=[])
    def scatter_kernel(x_hbm, ia_hbm, ib_hbm, o_hbm):
        def body(x_vmem, ia_vmem, ib_vmem):
            pltpu.sync_copy(x_vmem, o_hbm.at[ia_vmem.at[0]])
            pltpu.sync_copy(x_vmem, o_hbm.at[ib_vmem.at[0]])

        pltpu.emit_pipeline(
            body, grid=(n // SC_WINDOW,),
            in_specs=[pl.BlockSpec((SC_WINDOW, 128), index_map=lambda i: (i, 0)),
                      pl.BlockSpec((1, SC_WINDOW), index_map=lambda i: (i, 0)),
                      pl.BlockSpec((1, SC_WINDOW), index_map=lambda i: (i, 0))],
            out_specs=[],
            core_axis_name=("core", "subcore"), dimension_semantics=(pltpu.PARALLEL,),
        )(x_hbm, ia_hbm, ib_hbm)

    return scatter_kernel(src, idx_a, idx_b)


def _sc_gather2(table, idx_a, idx_b):
    n = idx_a.size
    sds = jax.ShapeDtypeStruct((n, 128), table.dtype)

    @functools.partial(pl.kernel, out_type=(sds, sds), mesh=_sc_mesh(), scratch_types=[])
    def gather_kernel(t_hbm, ia_hbm, ib_hbm, oa_hbm, ob_hbm):
        def body(ia_vmem, ib_vmem, oa_vmem, ob_vmem):
            pltpu.sync_copy(t_hbm.at[ia_vmem.at[0]], oa_vmem)
            pltpu.sync_copy(t_hbm.at[ib_vmem.at[0]], ob_vmem)

        pltpu.emit_pipeline(
            body, grid=(n // SC_WINDOW,),
            in_specs=[pl.BlockSpec((1, SC_WINDOW), index_map=lambda i: (i, 0)),
                      pl.BlockSpec((1, SC_WINDOW), index_map=lambda i: (i, 0))],
            out_specs=[pl.BlockSpec((SC_WINDOW, 128), index_map=lambda i: (i, 0)),
                       pl.BlockSpec((SC_WINDOW, 128), index_map=lambda i: (i, 0))],
            core_axis_name=("core", "subcore"), dimension_semantics=(pltpu.PARALLEL,),
        )(ia_hbm, ib_hbm, oa_hbm, ob_hbm)

    return gather_kernel(table, idx_a, idx_b)


def _index_kernel(route_ref, off_ref, fold_ref, spread_ref, i1_ref, i2_ref, *, tile):
    rec = route_ref[...]
    lane = lax.broadcasted_iota(jnp.int32, rec.shape, 1)
    off = off_ref[...]
    sub = lax.broadcasted_iota(jnp.int32, (tile * ROW_SPLIT // 128, 128), 1) % ROW_SPLIT
    for out_ref, e_col, r_col in ((i1_ref, ROUTE_E1, ROUTE_R1), (i2_ref, ROUTE_E2, ROUTE_R2)):
        e = rec[:, e_col:e_col + 1].astype(jnp.int32)
        pos = jnp.sum(jnp.where(lane == e, off, 0.0), axis=-1, keepdims=True) + rec[:, r_col:r_col + 1]
        hi = jnp.floor(pos * (1.0 / 256.0))
        lo = pos - 256.0 * hi
        moved = (256.0 * _dot(fold_ref[...], (hi * spread_ref[...]).astype(BF16))
                 + _dot(fold_ref[...], (lo * spread_ref[...]).astype(BF16)))
        out_ref[...] = moved.astype(jnp.int32) * ROW_SPLIT + sub


def _dispatch_plan(route, counts, n_tiles):
    n = route.shape[0]
    cnt = counts[0, :MOE_EXPERTS].astype(jnp.int32)
    padded = ((cnt + EXPERT_TILE - 1) // EXPERT_TILE) * EXPERT_TILE
    ends = jnp.cumsum(padded)
    off = ends - padded
    start = jnp.arange(n_tiles, dtype=jnp.int32) * EXPERT_TILE
    tile_expert = jnp.minimum(jnp.sum((start[:, None] >= ends[None, :]).astype(jnp.int32), axis=1), MOE_EXPERTS - 1)
    onehot = (tile_expert[:, None] == jnp.arange(MOE_EXPERTS, dtype=jnp.int32)[None, :]).astype(jnp.int32)
    seg_end = jnp.sum(onehot * (off + cnt)[None, :], axis=1)
    tile_rows = jnp.clip(seg_end - start, 0, EXPERT_TILE).astype(jnp.int32)
    prev = jnp.concatenate([jnp.full((1,), -1, jnp.int32), tile_expert[:-1]])
    tile_first = ((tile_expert != prev) & (tile_rows > 0)).astype(jnp.int32)

    tile = min(SEQ_TILE, n)
    off_row = jnp.zeros((1, ROUTE_LANES), F32).at[0, :MOE_EXPERTS].set(off.astype(F32))
    per_row = 128 // ROW_SPLIT
    t = np.arange(tile)
    out_rows = tile // per_row
    fold = jnp.asarray(np.arange(out_rows)[:, None] == (t // per_row)[None, :], dtype=BF16)
    spread = jnp.asarray((t % per_row)[:, None] == (np.arange(128) // ROW_SPLIT)[None, :], dtype=F32)
    idx1, idx2 = pl.pallas_call(
        functools.partial(_index_kernel, tile=tile),
        grid=(n // tile,),
        in_specs=[_rows(tile, ROUTE_LANES), _full((1, ROUTE_LANES)), _full(fold.shape), _full(spread.shape)],
        out_specs=[_rows(out_rows, 128)] * 2,
        out_shape=[jax.ShapeDtypeStruct((n // per_row, 128), jnp.int32)] * 2,
        compiler_params=_cparams("parallel"),
        name="dispatch_index",
    )(route, off_row, fold, spread)
    return idx1, idx2, tile_expert, tile_rows, tile_first


def _experts_kernel(te_ref, rows_ref, first_ref, xs_ref, wg_ref, wu_ref, wd_ref, y_ref, wgb, wub, wdb):
    i = pl.program_id(0)

    @pl.when(first_ref[i] == 1)
    def _():
        wgb[...] = wg_ref[0, 0].astype(BF16)
        wub[...] = wu_ref[0, 0].astype(BF16)
        wdb[...] = wd_ref[0, 0].astype(BF16)

    @pl.when(rows_ref[i] > 0)
    def _():
        x = _merge_rows(xs_ref, EXPERT_TILE)
        row = lax.broadcasted_iota(jnp.int32, x.shape, 0)
        x = jnp.where(row < rows_ref[i], x, 0.0).astype(BF16)
        act = _silu(_dot(x, wgb[...])) * _dot(x, wub[...])
        _split_rows(y_ref, _dot(act.astype(BF16), wdb[...]), EXPERT_TILE)


def _experts(xs, tile_expert, tile_rows, tile_first, w_gate, w_up, w_down, layer):
    n_tiles = tile_expert.shape[0]
    _, _, d, ff = w_gate.shape
    rows_blk = pl.BlockSpec((ROW_SPLIT * EXPERT_TILE, 128), lambda i, te, rows, first: (i, 0))
    return pl.pallas_call(
        _experts_kernel,
        grid_spec=pltpu.PrefetchScalarGridSpec(
            num_scalar_prefetch=3,
            grid=(n_tiles,),
            in_specs=[rows_blk,
                      pl.BlockSpec((1, 1, d, ff), lambda i, te, rows, first: (layer, te[i], 0, 0)),
                      pl.BlockSpec((1, 1, d, ff), lambda i, te, rows, first: (layer, te[i], 0, 0)),
                      pl.BlockSpec((1, 1, ff, d), lambda i, te, rows, first: (layer, te[i], 0, 0))],
            out_specs=rows_blk,
            scratch_shapes=[pltpu.VMEM((d, ff), BF16), pltpu.VMEM((d, ff), BF16), pltpu.VMEM((ff, d), BF16)],
        ),
        out_shape=jax.ShapeDtypeStruct(xs.shape, xs.dtype),
        compiler_params=_cparams("arbitrary"),
        name="moe_experts",
    )(tile_expert, tile_rows, tile_first, xs, w_gate, w_up, w_down)


def _combine_kernel(x1_ref, g1_ref, g2_ref, route_ref, gf_ref, o_ref, *, tile, final_norm):
    route = route_ref[...]
    out = (x1_ref[...] + route[:, ROUTE_W1:ROUTE_W1 + 1] * _merge_rows(g1_ref, tile)
           + route[:, ROUTE_W2:ROUTE_W2 + 1] * _merge_rows(g2_ref, tile))
    o_ref[...] = _rms(out, gf_ref[...]) if final_norm else out


def _combine(x1, g1, g2, route, g_final, final_norm):
    n, d = x1.shape
    tile = min(SEQ_TILE, n)
    return pl.pallas_call(
        functools.partial(_combine_kernel, tile=tile, final_norm=final_norm),
        grid=(n // tile,),
        in_specs=[_rows(tile, d), _rows(ROW_SPLIT * tile, 128), _rows(ROW_SPLIT * tile, 128),
                  _rows(tile, ROUTE_LANES), _full((1, d))],
        out_specs=_rows(tile, d),
        out_shape=jax.ShapeDtypeStruct((n, d), F32),
        compiler_params=_cparams("parallel"),
        name="moe_combine",
    )(x1, g1, g2, route, g_final)


def _moe(h2_rows, route, counts, x1, w_gate, w_up, w_down, layer, g_final, final_norm):
    n = x1.shape[0]
    n_tiles = (MOE_TOPK * n) // EXPERT_TILE + MOE_EXPERTS
    idx1, idx2, tile_expert, tile_rows, tile_first = _dispatch_plan(route, counts, n_tiles)
    xs = _sc_scatter2(h2_rows, idx1, idx2, ROW_SPLIT * n_tiles * EXPERT_TILE)
    ys = _experts(xs, tile_expert, tile_rows, tile_first, w_gate, w_up, w_down, layer)
    g1, g2 = _sc_gather2(ys, idx1, idx2)
    return _combine(x1, g1, g2, route, g_final, final_norm)


def kernel(x, positions, norm1_g, w_in, w_out, s5_a_re, s5_a_im, s5_log_dt, s5_b_re, s5_b_im, s5_c_re, s5_c_im, s5_d, s5_w_glu, s5_b_glu, m2_conv_w, m2_conv_b, m2_dt_bias, m2_a_log, m2_d, m2_norm_g, hg_lb_logits, hg_norm_g, norm2_g, moe_w_group, moe_b_group, moe_w_expert, moe_b_expert, moe_w_gate, moe_w_up, moe_w_down, final_norm_g):
    bsz, seqlen, d = x.shape
    assert bsz == 1 and seqlen % SEQ_TILE == 0 and (MOE_TOPK * seqlen) % EXPERT_TILE == 0
    depth = w_in.shape[0]

    lb_probs = jax.nn.softmax(hg_lb_logits.astype(F32), axis=0)
    lower_bounds = jnp.cumsum(lb_probs, axis=0) - lb_probs[0]
    cos_t, sin_t = _rope_tables(positions.reshape(seqlen, 1))

    xc = x.reshape(seqlen, d)
    for l in range(depth):
        u_b, u_f, p_ret, p_m2, p_hg = _in_proj(xc, norm1_g[l].reshape(1, d), _w_prep(w_in, l))
        y_s5 = _s5(u_b, u_f, s5_a_re[l], s5_a_im[l], s5_log_dt[l], s5_b_re[l], s5_b_im[l],
                   s5_c_re[l], s5_c_im[l], s5_d[l])
        y_ret = _retention(p_ret, cos_t, sin_t)
        y_m2 = _mamba2(p_m2, m2_conv_w[l], m2_conv_b[l], m2_dt_bias[l], m2_a_log[l], m2_d[l], m2_norm_g[l])
        y_hg = _hgrn2(p_hg, lower_bounds[l], hg_norm_g[l])
        w_route = jnp.zeros((d, ROUTE_LANES), F32)
        w_route = w_route.at[:, :MOE_GROUPS].set(moe_w_group[l])
        w_route = w_route.at[:, MOE_GROUPS:MOE_GROUPS + MOE_EXPERTS].set(moe_w_expert[l])
        b_route = jnp.zeros((1, ROUTE_LANES), F32)
        b_route = b_route.at[0, :MOE_GROUPS].set(moe_b_group[l])
        b_route = b_route.at[0, MOE_GROUPS:MOE_GROUPS + MOE_EXPERTS].set(moe_b_expert[l])
        x1, h2_rows, route, counts = _out_proj(xc, y_s5, y_ret, y_m2, y_hg, s5_w_glu[l].astype(BF16),
                                               s5_b_glu[l].reshape(1, -1), w_out[l].astype(BF16),
                                               norm2_g[l].reshape(1, d), w_route, b_route)
        xc = _moe(h2_rows, route, counts, x1, moe_w_gate, moe_w_up, moe_w_down, l,
                  final_norm_g.reshape(1, d), final_norm=(l == depth - 1))
    return xc.reshape(bsz, seqlen, d)
```

```python
import functools
import math

import numpy as np
import jax
import jax.numpy as jnp
from jax import lax
from jax.experimental import pallas as pl
from jax.experimental.pallas import tpu as pltpu
from jax.experimental.pallas import tpu_sc as plsc

F32 = jnp.float32
BF16 = jnp.bfloat16
NORM_EPS = 1e-6

GROUP_WIDTH = 256
HEAD_DIM = 64
N_HEADS = GROUP_WIDTH // HEAD_DIM
S5_GROUPS = 16
S5_CH = 16
S5_STATE = 64
S5_DT_CLAMP = -1e-4
M2_STATE = 128
M2_CONV = 4
M2_CONV_DIM = 768
ROPE_BASE = 10000.0
MOE_GROUPS = 4
MOE_PER_GROUP = 4
MOE_EXPERTS = 16
ROUTE_LANES = 128

SEQ_TILE = 512
CHUNK = 128
RET_CHUNK = 256
CHUNK_UNROLL = 4
S5_CHUNK = 16
MOE_TOPK = 2
EXPERT_TILE = 512
SC_WINDOW = 128
VMEM_LIMIT = 56 * 1024 * 1024


def _cparams(*sem):
    return pltpu.CompilerParams(dimension_semantics=sem, vmem_limit_bytes=VMEM_LIMIT)


def _dot(a, b):
    return jnp.dot(a, b, preferred_element_type=F32)


def _dot_nt(a, b):
    return lax.dot_general(a, b, (((1,), (1,)), ((), ())), preferred_element_type=F32)


def _dot_tn(a, b):
    return lax.dot_general(a, b, (((0,), (0,)), ((), ())), preferred_element_type=F32)


def _split2(x):
    hi = x.astype(BF16)
    return hi, (x - hi.astype(F32)).astype(BF16)


def _split3(x):
    hi = x.astype(BF16)
    r = x - hi.astype(F32)
    mid = r.astype(BF16)
    return hi, mid, (r - mid.astype(F32)).astype(BF16)


def _dot_exact_lhs(m, x):
    hi, mid, lo = _split3(x)
    return _dot(m, hi) + _dot(m, mid) + _dot(m, lo)


def _dot_exact_rhs(x, m):
    hi, lo = _split2(x)
    return _dot(hi, m) + _dot(lo, m)


def _sigmoid(x):
    return 1.0 / (1.0 + jnp.exp(-x))


def _silu(x):
    return x * _sigmoid(x)


def _rms(x, g):
    return x * lax.rsqrt(jnp.mean(x * x, axis=-1, keepdims=True) + NORM_EPS) * g


def _full(shape):
    return pl.BlockSpec(shape, lambda *_: (0,) * len(shape))


def _rows(tile, width):
    return pl.BlockSpec((tile, width), lambda i: (i, 0))


IN_SEGMENTS = (256, 1024, 1280, 1024)


DT_COL = 9 * GROUP_WIDTH
W_PREP_ROWS = 128


def _w_prep_kernel(w_ref, o_ref):
    o_ref[:, 0:DT_COL] = w_ref[0, :, 0:DT_COL].astype(BF16)
    head = lax.broadcasted_iota(jnp.int32, (W_PREP_ROWS, GROUP_WIDTH), 1) // HEAD_DIM
    rep = jnp.zeros((W_PREP_ROWS, GROUP_WIDTH), F32)
    for h in range(N_HEADS):
        rep = jnp.where(head == h, w_ref[0, :, DT_COL + h:DT_COL + h + 1], rep)
    o_ref[:, DT_COL:DT_COL + GROUP_WIDTH] = rep.astype(BF16)
    o_ref[:, DT_COL + GROUP_WIDTH:] = w_ref[0, :, DT_COL + N_HEADS:].astype(BF16)


def _w_prep(w_in, layer):
    _, d, n_in = w_in.shape
    return pl.pallas_call(
        _w_prep_kernel,
        grid=(d // W_PREP_ROWS,),
        in_specs=[pl.BlockSpec((1, W_PREP_ROWS, n_in), lambda i: (layer, i, 0))],
        out_specs=pl.BlockSpec((W_PREP_ROWS, sum(IN_SEGMENTS)), lambda i: (i, 0)),
        out_shape=jax.ShapeDtypeStruct((d, sum(IN_SEGMENTS)), BF16),
        compiler_params=_cparams("parallel"),
        name="w_in_prep",
    )(w_in)


def _in_proj_kernel(x_ref, g_ref, w_ref, ub_ref, uf_ref, ret_ref, m2_ref, hg_ref, u_tmp, *, tile):
    hb = _rms(x_ref[...], g_ref[...]).astype(BF16)
    c0, c1, c2, c3 = np.cumsum(IN_SEGMENTS)
    ret_ref[...] = _dot(hb, w_ref[:, c0:c1])
    m2_ref[...] = _dot(hb, w_ref[:, c1:c2])
    hg_ref[...] = _dot(hb, w_ref[:, c2:c3])
    u = _dot(hb, w_ref[:, 0:c0])
    for j in range(GROUP_WIDTH // 128):
        u_tmp[j] = u[:, 128 * j:128 * (j + 1)]
    for s in range(S5_CHUNK):
        for j in range(GROUP_WIDTH // 128):
            v = u_tmp[j, pl.ds(s, tile // S5_CHUNK, stride=S5_CHUNK), :]
            lanes = slice(GROUP_WIDTH * s + 128 * j, GROUP_WIDTH * s + 128 * (j + 1))
            uf_ref[:, lanes] = v
            ub_ref[:, lanes] = v.astype(BF16)


def _in_proj(x, g, w):
    n, d = x.shape
    tile = min(SEQ_TILE, n)
    fold = S5_CHUNK * GROUP_WIDTH
    return pl.pallas_call(
        functools.partial(_in_proj_kernel, tile=tile),
        grid=(n // tile,),
        in_specs=[_rows(tile, d), _full((1, d)), _full(w.shape)],
        out_specs=[_rows(tile // S5_CHUNK, fold)] * 2 + [_rows(tile, s) for s in IN_SEGMENTS[1:]],
        out_shape=[jax.ShapeDtypeStruct((n // S5_CHUNK, fold), BF16),
                   jax.ShapeDtypeStruct((n // S5_CHUNK, fold), F32)]
        + [jax.ShapeDtypeStruct((n, s), F32) for s in IN_SEGMENTS[1:]],
        scratch_shapes=[pltpu.VMEM((GROUP_WIDTH // 128, tile, 128), F32)],
        compiler_params=_cparams("parallel"),
        name="in_proj",
    )(x, g, w)


def _rope_kernel(pos_ref, invf_ref, cos_ref, sin_ref):
    ang = pos_ref[...].astype(F32) * invf_ref[...]
    cos_ref[...] = jnp.cos(ang)
    sin_ref[...] = jnp.sin(ang)


def _rope_tables(positions):
    n = positions.shape[0]
    tile = min(SEQ_TILE, n)
    half = HEAD_DIM // 2
    inv_freq = ROPE_BASE ** (-jnp.arange(half, dtype=F32) / half)
    invf = jnp.tile(inv_freq, 128 // half).reshape(1, 128)
    return pl.pallas_call(
        _rope_kernel,
        grid=(n // tile,),
        in_specs=[_rows(tile, 1), _full((1, 128))],
        out_specs=[_rows(tile, 128), _rows(tile, 128)],
        out_shape=[jax.ShapeDtypeStruct((n, 128), F32)] * 2,
        compiler_params=_cparams("parallel"),
        name="rope_tables",
    )(positions, invf)


def _head_mean_matrix():
    h = np.arange(GROUP_WIDTH) // HEAD_DIM
    return jnp.asarray((h[:, None] == h[None, :]) / HEAD_DIM, dtype=BF16)


def _head_block_mask():
    h = np.arange(GROUP_WIDTH) // HEAD_DIM
    return jnp.asarray(h[:, None] == h[None, :], dtype=F32)


HEAD_PAIRS = GROUP_WIDTH // 128


def _lanes(x, j):
    return x[:, 128 * j:128 * (j + 1)]


def _stack_pair(x):
    xb = x.astype(BF16)
    low = lax.broadcasted_iota(jnp.int32, x.shape, 1) < HEAD_DIM
    zero = jnp.zeros_like(xb)
    return jnp.concatenate([jnp.where(low, xb, zero), jnp.where(low, zero, xb)], axis=0)


def _pair_scores(q, kb):
    s = _dot_nt(_stack_pair(q), kb)
    return s[:q.shape[0]], s[q.shape[0]:]


def _pair_apply(a0, a1, v):
    return _dot(jnp.concatenate([a0.astype(BF16), a1.astype(BF16)], axis=1), _stack_pair(v))


def _ret_constants(chunk):
    lg = np.log1p(-np.exp2(-5.0 - np.arange(N_HEADS, dtype=np.float64)))
    idx = np.arange(chunk, dtype=np.float64)
    rel = idx[:, None] - idx[None, :]
    decay = np.where(rel >= 0, np.exp(np.maximum(rel, 0.0)[None] * lg[:, None, None]), 0.0)
    lane_lg = np.repeat(lg, HEAD_DIM)
    xi = np.exp((idx + 1.0)[:, None] * lane_lg[None, :])
    zeta = np.exp((chunk - 1.0 - idx)[:, None] * lane_lg[None, :])
    h = np.arange(GROUP_WIDTH) // HEAD_DIM
    gc = np.where(h[:, None] == h[None, :], np.exp(chunk * lane_lg)[:, None], 0.0)
    f = lambda a: jnp.asarray(a, dtype=F32)
    return f(decay), f(xi), f(zeta), f(gc)


def _ret_kernel(p_ref, cos_ref, sin_ref, dec_ref, xi_ref, zeta_ref, gc_ref, bm_ref, gm_ref,
                o_ref, r_ref, *, chunk, n_chunks):
    lane = lax.broadcasted_iota(jnp.int32, (chunk, GROUP_WIDTH), 1)
    first_half = (lane % HEAD_DIM) < (HEAD_DIM // 2)
    gm = gm_ref[...]

    def rope(t, cos2, sin2):
        rot = jnp.where(first_half, -pltpu.roll(t, GROUP_WIDTH - HEAD_DIM // 2, 1),
                        pltpu.roll(t, HEAD_DIM // 2, 1))
        return t * cos2 + rot * sin2

    def body(c, carry):
        rows = pl.ds(pl.multiple_of(c * chunk, chunk), chunk)
        cs = cos_ref[rows, :]
        sn = sin_ref[rows, :]
        cos2 = jnp.concatenate([cs, cs], axis=1)
        sin2 = jnp.concatenate([sn, sn], axis=1)
        q = rope(p_ref[rows, 0:256], cos2, sin2)
        k = rope(p_ref[rows, 256:512], cos2, sin2) * (HEAD_DIM ** -0.5)
        v = p_ref[rows, 512:768]
        g = p_ref[rows, 768:1024]
        kb = k.astype(BF16)
        inner = []
        for j in range(HEAD_PAIRS):
            s0, s1 = _pair_scores(_lanes(q, j), _lanes(kb, j))
            inner.append(_pair_apply(s0 * dec_ref[2 * j], s1 * dec_ref[2 * j + 1], _lanes(v, j)))
        inner = jnp.concatenate(inner, axis=1)
        r_prev = r_ref[...]
        cross = _dot((q * xi_ref[...]).astype(BF16), r_prev.astype(BF16))
        r_ref[...] = gc_ref[...] * r_prev + bm_ref[...] * _dot_tn(kb, (zeta_ref[...] * v).astype(BF16))
        o = inner + cross
        cen = o - _dot_exact_rhs(o, gm)
        var = _dot_exact_rhs(cen * cen, gm)
        o_ref[rows, :] = (cen * lax.rsqrt(var + NORM_EPS) * _silu(g)).astype(BF16)
        return carry

    lax.fori_loop(0, n_chunks, body, 0, unroll=CHUNK_UNROLL)


def _retention_call(proj, cos_t, sin_t, tile, chunk):
    decay, xi, zeta, gc = _ret_constants(chunk)
    body = functools.partial(_ret_kernel, chunk=chunk, n_chunks=tile // chunk)
    operands = (proj, cos_t, sin_t, decay, xi, zeta, gc, _head_block_mask(), _head_mean_matrix())
    specs = [_rows(tile, 1024), _rows(tile, 128), _rows(tile, 128), _full(decay.shape), _full(xi.shape),
             _full(zeta.shape), _full(gc.shape), _full((256, 256)), _full((256, 256))]
    return body, operands, specs, [pltpu.VMEM((GROUP_WIDTH, GROUP_WIDTH), F32)]


def _tri_matrix(chunk):
    i = np.arange(chunk)
    return jnp.asarray(i[:, None] >= i[None, :], dtype=BF16)


def _m2_kernel(p_ref, cw_ref, cb_ref, dtb_ref, alog_ref, d_ref, ng_ref, tri_ref,
               o_ref, tail_ref, ext_ref, act_ref, st_ref, *, tile, chunk, n_chunks):
    ext_ref[0:8, :] = tail_ref[...]
    ext_ref[8:tile + 8, :] = p_ref[:, 256:1024]
    tail_ref[...] = p_ref[tile - 8:tile, 256:1024]
    conv = cb_ref[...]
    for j in range(M2_CONV):
        lo = 8 - (M2_CONV - 1) + j
        conv = conv + cw_ref[j:j + 1, :] * ext_ref[lo:lo + tile, :]
    act_ref[...] = _silu(conv)
    a_lane = -jnp.exp(alog_ref[...])
    tri = tri_ref[...]
    ti = lax.broadcasted_iota(jnp.int32, (chunk, chunk), 0)
    si = lax.broadcasted_iota(jnp.int32, (chunk, chunk), 1)
    causal = ti >= si
    lane = lax.broadcasted_iota(jnp.int32, (chunk, 128), 1)

    def body(c, carry):
        start = pl.multiple_of(c * chunk, chunk)
        rows = pl.ds(start, chunk)
        xbc = act_ref[rows, :]
        xs = xbc[:, 0:256]
        z = p_ref[rows, 0:256]
        x_dt = p_ref[rows, 1024:1280] + dtb_ref[...]
        dt = jnp.maximum(x_dt, 0.0) + jnp.log1p(jnp.exp(-jnp.abs(x_dt)))
        acum = _dot_exact_lhs(tri, dt * a_lane)
        acum_t = acum.T
        a_last = acum[chunk - 1:chunk, :]
        e_acum = jnp.exp(acum)
        decs = jnp.exp(a_last - acum)
        d_chunk = jnp.exp(a_last)
        xc = xs * dt
        ys = []
        for g in range(2):
            sl = slice(128 * g, 128 * (g + 1))
            bmg = xbc[:, 256 + 128 * g:256 + 128 * (g + 1)].astype(BF16)
            cmg = xbc[:, 512 + 128 * g:512 + 128 * (g + 1)].astype(BF16)
            cb = _dot_nt(cmg, bmg)
            xcg = xc[:, sl]
            yd = jnp.zeros((chunk, 128), F32)
            for hh in range(2):
                col0 = 128 * g + HEAD_DIM * hh
                diff = acum[:, col0:col0 + 1] - acum_t[col0:col0 + 1, :]
                lm = jnp.where(causal, jnp.exp(jnp.where(causal, diff, 0.0)), 0.0)
                xm = jnp.where((lane // HEAD_DIM) == hh, xcg, 0.0).astype(BF16)
                yd = yd + _dot((cb * lm).astype(BF16), xm)
            st = st_ref[:, sl]
            y_off = _dot(cmg, st.astype(BF16)) * e_acum[:, sl]
            st_ref[:, sl] = d_chunk[:, sl] * st + _dot_tn(bmg, (xcg * decs[:, sl]).astype(BF16))
            ys.append(yd + y_off + d_ref[:, sl] * xs[:, sl])
        y = jnp.concatenate(ys, axis=1) * _silu(z)
        o_ref[rows, :] = _rms(y, ng_ref[...]).astype(BF16)
        return carry

    lax.fori_loop(0, n_chunks, body, 0, unroll=CHUNK_UNROLL)


def _mamba2_call(proj, conv_w, conv_b, dt_bias, a_log, d_skip, norm_g, tile, chunk):
    lanes = lambda v: jnp.repeat(v, HEAD_DIM).reshape(1, GROUP_WIDTH)
    body = functools.partial(_m2_kernel, tile=tile, chunk=chunk, n_chunks=tile // chunk)
    operands = (proj, conv_w, conv_b.reshape(1, -1), lanes(dt_bias), lanes(a_log), lanes(d_skip),
                norm_g.reshape(1, -1), _tri_matrix(chunk))
    specs = [_rows(tile, 1280), _full((M2_CONV, M2_CONV_DIM)), _full((1, M2_CONV_DIM)),
             _full((1, 256)), _full((1, 256)), _full((1, 256)), _full((1, 256)), _full((chunk, chunk))]
    scratch = [pltpu.VMEM((8, M2_CONV_DIM), F32), pltpu.VMEM((tile + 8, M2_CONV_DIM), F32),
               pltpu.VMEM((tile, M2_CONV_DIM), F32), pltpu.VMEM((M2_STATE, GROUP_WIDTH), F32)]
    return body, operands, specs, scratch


HG_MATMUL_LEVELS = 3


def _hg_exponent_matrix(chunk):
    levels = HG_MATMUL_LEVELS
    t = np.arange(chunk)[:, None]
    r = np.arange(chunk)[None, :]
    blocks = []
    for lvl in range(levels):
        b = 1 << lvl
        blk = t // b
        odd = (blk % 2) == 1
        q_rows = odd & (r >= blk * b) & (r <= t)
        k_rows = (~odd) & (r > t) & (r <= (blk + 1) * b - 1)
        blocks.append(q_rows | k_rows)
    blocks.append(r <= t)
    return jnp.asarray(np.concatenate(blocks, axis=0), dtype=BF16)


def _hg_kernel(p_ref, lb_ref, ng_ref, gexp_ref, bm_ref, gm_ref, o_ref, st_ref, *, chunk, n_chunks):
    levels = int(math.log2(chunk))
    row = lax.broadcasted_iota(jnp.int32, (chunk, 128), 0)
    odd_rows = [((row >> lvl) & 1) == 1 for lvl in range(levels)]
    row_wide = lax.broadcasted_iota(jnp.int32, (chunk, GROUP_WIDTH), 0)
    odd_rows_wide = [((row_wide >> lvl) & 1) == 1 for lvl in range(levels)]
    ti = lax.broadcasted_iota(jnp.int32, (chunk, chunk), 0)
    si = lax.broadcasted_iota(jnp.int32, (chunk, chunk), 1)
    pair_level = [((ti >> (lvl + 1)) == (si >> (lvl + 1))) & (((ti >> lvl) & 1) == 1) & (((si >> lvl) & 1) == 0)
                  for lvl in range(levels)]
    lb = lb_ref[...]
    gm = gm_ref[...]

    def body(c, carry):
        rows = pl.ds(pl.multiple_of(c * chunk, chunk), chunk)
        q = _silu(p_ref[rows, 0:256])
        forget = lb + (1.0 - lb) * _sigmoid(p_ref[rows, 256:512])
        k = 1.0 - forget
        v = p_ref[rows, 512:768]
        g = p_ref[rows, 768:1024]
        lf_hi, lf_lo = _split2(jnp.log(forget))
        expo = _dot(gexp_ref[...], lf_hi) + _dot(gexp_ref[...], lf_lo)
        bcum = expo[HG_MATMUL_LEVELS * chunk:(HG_MATMUL_LEVELS + 1) * chunk, :]

        def level_log_decay(lvl):
            if lvl < HG_MATMUL_LEVELS:
                return expo[lvl * chunk:(lvl + 1) * chunk, :]
            b = 1 << lvl
            ref = jnp.concatenate([jnp.broadcast_to(bcum[m + b - 1:m + b, :], (2 * b, GROUP_WIDTH))
                                   for m in range(0, chunk, 2 * b)], axis=0)
            return jnp.where(odd_rows_wide[lvl], bcum - ref, ref - bcum)


        log_decay = [level_log_decay(lvl) for lvl in range(levels)]
        intra = []
        for j in range(HEAD_PAIRS):
            qj, kj = _lanes(q, j), _lanes(k, j)
            a0, a1 = (jnp.where(ti == si, s, 0.0) for s in _pair_scores(qj, kj.astype(BF16)))
            for lvl in range(levels):
                w = jnp.exp(_lanes(log_decay[lvl], j)) * jnp.where(odd_rows[lvl], qj, kj)
                s0, s1 = _pair_scores(w, w.astype(BF16))
                a0 = jnp.where(pair_level[lvl], s0, a0)
                a1 = jnp.where(pair_level[lvl], s1, a1)
            intra.append(_pair_apply(a0, a1, _lanes(v, j)))
        intra = jnp.concatenate(intra, axis=1)

        b_last = bcum[chunk - 1:chunk, :]
        suffix = b_last - bcum
        st = st_ref[...]
        cross = _dot_nt((q * jnp.exp(bcum)).astype(BF16), st.astype(BF16))
        st_ref[...] = jnp.exp(b_last) * st + bm_ref[...] * _dot_tn(
            v.astype(BF16), (k * jnp.exp(suffix)).astype(BF16))
        o = intra + cross
        o = o * lax.rsqrt(_dot_exact_rhs(o * o, gm) + NORM_EPS) * ng_ref[...]
        o_ref[rows, :] = (o * _silu(g)).astype(BF16)
        return carry

    lax.fori_loop(0, n_chunks, body, 0, unroll=CHUNK_UNROLL)


def _hgrn2_call(proj, lower_bound, norm_g, tile, chunk):
    gexp = _hg_exponent_matrix(chunk)
    body = functools.partial(_hg_kernel, chunk=chunk, n_chunks=tile // chunk)
    operands = (proj, lower_bound.reshape(1, -1), norm_g.reshape(1, -1), gexp, _head_block_mask(),
                _head_mean_matrix())
    specs = [_rows(tile, 1024), _full((1, 256)), _full((1, 256)), _full(gexp.shape),
             _full((256, 256)), _full((256, 256))]
    return body, operands, specs, [pltpu.VMEM((GROUP_WIDTH, GROUP_WIDTH), F32)]


def _mixers_kernel(*refs, bodies, n_in, n_scratch):
    n_mix = len(bodies)
    ins, pos = [], 0
    for k in n_in:
        ins.append(refs[pos:pos + k])
        pos += k
    outs = refs[pos:pos + n_mix]
    pos += n_mix
    scratch = []
    for k in n_scratch:
        scratch.append(refs[pos:pos + k])
        pos += k

    @pl.when(pl.program_id(0) == 0)
    def _():
        for group in scratch:
            for ref in group:
                ref[...] = jnp.zeros_like(ref)

    for body, i, o, s in zip(bodies, ins, outs, scratch):
        body(*i, o, *s)


def _mixers(n, *calls):
    tile = min(SEQ_TILE, n)
    bodies = tuple(c[0] for c in calls)
    return pl.pallas_call(
        functools.partial(_mixers_kernel, bodies=bodies, n_in=tuple(len(c[1]) for c in calls),
                          n_scratch=tuple(len(c[3]) for c in calls)),
        grid=(n // tile,),
        in_specs=[s for c in calls for s in c[2]],
        out_specs=[_rows(tile, GROUP_WIDTH)] * len(calls),
        out_shape=[jax.ShapeDtypeStruct((n, GROUP_WIDTH), BF16)] * len(calls),
        scratch_shapes=[s for c in calls for s in c[3]],
        compiler_params=_cparams("arbitrary"),
        name="mixers",
    )(*[a for c in calls for a in c[1]])


S5_LANES = S5_GROUPS * 2 * S5_STATE


def _s5_rows(a_re, a_im, log_dt):
    are = jnp.minimum(a_re, S5_DT_CLAMP)
    dt = jnp.exp(log_dt)
    lam_re = are * dt
    lam_im = a_im * dt
    mag = jnp.exp(lam_re)
    ab_re = mag * jnp.cos(lam_im)
    ab_im = mag * jnp.sin(lam_im)
    den = are * are + a_im * a_im
    k_re = ((ab_re - 1.0) * are + ab_im * a_im) / den
    k_im = (ab_im * are - (ab_re - 1.0) * a_im) / den
    return lam_re, lam_im, k_re, k_im


def _s5_power(lam_re, lam_im, e):
    m = jnp.exp(e * lam_re)
    return m * jnp.cos(e * lam_im), m * jnp.sin(e * lam_im)


def _s5_state_kernel(u_ref, are_ref, aim_ref, ldt_ref, b1_ref, b2_ref, ca_ref, k_ref, sp_ref,
                     bb1_ref, bb2_ref, inc_ref, *, n_chunks):
    s = pl.program_id(0)
    lam_re, lam_im, k_re, k_im = _s5_rows(are_ref[...], aim_ref[...], ldt_ref[...])

    @pl.when(s == 0)
    def _():
        bb1_ref[...] = k_re * b1_ref[...] + k_im * b2_ref[...]
        bb2_ref[...] = k_re * b2_ref[...] - k_im * b1_ref[...]

    p_re, p_im = _s5_power(lam_re, lam_im, (S5_CHUNK - 1 - s).astype(F32))
    w = (p_re * bb1_ref[...] + p_im * bb2_ref[...]).astype(BF16)
    k_ref[0] = _dot_nt(w, ca_ref[...].astype(BF16)).astype(BF16)
    contrib = _dot(u_ref[...], w)

    @pl.when(s == 0)
    def _():
        inc_ref[...] = contrib

    @pl.when(s > 0)
    def _():
        inc_ref[...] += contrib

    @pl.when(s == S5_CHUNK - 1)
    def _():
        n_steps = max(1, int(math.ceil(math.log2(n_chunks))))
        step = lax.broadcasted_iota(jnp.int32, (16, S5_LANES), 0)
        e = (jnp.left_shift(1, step) * S5_CHUNK).astype(F32)
        a_re_all, a_im_all = _s5_power(lam_re, lam_im, e)
        row = lax.broadcasted_iota(jnp.int32, (n_chunks, 128), 0)
        half = S5_LANES // 2

        def shifted(x, sh):
            return jnp.where(row >= sh, pltpu.roll(x, sh, 0), 0.0)

        for j in range(half // 128):
            re_l, im_l = slice(128 * j, 128 * (j + 1)), slice(half + 128 * j, half + 128 * (j + 1))
            x_re, x_im = inc_ref[:, re_l], inc_ref[:, im_l]
            for k in range(n_steps):
                sh = 1 << k
                if sh >= n_chunks:
                    break
                a_re, a_im = a_re_all[k:k + 1, re_l], a_im_all[k:k + 1, re_l]
                p_re, p_im = shifted(x_re, sh), shifted(x_im, sh)
                x_re, x_im = x_re + a_re * p_re - a_im * p_im, x_im + a_re * p_im + a_im * p_re
            sp_ref[:, re_l] = shifted(x_re, 1).astype(BF16)
            sp_ref[:, im_l] = shifted(x_im, 1).astype(BF16)


def _s5_out_kernel(ub_ref, uf_ref, k_ref, sp_ref, are_ref, aim_ref, ldt_ref, ca_ref, cb_ref, d_ref,
                   y_ref, taps_ref):
    t = pl.program_id(0)
    fold = S5_CHUNK * GROUP_WIDTH

    @pl.when(t == 0)
    def _():
        for j in range(S5_CHUNK):
            taps_ref[GROUP_WIDTH * j:GROUP_WIDTH * (j + 1), :] = k_ref[j]
        taps_ref[fold:, :] = jnp.zeros((fold - GROUP_WIDTH, GROUP_WIDTH), BF16)

    lam_re, lam_im, _, _ = _s5_rows(are_ref[...], aim_ref[...], ldt_ref[...])
    p_re, p_im = _s5_power(lam_re, lam_im, (t + 1).astype(F32))
    w_out = (p_re * ca_ref[...] + p_im * cb_ref[...]).astype(BF16)
    start = pl.multiple_of((S5_CHUNK - 1 - t) * GROUP_WIDTH, GROUP_WIDTH)
    y_ref[...] = (_dot(ub_ref[...], taps_ref[pl.ds(start, fold), :]) + _dot_nt(sp_ref[...], w_out)
                  + d_ref[...] * uf_ref[...])


def _s5_embed(re, im):
    eye = jnp.eye(S5_GROUPS, dtype=F32)
    blocks = [(eye[:, None, :, None] * x[:, :, None, :]).reshape(S5_GROUPS * S5_CH, S5_LANES // 2) for x in (re, im)]
    return jnp.concatenate(blocks, axis=1)


def _s5(u_b, u_f, a_re, a_im, log_dt, b_re, b_im, c_re, c_im, d_skip):
    n_chunks, fold = u_b.shape
    row = lambda v: jnp.tile(v.reshape(1, S5_LANES // 2), (1, 2))
    are, aim = row(a_re), row(a_im)
    ldt = row(jnp.repeat(log_dt, S5_STATE))
    bt_re, bt_im = b_re.transpose(0, 2, 1), b_im.transpose(0, 2, 1)
    b1, b2 = _s5_embed(bt_re, bt_im), _s5_embed(-bt_im, bt_re)
    ca, cb = _s5_embed(c_re, -c_im), _s5_embed(-c_im, -c_re)
    col = lambda: pl.BlockSpec((n_chunks, GROUP_WIDTH), lambda s: (0, s))
    taps, s_prev = pl.pallas_call(
        functools.partial(_s5_state_kernel, n_chunks=n_chunks),
        grid=(S5_CHUNK,),
        in_specs=[col()] + [_full((1, S5_LANES))] * 3 + [_full((GROUP_WIDTH, S5_LANES))] * 3,
        out_specs=[pl.BlockSpec((1, GROUP_WIDTH, GROUP_WIDTH), lambda s: (s, 0, 0)),
                   _full((n_chunks, S5_LANES))],
        out_shape=[jax.ShapeDtypeStruct((S5_CHUNK, GROUP_WIDTH, GROUP_WIDTH), BF16),
                   jax.ShapeDtypeStruct((n_chunks, S5_LANES), BF16)],
        scratch_shapes=[pltpu.VMEM((GROUP_WIDTH, S5_LANES), F32), pltpu.VMEM((GROUP_WIDTH, S5_LANES), F32),
                        pltpu.VMEM((n_chunks, S5_LANES), F32)],
        compiler_params=_cparams("arbitrary"),
        name="s5_state",
    )(u_b, are, aim, ldt, b1, b2, ca)
    return pl.pallas_call(
        _s5_out_kernel,
        grid=(S5_CHUNK,),
        in_specs=[_full((n_chunks, fold)), col(), _full(taps.shape), _full(s_prev.shape)]
        + [_full((1, S5_LANES))] * 3 + [_full((GROUP_WIDTH, S5_LANES))] * 2 + [_full((1, GROUP_WIDTH))],
        out_specs=col(),
        out_shape=jax.ShapeDtypeStruct((n_chunks, fold), F32),
        scratch_shapes=[pltpu.VMEM(((2 * S5_CHUNK - 1) * GROUP_WIDTH, GROUP_WIDTH), BF16)],
        compiler_params=_cparams("arbitrary"),
        name="s5_out",
    )(u_b, u_f, taps, s_prev, are, aim, ldt, ca, cb, d_skip.reshape(1, GROUP_WIDTH))


EXPERT_ROW = 8


def _route(logits_t):
    tokens = logits_t.shape[1]
    big = jnp.int32(1 << 20)
    neg = jnp.float32(-jnp.inf)
    g_row = lax.broadcasted_iota(jnp.int32, (8, tokens), 0)
    is_group = g_row < MOE_GROUPS
    gl = jnp.where(is_group, logits_t[0:8, :], neg)
    ge = jnp.where(is_group, jnp.exp(gl - jnp.max(gl, axis=0, keepdims=True)), 0.0)
    gp = ge / jnp.sum(ge, axis=0, keepdims=True)
    p_g = jnp.max(gp, axis=0, keepdims=True)
    g_idx = jnp.min(jnp.where(is_group & (gp == p_g), g_row, big), axis=0, keepdims=True)
    e_row = lax.broadcasted_iota(jnp.int32, (MOE_EXPERTS, tokens), 0)
    in_group = (e_row // MOE_PER_GROUP) == g_idx
    el = jnp.where(in_group, logits_t[EXPERT_ROW:EXPERT_ROW + MOE_EXPERTS, :], neg)
    ee = jnp.where(in_group, jnp.exp(el - jnp.max(el, axis=0, keepdims=True)), 0.0)
    ep = ee / jnp.sum(ee, axis=0, keepdims=True)
    p1 = jnp.max(jnp.where(in_group, ep, -1.0), axis=0, keepdims=True)
    i1 = jnp.min(jnp.where(in_group & (ep == p1), e_row, big), axis=0, keepdims=True)
    rest = in_group & (e_row != i1)
    p2 = jnp.max(jnp.where(rest, ep, -1.0), axis=0, keepdims=True)
    i2 = jnp.min(jnp.where(rest & (ep == p2), e_row, big), axis=0, keepdims=True)
    tot = p1 + p2
    return i1, i2, p_g * p1 / tot, p_g * p2 / tot


ROW_SPLIT = 4
ROUTE_E1, ROUTE_E2, ROUTE_R1, ROUTE_R2, ROUTE_W1, ROUTE_W2 = range(6)
HIGH_HALF = 0xFFFF0000


def _split_rows(ref, value, rows):
    half = value.shape[1] // 2
    lo = lax.bitcast_convert_type(value[:, :half].astype(jnp.bfloat16).astype(F32), jnp.uint32)
    hi = lax.bitcast_convert_type(value[:, half:].astype(jnp.bfloat16).astype(F32), jnp.uint32)
    words = lax.bitcast_convert_type((lo >> 16) | (hi & jnp.uint32(HIGH_HALF)), jnp.int32)
    for j in range(ROW_SPLIT):
        ref[pl.ds(j, rows, stride=ROW_SPLIT), :] = words[:, 128 * j:128 * (j + 1)]


def _merge_rows(ref, rows):
    words = jnp.concatenate([ref[pl.ds(j, rows, stride=ROW_SPLIT), :] for j in range(ROW_SPLIT)], axis=1)
    words = lax.bitcast_convert_type(words, jnp.uint32)
    lo = lax.bitcast_convert_type(words << 16, F32)
    hi = lax.bitcast_convert_type(words & jnp.uint32(HIGH_HALF), F32)
    return jnp.concatenate([lo, hi], axis=1)


def _out_kernel(x_ref, s5_ref, ret_ref, m2_ref, hg_ref, wglu_ref, bglu_ref, wo_ref, g2_ref,
                wrh_ref, br_ref, stri_ref, x1_ref, h2_ref, route_ref, cnt_ref, s5_tmp, carry_ref,
                *, tile):
    @pl.when(pl.program_id(0) == 0)
    def _():
        carry_ref[...] = jnp.zeros_like(carry_ref)

    for s in range(S5_CHUNK):
        for j in range(GROUP_WIDTH // 128):
            lanes = slice(GROUP_WIDTH * s + 128 * j, GROUP_WIDTH * s + 128 * (j + 1))
            s5_tmp[j, pl.ds(s, tile // S5_CHUNK, stride=S5_CHUNK), :] = s5_ref[:, lanes]
    y = jnp.concatenate([s5_tmp[j] for j in range(GROUP_WIDTH // 128)], axis=1)
    y = y * (0.5 * (1.0 + jnp.tanh(math.sqrt(2.0 / math.pi) * (y + 0.044715 * (y * y * y)))))
    y = y * _sigmoid(_dot(y.astype(BF16), wglu_ref[...]) + bglu_ref[...])
    acc = x_ref[...] + _dot(y.astype(BF16), wo_ref[0:256, :])
    acc = acc + _dot(ret_ref[...], wo_ref[256:512, :])
    acc = acc + _dot(m2_ref[...], wo_ref[512:768, :])
    acc = acc + _dot(hg_ref[...], wo_ref[768:1024, :])
    x1_ref[...] = acc
    h2 = _rms(acc, g2_ref[...])
    _split_rows(h2_ref, h2, tile)
    hi, lo = _split2(h2)
    hw = _dot(hi, wrh_ref[...])
    logits = (hw[:, :ROUTE_LANES] + hw[:, ROUTE_LANES:] + _dot(lo, wrh_ref[:, :ROUTE_LANES])) + br_ref[...]
    e1, e2, w1, w2 = _route(logits.T)
    e_row = lax.broadcasted_iota(jnp.int32, (MOE_EXPERTS, tile), 0)
    picked = jnp.where((e_row == e1) | (e_row == e2), 1.0, 0.0)
    rank = carry_ref[:, 0:1] + _dot_nt(picked.astype(BF16), stri_ref[...])
    r1 = jnp.sum(jnp.where(e_row == e1, rank, 0.0), axis=0, keepdims=True)
    r2 = jnp.sum(jnp.where(e_row == e2, rank, 0.0), axis=0, keepdims=True)
    carry_ref[...] += jnp.sum(picked, axis=1, keepdims=True)
    cnt_ref[...] = carry_ref[...]
    rec_row = lax.broadcasted_iota(jnp.int32, (ROUTE_LANES, tile), 0)
    rec = jnp.zeros((ROUTE_LANES, tile), F32)
    for col, val in ((ROUTE_E1, e1.astype(F32)), (ROUTE_E2, e2.astype(F32)), (ROUTE_R1, r1), (ROUTE_R2, r2),
                     (ROUTE_W1, w1), (ROUTE_W2, w2)):
        rec = jnp.where(rec_row == col, val, rec)
    route_ref[...] = rec.T


def _out_proj(x, y_s5, y_ret, y_m2, y_hg, w_glu, b_glu, w_out, g2, w_route, b_route):
    n, d = x.shape
    tile = min(SEQ_TILE, n)
    wr_packed = jnp.concatenate(_split2(w_route), axis=1)
    i = np.arange(tile)
    strict_lower = jnp.asarray(i[:, None] > i[None, :], dtype=BF16)
    return pl.pallas_call(
        functools.partial(_out_kernel, tile=tile),
        grid=(n // tile,),
        in_specs=[_rows(tile, d), _rows(tile // S5_CHUNK, S5_CHUNK * GROUP_WIDTH)] + [_rows(tile, GROUP_WIDTH)] * 3
        + [_full((256, 256)), _full((1, 256)), _full((d, d)), _full((1, d)),
           _full((d, 2 * ROUTE_LANES)), _full((1, ROUTE_LANES)), _full((tile, tile))],
        out_specs=[_rows(tile, d), _rows(ROW_SPLIT * tile, 128), _rows(tile, ROUTE_LANES),
                   _full((MOE_EXPERTS, ROUTE_LANES))],
        out_shape=[jax.ShapeDtypeStruct((n, d), F32), jax.ShapeDtypeStruct((ROW_SPLIT * n, 128), jnp.int32),
                   jax.ShapeDtypeStruct((n, ROUTE_LANES), F32), jax.ShapeDtypeStruct((MOE_EXPERTS, ROUTE_LANES), F32)],
        scratch_shapes=[pltpu.VMEM((GROUP_WIDTH // 128, tile, 128), F32),
                        pltpu.VMEM((MOE_EXPERTS, ROUTE_LANES), F32)],
        compiler_params=_cparams("arbitrary"),
        name="out_proj_router",
    )(x, y_s5, y_ret, y_m2, y_hg, w_glu, b_glu, w_out, g2, wr_packed, b_route, strict_lower)


def _sc_mesh():
    return plsc.VectorSubcoreMesh(core_axis_name="core", subcore_axis_name="subcore")


def _sc_scatter2(src, idx_a, idx_b, n_out):
    n = src.shape[0]

    @functools.partial(pl.kernel, out_type=jax.ShapeDtypeStruct((n_out, 128), src.dtype), mesh=_sc_mesh(),
                       scratch_types=[])
    def scatter_kernel(x_hbm, ia_hbm, ib_hbm, o_hbm):
        def body(x_vmem, ia_vmem, ib_vmem):
            pltpu.sync_copy(x_vmem, o_hbm.at[ia_vmem.at[0]])
            pltpu.sync_copy(x_vmem, o_hbm.at[ib_vmem.at[0]])

        pltpu.emit_pipeline(
            body, grid=(n // SC_WINDOW,),
            in_specs=[pl.BlockSpec((SC_WINDOW, 128), index_map=lambda i: (i, 0)),
                      pl.BlockSpec((1, SC_WINDOW), index_map=lambda i: (i, 0)),
                      pl.BlockSpec((1, SC_WINDOW), index_map=lambda i: (i, 0))],
            out_specs=[],
            core_axis_name=("core", "subcore"), dimension_semantics=(pltpu.PARALLEL,),
        )(x_hbm, ia_hbm, ib_hbm)

    return scatter_kernel(src, idx_a, idx_b)


def _sc_gather2(table, idx_a, idx_b):
    n = idx_a.size
    sds = jax.ShapeDtypeStruct((n, 128), table.dtype)

    @functools.partial(pl.kernel, out_type=(sds, sds), mesh=_sc_mesh(), scratch_types=[])
    def gather_kernel(t_hbm, ia_hbm, ib_hbm, oa_hbm, ob_hbm):
        def body(ia_vmem, ib_vmem, oa_vmem, ob_vmem):
            pltpu.sync_copy(t_hbm.at[ia_vmem.at[0]], oa_vmem)
            pltpu.sync_copy(t_hbm.at[ib_vmem.at[0]], ob_vmem)

        pltpu.emit_pipeline(
            body, grid=(n // SC_WINDOW,),
            in_specs=[pl.BlockSpec((1, SC_WINDOW), index_map=lambda i: (i, 0)),
                      pl.BlockSpec((1, SC_WINDOW), index_map=lambda i: (i, 0))],
            out_specs=[pl.BlockSpec((SC_WINDOW, 128), index_map=lambda i: (i, 0)),
                       pl.BlockSpec((SC_WINDOW, 128), index_map=lambda i: (i, 0))],
            core_axis_name=("core", "subcore"), dimension_semantics=(pltpu.PARALLEL,),
        )(ia_hbm, ib_hbm, oa_hbm, ob_hbm)

    return gather_kernel(table, idx_a, idx_b)


def _index_kernel(route_ref, off_ref, fold_ref, spread_ref, i1_ref, i2_ref, *, tile):
    rec = route_ref[...]
    lane = lax.broadcasted_iota(jnp.int32, rec.shape, 1)
    off = off_ref[...]
    sub = lax.broadcasted_iota(jnp.int32, (tile * ROW_SPLIT // 128, 128), 1) % ROW_SPLIT
    for out_ref, e_col, r_col in ((i1_ref, ROUTE_E1, ROUTE_R1), (i2_ref, ROUTE_E2, ROUTE_R2)):
        e = rec[:, e_col:e_col + 1].astype(jnp.int32)
        pos = jnp.sum(jnp.where(lane == e, off, 0.0), axis=-1, keepdims=True) + rec[:, r_col:r_col + 1]
        hi = jnp.floor(pos * (1.0 / 256.0))
        lo = pos - 256.0 * hi
        moved = (256.0 * _dot(fold_ref[...], (hi * spread_ref[...]).astype(BF16))
                 + _dot(fold_ref[...], (lo * spread_ref[...]).astype(BF16)))
        out_ref[...] = moved.astype(jnp.int32) * ROW_SPLIT + sub


def _dispatch_plan(route, counts, n_tiles):
    n = route.shape[0]
    cnt = counts[:, 0].astype(jnp.int32)
    padded = ((cnt + EXPERT_TILE - 1) // EXPERT_TILE) * EXPERT_TILE
    ends = jnp.cumsum(padded)
    off = ends - padded
    start = jnp.arange(n_tiles, dtype=jnp.int32) * EXPERT_TILE
    tile_expert = jnp.minimum(jnp.sum((start[:, None] >= ends[None, :]).astype(jnp.int32), axis=1), MOE_EXPERTS - 1)
    onehot = (tile_expert[:, None] == jnp.arange(MOE_EXPERTS, dtype=jnp.int32)[None, :]).astype(jnp.int32)
    seg_end = jnp.sum(onehot * (off + cnt)[None, :], axis=1)
    tile_rows = jnp.clip(seg_end - start, 0, EXPERT_TILE).astype(jnp.int32)
    prev = jnp.concatenate([jnp.full((1,), -1, jnp.int32), tile_expert[:-1]])
    tile_first = ((tile_expert != prev) & (tile_rows > 0)).astype(jnp.int32)

    tile = min(SEQ_TILE, n)
    off_row = jnp.zeros((1, ROUTE_LANES), F32).at[0, :MOE_EXPERTS].set(off.astype(F32))
    per_row = 128 // ROW_SPLIT
    t = np.arange(tile)
    out_rows = tile // per_row
    fold = jnp.asarray(np.arange(out_rows)[:, None] == (t // per_row)[None, :], dtype=BF16)
    spread = jnp.asarray((t % per_row)[:, None] == (np.arange(128) // ROW_SPLIT)[None, :], dtype=F32)
    idx1, idx2 = pl.pallas_call(
        functools.partial(_index_kernel, tile=tile),
        grid=(n // tile,),
        in_specs=[_rows(tile, ROUTE_LANES), _full((1, ROUTE_LANES)), _full(fold.shape), _full(spread.shape)],
        out_specs=[_rows(out_rows, 128)] * 2,
        out_shape=[jax.ShapeDtypeStruct((n // per_row, 128), jnp.int32)] * 2,
        compiler_params=_cparams("parallel"),
        name="dispatch_index",
    )(route, off_row, fold, spread)
    return idx1, idx2, tile_expert, tile_rows, tile_first


def _experts_kernel(te_ref, rows_ref, first_ref, xs_ref, wg_ref, wu_ref, wd_ref, y_ref, wgb, wub, wdb):
    i = pl.program_id(0)

    @pl.when(first_ref[i] == 1)
    def _():
        wgb[...] = wg_ref[0, 0].astype(BF16)
        wub[...] = wu_ref[0, 0].astype(BF16)
        wdb[...] = wd_ref[0, 0].astype(BF16)

    @pl.when(rows_ref[i] > 0)
    def _():
        x = _merge_rows(xs_ref, EXPERT_TILE)
        row = lax.broadcasted_iota(jnp.int32, x.shape, 0)
        x = jnp.where(row < rows_ref[i], x, 0.0).astype(BF16)
        act = _silu(_dot(x, wgb[...])) * _dot(x, wub[...])
        _split_rows(y_ref, _dot(act.astype(BF16), wdb[...]), EXPERT_TILE)


def _experts(xs, tile_expert, tile_rows, tile_first, w_gate, w_up, w_down, layer):
    n_tiles = tile_expert.shape[0]
    _, _, d, ff = w_gate.shape
    rows_blk = pl.BlockSpec((ROW_SPLIT * EXPERT_TILE, 128), lambda i, te, rows, first: (i, 0))
    return pl.pallas_call(
        _experts_kernel,
        grid_spec=pltpu.PrefetchScalarGridSpec(
            num_scalar_prefetch=3,
            grid=(n_tiles,),
            in_specs=[rows_blk,
                      pl.BlockSpec((1, 1, d, ff), lambda i, te, rows, first: (layer, te[i], 0, 0)),
                      pl.BlockSpec((1, 1, d, ff), lambda i, te, rows, first: (layer, te[i], 0, 0)),
                      pl.BlockSpec((1, 1, ff, d), lambda i, te, rows, first: (layer, te[i], 0, 0))],
            out_specs=rows_blk,
            scratch_shapes=[pltpu.VMEM((d, ff), BF16), pltpu.VMEM((d, ff), BF16), pltpu.VMEM((ff, d), BF16)],
        ),
        out_shape=jax.ShapeDtypeStruct(xs.shape, xs.dtype),
        compiler_params=_cparams("arbitrary"),
        name="moe_experts",
    )(tile_expert, tile_rows, tile_first, xs, w_gate, w_up, w_down)


def _combine_kernel(x1_ref, g1_ref, g2_ref, route_ref, gf_ref, o_ref, *, tile, final_norm):
    route = route_ref[...]
    out = (x1_ref[...] + route[:, ROUTE_W1:ROUTE_W1 + 1] * _merge_rows(g1_ref, tile)
           + route[:, ROUTE_W2:ROUTE_W2 + 1] * _merge_rows(g2_ref, tile))
    o_ref[...] = _rms(out, gf_ref[...]) if final_norm else out


def _combine(x1, g1, g2, route, g_final, final_norm):
    n, d = x1.shape
    tile = min(SEQ_TILE, n)
    return pl.pallas_call(
        functools.partial(_combine_kernel, tile=tile, final_norm=final_norm),
        grid=(n // tile,),
        in_specs=[_rows(tile, d), _rows(ROW_SPLIT * tile, 128), _rows(ROW_SPLIT * tile, 128),
                  _rows(tile, ROUTE_LANES), _full((1, d))],
        out_specs=_rows(tile, d),
        out_shape=jax.ShapeDtypeStruct((n, d), F32),
        compiler_params=_cparams("parallel"),
        name="moe_combine",
    )(x1, g1, g2, route, g_final)


def _moe(h2_rows, route, counts, x1, w_gate, w_up, w_down, layer, g_final, final_norm):
    n = x1.shape[0]
    n_tiles = (MOE_TOPK * n) // EXPERT_TILE + MOE_EXPERTS
    idx1, idx2, tile_expert, tile_rows, tile_first = _dispatch_plan(route, counts, n_tiles)
    xs = _sc_scatter2(h2_rows, idx1, idx2, ROW_SPLIT * n_tiles * EXPERT_TILE)
    ys = _experts(xs, tile_expert, tile_rows, tile_first, w_gate, w_up, w_down, layer)
    g1, g2 = _sc_gather2(ys, idx1, idx2)
    return _combine(x1, g1, g2, route, g_final, final_norm)


def kernel(x, positions, norm1_g, w_in, w_out, s5_a_re, s5_a_im, s5_log_dt, s5_b_re, s5_b_im, s5_c_re, s5_c_im, s5_d, s5_w_glu, s5_b_glu, m2_conv_w, m2_conv_b, m2_dt_bias, m2_a_log, m2_d, m2_norm_g, hg_lb_logits, hg_norm_g, norm2_g, moe_w_group, moe_b_group, moe_w_expert, moe_b_expert, moe_w_gate, moe_w_up, moe_w_down, final_norm_g):
    bsz, seqlen, d = x.shape
    assert bsz == 1 and seqlen % SEQ_TILE == 0 and (MOE_TOPK * seqlen) % EXPERT_TILE == 0
    depth = w_in.shape[0]

    lb_probs = jax.nn.softmax(hg_lb_logits.astype(F32), axis=0)
    lower_bounds = jnp.cumsum(lb_probs, axis=0) - lb_probs[0]
    cos_t, sin_t = _rope_tables(positions.reshape(seqlen, 1))

    xc = x.reshape(seqlen, d)
    for l in range(depth):
        u_b, u_f, p_ret, p_m2, p_hg = _in_proj(xc, norm1_g[l].reshape(1, d), _w_prep(w_in, l))
        y_s5 = _s5(u_b, u_f, s5_a_re[l], s5_a_im[l], s5_log_dt[l], s5_b_re[l], s5_b_im[l],
                   s5_c_re[l], s5_c_im[l], s5_d[l])
        tile = min(SEQ_TILE, seqlen)
        chunk = min(CHUNK, tile)
        y_ret, y_m2, y_hg = _mixers(
            seqlen,
            _retention_call(p_ret, cos_t, sin_t, tile, min(RET_CHUNK, tile)),
            _mamba2_call(p_m2, m2_conv_w[l], m2_conv_b[l], m2_dt_bias[l], m2_a_log[l], m2_d[l], m2_norm_g[l],
                         tile, chunk),
            _hgrn2_call(p_hg, lower_bounds[l], hg_norm_g[l], tile, chunk))
        w_route = jnp.zeros((d, ROUTE_LANES), F32)
        w_route = w_route.at[:, :MOE_GROUPS].set(moe_w_group[l])
        w_route = w_route.at[:, EXPERT_ROW:EXPERT_ROW + MOE_EXPERTS].set(moe_w_expert[l])
        b_route = jnp.zeros((1, ROUTE_LANES), F32)
        b_route = b_route.at[0, :MOE_GROUPS].set(moe_b_group[l])
        b_route = b_route.at[0, EXPERT_ROW:EXPERT_ROW + MOE_EXPERTS].set(moe_b_expert[l])
        x1, h2_rows, route, counts = _out_proj(xc, y_s5, y_ret, y_m2, y_hg, s5_w_glu[l].astype(BF16),
                                               s5_b_glu[l].reshape(1, -1), w_out[l].astype(BF16),
                                               norm2_g[l].reshape(1, d), w_route, b_route)
        xc = _moe(h2_rows, route, counts, x1, moe_w_gate, moe_w_up, moe_w_down, l,
                  final_norm_g.reshape(1, d), final_norm=(l == depth - 1))
    return xc.reshape(bsz, seqlen, d)
```

```python
import functools
import math

import numpy as np
import jax
import jax.numpy as jnp
from jax import lax
from jax.experimental import pallas as pl
from jax.experimental.pallas import tpu as pltpu
from jax.experimental.pallas import tpu_sc as plsc

F32 = jnp.float32
BF16 = jnp.bfloat16
NORM_EPS = 1e-6

GROUP_WIDTH = 256
HEAD_DIM = 64
N_HEADS = GROUP_WIDTH // HEAD_DIM
S5_GROUPS = 16
S5_CH = 16
S5_STATE = 64
S5_DT_CLAMP = -1e-4
M2_STATE = 128
M2_CONV = 4
M2_CONV_DIM = 768
ROPE_BASE = 10000.0
MOE_GROUPS = 4
MOE_PER_GROUP = 4
MOE_EXPERTS = 16
ROUTE_LANES = 128

SEQ_TILE = 512
CHUNK = 128
RET_CHUNK = 256
CHUNK_UNROLL = 4
S5_CHUNK = 16
MOE_TOPK = 2
EXPERT_TILE = 512
SC_WINDOW = 128
VMEM_LIMIT = 56 * 1024 * 1024


def _cparams(*sem):
    return pltpu.CompilerParams(dimension_semantics=sem, vmem_limit_bytes=VMEM_LIMIT)


def _dot(a, b):
    return jnp.dot(a, b, preferred_element_type=F32)


def _dot_nt(a, b):
    return lax.dot_general(a, b, (((1,), (1,)), ((), ())), preferred_element_type=F32)


def _dot_tn(a, b):
    return lax.dot_general(a, b, (((0,), (0,)), ((), ())), preferred_element_type=F32)


def _split2(x):
    hi = x.astype(BF16)
    return hi, (x - hi.astype(F32)).astype(BF16)


def _split3(x):
    hi = x.astype(BF16)
    r = x - hi.astype(F32)
    mid = r.astype(BF16)
    return hi, mid, (r - mid.astype(F32)).astype(BF16)


def _dot_exact_lhs(m, x):
    hi, mid, lo = _split3(x)
    return _dot(m, hi) + _dot(m, mid) + _dot(m, lo)


def _dot_exact_rhs(x, m):
    hi, lo = _split2(x)
    return _dot(hi, m) + _dot(lo, m)


def _sigmoid(x):
    return 1.0 / (1.0 + jnp.exp(-x))


def _silu(x):
    return x * _sigmoid(x)


def _rms(x, g):
    return x * lax.rsqrt(jnp.mean(x * x, axis=-1, keepdims=True) + NORM_EPS) * g


def _full(shape):
    return pl.BlockSpec(shape, lambda *_: (0,) * len(shape))


def _rows(tile, width):
    return pl.BlockSpec((tile, width), lambda i: (i, 0))


def _pick(shape, layer):
    return pl.BlockSpec((None,) + tuple(shape), lambda *_: (layer,) + (0,) * len(shape))


IN_SEGMENTS = (256, 1024, 1280, 1024)


DT_COL = 9 * GROUP_WIDTH
W_PREP_ROWS = 128


def _w_prep_kernel(w_ref, o_ref):
    o_ref[:, 0:DT_COL] = w_ref[0, :, 0:DT_COL].astype(BF16)
    head = lax.broadcasted_iota(jnp.int32, (W_PREP_ROWS, GROUP_WIDTH), 1) // HEAD_DIM
    rep = jnp.zeros((W_PREP_ROWS, GROUP_WIDTH), F32)
    for h in range(N_HEADS):
        rep = jnp.where(head == h, w_ref[0, :, DT_COL + h:DT_COL + h + 1], rep)
    o_ref[:, DT_COL:DT_COL + GROUP_WIDTH] = rep.astype(BF16)
    o_ref[:, DT_COL + GROUP_WIDTH:] = w_ref[0, :, DT_COL + N_HEADS:].astype(BF16)


def _w_prep(w_in, layer):
    _, d, n_in = w_in.shape
    return pl.pallas_call(
        _w_prep_kernel,
        grid=(d // W_PREP_ROWS,),
        in_specs=[pl.BlockSpec((1, W_PREP_ROWS, n_in), lambda i: (layer, i, 0))],
        out_specs=pl.BlockSpec((W_PREP_ROWS, sum(IN_SEGMENTS)), lambda i: (i, 0)),
        out_shape=jax.ShapeDtypeStruct((d, sum(IN_SEGMENTS)), BF16),
        compiler_params=_cparams("parallel"),
        name="w_in_prep",
    )(w_in)


def _in_proj_kernel(x_ref, g_ref, w_ref, ub_ref, uf_ref, ret_ref, m2_ref, hg_ref, u_tmp, *, tile):
    hb = _rms(x_ref[...], g_ref[...]).astype(BF16)
    c0, c1, c2, c3 = np.cumsum(IN_SEGMENTS)
    ret_ref[...] = _dot(hb, w_ref[:, c0:c1])
    m2_ref[...] = _dot(hb, w_ref[:, c1:c2])
    hg_ref[...] = _dot(hb, w_ref[:, c2:c3])
    u = _dot(hb, w_ref[:, 0:c0])
    for j in range(GROUP_WIDTH // 128):
        u_tmp[j] = u[:, 128 * j:128 * (j + 1)]
    for s in range(S5_CHUNK):
        for j in range(GROUP_WIDTH // 128):
            v = u_tmp[j, pl.ds(s, tile // S5_CHUNK, stride=S5_CHUNK), :]
            lanes = slice(GROUP_WIDTH * s + 128 * j, GROUP_WIDTH * s + 128 * (j + 1))
            uf_ref[:, lanes] = v
            ub_ref[:, lanes] = v.astype(BF16)


def _in_proj(x, g, w, layer):
    n, d = x.shape
    tile = min(SEQ_TILE, n)
    fold = S5_CHUNK * GROUP_WIDTH
    return pl.pallas_call(
        functools.partial(_in_proj_kernel, tile=tile),
        grid=(n // tile,),
        in_specs=[_rows(tile, d), _pick((1, d), layer), _full(w.shape)],
        out_specs=[_rows(tile // S5_CHUNK, fold)] * 2 + [_rows(tile, s) for s in IN_SEGMENTS[1:]],
        out_shape=[jax.ShapeDtypeStruct((n // S5_CHUNK, fold), BF16),
                   jax.ShapeDtypeStruct((n // S5_CHUNK, fold), F32)]
        + [jax.ShapeDtypeStruct((n, s), F32) for s in IN_SEGMENTS[1:]],
        scratch_shapes=[pltpu.VMEM((GROUP_WIDTH // 128, tile, 128), F32)],
        compiler_params=_cparams("parallel"),
        name="in_proj",
    )(x, g, w)


def _rope_kernel(pos_ref, invf_ref, cos_ref, sin_ref):
    ang = pos_ref[...].astype(F32) * invf_ref[...]
    cos_ref[...] = jnp.cos(ang)
    sin_ref[...] = jnp.sin(ang)


def _rope_tables(positions):
    n = positions.shape[0]
    tile = min(SEQ_TILE, n)
    half = HEAD_DIM // 2
    inv_freq = ROPE_BASE ** (-jnp.arange(half, dtype=F32) / half)
    invf = jnp.tile(inv_freq, 128 // half).reshape(1, 128)
    return pl.pallas_call(
        _rope_kernel,
        grid=(n // tile,),
        in_specs=[_rows(tile, 1), _full((1, 128))],
        out_specs=[_rows(tile, 128), _rows(tile, 128)],
        out_shape=[jax.ShapeDtypeStruct((n, 128), F32)] * 2,
        compiler_params=_cparams("parallel"),
        name="rope_tables",
    )(positions, invf)


def _head_mean_matrix():
    h = np.arange(GROUP_WIDTH) // HEAD_DIM
    return jnp.asarray((h[:, None] == h[None, :]) / HEAD_DIM, dtype=BF16)


def _head_block_mask():
    h = np.arange(GROUP_WIDTH) // HEAD_DIM
    return jnp.asarray(h[:, None] == h[None, :], dtype=F32)


HEAD_PAIRS = GROUP_WIDTH // 128


def _lanes(x, j):
    return x[:, 128 * j:128 * (j + 1)]


def _stack_pair(x):
    xb = x.astype(BF16)
    low = lax.broadcasted_iota(jnp.int32, x.shape, 1) < HEAD_DIM
    zero = jnp.zeros_like(xb)
    return jnp.concatenate([jnp.where(low, xb, zero), jnp.where(low, zero, xb)], axis=0)


def _pair_scores(q, kb):
    s = _dot_nt(_stack_pair(q), kb)
    return s[:q.shape[0]], s[q.shape[0]:]


def _pair_apply(a0, a1, v):
    return _dot(jnp.concatenate([a0.astype(BF16), a1.astype(BF16)], axis=1), _stack_pair(v))


def _ret_constants(chunk):
    lg = np.log1p(-np.exp2(-5.0 - np.arange(N_HEADS, dtype=np.float64)))
    idx = np.arange(chunk, dtype=np.float64)
    rel = idx[:, None] - idx[None, :]
    decay = np.where(rel >= 0, np.exp(np.maximum(rel, 0.0)[None] * lg[:, None, None]), 0.0)
    lane_lg = np.repeat(lg, HEAD_DIM)
    xi = np.exp((idx + 1.0)[:, None] * lane_lg[None, :])
    zeta = np.exp((chunk - 1.0 - idx)[:, None] * lane_lg[None, :])
    h = np.arange(GROUP_WIDTH) // HEAD_DIM
    gc = np.where(h[:, None] == h[None, :], np.exp(chunk * lane_lg)[:, None], 0.0)
    f = lambda a: jnp.asarray(a, dtype=F32)
    return f(decay), f(xi), f(zeta), f(gc)


def _ret_kernel(p_ref, cos_ref, sin_ref, dec_ref, xi_ref, zeta_ref, gc_ref, bm_ref, gm_ref,
                o_ref, r_ref, *, chunk, n_chunks):
    lane = lax.broadcasted_iota(jnp.int32, (chunk, GROUP_WIDTH), 1)
    first_half = (lane % HEAD_DIM) < (HEAD_DIM // 2)
    gm = gm_ref[...]

    def rope(t, cos2, sin2):
        rot = jnp.where(first_half, -pltpu.roll(t, GROUP_WIDTH - HEAD_DIM // 2, 1),
                        pltpu.roll(t, HEAD_DIM // 2, 1))
        return t * cos2 + rot * sin2

    def body(c, carry):
        rows = pl.ds(pl.multiple_of(c * chunk, chunk), chunk)
        cs = cos_ref[rows, :]
        sn = sin_ref[rows, :]
        cos2 = jnp.concatenate([cs, cs], axis=1)
        sin2 = jnp.concatenate([sn, sn], axis=1)
        q = rope(p_ref[rows, 0:256], cos2, sin2)
        k = rope(p_ref[rows, 256:512], cos2, sin2) * (HEAD_DIM ** -0.5)
        v = p_ref[rows, 512:768]
        g = p_ref[rows, 768:1024]
        kb = k.astype(BF16)
        inner = []
        for j in range(HEAD_PAIRS):
            s0, s1 = _pair_scores(_lanes(q, j), _lanes(kb, j))
            inner.append(_pair_apply(s0 * dec_ref[2 * j], s1 * dec_ref[2 * j + 1], _lanes(v, j)))
        inner = jnp.concatenate(inner, axis=1)
        r_prev = r_ref[...]
        cross = _dot((q * xi_ref[...]).astype(BF16), r_prev.astype(BF16))
        r_ref[...] = gc_ref[...] * r_prev + bm_ref[...] * _dot_tn(kb, (zeta_ref[...] * v).astype(BF16))
        o = inner + cross
        cen = o - _dot_exact_rhs(o, gm)
        var = _dot_exact_rhs(cen * cen, gm)
        o_ref[rows, :] = (cen * lax.rsqrt(var + NORM_EPS) * _silu(g)).astype(BF16)
        return carry

    lax.fori_loop(0, n_chunks, body, 0, unroll=CHUNK_UNROLL)


def _retention_call(proj, cos_t, sin_t, tile, chunk):
    decay, xi, zeta, gc = _ret_constants(chunk)
    body = functools.partial(_ret_kernel, chunk=chunk, n_chunks=tile // chunk)
    operands = (proj, cos_t, sin_t, decay, xi, zeta, gc, _head_block_mask(), _head_mean_matrix())
    specs = [_rows(tile, 1024), _rows(tile, 128), _rows(tile, 128), _full(decay.shape), _full(xi.shape),
             _full(zeta.shape), _full(gc.shape), _full((256, 256)), _full((256, 256))]
    return body, operands, specs, [pltpu.VMEM((GROUP_WIDTH, GROUP_WIDTH), F32)]


def _tri_matrix(chunk):
    i = np.arange(chunk)
    return jnp.asarray(i[:, None] >= i[None, :], dtype=BF16)


def _m2_kernel(p_ref, cw_ref, cb_ref, dtb_ref, alog_ref, d_ref, ng_ref, tri_ref,
               o_ref, tail_ref, ext_ref, act_ref, st_ref, *, tile, chunk, n_chunks):
    ext_ref[0:8, :] = tail_ref[...]
    ext_ref[8:tile + 8, :] = p_ref[:, 256:1024]
    tail_ref[...] = p_ref[tile - 8:tile, 256:1024]
    conv = cb_ref[...]
    for j in range(M2_CONV):
        lo = 8 - (M2_CONV - 1) + j
        conv = conv + cw_ref[j:j + 1, :] * ext_ref[lo:lo + tile, :]
    act_ref[...] = _silu(conv)
    a_lane = -jnp.exp(alog_ref[...])
    tri = tri_ref[...]
    ti = lax.broadcasted_iota(jnp.int32, (chunk, chunk), 0)
    si = lax.broadcasted_iota(jnp.int32, (chunk, chunk), 1)
    causal = ti >= si
    lane = lax.broadcasted_iota(jnp.int32, (chunk, 128), 1)

    def body(c, carry):
        start = pl.multiple_of(c * chunk, chunk)
        rows = pl.ds(start, chunk)
        xbc = act_ref[rows, :]
        xs = xbc[:, 0:256]
        z = p_ref[rows, 0:256]
        x_dt = p_ref[rows, 1024:1280] + dtb_ref[...]
        dt = jnp.maximum(x_dt, 0.0) + jnp.log1p(jnp.exp(-jnp.abs(x_dt)))
        acum = _dot_exact_lhs(tri, dt * a_lane)
        acum_t = acum.T
        a_last = acum[chunk - 1:chunk, :]
        e_acum = jnp.exp(acum)
        decs = jnp.exp(a_last - acum)
        d_chunk = jnp.exp(a_last)
        xc = xs * dt
        ys = []
        for g in range(2):
            sl = slice(128 * g, 128 * (g + 1))
            bmg = xbc[:, 256 + 128 * g:256 + 128 * (g + 1)].astype(BF16)
            cmg = xbc[:, 512 + 128 * g:512 + 128 * (g + 1)].astype(BF16)
            cb = _dot_nt(cmg, bmg)
            xcg = xc[:, sl]
            yd = jnp.zeros((chunk, 128), F32)
            for hh in range(2):
                col0 = 128 * g + HEAD_DIM * hh
                diff = acum[:, col0:col0 + 1] - acum_t[col0:col0 + 1, :]
                lm = jnp.where(causal, jnp.exp(jnp.where(causal, diff, 0.0)), 0.0)
                xm = jnp.where((lane // HEAD_DIM) == hh, xcg, 0.0).astype(BF16)
                yd = yd + _dot((cb * lm).astype(BF16), xm)
            st = st_ref[:, sl]
            y_off = _dot(cmg, st.astype(BF16)) * e_acum[:, sl]
            st_ref[:, sl] = d_chunk[:, sl] * st + _dot_tn(bmg, (xcg * decs[:, sl]).astype(BF16))
            ys.append(yd + y_off + d_ref[:, sl] * xs[:, sl])
        y = jnp.concatenate(ys, axis=1) * _silu(z)
        o_ref[rows, :] = _rms(y, ng_ref[...]).astype(BF16)
        return carry

    lax.fori_loop(0, n_chunks, body, 0, unroll=CHUNK_UNROLL)


def _mamba2_params(conv_w, conv_b, dt_bias, a_log, d_skip, norm_g):
    depth = conv_w.shape[0]
    lanes = lambda v: jnp.repeat(v, HEAD_DIM, axis=1).reshape(depth, 1, GROUP_WIDTH)
    return (conv_w, conv_b.reshape(depth, 1, -1), lanes(dt_bias), lanes(a_log), lanes(d_skip),
            norm_g.reshape(depth, 1, -1))


def _mamba2_call(proj, params, layer, tile, chunk):
    body = functools.partial(_m2_kernel, tile=tile, chunk=chunk, n_chunks=tile // chunk)
    operands = (proj,) + tuple(params) + (_tri_matrix(chunk),)
    specs = ([_rows(tile, 1280), _pick((M2_CONV, M2_CONV_DIM), layer), _pick((1, M2_CONV_DIM), layer)]
             + [_pick((1, GROUP_WIDTH), layer)] * 4 + [_full((chunk, chunk))])
    scratch = [pltpu.VMEM((8, M2_CONV_DIM), F32), pltpu.VMEM((tile + 8, M2_CONV_DIM), F32),
               pltpu.VMEM((tile, M2_CONV_DIM), F32), pltpu.VMEM((M2_STATE, GROUP_WIDTH), F32)]
    return body, operands, specs, scratch


HG_MATMUL_LEVELS = 3


def _hg_exponent_matrix(chunk):
    levels = HG_MATMUL_LEVELS
    t = np.arange(chunk)[:, None]
    r = np.arange(chunk)[None, :]
    blocks = []
    for lvl in range(levels):
        b = 1 << lvl
        blk = t // b
        odd = (blk % 2) == 1
        q_rows = odd & (r >= blk * b) & (r <= t)
        k_rows = (~odd) & (r > t) & (r <= (blk + 1) * b - 1)
        blocks.append(q_rows | k_rows)
    blocks.append(r <= t)
    return jnp.asarray(np.concatenate(blocks, axis=0), dtype=BF16)


def _hg_kernel(p_ref, lb_ref, ng_ref, gexp_ref, bm_ref, gm_ref, o_ref, st_ref, *, chunk, n_chunks):
    levels = int(math.log2(chunk))
    row = lax.broadcasted_iota(jnp.int32, (chunk, 128), 0)
    odd_rows = [((row >> lvl) & 1) == 1 for lvl in range(levels)]
    row_wide = lax.broadcasted_iota(jnp.int32, (chunk, GROUP_WIDTH), 0)
    odd_rows_wide = [((row_wide >> lvl) & 1) == 1 for lvl in range(levels)]
    ti = lax.broadcasted_iota(jnp.int32, (chunk, chunk), 0)
    si = lax.broadcasted_iota(jnp.int32, (chunk, chunk), 1)
    pair_level = [((ti >> (lvl + 1)) == (si >> (lvl + 1))) & (((ti >> lvl) & 1) == 1) & (((si >> lvl) & 1) == 0)
                  for lvl in range(levels)]
    lb = lb_ref[...]
    gm = gm_ref[...]

    def body(c, carry):
        rows = pl.ds(pl.multiple_of(c * chunk, chunk), chunk)
        q = _silu(p_ref[rows, 0:256])
        forget = lb + (1.0 - lb) * _sigmoid(p_ref[rows, 256:512])
        k = 1.0 - forget
        v = p_ref[rows, 512:768]
        g = p_ref[rows, 768:1024]
        lf_hi, lf_lo = _split2(jnp.log(forget))
        expo = _dot(gexp_ref[...], lf_hi) + _dot(gexp_ref[...], lf_lo)
        bcum = expo[HG_MATMUL_LEVELS * chunk:(HG_MATMUL_LEVELS + 1) * chunk, :]

        def level_log_decay(lvl):
            if lvl < HG_MATMUL_LEVELS:
                return expo[lvl * chunk:(lvl + 1) * chunk, :]
            b = 1 << lvl
            ref = jnp.concatenate([jnp.broadcast_to(bcum[m + b - 1:m + b, :], (2 * b, GROUP_WIDTH))
                                   for m in range(0, chunk, 2 * b)], axis=0)
            return jnp.where(odd_rows_wide[lvl], bcum - ref, ref - bcum)


        log_decay = [level_log_decay(lvl) for lvl in range(levels)]
        intra = []
        for j in range(HEAD_PAIRS):
            qj, kj = _lanes(q, j), _lanes(k, j)
            a0, a1 = (jnp.where(ti == si, s, 0.0) for s in _pair_scores(qj, kj.astype(BF16)))
            for lvl in range(levels):
                w = jnp.exp(_lanes(log_decay[lvl], j)) * jnp.where(odd_rows[lvl], qj, kj)
                s0, s1 = _pair_scores(w, w.astype(BF16))
                a0 = jnp.where(pair_level[lvl], s0, a0)
                a1 = jnp.where(pair_level[lvl], s1, a1)
            intra.append(_pair_apply(a0, a1, _lanes(v, j)))
        intra = jnp.concatenate(intra, axis=1)

        b_last = bcum[chunk - 1:chunk, :]
        suffix = b_last - bcum
        st = st_ref[...]
        cross = _dot_nt((q * jnp.exp(bcum)).astype(BF16), st.astype(BF16))
        st_ref[...] = jnp.exp(b_last) * st + bm_ref[...] * _dot_tn(
            v.astype(BF16), (k * jnp.exp(suffix)).astype(BF16))
        o = intra + cross
        o = o * lax.rsqrt(_dot_exact_rhs(o * o, gm) + NORM_EPS) * ng_ref[...]
        o_ref[rows, :] = (o * _silu(g)).astype(BF16)
        return carry

    lax.fori_loop(0, n_chunks, body, 0, unroll=CHUNK_UNROLL)


def _hgrn2_call(proj, lower_bounds, norm_g, layer, tile, chunk):
    gexp = _hg_exponent_matrix(chunk)
    body = functools.partial(_hg_kernel, chunk=chunk, n_chunks=tile // chunk)
    operands = (proj, lower_bounds, norm_g, gexp, _head_block_mask(), _head_mean_matrix())
    specs = [_rows(tile, 1024), _pick((1, GROUP_WIDTH), layer), _pick((1, GROUP_WIDTH), layer), _full(gexp.shape),
             _full((256, 256)), _full((256, 256))]
    return body, operands, specs, [pltpu.VMEM((GROUP_WIDTH, GROUP_WIDTH), F32)]


def _mixers_kernel(*refs, bodies, n_in, n_scratch):
    n_mix = len(bodies)
    ins, pos = [], 0
    for k in n_in:
        ins.append(refs[pos:pos + k])
        pos += k
    outs = refs[pos:pos + n_mix]
    pos += n_mix
    scratch = []
    for k in n_scratch:
        scratch.append(refs[pos:pos + k])
        pos += k

    @pl.when(pl.program_id(0) == 0)
    def _():
        for group in scratch:
            for ref in group:
                ref[...] = jnp.zeros_like(ref)

    for body, i, o, s in zip(bodies, ins, outs, scratch):
        body(*i, o, *s)


def _mixers(n, *calls):
    tile = min(SEQ_TILE, n)
    bodies = tuple(c[0] for c in calls)
    return pl.pallas_call(
        functools.partial(_mixers_kernel, bodies=bodies, n_in=tuple(len(c[1]) for c in calls),
                          n_scratch=tuple(len(c[3]) for c in calls)),
        grid=(n // tile,),
        in_specs=[s for c in calls for s in c[2]],
        out_specs=[_rows(tile, GROUP_WIDTH)] * len(calls),
        out_shape=[jax.ShapeDtypeStruct((n, GROUP_WIDTH), BF16)] * len(calls),
        scratch_shapes=[s for c in calls for s in c[3]],
        compiler_params=_cparams("arbitrary"),
        name="mixers",
    )(*[a for c in calls for a in c[1]])


S5_LANES = S5_GROUPS * 2 * S5_STATE


def _s5_rows(a_re, a_im, log_dt):
    are = jnp.minimum(a_re, S5_DT_CLAMP)
    dt = jnp.exp(log_dt)
    lam_re = are * dt
    lam_im = a_im * dt
    mag = jnp.exp(lam_re)
    ab_re = mag * jnp.cos(lam_im)
    ab_im = mag * jnp.sin(lam_im)
    den = are * are + a_im * a_im
    k_re = ((ab_re - 1.0) * are + ab_im * a_im) / den
    k_im = (ab_im * are - (ab_re - 1.0) * a_im) / den
    return lam_re, lam_im, k_re, k_im


def _s5_power(lam_re, lam_im, e):
    m = jnp.exp(e * lam_re)
    return m * jnp.cos(e * lam_im), m * jnp.sin(e * lam_im)


def _s5_state_kernel(u_ref, are_ref, aim_ref, ldt_ref, b1_ref, b2_ref, ca_ref, k_ref, sp_ref,
                     bb1_ref, bb2_ref, inc_ref, *, n_chunks):
    s = pl.program_id(0)
    lam_re, lam_im, k_re, k_im = _s5_rows(are_ref[...], aim_ref[...], ldt_ref[...])

    @pl.when(s == 0)
    def _():
        bb1_ref[...] = k_re * b1_ref[...] + k_im * b2_ref[...]
        bb2_ref[...] = k_re * b2_ref[...] - k_im * b1_ref[...]

    p_re, p_im = _s5_power(lam_re, lam_im, (S5_CHUNK - 1 - s).astype(F32))
    w = (p_re * bb1_ref[...] + p_im * bb2_ref[...]).astype(BF16)
    k_ref[0] = _dot_nt(w, ca_ref[...].astype(BF16)).astype(BF16)
    contrib = _dot(u_ref[...], w)

    @pl.when(s == 0)
    def _():
        inc_ref[...] = contrib

    @pl.when(s > 0)
    def _():
        inc_ref[...] += contrib

    @pl.when(s == S5_CHUNK - 1)
    def _():
        n_steps = max(1, int(math.ceil(math.log2(n_chunks))))
        step = lax.broadcasted_iota(jnp.int32, (16, S5_LANES), 0)
        e = (jnp.left_shift(1, step) * S5_CHUNK).astype(F32)
        a_re_all, a_im_all = _s5_power(lam_re, lam_im, e)
        row = lax.broadcasted_iota(jnp.int32, (n_chunks, 128), 0)
        half = S5_LANES // 2

        def shifted(x, sh):
            return jnp.where(row >= sh, pltpu.roll(x, sh, 0), 0.0)

        for j in range(half // 128):
            re_l, im_l = slice(128 * j, 128 * (j + 1)), slice(half + 128 * j, half + 128 * (j + 1))
            x_re, x_im = inc_ref[:, re_l], inc_ref[:, im_l]
            for k in range(n_steps):
                sh = 1 << k
                if sh >= n_chunks:
                    break
                a_re, a_im = a_re_all[k:k + 1, re_l], a_im_all[k:k + 1, re_l]
                p_re, p_im = shifted(x_re, sh), shifted(x_im, sh)
                x_re, x_im = x_re + a_re * p_re - a_im * p_im, x_im + a_re * p_im + a_im * p_re
            sp_ref[:, re_l] = shifted(x_re, 1).astype(BF16)
            sp_ref[:, im_l] = shifted(x_im, 1).astype(BF16)


def _s5_out_kernel(ub_ref, uf_ref, k_ref, sp_ref, are_ref, aim_ref, ldt_ref, ca_ref, cb_ref, d_ref,
                   y_ref, taps_ref):
    t = pl.program_id(0)
    fold = S5_CHUNK * GROUP_WIDTH

    @pl.when(t == 0)
    def _():
        for j in range(S5_CHUNK):
            taps_ref[GROUP_WIDTH * j:GROUP_WIDTH * (j + 1), :] = k_ref[j]
        taps_ref[fold:, :] = jnp.zeros((fold - GROUP_WIDTH, GROUP_WIDTH), BF16)

    lam_re, lam_im, _, _ = _s5_rows(are_ref[...], aim_ref[...], ldt_ref[...])
    p_re, p_im = _s5_power(lam_re, lam_im, (t + 1).astype(F32))
    w_out = (p_re * ca_ref[...] + p_im * cb_ref[...]).astype(BF16)
    start = pl.multiple_of((S5_CHUNK - 1 - t) * GROUP_WIDTH, GROUP_WIDTH)
    y_ref[...] = (_dot(ub_ref[...], taps_ref[pl.ds(start, fold), :]) + _dot_nt(sp_ref[...], w_out)
                  + d_ref[...] * uf_ref[...])


def _s5_embed(re, im):
    eye = jnp.eye(S5_GROUPS, dtype=F32)
    blocks = [(eye[None, :, None, :, None] * x[:, :, :, None, :]).reshape(-1, S5_GROUPS * S5_CH, S5_LANES // 2)
              for x in (re, im)]
    return jnp.concatenate(blocks, axis=2)


def _s5_params(a_re, a_im, log_dt, b_re, b_im, c_re, c_im, d_skip):
    depth = a_re.shape[0]
    row = lambda v: jnp.tile(v.reshape(depth, 1, S5_LANES // 2), (1, 1, 2))
    bt_re, bt_im = b_re.transpose(0, 1, 3, 2), b_im.transpose(0, 1, 3, 2)
    return dict(are=row(a_re), aim=row(a_im), ldt=row(jnp.repeat(log_dt, S5_STATE, axis=1)),
                b1=_s5_embed(bt_re, bt_im), b2=_s5_embed(-bt_im, bt_re),
                ca=_s5_embed(c_re, -c_im), cb=_s5_embed(-c_im, -c_re),
                d=d_skip.reshape(depth, 1, GROUP_WIDTH))


def _s5(u_b, u_f, p, layer):
    n_chunks, fold = u_b.shape
    row_spec, mat_spec = _pick((1, S5_LANES), layer), _pick((GROUP_WIDTH, S5_LANES), layer)
    col = lambda: pl.BlockSpec((n_chunks, GROUP_WIDTH), lambda s: (0, s))
    taps, s_prev = pl.pallas_call(
        functools.partial(_s5_state_kernel, n_chunks=n_chunks),
        grid=(S5_CHUNK,),
        in_specs=[col()] + [row_spec] * 3 + [mat_spec] * 3,
        out_specs=[pl.BlockSpec((1, GROUP_WIDTH, GROUP_WIDTH), lambda s: (s, 0, 0)),
                   _full((n_chunks, S5_LANES))],
        out_shape=[jax.ShapeDtypeStruct((S5_CHUNK, GROUP_WIDTH, GROUP_WIDTH), BF16),
                   jax.ShapeDtypeStruct((n_chunks, S5_LANES), BF16)],
        scratch_shapes=[pltpu.VMEM((GROUP_WIDTH, S5_LANES), F32), pltpu.VMEM((GROUP_WIDTH, S5_LANES), F32),
                        pltpu.VMEM((n_chunks, S5_LANES), F32)],
        compiler_params=_cparams("arbitrary"),
        name="s5_state",
    )(u_b, p['are'], p['aim'], p['ldt'], p['b1'], p['b2'], p['ca'])
    return pl.pallas_call(
        _s5_out_kernel,
        grid=(S5_CHUNK,),
        in_specs=[_full((n_chunks, fold)), col(), _full(taps.shape), _full(s_prev.shape)]
        + [row_spec] * 3 + [mat_spec] * 2 + [_pick((1, GROUP_WIDTH), layer)],
        out_specs=col(),
        out_shape=jax.ShapeDtypeStruct((n_chunks, fold), F32),
        scratch_shapes=[pltpu.VMEM(((2 * S5_CHUNK - 1) * GROUP_WIDTH, GROUP_WIDTH), BF16)],
        compiler_params=_cparams("arbitrary"),
        name="s5_out",
    )(u_b, u_f, taps, s_prev, p['are'], p['aim'], p['ldt'], p['ca'], p['cb'], p['d'])


EXPERT_ROW = 8


def _route(logits_t):
    tokens = logits_t.shape[1]
    big = jnp.int32(1 << 20)
    neg = jnp.float32(-jnp.inf)
    g_row = lax.broadcasted_iota(jnp.int32, (8, tokens), 0)
    is_group = g_row < MOE_GROUPS
    gl = jnp.where(is_group, logits_t[0:8, :], neg)
    ge = jnp.where(is_group, jnp.exp(gl - jnp.max(gl, axis=0, keepdims=True)), 0.0)
    gp = ge / jnp.sum(ge, axis=0, keepdims=True)
    p_g = jnp.max(gp, axis=0, keepdims=True)
    g_idx = jnp.min(jnp.where(is_group & (gp == p_g), g_row, big), axis=0, keepdims=True)
    e_row = lax.broadcasted_iota(jnp.int32, (MOE_EXPERTS, tokens), 0)
    in_group = (e_row // MOE_PER_GROUP) == g_idx
    el = jnp.where(in_group, logits_t[EXPERT_ROW:EXPERT_ROW + MOE_EXPERTS, :], neg)
    ee = jnp.where(in_group, jnp.exp(el - jnp.max(el, axis=0, keepdims=True)), 0.0)
    ep = ee / jnp.sum(ee, axis=0, keepdims=True)
    p1 = jnp.max(jnp.where(in_group, ep, -1.0), axis=0, keepdims=True)
    i1 = jnp.min(jnp.where(in_group & (ep == p1), e_row, big), axis=0, keepdims=True)
    rest = in_group & (e_row != i1)
    p2 = jnp.max(jnp.where(rest, ep, -1.0), axis=0, keepdims=True)
    i2 = jnp.min(jnp.where(rest & (ep == p2), e_row, big), axis=0, keepdims=True)
    tot = p1 + p2
    return i1, i2, p_g * p1 / tot, p_g * p2 / tot


ROW_SPLIT = 4
ROUTE_E1, ROUTE_E2, ROUTE_R1, ROUTE_R2, ROUTE_W1, ROUTE_W2 = range(6)
HIGH_HALF = 0xFFFF0000


def _split_rows(ref, value, rows):
    half = value.shape[1] // 2
    lo = lax.bitcast_convert_type(value[:, :half].astype(jnp.bfloat16).astype(F32), jnp.uint32)
    hi = lax.bitcast_convert_type(value[:, half:].astype(jnp.bfloat16).astype(F32), jnp.uint32)
    words = lax.bitcast_convert_type((lo >> 16) | (hi & jnp.uint32(HIGH_HALF)), jnp.int32)
    for j in range(ROW_SPLIT):
        ref[pl.ds(j, rows, stride=ROW_SPLIT), :] = words[:, 128 * j:128 * (j + 1)]


def _merge_rows(ref, rows):
    words = jnp.concatenate([ref[pl.ds(j, rows, stride=ROW_SPLIT), :] for j in range(ROW_SPLIT)], axis=1)
    words = lax.bitcast_convert_type(words, jnp.uint32)
    lo = lax.bitcast_convert_type(words << 16, F32)
    hi = lax.bitcast_convert_type(words & jnp.uint32(HIGH_HALF), F32)
    return jnp.concatenate([lo, hi], axis=1)


def _out_kernel(x_ref, s5_ref, ret_ref, m2_ref, hg_ref, wglu_ref, bglu_ref, wo_ref, g2_ref,
                wrh_ref, br_ref, stri_ref, x1_ref, h2_ref, route_ref, cnt_ref, s5_tmp, carry_ref,
                *, tile):
    @pl.when(pl.program_id(0) == 0)
    def _():
        carry_ref[...] = jnp.zeros_like(carry_ref)

    for s in range(S5_CHUNK):
        for j in range(GROUP_WIDTH // 128):
            lanes = slice(GROUP_WIDTH * s + 128 * j, GROUP_WIDTH * s + 128 * (j + 1))
            s5_tmp[j, pl.ds(s, tile // S5_CHUNK, stride=S5_CHUNK), :] = s5_ref[:, lanes]
    y = jnp.concatenate([s5_tmp[j] for j in range(GROUP_WIDTH // 128)], axis=1)
    y = y * (0.5 * (1.0 + jnp.tanh(math.sqrt(2.0 / math.pi) * (y + 0.044715 * (y * y * y)))))
    y = y * _sigmoid(_dot(y.astype(BF16), wglu_ref[...]) + bglu_ref[...])
    acc = x_ref[...] + _dot(y.astype(BF16), wo_ref[0:256, :])
    acc = acc + _dot(ret_ref[...], wo_ref[256:512, :])
    acc = acc + _dot(m2_ref[...], wo_ref[512:768, :])
    acc = acc + _dot(hg_ref[...], wo_ref[768:1024, :])
    x1_ref[...] = acc
    h2 = _rms(acc, g2_ref[...])
    _split_rows(h2_ref, h2, tile)
    hi, lo = _split2(h2)
    hw = _dot(hi, wrh_ref[...])
    logits = (hw[:, :ROUTE_LANES] + hw[:, ROUTE_LANES:] + _dot(lo, wrh_ref[:, :ROUTE_LANES])) + br_ref[...]
    e1, e2, w1, w2 = _route(logits.T)
    e_row = lax.broadcasted_iota(jnp.int32, (MOE_EXPERTS, tile), 0)
    picked = jnp.where((e_row == e1) | (e_row == e2), 1.0, 0.0)
    rank = carry_ref[:, 0:1] + _dot_nt(picked.astype(BF16), stri_ref[...])
    r1 = jnp.sum(jnp.where(e_row == e1, rank, 0.0), axis=0, keepdims=True)
    r2 = jnp.sum(jnp.where(e_row == e2, rank, 0.0), axis=0, keepdims=True)
    carry_ref[...] += jnp.sum(picked, axis=1, keepdims=True)
    cnt_ref[...] = carry_ref[...]
    rec_row = lax.broadcasted_iota(jnp.int32, (ROUTE_LANES, tile), 0)
    rec = jnp.zeros((ROUTE_LANES, tile), F32)
    for col, val in ((ROUTE_E1, e1.astype(F32)), (ROUTE_E2, e2.astype(F32)), (ROUTE_R1, r1), (ROUTE_R2, r2),
                     (ROUTE_W1, w1), (ROUTE_W2, w2)):
        rec = jnp.where(rec_row == col, val, rec)
    route_ref[...] = rec.T


def _out_proj(x, y_s5, y_ret, y_m2, y_hg, w_glu, b_glu, w_out, g2, wr_packed, b_route, layer):
    n, d = x.shape
    tile = min(SEQ_TILE, n)
    i = np.arange(tile)
    strict_lower = jnp.asarray(i[:, None] > i[None, :], dtype=BF16)
    return pl.pallas_call(
        functools.partial(_out_kernel, tile=tile),
        grid=(n // tile,),
        in_specs=[_rows(tile, d), _rows(tile // S5_CHUNK, S5_CHUNK * GROUP_WIDTH)] + [_rows(tile, GROUP_WIDTH)] * 3
        + [_pick((256, 256), layer), _pick((1, 256), layer), _pick((d, d), layer), _pick((1, d), layer),
           _pick((d, 2 * ROUTE_LANES), layer), _pick((1, ROUTE_LANES), layer), _full((tile, tile))],
        out_specs=[_rows(tile, d), _rows(ROW_SPLIT * tile, 128), _rows(tile, ROUTE_LANES),
                   _full((MOE_EXPERTS, ROUTE_LANES))],
        out_shape=[jax.ShapeDtypeStruct((n, d), F32), jax.ShapeDtypeStruct((ROW_SPLIT * n, 128), jnp.int32),
                   jax.ShapeDtypeStruct((n, ROUTE_LANES), F32), jax.ShapeDtypeStruct((MOE_EXPERTS, ROUTE_LANES), F32)],
        scratch_shapes=[pltpu.VMEM((GROUP_WIDTH // 128, tile, 128), F32),
                        pltpu.VMEM((MOE_EXPERTS, ROUTE_LANES), F32)],
        compiler_params=_cparams("arbitrary"),
        name="out_proj_router",
    )(x, y_s5, y_ret, y_m2, y_hg, w_glu, b_glu, w_out, g2, wr_packed, b_route, strict_lower)


def _sc_mesh():
    return plsc.VectorSubcoreMesh(core_axis_name="core", subcore_axis_name="subcore")


def _sc_scatter2(src, idx_a, idx_b, n_out):
    n = src.shape[0]

    @functools.partial(pl.kernel, out_type=jax.ShapeDtypeStruct((n_out, 128), src.dtype), mesh=_sc_mesh(),
                       scratch_types=[])
    def scatter_kernel(x_hbm, ia_hbm, ib_hbm, o_hbm):
        def body(x_vmem, ia_vmem, ib_vmem):
            pltpu.sync_copy(x_vmem, o_hbm.at[ia_vmem.at[0]])
            pltpu.sync_copy(x_vmem, o_hbm.at[ib_vmem.at[0]])

        pltpu.emit_pipeline(
            body, grid=(n // SC_WINDOW,),
            in_specs=[pl.BlockSpec((SC_WINDOW, 128), index_map=lambda i: (i, 0)),
                      pl.BlockSpec((1, SC_WINDOW), index_map=lambda i: (i, 0)),
                      pl.BlockSpec((1, SC_WINDOW), index_map=lambda i: (i, 0))],
            out_specs=[],
            core_axis_name=("core", "subcore"), dimension_semantics=(pltpu.PARALLEL,),
        )(x_hbm, ia_hbm, ib_hbm)

    return scatter_kernel(src, idx_a, idx_b)


def _sc_gather2(table, idx_a, idx_b):
    n = idx_a.size
    sds = jax.ShapeDtypeStruct((n, 128), table.dtype)

    @functools.partial(pl.kernel, out_type=(sds, sds), mesh=_sc_mesh(), scratch_types=[])
    def gather_kernel(t_hbm, ia_hbm, ib_hbm, oa_hbm, ob_hbm):
        def body(ia_vmem, ib_vmem, oa_vmem, ob_vmem):
            pltpu.sync_copy(t_hbm.at[ia_vmem.at[0]], oa_vmem)
            pltpu.sync_copy(t_hbm.at[ib_vmem.at[0]], ob_vmem)

        pltpu.emit_pipeline(
            body, grid=(n // SC_WINDOW,),
            in_specs=[pl.BlockSpec((1, SC_WINDOW), index_map=lambda i: (i, 0)),
                      pl.BlockSpec((1, SC_WINDOW), index_map=lambda i: (i, 0))],
            out_specs=[pl.BlockSpec((SC_WINDOW, 128), index_map=lambda i: (i, 0)),
                       pl.BlockSpec((SC_WINDOW, 128), index_map=lambda i: (i, 0))],
            core_axis_name=("core", "subcore"), dimension_semantics=(pltpu.PARALLEL,),
        )(ia_hbm, ib_hbm, oa_hbm, ob_hbm)

    return gather_kernel(table, idx_a, idx_b)


def _index_kernel(route_ref, off_ref, fold_ref, spread_ref, i1_ref, i2_ref, *, tile):
    rec = route_ref[...]
    lane = lax.broadcasted_iota(jnp.int32, rec.shape, 1)
    off = off_ref[...]
    sub = lax.broadcasted_iota(jnp.int32, (tile * ROW_SPLIT // 128, 128), 1) % ROW_SPLIT
    for out_ref, e_col, r_col in ((i1_ref, ROUTE_E1, ROUTE_R1), (i2_ref, ROUTE_E2, ROUTE_R2)):
        e = rec[:, e_col:e_col + 1].astype(jnp.int32)
        pos = jnp.sum(jnp.where(lane == e, off, 0.0), axis=-1, keepdims=True) + rec[:, r_col:r_col + 1]
        hi = jnp.floor(pos * (1.0 / 256.0))
        lo = pos - 256.0 * hi
        moved = (256.0 * _dot(fold_ref[...], (hi * spread_ref[...]).astype(BF16))
                 + _dot(fold_ref[...], (lo * spread_ref[...]).astype(BF16)))
        out_ref[...] = moved.astype(jnp.int32) * ROW_SPLIT + sub


def _dispatch_plan(route, counts, n_tiles):
    n = route.shape[0]
    cnt = counts[:, 0].astype(jnp.int32)
    padded = ((cnt + EXPERT_TILE - 1) // EXPERT_TILE) * EXPERT_TILE
    ends = jnp.cumsum(padded)
    off = ends - padded
    start = jnp.arange(n_tiles, dtype=jnp.int32) * EXPERT_TILE
    tile_expert = jnp.minimum(jnp.sum((start[:, None] >= ends[None, :]).astype(jnp.int32), axis=1), MOE_EXPERTS - 1)
    onehot = (tile_expert[:, None] == jnp.arange(MOE_EXPERTS, dtype=jnp.int32)[None, :]).astype(jnp.int32)
    seg_end = jnp.sum(onehot * (off + cnt)[None, :], axis=1)
    tile_rows = jnp.clip(seg_end - start, 0, EXPERT_TILE).astype(jnp.int32)
    prev = jnp.concatenate([jnp.full((1,), -1, jnp.int32), tile_expert[:-1]])
    tile_first = ((tile_expert != prev) & (tile_rows > 0)).astype(jnp.int32)

    tile = min(SEQ_TILE, n)
    off_row = jnp.zeros((1, ROUTE_LANES), F32).at[0, :MOE_EXPERTS].set(off.astype(F32))
    per_row = 128 // ROW_SPLIT
    t = np.arange(tile)
    out_rows = tile // per_row
    fold = jnp.asarray(np.arange(out_rows)[:, None] == (t // per_row)[None, :], dtype=BF16)
    spread = jnp.asarray((t % per_row)[:, None] == (np.arange(128) // ROW_SPLIT)[None, :], dtype=F32)
    idx1, idx2 = pl.pallas_call(
        functools.partial(_index_kernel, tile=tile),
        grid=(n // tile,),
        in_specs=[_rows(tile, ROUTE_LANES), _full((1, ROUTE_LANES)), _full(fold.shape), _full(spread.shape)],
        out_specs=[_rows(out_rows, 128)] * 2,
        out_shape=[jax.ShapeDtypeStruct((n // per_row, 128), jnp.int32)] * 2,
        compiler_params=_cparams("parallel"),
        name="dispatch_index",
    )(route, off_row, fold, spread)
    return idx1, idx2, tile_expert, tile_rows, tile_first


def _experts_kernel(te_ref, rows_ref, first_ref, xs_ref, wg_ref, wu_ref, wd_ref, y_ref, wgb, wub, wdb):
    i = pl.program_id(0)

    @pl.when(first_ref[i] == 1)
    def _():
        wgb[...] = wg_ref[0, 0].astype(BF16)
        wub[...] = wu_ref[0, 0].astype(BF16)
        wdb[...] = wd_ref[0, 0].astype(BF16)

    @pl.when(rows_ref[i] > 0)
    def _():
        x = _merge_rows(xs_ref, EXPERT_TILE)
        row = lax.broadcasted_iota(jnp.int32, x.shape, 0)
        x = jnp.where(row < rows_ref[i], x, 0.0).astype(BF16)
        act = _silu(_dot(x, wgb[...])) * _dot(x, wub[...])
        _split_rows(y_ref, _dot(act.astype(BF16), wdb[...]), EXPERT_TILE)


def _experts(xs, tile_expert, tile_rows, tile_first, w_gate, w_up, w_down, layer):
    n_tiles = tile_expert.shape[0]
    _, _, d, ff = w_gate.shape
    rows_blk = pl.BlockSpec((ROW_SPLIT * EXPERT_TILE, 128), lambda i, te, rows, first: (i, 0))
    return pl.pallas_call(
        _experts_kernel,
        grid_spec=pltpu.PrefetchScalarGridSpec(
            num_scalar_prefetch=3,
            grid=(n_tiles,),
            in_specs=[rows_blk,
                      pl.BlockSpec((1, 1, d, ff), lambda i, te, rows, first: (layer, te[i], 0, 0)),
                      pl.BlockSpec((1, 1, d, ff), lambda i, te, rows, first: (layer, te[i], 0, 0)),
                      pl.BlockSpec((1, 1, ff, d), lambda i, te, rows, first: (layer, te[i], 0, 0))],
            out_specs=rows_blk,
            scratch_shapes=[pltpu.VMEM((d, ff), BF16), pltpu.VMEM((d, ff), BF16), pltpu.VMEM((ff, d), BF16)],
        ),
        out_shape=jax.ShapeDtypeStruct(xs.shape, xs.dtype),
        compiler_params=_cparams("arbitrary"),
        name="moe_experts",
    )(tile_expert, tile_rows, tile_first, xs, w_gate, w_up, w_down)


def _combine_kernel(x1_ref, g1_ref, g2_ref, route_ref, gf_ref, o_ref, *, tile, final_norm):
    route = route_ref[...]
    out = (x1_ref[...] + route[:, ROUTE_W1:ROUTE_W1 + 1] * _merge_rows(g1_ref, tile)
           + route[:, ROUTE_W2:ROUTE_W2 + 1] * _merge_rows(g2_ref, tile))
    o_ref[...] = _rms(out, gf_ref[...]) if final_norm else out


def _combine(x1, g1, g2, route, g_final, final_norm):
    n, d = x1.shape
    tile = min(SEQ_TILE, n)
    return pl.pallas_call(
        functools.partial(_combine_kernel, tile=tile, final_norm=final_norm),
        grid=(n // tile,),
        in_specs=[_rows(tile, d), _rows(ROW_SPLIT * tile, 128), _rows(ROW_SPLIT * tile, 128),
                  _rows(tile, ROUTE_LANES), _full((1, d))],
        out_specs=_rows(tile, d),
        out_shape=jax.ShapeDtypeStruct((n, d), F32),
        compiler_params=_cparams("parallel"),
        name="moe_combine",
    )(x1, g1, g2, route, g_final)


def _moe(h2_rows, route, counts, x1, w_gate, w_up, w_down, layer, g_final, final_norm):
    n = x1.shape[0]
    n_tiles = (MOE_TOPK * n) // EXPERT_TILE + MOE_EXPERTS
    idx1, idx2, tile_expert, tile_rows, tile_first = _dispatch_plan(route, counts, n_tiles)
    xs = _sc_scatter2(h2_rows, idx1, idx2, ROW_SPLIT * n_tiles * EXPERT_TILE)
    ys = _experts(xs, tile_expert, tile_rows, tile_first, w_gate, w_up, w_down, layer)
    g1, g2 = _sc_gather2(ys, idx1, idx2)
    return _combine(x1, g1, g2, route, g_final, final_norm)


def kernel(x, positions, norm1_g, w_in, w_out, s5_a_re, s5_a_im, s5_log_dt, s5_b_re, s5_b_im, s5_c_re, s5_c_im, s5_d, s5_w_glu, s5_b_glu, m2_conv_w, m2_conv_b, m2_dt_bias, m2_a_log, m2_d, m2_norm_g, hg_lb_logits, hg_norm_g, norm2_g, moe_w_group, moe_b_group, moe_w_expert, moe_b_expert, moe_w_gate, moe_w_up, moe_w_down, final_norm_g):
    bsz, seqlen, d = x.shape
    assert bsz == 1 and seqlen % SEQ_TILE == 0 and (MOE_TOPK * seqlen) % EXPERT_TILE == 0
    depth = w_in.shape[0]

    lb_probs = jax.nn.softmax(hg_lb_logits.astype(F32), axis=0)
    lower_bounds = (jnp.cumsum(lb_probs, axis=0) - lb_probs[0]).reshape(depth, 1, GROUP_WIDTH)
    cos_t, sin_t = _rope_tables(positions.reshape(seqlen, 1))

    s5_p = _s5_params(s5_a_re, s5_a_im, s5_log_dt, s5_b_re, s5_b_im, s5_c_re, s5_c_im, s5_d)
    m2_p = _mamba2_params(m2_conv_w, m2_conv_b, m2_dt_bias, m2_a_log, m2_d, m2_norm_g)
    w_route = jnp.zeros((depth, d, ROUTE_LANES), F32)
    w_route = w_route.at[:, :, :MOE_GROUPS].set(moe_w_group)
    w_route = w_route.at[:, :, EXPERT_ROW:EXPERT_ROW + MOE_EXPERTS].set(moe_w_expert)
    wr_packed = jnp.concatenate(_split2(w_route), axis=2)
    b_route = jnp.zeros((depth, 1, ROUTE_LANES), F32)
    b_route = b_route.at[:, 0, :MOE_GROUPS].set(moe_b_group)
    b_route = b_route.at[:, 0, EXPERT_ROW:EXPERT_ROW + MOE_EXPERTS].set(moe_b_expert)
    w_glu_b, w_out_b = s5_w_glu.astype(BF16), w_out.astype(BF16)
    b_glu = s5_b_glu.reshape(depth, 1, GROUP_WIDTH)
    g1, g2 = norm1_g.reshape(depth, 1, d), norm2_g.reshape(depth, 1, d)
    hg_g = hg_norm_g.reshape(depth, 1, GROUP_WIDTH)
    g_final = final_norm_g.reshape(1, d)
    tile = min(SEQ_TILE, seqlen)
    chunk = min(CHUNK, tile)

    xc = x.reshape(seqlen, d)
    for l in range(depth):
        u_b, u_f, p_ret, p_m2, p_hg = _in_proj(xc, g1, _w_prep(w_in, l), l)
        y_s5 = _s5(u_b, u_f, s5_p, l)
        y_ret, y_m2, y_hg = _mixers(
            seqlen,
            _retention_call(p_ret, cos_t, sin_t, tile, min(RET_CHUNK, tile)),
            _mamba2_call(p_m2, m2_p, l, tile, chunk),
            _hgrn2_call(p_hg, lower_bounds, hg_g, l, tile, chunk))
        x1, h2_rows, route, counts = _out_proj(xc, y_s5, y_ret, y_m2, y_hg, w_glu_b, b_glu, w_out_b, g2,
                                               wr_packed, b_route, l)
        xc = _moe(h2_rows, route, counts, x1, moe_w_gate, moe_w_up, moe_w_down, l, g_final,
                  final_norm=(l == depth - 1))
    return xc.reshape(bsz, seqlen, d)
```

```python
import functools
import math

import numpy as np
import jax
import jax.numpy as jnp
from jax import lax
from jax.experimental import pallas as pl
from jax.experimental.pallas import tpu as pltpu
from jax.experimental.pallas import tpu_sc as plsc

F32 = jnp.float32
BF16 = jnp.bfloat16
NORM_EPS = 1e-6

GROUP_WIDTH = 256
HEAD_DIM = 64
N_HEADS = GROUP_WIDTH // HEAD_DIM
S5_GROUPS = 16
S5_CH = 16
S5_STATE = 64
S5_DT_CLAMP = -1e-4
M2_STATE = 128
M2_CONV = 4
M2_CONV_DIM = 768
ROPE_BASE = 10000.0
MOE_GROUPS = 4
MOE_PER_GROUP = 4
MOE_EXPERTS = 16
ROUTE_LANES = 128

SEQ_TILE = 512
CHUNK = 128
RET_CHUNK = 256
CHUNK_UNROLL = 4
S5_CHUNK = 16
MOE_TOPK = 2
EXPERT_TILE = 512
SC_WINDOW = 128
VMEM_LIMIT = 56 * 1024 * 1024


def _cparams(*sem):
    return pltpu.CompilerParams(dimension_semantics=sem, vmem_limit_bytes=VMEM_LIMIT)


def _dot(a, b):
    return jnp.dot(a, b, preferred_element_type=F32)


def _dot_nt(a, b):
    return lax.dot_general(a, b, (((1,), (1,)), ((), ())), preferred_element_type=F32)


def _dot_tn(a, b):
    return lax.dot_general(a, b, (((0,), (0,)), ((), ())), preferred_element_type=F32)


def _split2(x):
    hi = x.astype(BF16)
    return hi, (x - hi.astype(F32)).astype(BF16)


def _split3(x):
    hi = x.astype(BF16)
    r = x - hi.astype(F32)
    mid = r.astype(BF16)
    return hi, mid, (r - mid.astype(F32)).astype(BF16)


def _dot_exact_lhs(m, x):
    hi, mid, lo = _split3(x)
    return _dot(m, hi) + _dot(m, mid) + _dot(m, lo)


def _dot_exact_rhs(x, m):
    hi, lo = _split2(x)
    return _dot(hi, m) + _dot(lo, m)


def _sigmoid(x):
    return 1.0 / (1.0 + jnp.exp(-x))


def _silu(x):
    return x * _sigmoid(x)


def _rms(x, g):
    return x * lax.rsqrt(jnp.mean(x * x, axis=-1, keepdims=True) + NORM_EPS) * g


def _full(shape):
    return pl.BlockSpec(shape, lambda *_: (0,) * len(shape))


def _rows(tile, width):
    return pl.BlockSpec((tile, width), lambda i: (i, 0))


def _pick(shape, layer):
    return pl.BlockSpec((None,) + tuple(shape), lambda *_: (layer,) + (0,) * len(shape))


IN_SEGMENTS = (256, 1024, 1280, 1024)


DT_COL = 9 * GROUP_WIDTH
W_PREP_ROWS = 128


def _w_prep_kernel(w_ref, o_ref):
    o_ref[:, 0:DT_COL] = w_ref[0, :, 0:DT_COL].astype(BF16)
    head = lax.broadcasted_iota(jnp.int32, (W_PREP_ROWS, GROUP_WIDTH), 1) // HEAD_DIM
    rep = jnp.zeros((W_PREP_ROWS, GROUP_WIDTH), F32)
    for h in range(N_HEADS):
        rep = jnp.where(head == h, w_ref[0, :, DT_COL + h:DT_COL + h + 1], rep)
    o_ref[:, DT_COL:DT_COL + GROUP_WIDTH] = rep.astype(BF16)
    o_ref[:, DT_COL + GROUP_WIDTH:] = w_ref[0, :, DT_COL + N_HEADS:].astype(BF16)


def _w_prep(w_in, layer):
    _, d, n_in = w_in.shape
    return pl.pallas_call(
        _w_prep_kernel,
        grid=(d // W_PREP_ROWS,),
        in_specs=[pl.BlockSpec((1, W_PREP_ROWS, n_in), lambda i: (layer, i, 0))],
        out_specs=pl.BlockSpec((W_PREP_ROWS, sum(IN_SEGMENTS)), lambda i: (i, 0)),
        out_shape=jax.ShapeDtypeStruct((d, sum(IN_SEGMENTS)), BF16),
        compiler_params=_cparams("parallel"),
        name="w_in_prep",
    )(w_in)


def _in_proj_kernel(x_ref, g_ref, w_ref, ub_ref, uf_ref, ret_ref, m2_ref, hg_ref, u_tmp, *, tile):
    hb = _rms(x_ref[...], g_ref[...]).astype(BF16)
    c0, c1, c2, c3 = np.cumsum(IN_SEGMENTS)
    ret_ref[...] = _dot(hb, w_ref[:, c0:c1])
    m2_ref[...] = _dot(hb, w_ref[:, c1:c2])
    hg_ref[...] = _dot(hb, w_ref[:, c2:c3])
    u = _dot(hb, w_ref[:, 0:c0])
    for j in range(GROUP_WIDTH // 128):
        u_tmp[j] = u[:, 128 * j:128 * (j + 1)]
    for s in range(S5_CHUNK):
        for j in range(GROUP_WIDTH // 128):
            v = u_tmp[j, pl.ds(s, tile // S5_CHUNK, stride=S5_CHUNK), :]
            lanes = slice(GROUP_WIDTH * s + 128 * j, GROUP_WIDTH * s + 128 * (j + 1))
            uf_ref[:, lanes] = v
            ub_ref[:, lanes] = v.astype(BF16)


def _in_proj(x, g, w, layer):
    n, d = x.shape
    tile = min(SEQ_TILE, n)
    fold = S5_CHUNK * GROUP_WIDTH
    return pl.pallas_call(
        functools.partial(_in_proj_kernel, tile=tile),
        grid=(n // tile,),
        in_specs=[_rows(tile, d), _pick((1, d), layer), _full(w.shape)],
        out_specs=[_rows(tile // S5_CHUNK, fold)] * 2 + [_rows(tile, s) for s in IN_SEGMENTS[1:]],
        out_shape=[jax.ShapeDtypeStruct((n // S5_CHUNK, fold), BF16),
                   jax.ShapeDtypeStruct((n // S5_CHUNK, fold), F32)]
        + [jax.ShapeDtypeStruct((n, s), F32) for s in IN_SEGMENTS[1:]],
        scratch_shapes=[pltpu.VMEM((GROUP_WIDTH // 128, tile, 128), F32)],
        compiler_params=_cparams("parallel"),
        name="in_proj",
    )(x, g, w)


def _rope_kernel(pos_ref, invf_ref, cos_ref, sin_ref):
    ang = pos_ref[...].astype(F32) * invf_ref[...]
    cos_ref[...] = jnp.cos(ang)
    sin_ref[...] = jnp.sin(ang)


def _rope_tables(positions):
    n = positions.shape[0]
    tile = min(SEQ_TILE, n)
    half = HEAD_DIM // 2
    inv_freq = ROPE_BASE ** (-jnp.arange(half, dtype=F32) / half)
    invf = jnp.tile(inv_freq, 128 // half).reshape(1, 128)
    return pl.pallas_call(
        _rope_kernel,
        grid=(n // tile,),
        in_specs=[_rows(tile, 1), _full((1, 128))],
        out_specs=[_rows(tile, 128), _rows(tile, 128)],
        out_shape=[jax.ShapeDtypeStruct((n, 128), F32)] * 2,
        compiler_params=_cparams("parallel"),
        name="rope_tables",
    )(positions, invf)


def _head_mean_matrix():
    h = np.arange(GROUP_WIDTH) // HEAD_DIM
    return jnp.asarray((h[:, None] == h[None, :]) / HEAD_DIM, dtype=BF16)


def _head_block_mask():
    h = np.arange(GROUP_WIDTH) // HEAD_DIM
    return jnp.asarray(h[:, None] == h[None, :], dtype=F32)


HEAD_PAIRS = GROUP_WIDTH // 128


def _lanes(x, j):
    return x[:, 128 * j:128 * (j + 1)]


def _stack_pair(x):
    xb = x.astype(BF16)
    low = lax.broadcasted_iota(jnp.int32, x.shape, 1) < HEAD_DIM
    zero = jnp.zeros_like(xb)
    return jnp.concatenate([jnp.where(low, xb, zero), jnp.where(low, zero, xb)], axis=0)


def _pair_scores(q, kb):
    s = _dot_nt(_stack_pair(q), kb)
    return s[:q.shape[0]], s[q.shape[0]:]


def _pair_apply(a0, a1, v):
    return _dot(jnp.concatenate([a0.astype(BF16), a1.astype(BF16)], axis=1), _stack_pair(v))


def _ret_constants(chunk):
    lg = np.log1p(-np.exp2(-5.0 - np.arange(N_HEADS, dtype=np.float64)))
    idx = np.arange(chunk, dtype=np.float64)
    rel = idx[:, None] - idx[None, :]
    decay = np.where(rel >= 0, np.exp(np.maximum(rel, 0.0)[None] * lg[:, None, None]), 0.0)
    lane_lg = np.repeat(lg, HEAD_DIM)
    xi = np.exp((idx + 1.0)[:, None] * lane_lg[None, :])
    zeta = np.exp((chunk - 1.0 - idx)[:, None] * lane_lg[None, :])
    h = np.arange(GROUP_WIDTH) // HEAD_DIM
    gc = np.where(h[:, None] == h[None, :], np.exp(chunk * lane_lg)[:, None], 0.0)
    f = lambda a: jnp.asarray(a, dtype=F32)
    return f(decay), f(xi), f(zeta), f(gc)


def _ret_kernel(p_ref, cos_ref, sin_ref, dec_ref, xi_ref, zeta_ref, gc_ref, bm_ref, gm_ref,
                o_ref, r_ref, *, chunk, n_chunks):
    lane = lax.broadcasted_iota(jnp.int32, (chunk, GROUP_WIDTH), 1)
    first_half = (lane % HEAD_DIM) < (HEAD_DIM // 2)
    gm = gm_ref[...]

    def rope(t, cos2, sin2):
        rot = jnp.where(first_half, -pltpu.roll(t, GROUP_WIDTH - HEAD_DIM // 2, 1),
                        pltpu.roll(t, HEAD_DIM // 2, 1))
        return t * cos2 + rot * sin2

    def body(c, carry):
        rows = pl.ds(pl.multiple_of(c * chunk, chunk), chunk)
        cs = cos_ref[rows, :]
        sn = sin_ref[rows, :]
        cos2 = jnp.concatenate([cs, cs], axis=1)
        sin2 = jnp.concatenate([sn, sn], axis=1)
        q = rope(p_ref[rows, 0:256], cos2, sin2)
        k = rope(p_ref[rows, 256:512], cos2, sin2) * (HEAD_DIM ** -0.5)
        v = p_ref[rows, 512:768]
        g = p_ref[rows, 768:1024]
        kb = k.astype(BF16)
        inner = []
        for j in range(HEAD_PAIRS):
            s0, s1 = _pair_scores(_lanes(q, j), _lanes(kb, j))
            inner.append(_pair_apply(s0 * dec_ref[2 * j], s1 * dec_ref[2 * j + 1], _lanes(v, j)))
        inner = jnp.concatenate(inner, axis=1)
        r_prev = r_ref[...]
        cross = _dot((q * xi_ref[...]).astype(BF16), r_prev.astype(BF16))
        r_ref[...] = gc_ref[...] * r_prev + bm_ref[...] * _dot_tn(kb, (zeta_ref[...] * v).astype(BF16))
        o = inner + cross
        cen = o - _dot_exact_rhs(o, gm)
        var = _dot_exact_rhs(cen * cen, gm)
        o_ref[rows, :] = (cen * lax.rsqrt(var + NORM_EPS) * _silu(g)).astype(BF16)
        return carry

    lax.fori_loop(0, n_chunks, body, 0, unroll=CHUNK_UNROLL)


def _retention_call(proj, cos_t, sin_t, tile, chunk):
    decay, xi, zeta, gc = _ret_constants(chunk)
    body = functools.partial(_ret_kernel, chunk=chunk, n_chunks=tile // chunk)
    operands = (proj, cos_t, sin_t, decay, xi, zeta, gc, _head_block_mask(), _head_mean_matrix())
    specs = [_rows(tile, 1024), _rows(tile, 128), _rows(tile, 128), _full(decay.shape), _full(xi.shape),
             _full(zeta.shape), _full(gc.shape), _full((256, 256)), _full((256, 256))]
    return body, operands, specs, [pltpu.VMEM((GROUP_WIDTH, GROUP_WIDTH), F32)]


def _tri_matrix(chunk):
    i = np.arange(chunk)
    return jnp.asarray(i[:, None] >= i[None, :], dtype=BF16)


def _m2_kernel(p_ref, cw_ref, cb_ref, dtb_ref, alog_ref, d_ref, ng_ref, tri_ref,
               o_ref, tail_ref, ext_ref, act_ref, st_ref, *, tile, chunk, n_chunks):
    ext_ref[0:8, :] = tail_ref[...]
    ext_ref[8:tile + 8, :] = p_ref[:, 256:1024]
    tail_ref[...] = p_ref[tile - 8:tile, 256:1024]
    conv = cb_ref[...]
    for j in range(M2_CONV):
        lo = 8 - (M2_CONV - 1) + j
        conv = conv + cw_ref[j:j + 1, :] * ext_ref[lo:lo + tile, :]
    act_ref[...] = _silu(conv)
    a_lane = -jnp.exp(alog_ref[...])
    tri = tri_ref[...]
    ti = lax.broadcasted_iota(jnp.int32, (chunk, chunk), 0)
    si = lax.broadcasted_iota(jnp.int32, (chunk, chunk), 1)
    causal = ti >= si
    lane = lax.broadcasted_iota(jnp.int32, (chunk, 128), 1)

    def body(c, carry):
        start = pl.multiple_of(c * chunk, chunk)
        rows = pl.ds(start, chunk)
        xbc = act_ref[rows, :]
        xs = xbc[:, 0:256]
        z = p_ref[rows, 0:256]
        x_dt = p_ref[rows, 1024:1280] + dtb_ref[...]
        dt = jnp.maximum(x_dt, 0.0) + jnp.log1p(jnp.exp(-jnp.abs(x_dt)))
        acum = _dot_exact_lhs(tri, dt * a_lane)
        acum_t = acum.T
        a_last = acum[chunk - 1:chunk, :]
        e_acum = jnp.exp(acum)
        decs = jnp.exp(a_last - acum)
        d_chunk = jnp.exp(a_last)
        xc = xs * dt
        ys = []
        for g in range(2):
            sl = slice(128 * g, 128 * (g + 1))
            bmg = xbc[:, 256 + 128 * g:256 + 128 * (g + 1)].astype(BF16)
            cmg = xbc[:, 512 + 128 * g:512 + 128 * (g + 1)].astype(BF16)
            cb = _dot_nt(cmg, bmg)
            xcg = xc[:, sl]
            yd = jnp.zeros((chunk, 128), F32)
            for hh in range(2):
                col0 = 128 * g + HEAD_DIM * hh
                diff = acum[:, col0:col0 + 1] - acum_t[col0:col0 + 1, :]
                lm = jnp.where(causal, jnp.exp(jnp.where(causal, diff, 0.0)), 0.0)
                xm = jnp.where((lane // HEAD_DIM) == hh, xcg, 0.0).astype(BF16)
                yd = yd + _dot((cb * lm).astype(BF16), xm)
            st = st_ref[:, sl]
            y_off = _dot(cmg, st.astype(BF16)) * e_acum[:, sl]
            st_ref[:, sl] = d_chunk[:, sl] * st + _dot_tn(bmg, (xcg * decs[:, sl]).astype(BF16))
            ys.append(yd + y_off + d_ref[:, sl] * xs[:, sl])
        y = jnp.concatenate(ys, axis=1) * _silu(z)
        o_ref[rows, :] = _rms(y, ng_ref[...]).astype(BF16)
        return carry

    lax.fori_loop(0, n_chunks, body, 0, unroll=CHUNK_UNROLL)


def _mamba2_params(conv_w, conv_b, dt_bias, a_log, d_skip, norm_g):
    depth = conv_w.shape[0]
    lanes = lambda v: jnp.repeat(v, HEAD_DIM, axis=1).reshape(depth, 1, GROUP_WIDTH)
    return (conv_w, conv_b.reshape(depth, 1, -1), lanes(dt_bias), lanes(a_log), lanes(d_skip),
            norm_g.reshape(depth, 1, -1))


def _mamba2_call(proj, params, layer, tile, chunk):
    body = functools.partial(_m2_kernel, tile=tile, chunk=chunk, n_chunks=tile // chunk)
    operands = (proj,) + tuple(params) + (_tri_matrix(chunk),)
    specs = ([_rows(tile, 1280), _pick((M2_CONV, M2_CONV_DIM), layer), _pick((1, M2_CONV_DIM), layer)]
             + [_pick((1, GROUP_WIDTH), layer)] * 4 + [_full((chunk, chunk))])
    scratch = [pltpu.VMEM((8, M2_CONV_DIM), F32), pltpu.VMEM((tile + 8, M2_CONV_DIM), F32),
               pltpu.VMEM((tile, M2_CONV_DIM), F32), pltpu.VMEM((M2_STATE, GROUP_WIDTH), F32)]
    return body, operands, specs, scratch


HG_MATMUL_LEVELS = 3


def _hg_exponent_matrix(chunk):
    levels = HG_MATMUL_LEVELS
    t = np.arange(chunk)[:, None]
    r = np.arange(chunk)[None, :]
    blocks = []
    for lvl in range(levels):
        b = 1 << lvl
        blk = t // b
        odd = (blk % 2) == 1
        q_rows = odd & (r >= blk * b) & (r <= t)
        k_rows = (~odd) & (r > t) & (r <= (blk + 1) * b - 1)
        blocks.append(q_rows | k_rows)
    blocks.append(r <= t)
    return jnp.asarray(np.concatenate(blocks, axis=0), dtype=BF16)


def _hg_kernel(p_ref, lb_ref, ng_ref, gexp_ref, bm_ref, gm_ref, o_ref, st_ref, *, chunk, n_chunks):
    levels = int(math.log2(chunk))
    row = lax.broadcasted_iota(jnp.int32, (chunk, 128), 0)
    odd_rows = [((row >> lvl) & 1) == 1 for lvl in range(levels)]
    row_wide = lax.broadcasted_iota(jnp.int32, (chunk, GROUP_WIDTH), 0)
    odd_rows_wide = [((row_wide >> lvl) & 1) == 1 for lvl in range(levels)]
    ti = lax.broadcasted_iota(jnp.int32, (chunk, chunk), 0)
    si = lax.broadcasted_iota(jnp.int32, (chunk, chunk), 1)
    pair_level = [((ti >> (lvl + 1)) == (si >> (lvl + 1))) & (((ti >> lvl) & 1) == 1) & (((si >> lvl) & 1) == 0)
                  for lvl in range(levels)]
    lb = lb_ref[...]
    gm = gm_ref[...]

    def body(c, carry):
        rows = pl.ds(pl.multiple_of(c * chunk, chunk), chunk)
        q = _silu(p_ref[rows, 0:256])
        forget = lb + (1.0 - lb) * _sigmoid(p_ref[rows, 256:512])
        k = 1.0 - forget
        v = p_ref[rows, 512:768]
        g = p_ref[rows, 768:1024]
        lf_hi, lf_lo = _split2(jnp.log(forget))
        expo = _dot(gexp_ref[...], lf_hi) + _dot(gexp_ref[...], lf_lo)
        bcum = expo[HG_MATMUL_LEVELS * chunk:(HG_MATMUL_LEVELS + 1) * chunk, :]

        def level_log_decay(lvl):
            if lvl < HG_MATMUL_LEVELS:
                return expo[lvl * chunk:(lvl + 1) * chunk, :]
            b = 1 << lvl
            ref = jnp.concatenate([jnp.broadcast_to(bcum[m + b - 1:m + b, :], (2 * b, GROUP_WIDTH))
                                   for m in range(0, chunk, 2 * b)], axis=0)
            return jnp.where(odd_rows_wide[lvl], bcum - ref, ref - bcum)


        log_decay = [level_log_decay(lvl) for lvl in range(levels)]
        intra = []
        for j in range(HEAD_PAIRS):
            qj, kj = _lanes(q, j), _lanes(k, j)
            a0, a1 = (jnp.where(ti == si, s, 0.0) for s in _pair_scores(qj, kj.astype(BF16)))
            for lvl in range(levels):
                w = jnp.exp(_lanes(log_decay[lvl], j)) * jnp.where(odd_rows[lvl], qj, kj)
                s0, s1 = _pair_scores(w, w.astype(BF16))
                a0 = jnp.where(pair_level[lvl], s0, a0)
                a1 = jnp.where(pair_level[lvl], s1, a1)
            intra.append(_pair_apply(a0, a1, _lanes(v, j)))
        intra = jnp.concatenate(intra, axis=1)

        b_last = bcum[chunk - 1:chunk, :]
        suffix = b_last - bcum
        st = st_ref[...]
        cross = _dot_nt((q * jnp.exp(bcum)).astype(BF16), st.astype(BF16))
        st_ref[...] = jnp.exp(b_last) * st + bm_ref[...] * _dot_tn(
            v.astype(BF16), (k * jnp.exp(suffix)).astype(BF16))
        o = intra + cross
        o = o * lax.rsqrt(_dot_exact_rhs(o * o, gm) + NORM_EPS) * ng_ref[...]
        o_ref[rows, :] = (o * _silu(g)).astype(BF16)
        return carry

    lax.fori_loop(0, n_chunks, body, 0, unroll=CHUNK_UNROLL)


def _hgrn2_call(proj, lower_bounds, norm_g, layer, tile, chunk):
    gexp = _hg_exponent_matrix(chunk)
    body = functools.partial(_hg_kernel, chunk=chunk, n_chunks=tile // chunk)
    operands = (proj, lower_bounds, norm_g, gexp, _head_block_mask(), _head_mean_matrix())
    specs = [_rows(tile, 1024), _pick((1, GROUP_WIDTH), layer), _pick((1, GROUP_WIDTH), layer), _full(gexp.shape),
             _full((256, 256)), _full((256, 256))]
    return body, operands, specs, [pltpu.VMEM((GROUP_WIDTH, GROUP_WIDTH), F32)]


def _mixers_kernel(*refs, bodies, n_in, n_scratch):
    n_mix = len(bodies)
    ins, pos = [], 0
    for k in n_in:
        ins.append(refs[pos:pos + k])
        pos += k
    outs = refs[pos:pos + n_mix]
    pos += n_mix
    scratch = []
    for k in n_scratch:
        scratch.append(refs[pos:pos + k])
        pos += k

    @pl.when(pl.program_id(0) == 0)
    def _():
        for group in scratch:
            for ref in group:
                ref[...] = jnp.zeros_like(ref)

    for body, i, o, s in zip(bodies, ins, outs, scratch):
        body(*i, o, *s)


def _mixers(n, *calls):
    tile = min(SEQ_TILE, n)
    bodies = tuple(c[0] for c in calls)
    return pl.pallas_call(
        functools.partial(_mixers_kernel, bodies=bodies, n_in=tuple(len(c[1]) for c in calls),
                          n_scratch=tuple(len(c[3]) for c in calls)),
        grid=(n // tile,),
        in_specs=[s for c in calls for s in c[2]],
        out_specs=[_rows(tile, GROUP_WIDTH)] * len(calls),
        out_shape=[jax.ShapeDtypeStruct((n, GROUP_WIDTH), BF16)] * len(calls),
        scratch_shapes=[s for c in calls for s in c[3]],
        compiler_params=_cparams("arbitrary"),
        name="mixers",
    )(*[a for c in calls for a in c[1]])


S5_LANES = S5_GROUPS * 2 * S5_STATE
S5_TAP_SPLITS = 4


def _s5_rows(a_re, a_im, log_dt):
    are = jnp.minimum(a_re, S5_DT_CLAMP)
    dt = jnp.exp(log_dt)
    lam_re = are * dt
    lam_im = a_im * dt
    mag = jnp.exp(lam_re)
    ab_re = mag * jnp.cos(lam_im)
    ab_im = mag * jnp.sin(lam_im)
    den = are * are + a_im * a_im
    k_re = ((ab_re - 1.0) * are + ab_im * a_im) / den
    k_im = (ab_im * are - (ab_re - 1.0) * a_im) / den
    return lam_re, lam_im, k_re, k_im


def _s5_power(lam_re, lam_im, e):
    m = jnp.exp(e * lam_re)
    return m * jnp.cos(e * lam_im), m * jnp.sin(e * lam_im)


def _s5_state_kernel(u_ref, are_ref, aim_ref, ldt_ref, b1_ref, b2_ref, ca_ref, k_ref, sp_ref,
                     bb1_ref, bb2_ref, inc_ref, *, n_chunks):
    s = pl.program_id(0)
    lam_re, lam_im, k_re, k_im = _s5_rows(are_ref[...], aim_ref[...], ldt_ref[...])

    @pl.when(s == 0)
    def _():
        bb1_ref[...] = k_re * b1_ref[...] + k_im * b2_ref[...]
        bb2_ref[...] = k_re * b2_ref[...] - k_im * b1_ref[...]

    p_re, p_im = _s5_power(lam_re, lam_im, (S5_CHUNK - 1 - s).astype(F32))
    w = (p_re * bb1_ref[...] + p_im * bb2_ref[...]).astype(BF16)
    k_ref[0] = _dot_nt(w, ca_ref[...].astype(BF16)).astype(BF16)
    contrib = _dot(u_ref[...], w)

    @pl.when(s == 0)
    def _():
        inc_ref[...] = contrib

    @pl.when(s > 0)
    def _():
        inc_ref[...] += contrib

    @pl.when(s == S5_CHUNK - 1)
    def _():
        n_steps = max(1, int(math.ceil(math.log2(n_chunks))))
        step = lax.broadcasted_iota(jnp.int32, (16, S5_LANES), 0)
        e = (jnp.left_shift(1, step) * S5_CHUNK).astype(F32)
        a_re_all, a_im_all = _s5_power(lam_re, lam_im, e)
        row = lax.broadcasted_iota(jnp.int32, (n_chunks, 128), 0)
        half = S5_LANES // 2

        def shifted(x, sh):
            return jnp.where(row >= sh, pltpu.roll(x, sh, 0), 0.0)

        for j in range(half // 128):
            re_l, im_l = slice(128 * j, 128 * (j + 1)), slice(half + 128 * j, half + 128 * (j + 1))
            x_re, x_im = inc_ref[:, re_l], inc_ref[:, im_l]
            for k in range(n_steps):
                sh = 1 << k
                if sh >= n_chunks:
                    break
                a_re, a_im = a_re_all[k:k + 1, re_l], a_im_all[k:k + 1, re_l]
                p_re, p_im = shifted(x_re, sh), shifted(x_im, sh)
                x_re, x_im = x_re + a_re * p_re - a_im * p_im, x_im + a_re * p_im + a_im * p_re
            sp_ref[:, re_l] = shifted(x_re, 1).astype(BF16)
            sp_ref[:, im_l] = shifted(x_im, 1).astype(BF16)


def _s5_out_kernel(ub_ref, uf_ref, k_ref, sp_ref, are_ref, aim_ref, ldt_ref, ca_ref, cb_ref, d_ref,
                   y_ref, taps_ref):
    t = pl.program_id(0)
    fold = S5_CHUNK * GROUP_WIDTH

    @pl.when(t == 0)
    def _():
        for j in range(S5_CHUNK):
            taps_ref[GROUP_WIDTH * j:GROUP_WIDTH * (j + 1), :] = k_ref[j]
        taps_ref[fold:, :] = jnp.zeros((fold - GROUP_WIDTH, GROUP_WIDTH), BF16)

    lam_re, lam_im, _, _ = _s5_rows(are_ref[...], aim_ref[...], ldt_ref[...])
    p_re, p_im = _s5_power(lam_re, lam_im, (t + 1).astype(F32))
    w_out = (p_re * ca_ref[...] + p_im * cb_ref[...]).astype(BF16)
    start = pl.multiple_of((S5_CHUNK - 1 - t) * GROUP_WIDTH, GROUP_WIDTH)
    y_ref[...] = _dot_nt(sp_ref[...], w_out) + d_ref[...] * uf_ref[...]
    quarter = S5_CHUNK // S5_TAP_SPLITS
    for part in range(S5_TAP_SPLITS):
        width = (part + 1) * quarter * GROUP_WIDTH

        @pl.when((t >= part * quarter) & (t < (part + 1) * quarter))
        def _():
            y_ref[...] += _dot(ub_ref[:, 0:width], taps_ref[pl.ds(start, width), :])


def _s5_embed(re, im):
    eye = jnp.eye(S5_GROUPS, dtype=F32)
    blocks = [(eye[None, :, None, :, None] * x[:, :, :, None, :]).reshape(-1, S5_GROUPS * S5_CH, S5_LANES // 2)
              for x in (re, im)]
    return jnp.concatenate(blocks, axis=2)


def _s5_params(a_re, a_im, log_dt, b_re, b_im, c_re, c_im, d_skip):
    depth = a_re.shape[0]
    row = lambda v: jnp.tile(v.reshape(depth, 1, S5_LANES // 2), (1, 1, 2))
    bt_re, bt_im = b_re.transpose(0, 1, 3, 2), b_im.transpose(0, 1, 3, 2)
    return dict(are=row(a_re), aim=row(a_im), ldt=row(jnp.repeat(log_dt, S5_STATE, axis=1)),
                b1=_s5_embed(bt_re, bt_im), b2=_s5_embed(-bt_im, bt_re),
                ca=_s5_embed(c_re, -c_im), cb=_s5_embed(-c_im, -c_re),
                d=d_skip.reshape(depth, 1, GROUP_WIDTH))


def _s5(u_b, u_f, p, layer):
    n_chunks, fold = u_b.shape
    row_spec, mat_spec = _pick((1, S5_LANES), layer), _pick((GROUP_WIDTH, S5_LANES), layer)
    col = lambda: pl.BlockSpec((n_chunks, GROUP_WIDTH), lambda s: (0, s))
    taps, s_prev = pl.pallas_call(
        functools.partial(_s5_state_kernel, n_chunks=n_chunks),
        grid=(S5_CHUNK,),
        in_specs=[col()] + [row_spec] * 3 + [mat_spec] * 3,
        out_specs=[pl.BlockSpec((1, GROUP_WIDTH, GROUP_WIDTH), lambda s: (s, 0, 0)),
                   _full((n_chunks, S5_LANES))],
        out_shape=[jax.ShapeDtypeStruct((S5_CHUNK, GROUP_WIDTH, GROUP_WIDTH), BF16),
                   jax.ShapeDtypeStruct((n_chunks, S5_LANES), BF16)],
        scratch_shapes=[pltpu.VMEM((GROUP_WIDTH, S5_LANES), F32), pltpu.VMEM((GROUP_WIDTH, S5_LANES), F32),
                        pltpu.VMEM((n_chunks, S5_LANES), F32)],
        compiler_params=_cparams("arbitrary"),
        name="s5_state",
    )(u_b, p['are'], p['aim'], p['ldt'], p['b1'], p['b2'], p['ca'])
    return pl.pallas_call(
        _s5_out_kernel,
        grid=(S5_CHUNK,),
        in_specs=[_full((n_chunks, fold)), col(), _full(taps.shape), _full(s_prev.shape)]
        + [row_spec] * 3 + [mat_spec] * 2 + [_pick((1, GROUP_WIDTH), layer)],
        out_specs=col(),
        out_shape=jax.ShapeDtypeStruct((n_chunks, fold), F32),
        scratch_shapes=[pltpu.VMEM(((2 * S5_CHUNK - 1) * GROUP_WIDTH, GROUP_WIDTH), BF16)],
        compiler_params=_cparams("arbitrary"),
        name="s5_out",
    )(u_b, u_f, taps, s_prev, p['are'], p['aim'], p['ldt'], p['ca'], p['cb'], p['d'])


EXPERT_ROW = 8


def _route(logits_t):
    tokens = logits_t.shape[1]
    big = jnp.int32(1 << 20)
    neg = jnp.float32(-jnp.inf)
    g_row = lax.broadcasted_iota(jnp.int32, (8, tokens), 0)
    is_group = g_row < MOE_GROUPS
    gl = jnp.where(is_group, logits_t[0:8, :], neg)
    ge = jnp.where(is_group, jnp.exp(gl - jnp.max(gl, axis=0, keepdims=True)), 0.0)
    gp = ge / jnp.sum(ge, axis=0, keepdims=True)
    p_g = jnp.max(gp, axis=0, keepdims=True)
    g_idx = jnp.min(jnp.where(is_group & (gp == p_g), g_row, big), axis=0, keepdims=True)
    e_row = lax.broadcasted_iota(jnp.int32, (MOE_EXPERTS, tokens), 0)
    in_group = (e_row // MOE_PER_GROUP) == g_idx
    el = jnp.where(in_group, logits_t[EXPERT_ROW:EXPERT_ROW + MOE_EXPERTS, :], neg)
    ee = jnp.where(in_group, jnp.exp(el - jnp.max(el, axis=0, keepdims=True)), 0.0)
    ep = ee / jnp.sum(ee, axis=0, keepdims=True)
    p1 = jnp.max(jnp.where(in_group, ep, -1.0), axis=0, keepdims=True)
    i1 = jnp.min(jnp.where(in_group & (ep == p1), e_row, big), axis=0, keepdims=True)
    rest = in_group & (e_row != i1)
    p2 = jnp.max(jnp.where(rest, ep, -1.0), axis=0, keepdims=True)
    i2 = jnp.min(jnp.where(rest & (ep == p2), e_row, big), axis=0, keepdims=True)
    tot = p1 + p2
    return i1, i2, p_g * p1 / tot, p_g * p2 / tot


ROW_SPLIT = 4
ROUTE_E1, ROUTE_E2, ROUTE_R1, ROUTE_R2, ROUTE_W1, ROUTE_W2 = range(6)
HIGH_HALF = 0xFFFF0000


def _split_rows(ref, value, rows):
    half = value.shape[1] // 2
    lo = lax.bitcast_convert_type(value[:, :half].astype(jnp.bfloat16).astype(F32), jnp.uint32)
    hi = lax.bitcast_convert_type(value[:, half:].astype(jnp.bfloat16).astype(F32), jnp.uint32)
    words = lax.bitcast_convert_type((lo >> 16) | (hi & jnp.uint32(HIGH_HALF)), jnp.int32)
    for j in range(ROW_SPLIT):
        ref[pl.ds(j, rows, stride=ROW_SPLIT), :] = words[:, 128 * j:128 * (j + 1)]


def _merge_rows(ref, rows):
    words = jnp.concatenate([ref[pl.ds(j, rows, stride=ROW_SPLIT), :] for j in range(ROW_SPLIT)], axis=1)
    words = lax.bitcast_convert_type(words, jnp.uint32)
    lo = lax.bitcast_convert_type(words << 16, F32)
    hi = lax.bitcast_convert_type(words & jnp.uint32(HIGH_HALF), F32)
    return jnp.concatenate([lo, hi], axis=1)


def _out_kernel(x_ref, s5_ref, ret_ref, m2_ref, hg_ref, wglu_ref, bglu_ref, wo_ref, g2_ref,
                wrh_ref, br_ref, stri_ref, x1_ref, h2_ref, route_ref, cnt_ref, s5_tmp, carry_ref,
                *, tile):
    @pl.when(pl.program_id(0) == 0)
    def _():
        carry_ref[...] = jnp.zeros_like(carry_ref)

    for s in range(S5_CHUNK):
        for j in range(GROUP_WIDTH // 128):
            lanes = slice(GROUP_WIDTH * s + 128 * j, GROUP_WIDTH * s + 128 * (j + 1))
            s5_tmp[j, pl.ds(s, tile // S5_CHUNK, stride=S5_CHUNK), :] = s5_ref[:, lanes]
    y = jnp.concatenate([s5_tmp[j] for j in range(GROUP_WIDTH // 128)], axis=1)
    y = y * (0.5 * (1.0 + jnp.tanh(math.sqrt(2.0 / math.pi) * (y + 0.044715 * (y * y * y)))))
    y = y * _sigmoid(_dot(y.astype(BF16), wglu_ref[...]) + bglu_ref[...])
    acc = x_ref[...] + _dot(y.astype(BF16), wo_ref[0:256, :])
    acc = acc + _dot(ret_ref[...], wo_ref[256:512, :])
    acc = acc + _dot(m2_ref[...], wo_ref[512:768, :])
    acc = acc + _dot(hg_ref[...], wo_ref[768:1024, :])
    x1_ref[...] = acc
    h2 = _rms(acc, g2_ref[...])
    _split_rows(h2_ref, h2, tile)
    hi, lo = _split2(h2)
    hw = _dot(hi, wrh_ref[...])
    logits = (hw[:, :ROUTE_LANES] + hw[:, ROUTE_LANES:] + _dot(lo, wrh_ref[:, :ROUTE_LANES])) + br_ref[...]
    e1, e2, w1, w2 = _route(logits.T)
    e_row = lax.broadcasted_iota(jnp.int32, (MOE_EXPERTS, tile), 0)
    picked = jnp.where((e_row == e1) | (e_row == e2), 1.0, 0.0)
    rank = carry_ref[:, 0:1] + _dot_nt(picked.astype(BF16), stri_ref[...])
    r1 = jnp.sum(jnp.where(e_row == e1, rank, 0.0), axis=0, keepdims=True)
    r2 = jnp.sum(jnp.where(e_row == e2, rank, 0.0), axis=0, keepdims=True)
    carry_ref[...] += jnp.sum(picked, axis=1, keepdims=True)
    cnt_ref[...] = carry_ref[...]
    rec_row = lax.broadcasted_iota(jnp.int32, (ROUTE_LANES, tile), 0)
    rec = jnp.zeros((ROUTE_LANES, tile), F32)
    for col, val in ((ROUTE_E1, e1.astype(F32)), (ROUTE_E2, e2.astype(F32)), (ROUTE_R1, r1), (ROUTE_R2, r2),
                     (ROUTE_W1, w1), (ROUTE_W2, w2)):
        rec = jnp.where(rec_row == col, val, rec)
    route_ref[...] = rec.T


def _out_proj(x, y_s5, y_ret, y_m2, y_hg, w_glu, b_glu, w_out, g2, wr_packed, b_route, layer):
    n, d = x.shape
    tile = min(SEQ_TILE, n)
    i = np.arange(tile)
    strict_lower = jnp.asarray(i[:, None] > i[None, :], dtype=BF16)
    return pl.pallas_call(
        functools.partial(_out_kernel, tile=tile),
        grid=(n // tile,),
        in_specs=[_rows(tile, d), _rows(tile // S5_CHUNK, S5_CHUNK * GROUP_WIDTH)] + [_rows(tile, GROUP_WIDTH)] * 3
        + [_pick((256, 256), layer), _pick((1, 256), layer), _pick((d, d), layer), _pick((1, d), layer),
           _pick((d, 2 * ROUTE_LANES), layer), _pick((1, ROUTE_LANES), layer), _full((tile, tile))],
        out_specs=[_rows(tile, d), _rows(ROW_SPLIT * tile, 128), _rows(tile, ROUTE_LANES),
                   _full((MOE_EXPERTS, ROUTE_LANES))],
        out_shape=[jax.ShapeDtypeStruct((n, d), F32), jax.ShapeDtypeStruct((ROW_SPLIT * n, 128), jnp.int32),
                   jax.ShapeDtypeStruct((n, ROUTE_LANES), F32), jax.ShapeDtypeStruct((MOE_EXPERTS, ROUTE_LANES), F32)],
        scratch_shapes=[pltpu.VMEM((GROUP_WIDTH // 128, tile, 128), F32),
                        pltpu.VMEM((MOE_EXPERTS, ROUTE_LANES), F32)],
        compiler_params=_cparams("arbitrary"),
        name="out_proj_router",
    )(x, y_s5, y_ret, y_m2, y_hg, w_glu, b_glu, w_out, g2, wr_packed, b_route, strict_lower)


def _sc_mesh():
    return plsc.VectorSubcoreMesh(core_axis_name="core", subcore_axis_name="subcore")


def _sc_scatter2(src, idx_a, idx_b, n_out):
    n = src.shape[0]

    @functools.partial(pl.kernel, out_type=jax.ShapeDtypeStruct((n_out, 128), src.dtype), mesh=_sc_mesh(),
                       scratch_types=[])
    def scatter_kernel(x_hbm, ia_hbm, ib_hbm, o_hbm):
        def body(x_vmem, ia_vmem, ib_vmem):
            pltpu.sync_copy(x_vmem, o_hbm.at[ia_vmem.at[0]])
            pltpu.sync_copy(x_vmem, o_hbm.at[ib_vmem.at[0]])

        pltpu.emit_pipeline(
            body, grid=(n // SC_WINDOW,),
            in_specs=[pl.BlockSpec((SC_WINDOW, 128), index_map=lambda i: (i, 0)),
                      pl.BlockSpec((1, SC_WINDOW), index_map=lambda i: (i, 0)),
                      pl.BlockSpec((1, SC_WINDOW), index_map=lambda i: (i, 0))],
            out_specs=[],
            core_axis_name=("core", "subcore"), dimension_semantics=(pltpu.PARALLEL,),
        )(x_hbm, ia_hbm, ib_hbm)

    return scatter_kernel(src, idx_a, idx_b)


def _sc_gather2(table, idx_a, idx_b):
    n = idx_a.size
    sds = jax.ShapeDtypeStruct((n, 128), table.dtype)

    @functools.partial(pl.kernel, out_type=(sds, sds), mesh=_sc_mesh(), scratch_types=[])
    def gather_kernel(t_hbm, ia_hbm, ib_hbm, oa_hbm, ob_hbm):
        def body(ia_vmem, ib_vmem, oa_vmem, ob_vmem):
            pltpu.sync_copy(t_hbm.at[ia_vmem.at[0]], oa_vmem)
            pltpu.sync_copy(t_hbm.at[ib_vmem.at[0]], ob_vmem)

        pltpu.emit_pipeline(
            body, grid=(n // SC_WINDOW,),
            in_specs=[pl.BlockSpec((1, SC_WINDOW), index_map=lambda i: (i, 0)),
                      pl.BlockSpec((1, SC_WINDOW), index_map=lambda i: (i, 0))],
            out_specs=[pl.BlockSpec((SC_WINDOW, 128), index_map=lambda i: (i, 0)),
                       pl.BlockSpec((SC_WINDOW, 128), index_map=lambda i: (i, 0))],
            core_axis_name=("core", "subcore"), dimension_semantics=(pltpu.PARALLEL,),
        )(ia_hbm, ib_hbm, oa_hbm, ob_hbm)

    return gather_kernel(table, idx_a, idx_b)


def _index_kernel(route_ref, off_ref, fold_ref, spread_ref, i1_ref, i2_ref, *, tile):
    rec = route_ref[...]
    lane = lax.broadcasted_iota(jnp.int32, rec.shape, 1)
    off = off_ref[...]
    sub = lax.broadcasted_iota(jnp.int32, (tile * ROW_SPLIT // 128, 128), 1) % ROW_SPLIT
    for out_ref, e_col, r_col in ((i1_ref, ROUTE_E1, ROUTE_R1), (i2_ref, ROUTE_E2, ROUTE_R2)):
        e = rec[:, e_col:e_col + 1].astype(jnp.int32)
        pos = jnp.sum(jnp.where(lane == e, off, 0.0), axis=-1, keepdims=True) + rec[:, r_col:r_col + 1]
        hi = jnp.floor(pos * (1.0 / 256.0))
        lo = pos - 256.0 * hi
        moved = (256.0 * _dot(fold_ref[...], (hi * spread_ref[...]).astype(BF16))
                 + _dot(fold_ref[...], (lo * spread_ref[...]).astype(BF16)))
        out_ref[...] = moved.astype(jnp.int32) * ROW_SPLIT + sub


def _dispatch_plan(route, counts, n_tiles):
    n = route.shape[0]
    cnt = counts[:, 0].astype(jnp.int32)
    padded = ((cnt + EXPERT_TILE - 1) // EXPERT_TILE) * EXPERT_TILE
    ends = jnp.cumsum(padded)
    off = ends - padded
    start = jnp.arange(n_tiles, dtype=jnp.int32) * EXPERT_TILE
    tile_expert = jnp.minimum(jnp.sum((start[:, None] >= ends[None, :]).astype(jnp.int32), axis=1), MOE_EXPERTS - 1)
    onehot = (tile_expert[:, None] == jnp.arange(MOE_EXPERTS, dtype=jnp.int32)[None, :]).astype(jnp.int32)
    seg_end = jnp.sum(onehot * (off + cnt)[None, :], axis=1)
    tile_rows = jnp.clip(seg_end - start, 0, EXPERT_TILE).astype(jnp.int32)
    prev = jnp.concatenate([jnp.full((1,), -1, jnp.int32), tile_expert[:-1]])
    tile_first = ((tile_expert != prev) & (tile_rows > 0)).astype(jnp.int32)

    tile = min(SEQ_TILE, n)
    off_row = jnp.zeros((1, ROUTE_LANES), F32).at[0, :MOE_EXPERTS].set(off.astype(F32))
    per_row = 128 // ROW_SPLIT
    t = np.arange(tile)
    out_rows = tile // per_row
    fold = jnp.asarray(np.arange(out_rows)[:, None] == (t // per_row)[None, :], dtype=BF16)
    spread = jnp.asarray((t % per_row)[:, None] == (np.arange(128) // ROW_SPLIT)[None, :], dtype=F32)
    idx1, idx2 = pl.pallas_call(
        functools.partial(_index_kernel, tile=tile),
        grid=(n // tile,),
        in_specs=[_rows(tile, ROUTE_LANES), _full((1, ROUTE_LANES)), _full(fold.shape), _full(spread.shape)],
        out_specs=[_rows(out_rows, 128)] * 2,
        out_shape=[jax.ShapeDtypeStruct((n // per_row, 128), jnp.int32)] * 2,
        compiler_params=_cparams("parallel"),
        name="dispatch_index",
    )(route, off_row, fold, spread)
    return idx1, idx2, tile_expert, tile_rows, tile_first


def _experts_kernel(te_ref, rows_ref, first_ref, xs_ref, wg_ref, wu_ref, wd_ref, y_ref, wgb, wub, wdb):
    i = pl.program_id(0)

    @pl.when(first_ref[i] == 1)
    def _():
        wgb[...] = wg_ref[0, 0].astype(BF16)
        wub[...] = wu_ref[0, 0].astype(BF16)
        wdb[...] = wd_ref[0, 0].astype(BF16)

    @pl.when(rows_ref[i] > 0)
    def _():
        x = _merge_rows(xs_ref, EXPERT_TILE)
        row = lax.broadcasted_iota(jnp.int32, x.shape, 0)
        x = jnp.where(row < rows_ref[i], x, 0.0).astype(BF16)
        act = _silu(_dot(x, wgb[...])) * _dot(x, wub[...])
        _split_rows(y_ref, _dot(act.astype(BF16), wdb[...]), EXPERT_TILE)


def _experts(xs, tile_expert, tile_rows, tile_first, w_gate, w_up, w_down, layer):
    n_tiles = tile_expert.shape[0]
    _, _, d, ff = w_gate.shape
    rows_blk = pl.BlockSpec((ROW_SPLIT * EXPERT_TILE, 128), lambda i, te, rows, first: (i, 0))
    return pl.pallas_call(
        _experts_kernel,
        grid_spec=pltpu.PrefetchScalarGridSpec(
            num_scalar_prefetch=3,
            grid=(n_tiles,),
            in_specs=[rows_blk,
                      pl.BlockSpec((1, 1, d, ff), lambda i, te, rows, first: (layer, te[i], 0, 0)),
                      pl.BlockSpec((1, 1, d, ff), lambda i, te, rows, first: (layer, te[i], 0, 0)),
                      pl.BlockSpec((1, 1, ff, d), lambda i, te, rows, first: (layer, te[i], 0, 0))],
            out_specs=rows_blk,
            scratch_shapes=[pltpu.VMEM((d, ff), BF16), pltpu.VMEM((d, ff), BF16), pltpu.VMEM((ff, d), BF16)],
        ),
        out_shape=jax.ShapeDtypeStruct(xs.shape, xs.dtype),
        compiler_params=_cparams("arbitrary"),
        name="moe_experts",
    )(tile_expert, tile_rows, tile_first, xs, w_gate, w_up, w_down)


def _combine_kernel(x1_ref, g1_ref, g2_ref, route_ref, gf_ref, o_ref, *, tile, final_norm):
    route = route_ref[...]
    out = (x1_ref[...] + route[:, ROUTE_W1:ROUTE_W1 + 1] * _merge_rows(g1_ref, tile)
           + route[:, ROUTE_W2:ROUTE_W2 + 1] * _merge_rows(g2_ref, tile))
    o_ref[...] = _rms(out, gf_ref[...]) if final_norm else out


def _combine(x1, g1, g2, route, g_final, final_norm):
    n, d = x1.shape
    tile = min(SEQ_TILE, n)
    return pl.pallas_call(
        functools.partial(_combine_kernel, tile=tile, final_norm=final_norm),
        grid=(n // tile,),
        in_specs=[_rows(tile, d), _rows(ROW_SPLIT * tile, 128), _rows(ROW_SPLIT * tile, 128),
                  _rows(tile, ROUTE_LANES), _full((1, d))],
        out_specs=_rows(tile, d),
        out_shape=jax.ShapeDtypeStruct((n, d), F32),
        compiler_params=_cparams("parallel"),
        name="moe_combine",
    )(x1, g1, g2, route, g_final)


def _moe(h2_rows, route, counts, x1, w_gate, w_up, w_down, layer, g_final, final_norm):
    n = x1.shape[0]
    n_tiles = (MOE_TOPK * n) // EXPERT_TILE + MOE_EXPERTS
    idx1, idx2, tile_expert, tile_rows, tile_first = _dispatch_plan(route, counts, n_tiles)
    xs = _sc_scatter2(h2_rows, idx1, idx2, ROW_SPLIT * n_tiles * EXPERT_TILE)
    ys = _experts(xs, tile_expert, tile_rows, tile_first, w_gate, w_up, w_down, layer)
    g1, g2 = _sc_gather2(ys, idx1, idx2)
    return _combine(x1, g1, g2, route, g_final, final_norm)


def kernel(x, positions, norm1_g, w_in, w_out, s5_a_re, s5_a_im, s5_log_dt, s5_b_re, s5_b_im, s5_c_re, s5_c_im, s5_d, s5_w_glu, s5_b_glu, m2_conv_w, m2_conv_b, m2_dt_bias, m2_a_log, m2_d, m2_norm_g, hg_lb_logits, hg_norm_g, norm2_g, moe_w_group, moe_b_group, moe_w_expert, moe_b_expert, moe_w_gate, moe_w_up, moe_w_down, final_norm_g):
    bsz, seqlen, d = x.shape
    assert bsz == 1 and seqlen % SEQ_TILE == 0 and (MOE_TOPK * seqlen) % EXPERT_TILE == 0
    depth = w_in.shape[0]

    lb_probs = jax.nn.softmax(hg_lb_logits.astype(F32), axis=0)
    lower_bounds = (jnp.cumsum(lb_probs, axis=0) - lb_probs[0]).reshape(depth, 1, GROUP_WIDTH)
    cos_t, sin_t = _rope_tables(positions.reshape(seqlen, 1))

    s5_p = _s5_params(s5_a_re, s5_a_im, s5_log_dt, s5_b_re, s5_b_im, s5_c_re, s5_c_im, s5_d)
    m2_p = _mamba2_params(m2_conv_w, m2_conv_b, m2_dt_bias, m2_a_log, m2_d, m2_norm_g)
    w_route = jnp.zeros((depth, d, ROUTE_LANES), F32)
    w_route = w_route.at[:, :, :MOE_GROUPS].set(moe_w_group)
    w_route = w_route.at[:, :, EXPERT_ROW:EXPERT_ROW + MOE_EXPERTS].set(moe_w_expert)
    wr_packed = jnp.concatenate(_split2(w_route), axis=2)
    b_route = jnp.zeros((depth, 1, ROUTE_LANES), F32)
    b_route = b_route.at[:, 0, :MOE_GROUPS].set(moe_b_group)
    b_route = b_route.at[:, 0, EXPERT_ROW:EXPERT_ROW + MOE_EXPERTS].set(moe_b_expert)
    w_glu_b, w_out_b = s5_w_glu.astype(BF16), w_out.astype(BF16)
    b_glu = s5_b_glu.reshape(depth, 1, GROUP_WIDTH)
    g1, g2 = norm1_g.reshape(depth, 1, d), norm2_g.reshape(depth, 1, d)
    hg_g = hg_norm_g.reshape(depth, 1, GROUP_WIDTH)
    g_final = final_norm_g.reshape(1, d)
    tile = min(SEQ_TILE, seqlen)
    chunk = min(CHUNK, tile)

    xc = x.reshape(seqlen, d)
    for l in range(depth):
        u_b, u_f, p_ret, p_m2, p_hg = _in_proj(xc, g1, _w_prep(w_in, l), l)
        y_s5 = _s5(u_b, u_f, s5_p, l)
        y_ret, y_m2, y_hg = _mixers(
            seqlen,
            _retention_call(p_ret, cos_t, sin_t, tile, min(RET_CHUNK, tile)),
            _mamba2_call(p_m2, m2_p, l, tile, chunk),
            _hgrn2_call(p_hg, lower_bounds, hg_g, l, tile, chunk))
        x1, h2_rows, route, counts = _out_proj(xc, y_s5, y_ret, y_m2, y_hg, w_glu_b, b_glu, w_out_b, g2,
                                               wr_packed, b_route, l)
        xc = _moe(h2_rows, route, counts, x1, moe_w_gate, moe_w_up, moe_w_down, l, g_final,
                  final_norm=(l == depth - 1))
    return xc.reshape(bsz, seqlen, d)
```

```python
import functools
import math

import numpy as np
import jax
import jax.numpy as jnp
from jax import lax
from jax.experimental import pallas as pl
from jax.experimental.pallas import tpu as pltpu
from jax.experimental.pallas import tpu_sc as plsc

F32 = jnp.float32
BF16 = jnp.bfloat16
NORM_EPS = 1e-6

GROUP_WIDTH = 256
HEAD_DIM = 64
N_HEADS = GROUP_WIDTH // HEAD_DIM
S5_GROUPS = 16
S5_CH = 16
S5_STATE = 64
S5_DT_CLAMP = -1e-4
M2_STATE = 128
M2_CONV = 4
M2_CONV_DIM = 768
ROPE_BASE = 10000.0
MOE_GROUPS = 4
MOE_PER_GROUP = 4
MOE_EXPERTS = 16
ROUTE_LANES = 128

SEQ_TILE = 512
CHUNK = 128
RET_CHUNK = 256
CHUNK_UNROLL = 4
S5_CHUNK = 16
MOE_TOPK = 2
EXPERT_TILE = 512
SC_WINDOW = 128
VMEM_LIMIT = 56 * 1024 * 1024


def _cparams(*sem):
    return pltpu.CompilerParams(dimension_semantics=sem, vmem_limit_bytes=VMEM_LIMIT)


def _dot(a, b):
    return jnp.dot(a, b, preferred_element_type=F32)


def _dot_nt(a, b):
    return lax.dot_general(a, b, (((1,), (1,)), ((), ())), preferred_element_type=F32)


def _dot_tn(a, b):
    return lax.dot_general(a, b, (((0,), (0,)), ((), ())), preferred_element_type=F32)


def _split2(x):
    hi = x.astype(BF16)
    return hi, (x - hi.astype(F32)).astype(BF16)


def _split3(x):
    hi = x.astype(BF16)
    r = x - hi.astype(F32)
    mid = r.astype(BF16)
    return hi, mid, (r - mid.astype(F32)).astype(BF16)


def _dot_exact_lhs(m, x):
    hi, mid, lo = _split3(x)
    return _dot(m, hi) + _dot(m, mid) + _dot(m, lo)


def _dot_exact_rhs(x, m):
    hi, lo = _split2(x)
    return _dot(hi, m) + _dot(lo, m)


def _sigmoid(x):
    return 1.0 / (1.0 + jnp.exp(-x))


def _silu(x):
    return x * _sigmoid(x)


def _rms(x, g):
    return x * lax.rsqrt(jnp.mean(x * x, axis=-1, keepdims=True) + NORM_EPS) * g


def _full(shape):
    return pl.BlockSpec(shape, lambda *_: (0,) * len(shape))


def _rows(tile, width):
    return pl.BlockSpec((tile, width), lambda i: (i, 0))


def _pick(shape, layer):
    return pl.BlockSpec((None,) + tuple(shape), lambda *_: (layer,) + (0,) * len(shape))


IN_SEGMENTS = (256, 1024, 1280, 1024)


DT_COL = 9 * GROUP_WIDTH
W_PREP_ROWS = 128


def _w_prep_kernel(w_ref, o_ref):
    o_ref[:, 0:DT_COL] = w_ref[0, :, 0:DT_COL].astype(BF16)
    head = lax.broadcasted_iota(jnp.int32, (W_PREP_ROWS, GROUP_WIDTH), 1) // HEAD_DIM
    rep = jnp.zeros((W_PREP_ROWS, GROUP_WIDTH), F32)
    for h in range(N_HEADS):
        rep = jnp.where(head == h, w_ref[0, :, DT_COL + h:DT_COL + h + 1], rep)
    o_ref[:, DT_COL:DT_COL + GROUP_WIDTH] = rep.astype(BF16)
    o_ref[:, DT_COL + GROUP_WIDTH:] = w_ref[0, :, DT_COL + N_HEADS:].astype(BF16)


def _w_prep(w_in, layer):
    _, d, n_in = w_in.shape
    return pl.pallas_call(
        _w_prep_kernel,
        grid=(d // W_PREP_ROWS,),
        in_specs=[pl.BlockSpec((1, W_PREP_ROWS, n_in), lambda i: (layer, i, 0))],
        out_specs=pl.BlockSpec((W_PREP_ROWS, sum(IN_SEGMENTS)), lambda i: (i, 0)),
        out_shape=jax.ShapeDtypeStruct((d, sum(IN_SEGMENTS)), BF16),
        compiler_params=_cparams("parallel"),
        name="w_in_prep",
    )(w_in)


def _in_proj_kernel(x_ref, g_ref, w_ref, ub_ref, uf_ref, ret_ref, m2_ref, hg_ref, u_tmp, *, tile):
    hb = _rms(x_ref[...], g_ref[...]).astype(BF16)
    c0, c1, c2, c3 = np.cumsum(IN_SEGMENTS)
    ret_ref[...] = _dot(hb, w_ref[:, c0:c1])
    m2_ref[...] = _dot(hb, w_ref[:, c1:c2])
    hg_ref[...] = _dot(hb, w_ref[:, c2:c3])
    u = _dot(hb, w_ref[:, 0:c0])
    for j in range(GROUP_WIDTH // 128):
        u_tmp[j] = u[:, 128 * j:128 * (j + 1)]
    for s in range(S5_CHUNK):
        for j in range(GROUP_WIDTH // 128):
            v = u_tmp[j, pl.ds(s, tile // S5_CHUNK, stride=S5_CHUNK), :]
            lanes = slice(GROUP_WIDTH * s + 128 * j, GROUP_WIDTH * s + 128 * (j + 1))
            uf_ref[:, lanes] = v
            ub_ref[:, lanes] = v.astype(BF16)


def _in_proj(x, g, w, layer):
    n, d = x.shape
    tile = min(SEQ_TILE, n)
    fold = S5_CHUNK * GROUP_WIDTH
    return pl.pallas_call(
        functools.partial(_in_proj_kernel, tile=tile),
        grid=(n // tile,),
        in_specs=[_rows(tile, d), _pick((1, d), layer), _full(w.shape)],
        out_specs=[_rows(tile // S5_CHUNK, fold)] * 2 + [_rows(tile, s) for s in IN_SEGMENTS[1:]],
        out_shape=[jax.ShapeDtypeStruct((n // S5_CHUNK, fold), BF16),
                   jax.ShapeDtypeStruct((n // S5_CHUNK, fold), F32)]
        + [jax.ShapeDtypeStruct((n, s), F32) for s in IN_SEGMENTS[1:]],
        scratch_shapes=[pltpu.VMEM((GROUP_WIDTH // 128, tile, 128), F32)],
        compiler_params=_cparams("parallel"),
        name="in_proj",
    )(x, g, w)


def _rope_kernel(pos_ref, invf_ref, cos_ref, sin_ref):
    ang = pos_ref[...].astype(F32) * invf_ref[...]
    cos_ref[...] = jnp.cos(ang)
    sin_ref[...] = jnp.sin(ang)


def _rope_tables(positions):
    n = positions.shape[0]
    tile = min(SEQ_TILE, n)
    half = HEAD_DIM // 2
    inv_freq = ROPE_BASE ** (-jnp.arange(half, dtype=F32) / half)
    invf = jnp.tile(inv_freq, 128 // half).reshape(1, 128)
    return pl.pallas_call(
        _rope_kernel,
        grid=(n // tile,),
        in_specs=[_rows(tile, 1), _full((1, 128))],
        out_specs=[_rows(tile, 128), _rows(tile, 128)],
        out_shape=[jax.ShapeDtypeStruct((n, 128), F32)] * 2,
        compiler_params=_cparams("parallel"),
        name="rope_tables",
    )(positions, invf)


def _head_mean_matrix():
    h = np.arange(GROUP_WIDTH) // HEAD_DIM
    return jnp.asarray((h[:, None] == h[None, :]) / HEAD_DIM, dtype=BF16)


def _head_block_mask():
    h = np.arange(GROUP_WIDTH) // HEAD_DIM
    return jnp.asarray(h[:, None] == h[None, :], dtype=F32)


HEAD_PAIRS = GROUP_WIDTH // 128


def _lanes(x, j):
    return x[:, 128 * j:128 * (j + 1)]


def _stack_pair(x):
    xb = x.astype(BF16)
    low = lax.broadcasted_iota(jnp.int32, x.shape, 1) < HEAD_DIM
    zero = jnp.zeros_like(xb)
    return jnp.concatenate([jnp.where(low, xb, zero), jnp.where(low, zero, xb)], axis=0)


def _pair_scores(q, kb):
    s = _dot_nt(_stack_pair(q), kb)
    return s[:q.shape[0]], s[q.shape[0]:]


def _pair_apply(a0, a1, v):
    return _dot(jnp.concatenate([a0.astype(BF16), a1.astype(BF16)], axis=1), _stack_pair(v))


def _ret_constants(chunk):
    lg = np.log1p(-np.exp2(-5.0 - np.arange(N_HEADS, dtype=np.float64)))
    idx = np.arange(chunk, dtype=np.float64)
    rel = idx[:, None] - idx[None, :]
    decay = np.where(rel >= 0, np.exp(np.maximum(rel, 0.0)[None] * lg[:, None, None]), 0.0)
    lane_lg = np.repeat(lg, HEAD_DIM)
    xi = np.exp((idx + 1.0)[:, None] * lane_lg[None, :])
    zeta = np.exp((chunk - 1.0 - idx)[:, None] * lane_lg[None, :])
    h = np.arange(GROUP_WIDTH) // HEAD_DIM
    gc = np.where(h[:, None] == h[None, :], np.exp(chunk * lane_lg)[:, None], 0.0)
    f = lambda a: jnp.asarray(a, dtype=F32)
    return f(decay), f(xi), f(zeta), f(gc)


def _ret_kernel(p_ref, cos_ref, sin_ref, dec_ref, xi_ref, zeta_ref, gc_ref, bm_ref, gm_ref,
                o_ref, r_ref, *, chunk, n_chunks):
    lane = lax.broadcasted_iota(jnp.int32, (chunk, GROUP_WIDTH), 1)
    first_half = (lane % HEAD_DIM) < (HEAD_DIM // 2)
    gm = gm_ref[...]

    def rope(t, cos2, sin2):
        rot = jnp.where(first_half, -pltpu.roll(t, GROUP_WIDTH - HEAD_DIM // 2, 1),
                        pltpu.roll(t, HEAD_DIM // 2, 1))
        return t * cos2 + rot * sin2

    def body(c, carry):
        rows = pl.ds(pl.multiple_of(c * chunk, chunk), chunk)
        cs = cos_ref[rows, :]
        sn = sin_ref[rows, :]
        cos2 = jnp.concatenate([cs, cs], axis=1)
        sin2 = jnp.concatenate([sn, sn], axis=1)
        q = rope(p_ref[rows, 0:256], cos2, sin2)
        k = rope(p_ref[rows, 256:512], cos2, sin2) * (HEAD_DIM ** -0.5)
        v = p_ref[rows, 512:768]
        g = p_ref[rows, 768:1024]
        kb = k.astype(BF16)
        inner = []
        for j in range(HEAD_PAIRS):
            s0, s1 = _pair_scores(_lanes(q, j), _lanes(kb, j))
            inner.append(_pair_apply(s0 * dec_ref[2 * j], s1 * dec_ref[2 * j + 1], _lanes(v, j)))
        inner = jnp.concatenate(inner, axis=1)
        r_prev = r_ref[...]
        cross = _dot((q * xi_ref[...]).astype(BF16), r_prev.astype(BF16))
        r_ref[...] = gc_ref[...] * r_prev + bm_ref[...] * _dot_tn(kb, (zeta_ref[...] * v).astype(BF16))
        o = inner + cross
        cen = o - _dot_exact_rhs(o, gm)
        var = _dot_exact_rhs(cen * cen, gm)
        o_ref[rows, :] = (cen * lax.rsqrt(var + NORM_EPS) * _silu(g)).astype(BF16)
        return carry

    lax.fori_loop(0, n_chunks, body, 0, unroll=CHUNK_UNROLL)


def _retention_call(proj, cos_t, sin_t, tile, chunk):
    decay, xi, zeta, gc = _ret_constants(chunk)
    body = functools.partial(_ret_kernel, chunk=chunk, n_chunks=tile // chunk)
    operands = (proj, cos_t, sin_t, decay, xi, zeta, gc, _head_block_mask(), _head_mean_matrix())
    specs = [_rows(tile, 1024), _rows(tile, 128), _rows(tile, 128), _full(decay.shape), _full(xi.shape),
             _full(zeta.shape), _full(gc.shape), _full((256, 256)), _full((256, 256))]
    return body, operands, specs, [pltpu.VMEM((GROUP_WIDTH, GROUP_WIDTH), F32)]


def _tri_matrix(chunk):
    i = np.arange(chunk)
    return jnp.asarray(i[:, None] >= i[None, :], dtype=BF16)


def _m2_kernel(p_ref, cw_ref, cb_ref, dtb_ref, alog_ref, d_ref, ng_ref, tri_ref,
               o_ref, tail_ref, ext_ref, act_ref, st_ref, *, tile, chunk, n_chunks):
    ext_ref[0:8, :] = tail_ref[...]
    ext_ref[8:tile + 8, :] = p_ref[:, 256:1024]
    tail_ref[...] = p_ref[tile - 8:tile, 256:1024]
    conv = cb_ref[...]
    for j in range(M2_CONV):
        lo = 8 - (M2_CONV - 1) + j
        conv = conv + cw_ref[j:j + 1, :] * ext_ref[lo:lo + tile, :]
    act_ref[...] = _silu(conv)
    a_lane = -jnp.exp(alog_ref[...])
    tri = tri_ref[...]
    ti = lax.broadcasted_iota(jnp.int32, (chunk, chunk), 0)
    si = lax.broadcasted_iota(jnp.int32, (chunk, chunk), 1)
    causal = ti >= si
    lane = lax.broadcasted_iota(jnp.int32, (chunk, 128), 1)

    def body(c, carry):
        start = pl.multiple_of(c * chunk, chunk)
        rows = pl.ds(start, chunk)
        xbc = act_ref[rows, :]
        xs = xbc[:, 0:256]
        z = p_ref[rows, 0:256]
        x_dt = p_ref[rows, 1024:1280] + dtb_ref[...]
        dt = jnp.maximum(x_dt, 0.0) + jnp.log1p(jnp.exp(-jnp.abs(x_dt)))
        acum = _dot_exact_lhs(tri, dt * a_lane)
        acum_t = acum.T
        a_last = acum[chunk - 1:chunk, :]
        e_acum = jnp.exp(acum)
        decs = jnp.exp(a_last - acum)
        d_chunk = jnp.exp(a_last)
        xc = xs * dt
        ys = []
        for g in range(2):
            sl = slice(128 * g, 128 * (g + 1))
            bmg = xbc[:, 256 + 128 * g:256 + 128 * (g + 1)].astype(BF16)
            cmg = xbc[:, 512 + 128 * g:512 + 128 * (g + 1)].astype(BF16)
            cb = _dot_nt(cmg, bmg)
            xcg = xc[:, sl]
            yd = jnp.zeros((chunk, 128), F32)
            for hh in range(2):
                col0 = 128 * g + HEAD_DIM * hh
                diff = acum[:, col0:col0 + 1] - acum_t[col0:col0 + 1, :]
                lm = jnp.where(causal, jnp.exp(jnp.where(causal, diff, 0.0)), 0.0)
                xm = jnp.where((lane // HEAD_DIM) == hh, xcg, 0.0).astype(BF16)
                yd = yd + _dot((cb * lm).astype(BF16), xm)
            st = st_ref[:, sl]
            y_off = _dot(cmg, st.astype(BF16)) * e_acum[:, sl]
            st_ref[:, sl] = d_chunk[:, sl] * st + _dot_tn(bmg, (xcg * decs[:, sl]).astype(BF16))
            ys.append(yd + y_off + d_ref[:, sl] * xs[:, sl])
        y = jnp.concatenate(ys, axis=1) * _silu(z)
        o_ref[rows, :] = _rms(y, ng_ref[...]).astype(BF16)
        return carry

    lax.fori_loop(0, n_chunks, body, 0, unroll=CHUNK_UNROLL)


def _mamba2_params(conv_w, conv_b, dt_bias, a_log, d_skip, norm_g):
    depth = conv_w.shape[0]
    lanes = lambda v: jnp.repeat(v, HEAD_DIM, axis=1).reshape(depth, 1, GROUP_WIDTH)
    return (conv_w, conv_b.reshape(depth, 1, -1), lanes(dt_bias), lanes(a_log), lanes(d_skip),
            norm_g.reshape(depth, 1, -1))


def _mamba2_call(proj, params, layer, tile, chunk):
    body = functools.partial(_m2_kernel, tile=tile, chunk=chunk, n_chunks=tile // chunk)
    operands = (proj,) + tuple(params) + (_tri_matrix(chunk),)
    specs = ([_rows(tile, 1280), _pick((M2_CONV, M2_CONV_DIM), layer), _pick((1, M2_CONV_DIM), layer)]
             + [_pick((1, GROUP_WIDTH), layer)] * 4 + [_full((chunk, chunk))])
    scratch = [pltpu.VMEM((8, M2_CONV_DIM), F32), pltpu.VMEM((tile + 8, M2_CONV_DIM), F32),
               pltpu.VMEM((tile, M2_CONV_DIM), F32), pltpu.VMEM((M2_STATE, GROUP_WIDTH), F32)]
    return body, operands, specs, scratch


HG_MATMUL_LEVELS = 3


def _hg_exponent_matrix(chunk):
    levels = HG_MATMUL_LEVELS
    t = np.arange(chunk)[:, None]
    r = np.arange(chunk)[None, :]
    blocks = []
    for lvl in range(levels):
        b = 1 << lvl
        blk = t // b
        odd = (blk % 2) == 1
        q_rows = odd & (r >= blk * b) & (r <= t)
        k_rows = (~odd) & (r > t) & (r <= (blk + 1) * b - 1)
        blocks.append(q_rows | k_rows)
    blocks.append(r <= t)
    return jnp.asarray(np.concatenate(blocks, axis=0), dtype=BF16)


def _hg_kernel(p_ref, lb_ref, ng_ref, gexp_ref, bm_ref, gm_ref, o_ref, st_ref, *, chunk, n_chunks):
    levels = int(math.log2(chunk))
    row = lax.broadcasted_iota(jnp.int32, (chunk, 128), 0)
    odd_rows = [((row >> lvl) & 1) == 1 for lvl in range(levels)]
    row_wide = lax.broadcasted_iota(jnp.int32, (chunk, GROUP_WIDTH), 0)
    odd_rows_wide = [((row_wide >> lvl) & 1) == 1 for lvl in range(levels)]
    ti = lax.broadcasted_iota(jnp.int32, (chunk, chunk), 0)
    si = lax.broadcasted_iota(jnp.int32, (chunk, chunk), 1)
    pair_level = [((ti >> (lvl + 1)) == (si >> (lvl + 1))) & (((ti >> lvl) & 1) == 1) & (((si >> lvl) & 1) == 0)
                  for lvl in range(levels)]
    lb = lb_ref[...]
    gm = gm_ref[...]

    def body(c, carry):
        rows = pl.ds(pl.multiple_of(c * chunk, chunk), chunk)
        q = _silu(p_ref[rows, 0:256])
        forget = lb + (1.0 - lb) * _sigmoid(p_ref[rows, 256:512])
        k = 1.0 - forget
        v = p_ref[rows, 512:768]
        g = p_ref[rows, 768:1024]
        lf_hi, lf_lo = _split2(jnp.log(forget))
        expo = _dot(gexp_ref[...], lf_hi) + _dot(gexp_ref[...], lf_lo)
        bcum = expo[HG_MATMUL_LEVELS * chunk:(HG_MATMUL_LEVELS + 1) * chunk, :]

        def level_log_decay(lvl):
            if lvl < HG_MATMUL_LEVELS:
                return expo[lvl * chunk:(lvl + 1) * chunk, :]
            b = 1 << lvl
            ref = jnp.concatenate([jnp.broadcast_to(bcum[m + b - 1:m + b, :], (2 * b, GROUP_WIDTH))
                                   for m in range(0, chunk, 2 * b)], axis=0)
            return jnp.where(odd_rows_wide[lvl], bcum - ref, ref - bcum)


        log_decay = [level_log_decay(lvl) for lvl in range(levels)]
        intra = []
        for j in range(HEAD_PAIRS):
            qj, kj = _lanes(q, j), _lanes(k, j)
            a0, a1 = (jnp.where(ti == si, s, 0.0) for s in _pair_scores(qj, kj.astype(BF16)))
            for lvl in range(levels):
                w = jnp.exp(_lanes(log_decay[lvl], j)) * jnp.where(odd_rows[lvl], qj, kj)
                s0, s1 = _pair_scores(w, w.astype(BF16))
                a0 = jnp.where(pair_level[lvl], s0, a0)
                a1 = jnp.where(pair_level[lvl], s1, a1)
            intra.append(_pair_apply(a0, a1, _lanes(v, j)))
        intra = jnp.concatenate(intra, axis=1)

        b_last = bcum[chunk - 1:chunk, :]
        suffix = b_last - bcum
        st = st_ref[...]
        cross = _dot_nt((q * jnp.exp(bcum)).astype(BF16), st.astype(BF16))
        st_ref[...] = jnp.exp(b_last) * st + bm_ref[...] * _dot_tn(
            v.astype(BF16), (k * jnp.exp(suffix)).astype(BF16))
        o = intra + cross
        o = o * lax.rsqrt(_dot_exact_rhs(o * o, gm) + NORM_EPS) * ng_ref[...]
        o_ref[rows, :] = (o * _silu(g)).astype(BF16)
        return carry

    lax.fori_loop(0, n_chunks, body, 0, unroll=CHUNK_UNROLL)


def _hgrn2_call(proj, lower_bounds, norm_g, layer, tile, chunk):
    gexp = _hg_exponent_matrix(chunk)
    body = functools.partial(_hg_kernel, chunk=chunk, n_chunks=tile // chunk)
    operands = (proj, lower_bounds, norm_g, gexp, _head_block_mask(), _head_mean_matrix())
    specs = [_rows(tile, 1024), _pick((1, GROUP_WIDTH), layer), _pick((1, GROUP_WIDTH), layer), _full(gexp.shape),
             _full((256, 256)), _full((256, 256))]
    return body, operands, specs, [pltpu.VMEM((GROUP_WIDTH, GROUP_WIDTH), F32)]


def _mixers_kernel(*refs, bodies, n_in, n_scratch):
    n_mix = len(bodies)
    ins, pos = [], 0
    for k in n_in:
        ins.append(refs[pos:pos + k])
        pos += k
    outs = refs[pos:pos + n_mix]
    pos += n_mix
    scratch = []
    for k in n_scratch:
        scratch.append(refs[pos:pos + k])
        pos += k

    @pl.when(pl.program_id(0) == 0)
    def _():
        for group in scratch:
            for ref in group:
                ref[...] = jnp.zeros_like(ref)

    for body, i, o, s in zip(bodies, ins, outs, scratch):
        body(*i, o, *s)


def _mixers(n, *calls):
    tile = min(SEQ_TILE, n)
    bodies = tuple(c[0] for c in calls)
    return pl.pallas_call(
        functools.partial(_mixers_kernel, bodies=bodies, n_in=tuple(len(c[1]) for c in calls),
                          n_scratch=tuple(len(c[3]) for c in calls)),
        grid=(n // tile,),
        in_specs=[s for c in calls for s in c[2]],
        out_specs=[_rows(tile, GROUP_WIDTH)] * len(calls),
        out_shape=[jax.ShapeDtypeStruct((n, GROUP_WIDTH), BF16)] * len(calls),
        scratch_shapes=[s for c in calls for s in c[3]],
        compiler_params=_cparams("arbitrary"),
        name="mixers",
    )(*[a for c in calls for a in c[1]])


S5_LANES = S5_GROUPS * 2 * S5_STATE
S5_TAP_SPLITS = 4


def _s5_rows(a_re, a_im, log_dt):
    are = jnp.minimum(a_re, S5_DT_CLAMP)
    dt = jnp.exp(log_dt)
    lam_re = are * dt
    lam_im = a_im * dt
    mag = jnp.exp(lam_re)
    ab_re = mag * jnp.cos(lam_im)
    ab_im = mag * jnp.sin(lam_im)
    den = are * are + a_im * a_im
    k_re = ((ab_re - 1.0) * are + ab_im * a_im) / den
    k_im = (ab_im * are - (ab_re - 1.0) * a_im) / den
    return lam_re, lam_im, k_re, k_im


def _s5_power(lam_re, lam_im, e):
    m = jnp.exp(e * lam_re)
    return m * jnp.cos(e * lam_im), m * jnp.sin(e * lam_im)


def _s5_state_kernel(u_ref, are_ref, aim_ref, ldt_ref, b1_ref, b2_ref, ca_ref, k_ref, sp_ref,
                     bb1_ref, bb2_ref, inc_ref, *, n_chunks):
    s = pl.program_id(0)
    lam_re, lam_im, k_re, k_im = _s5_rows(are_ref[...], aim_ref[...], ldt_ref[...])

    @pl.when(s == 0)
    def _():
        bb1_ref[...] = k_re * b1_ref[...] + k_im * b2_ref[...]
        bb2_ref[...] = k_re * b2_ref[...] - k_im * b1_ref[...]

    p_re, p_im = _s5_power(lam_re, lam_im, (S5_CHUNK - 1 - s).astype(F32))
    w = (p_re * bb1_ref[...] + p_im * bb2_ref[...]).astype(BF16)
    k_ref[0] = _dot_nt(w, ca_ref[...].astype(BF16)).astype(BF16)
    contrib = _dot(u_ref[...], w)

    @pl.when(s == 0)
    def _():
        inc_ref[...] = contrib

    @pl.when(s > 0)
    def _():
        inc_ref[...] += contrib

    @pl.when(s == S5_CHUNK - 1)
    def _():
        n_steps = max(1, int(math.ceil(math.log2(n_chunks))))
        step = lax.broadcasted_iota(jnp.int32, (16, S5_LANES), 0)
        e = (jnp.left_shift(1, step) * S5_CHUNK).astype(F32)
        a_re_all, a_im_all = _s5_power(lam_re, lam_im, e)
        row = lax.broadcasted_iota(jnp.int32, (n_chunks, 128), 0)
        half = S5_LANES // 2

        def shifted(x, sh):
            return jnp.where(row >= sh, pltpu.roll(x, sh, 0), 0.0)

        for j in range(half // 128):
            re_l, im_l = slice(128 * j, 128 * (j + 1)), slice(half + 128 * j, half + 128 * (j + 1))
            x_re, x_im = inc_ref[:, re_l], inc_ref[:, im_l]
            for k in range(n_steps):
                sh = 1 << k
                if sh >= n_chunks:
                    break
                a_re, a_im = a_re_all[k:k + 1, re_l], a_im_all[k:k + 1, re_l]
                p_re, p_im = shifted(x_re, sh), shifted(x_im, sh)
                x_re, x_im = x_re + a_re * p_re - a_im * p_im, x_im + a_re * p_im + a_im * p_re
            sp_ref[:, re_l] = shifted(x_re, 1).astype(BF16)
            sp_ref[:, im_l] = shifted(x_im, 1).astype(BF16)


def _s5_out_kernel(ub_ref, uf_ref, k_ref, sp_ref, are_ref, aim_ref, ldt_ref, ca_ref, cb_ref, d_ref,
                   y_ref, taps_ref):
    t = pl.program_id(0)
    fold = S5_CHUNK * GROUP_WIDTH

    @pl.when(t == 0)
    def _():
        for j in range(S5_CHUNK):
            taps_ref[GROUP_WIDTH * j:GROUP_WIDTH * (j + 1), :] = k_ref[j]
        taps_ref[fold:, :] = jnp.zeros((fold - GROUP_WIDTH, GROUP_WIDTH), BF16)

    lam_re, lam_im, _, _ = _s5_rows(are_ref[...], aim_ref[...], ldt_ref[...])
    p_re, p_im = _s5_power(lam_re, lam_im, (t + 1).astype(F32))
    w_out = (p_re * ca_ref[...] + p_im * cb_ref[...]).astype(BF16)
    start = pl.multiple_of((S5_CHUNK - 1 - t) * GROUP_WIDTH, GROUP_WIDTH)
    y_ref[...] = _dot_nt(sp_ref[...], w_out) + d_ref[...] * uf_ref[...]
    quarter = S5_CHUNK // S5_TAP_SPLITS
    for part in range(S5_TAP_SPLITS):
        width = (part + 1) * quarter * GROUP_WIDTH

        @pl.when((t >= part * quarter) & (t < (part + 1) * quarter))
        def _():
            y_ref[...] += _dot(ub_ref[:, 0:width], taps_ref[pl.ds(start, width), :])


def _s5_embed(re, im):
    eye = jnp.eye(S5_GROUPS, dtype=F32)
    blocks = [(eye[None, :, None, :, None] * x[:, :, :, None, :]).reshape(-1, S5_GROUPS * S5_CH, S5_LANES // 2)
              for x in (re, im)]
    return jnp.concatenate(blocks, axis=2)


def _s5_params(a_re, a_im, log_dt, b_re, b_im, c_re, c_im, d_skip):
    depth = a_re.shape[0]
    row = lambda v: jnp.tile(v.reshape(depth, 1, S5_LANES // 2), (1, 1, 2))
    bt_re, bt_im = b_re.transpose(0, 1, 3, 2), b_im.transpose(0, 1, 3, 2)
    return dict(are=row(a_re), aim=row(a_im), ldt=row(jnp.repeat(log_dt, S5_STATE, axis=1)),
                b1=_s5_embed(bt_re, bt_im), b2=_s5_embed(-bt_im, bt_re),
                ca=_s5_embed(c_re, -c_im), cb=_s5_embed(-c_im, -c_re),
                d=d_skip.reshape(depth, 1, GROUP_WIDTH))


def _s5(u_b, u_f, p, layer):
    n_chunks, fold = u_b.shape
    row_spec, mat_spec = _pick((1, S5_LANES), layer), _pick((GROUP_WIDTH, S5_LANES), layer)
    col = lambda: pl.BlockSpec((n_chunks, GROUP_WIDTH), lambda s: (0, s))
    taps, s_prev = pl.pallas_call(
        functools.partial(_s5_state_kernel, n_chunks=n_chunks),
        grid=(S5_CHUNK,),
        in_specs=[col()] + [row_spec] * 3 + [mat_spec] * 3,
        out_specs=[pl.BlockSpec((1, GROUP_WIDTH, GROUP_WIDTH), lambda s: (s, 0, 0)),
                   _full((n_chunks, S5_LANES))],
        out_shape=[jax.ShapeDtypeStruct((S5_CHUNK, GROUP_WIDTH, GROUP_WIDTH), BF16),
                   jax.ShapeDtypeStruct((n_chunks, S5_LANES), BF16)],
        scratch_shapes=[pltpu.VMEM((GROUP_WIDTH, S5_LANES), F32), pltpu.VMEM((GROUP_WIDTH, S5_LANES), F32),
                        pltpu.VMEM((n_chunks, S5_LANES), F32)],
        compiler_params=_cparams("arbitrary"),
        name="s5_state",
    )(u_b, p['are'], p['aim'], p['ldt'], p['b1'], p['b2'], p['ca'])
    return pl.pallas_call(
        _s5_out_kernel,
        grid=(S5_CHUNK,),
        in_specs=[_full((n_chunks, fold)), col(), _full(taps.shape), _full(s_prev.shape)]
        + [row_spec] * 3 + [mat_spec] * 2 + [_pick((1, GROUP_WIDTH), layer)],
        out_specs=col(),
        out_shape=jax.ShapeDtypeStruct((n_chunks, fold), F32),
        scratch_shapes=[pltpu.VMEM(((2 * S5_CHUNK - 1) * GROUP_WIDTH, GROUP_WIDTH), BF16)],
        compiler_params=_cparams("arbitrary"),
        name="s5_out",
    )(u_b, u_f, taps, s_prev, p['are'], p['aim'], p['ldt'], p['ca'], p['cb'], p['d'])


EXPERT_ROW = 8


def _route(logits_t):
    tokens = logits_t.shape[1]
    big = jnp.int32(1 << 20)
    neg = jnp.float32(-jnp.inf)
    g_row = lax.broadcasted_iota(jnp.int32, (8, tokens), 0)
    is_group = g_row < MOE_GROUPS
    gl = jnp.where(is_group, logits_t[0:8, :], neg)
    ge = jnp.where(is_group, jnp.exp(gl - jnp.max(gl, axis=0, keepdims=True)), 0.0)
    gp = ge / jnp.sum(ge, axis=0, keepdims=True)
    p_g = jnp.max(gp, axis=0, keepdims=True)
    g_idx = jnp.min(jnp.where(is_group & (gp == p_g), g_row, big), axis=0, keepdims=True)
    e_row = lax.broadcasted_iota(jnp.int32, (MOE_EXPERTS, tokens), 0)
    in_group = (e_row // MOE_PER_GROUP) == g_idx
    el = jnp.where(in_group, logits_t[EXPERT_ROW:EXPERT_ROW + MOE_EXPERTS, :], neg)
    ee = jnp.where(in_group, jnp.exp(el - jnp.max(el, axis=0, keepdims=True)), 0.0)
    ep = ee / jnp.sum(ee, axis=0, keepdims=True)
    p1 = jnp.max(jnp.where(in_group, ep, -1.0), axis=0, keepdims=True)
    i1 = jnp.min(jnp.where(in_group & (ep == p1), e_row, big), axis=0, keepdims=True)
    rest = in_group & (e_row != i1)
    p2 = jnp.max(jnp.where(rest, ep, -1.0), axis=0, keepdims=True)
    i2 = jnp.min(jnp.where(rest & (ep == p2), e_row, big), axis=0, keepdims=True)
    tot = p1 + p2
    return i1, i2, p_g * p1 / tot, p_g * p2 / tot


INDEX_DIGIT_BITS = 6
ROW_SPLIT = 4
ROUTE_E1, ROUTE_E2, ROUTE_R1, ROUTE_R2, ROUTE_W1, ROUTE_W2 = range(6)
HIGH_HALF = 0xFFFF0000


def _split_rows(ref, value, rows):
    half = value.shape[1] // 2
    lo = lax.bitcast_convert_type(value[:, :half].astype(jnp.bfloat16).astype(F32), jnp.uint32)
    hi = lax.bitcast_convert_type(value[:, half:].astype(jnp.bfloat16).astype(F32), jnp.uint32)
    words = lax.bitcast_convert_type((lo >> 16) | (hi & jnp.uint32(HIGH_HALF)), jnp.int32)
    for j in range(ROW_SPLIT):
        ref[pl.ds(j, rows, stride=ROW_SPLIT), :] = words[:, 128 * j:128 * (j + 1)]


def _merge_rows(ref, rows):
    words = jnp.concatenate([ref[pl.ds(j, rows, stride=ROW_SPLIT), :] for j in range(ROW_SPLIT)], axis=1)
    words = lax.bitcast_convert_type(words, jnp.uint32)
    lo = lax.bitcast_convert_type(words << 16, F32)
    hi = lax.bitcast_convert_type(words & jnp.uint32(HIGH_HALF), F32)
    return jnp.concatenate([lo, hi], axis=1)


def _out_kernel(x_ref, s5_ref, ret_ref, m2_ref, hg_ref, wglu_ref, bglu_ref, wo_ref, g2_ref,
                wrh_ref, br_ref, stri_ref, spread_ref, x1_ref, h2_ref, route_ref, cnt_ref, i1_ref, i2_ref,
                s5_tmp, carry_ref, *, tile, n_tokens):
    @pl.when(pl.program_id(0) == 0)
    def _():
        carry_ref[...] = jnp.zeros_like(carry_ref)

    for s in range(S5_CHUNK):
        for j in range(GROUP_WIDTH // 128):
            lanes = slice(GROUP_WIDTH * s + 128 * j, GROUP_WIDTH * s + 128 * (j + 1))
            s5_tmp[j, pl.ds(s, tile // S5_CHUNK, stride=S5_CHUNK), :] = s5_ref[:, lanes]
    y = jnp.concatenate([s5_tmp[j] for j in range(GROUP_WIDTH // 128)], axis=1)
    y = y * (0.5 * (1.0 + jnp.tanh(math.sqrt(2.0 / math.pi) * (y + 0.044715 * (y * y * y)))))
    y = y * _sigmoid(_dot(y.astype(BF16), wglu_ref[...]) + bglu_ref[...])
    acc = x_ref[...] + _dot(y.astype(BF16), wo_ref[0:256, :])
    acc = acc + _dot(ret_ref[...], wo_ref[256:512, :])
    acc = acc + _dot(m2_ref[...], wo_ref[512:768, :])
    acc = acc + _dot(hg_ref[...], wo_ref[768:1024, :])
    x1_ref[...] = acc
    h2 = _rms(acc, g2_ref[...])
    _split_rows(h2_ref, h2, tile)
    hi, lo = _split2(h2)
    hw = _dot(hi, wrh_ref[...])
    logits = (hw[:, :ROUTE_LANES] + hw[:, ROUTE_LANES:] + _dot(lo, wrh_ref[:, :ROUTE_LANES])) + br_ref[...]
    e1, e2, w1, w2 = _route(logits.T)
    e_row = lax.broadcasted_iota(jnp.int32, (MOE_EXPERTS, tile), 0)
    picked = jnp.where((e_row == e1) | (e_row == e2), 1.0, 0.0)
    rank = carry_ref[:, 0:1] + _dot_nt(picked.astype(BF16), stri_ref[...])
    r1 = jnp.sum(jnp.where(e_row == e1, rank, 0.0), axis=0, keepdims=True)
    r2 = jnp.sum(jnp.where(e_row == e2, rank, 0.0), axis=0, keepdims=True)
    carry_ref[...] += jnp.sum(picked, axis=1, keepdims=True)
    cnt_ref[...] = carry_ref[...]
    rec_row = lax.broadcasted_iota(jnp.int32, (ROUTE_LANES, tile), 0)
    rec = jnp.zeros((ROUTE_LANES, tile), F32)
    for col, val in ((ROUTE_E1, e1.astype(F32)), (ROUTE_E2, e2.astype(F32)), (ROUTE_R1, r1), (ROUTE_R2, r2),
                     (ROUTE_W1, w1), (ROUTE_W2, w2)):
        rec = jnp.where(rec_row == col, val, rec)
    route_ref[...] = rec.T
    digit_row = lax.broadcasted_iota(jnp.int32, (8, tile), 0)
    digits = jnp.zeros((8, tile), F32)
    for slot, (e, r) in enumerate(((e1, r1), (e2, r2))):
        pos = e.astype(F32) * float(n_tokens) + r
        for k, shift in enumerate((2 * INDEX_DIGIT_BITS, INDEX_DIGIT_BITS, 0)):
            digit = jnp.floor(pos * (1.0 / (1 << shift)))
            pos = pos - digit * float(1 << shift)
            digits = jnp.where(digit_row == 3 * slot + k, digit, digits)
    sub = lax.broadcasted_iota(jnp.int32, (1, 128 * ROW_SPLIT), 1) % ROW_SPLIT
    for q in range(tile // 128):
        o = _dot(digits[:, 128 * q:128 * (q + 1)].astype(BF16), spread_ref[...])
        for slot, out_ref in enumerate((i1_ref, i2_ref)):
            moved = (float(1 << (2 * INDEX_DIGIT_BITS)) * o[3 * slot:3 * slot + 1]
                     + float(1 << INDEX_DIGIT_BITS) * o[3 * slot + 1:3 * slot + 2] + o[3 * slot + 2:3 * slot + 3])
            idx = moved.astype(jnp.int32) * ROW_SPLIT + sub
            for j in range(ROW_SPLIT):
                row = ROW_SPLIT * q + j
                out_ref[row:row + 1, :] = idx[:, 128 * j:128 * (j + 1)]


def _out_proj(x, y_s5, y_ret, y_m2, y_hg, w_glu, b_glu, w_out, g2, wr_packed, b_route, layer):
    n, d = x.shape
    tile = min(SEQ_TILE, n)
    i = np.arange(tile)
    strict_lower = jnp.asarray(i[:, None] > i[None, :], dtype=BF16)
    assert MOE_EXPERTS * n <= (1 << (3 * INDEX_DIGIT_BITS))
    per_row = 128 // ROW_SPLIT
    out_rows = tile // per_row
    lanes = np.arange(128 * ROW_SPLIT)
    spread = jnp.asarray(np.arange(128)[:, None] == (lanes // ROW_SPLIT)[None, :], dtype=BF16)
    return pl.pallas_call(
        functools.partial(_out_kernel, tile=tile, n_tokens=n),
        grid=(n // tile,),
        in_specs=[_rows(tile, d), _rows(tile // S5_CHUNK, S5_CHUNK * GROUP_WIDTH)] + [_rows(tile, GROUP_WIDTH)] * 3
        + [_pick((256, 256), layer), _pick((1, 256), layer), _pick((d, d), layer), _pick((1, d), layer),
           _pick((d, 2 * ROUTE_LANES), layer), _pick((1, ROUTE_LANES), layer), _full((tile, tile)),
           _full(spread.shape)],
        out_specs=[_rows(tile, d), _rows(ROW_SPLIT * tile, 128), _rows(tile, ROUTE_LANES),
                   _full((MOE_EXPERTS, ROUTE_LANES)), _rows(out_rows, 128), _rows(out_rows, 128)],
        out_shape=[jax.ShapeDtypeStruct((n, d), F32), jax.ShapeDtypeStruct((ROW_SPLIT * n, 128), jnp.int32),
                   jax.ShapeDtypeStruct((n, ROUTE_LANES), F32), jax.ShapeDtypeStruct((MOE_EXPERTS, ROUTE_LANES), F32)]
        + [jax.ShapeDtypeStruct((n // per_row, 128), jnp.int32)] * 2,
        scratch_shapes=[pltpu.VMEM((GROUP_WIDTH // 128, tile, 128), F32),
                        pltpu.VMEM((MOE_EXPERTS, ROUTE_LANES), F32)],
        compiler_params=_cparams("arbitrary"),
        name="out_proj_router",
    )(x, y_s5, y_ret, y_m2, y_hg, w_glu, b_glu, w_out, g2, wr_packed, b_route, strict_lower, spread)


def _sc_mesh():
    return plsc.VectorSubcoreMesh(core_axis_name="core", subcore_axis_name="subcore")


def _sc_scatter2(src, idx_a, idx_b, n_out):
    n = src.shape[0]

    @functools.partial(pl.kernel, out_type=jax.ShapeDtypeStruct((n_out, 128), src.dtype), mesh=_sc_mesh(),
                       scratch_types=[])
    def scatter_kernel(x_hbm, ia_hbm, ib_hbm, o_hbm):
        def body(x_vmem, ia_vmem, ib_vmem):
            pltpu.sync_copy(x_vmem, o_hbm.at[ia_vmem.at[0]])
            pltpu.sync_copy(x_vmem, o_hbm.at[ib_vmem.at[0]])

        pltpu.emit_pipeline(
            body, grid=(n // SC_WINDOW,),
            in_specs=[pl.BlockSpec((SC_WINDOW, 128), index_map=lambda i: (i, 0)),
                      pl.BlockSpec((1, SC_WINDOW), index_map=lambda i: (i, 0)),
                      pl.BlockSpec((1, SC_WINDOW), index_map=lambda i: (i, 0))],
            out_specs=[],
            core_axis_name=("core", "subcore"), dimension_semantics=(pltpu.PARALLEL,),
        )(x_hbm, ia_hbm, ib_hbm)

    return scatter_kernel(src, idx_a, idx_b)


def _sc_gather2(table, idx_a, idx_b):
    n = idx_a.size
    sds = jax.ShapeDtypeStruct((n, 128), table.dtype)

    @functools.partial(pl.kernel, out_type=(sds, sds), mesh=_sc_mesh(), scratch_types=[])
    def gather_kernel(t_hbm, ia_hbm, ib_hbm, oa_hbm, ob_hbm):
        def body(ia_vmem, ib_vmem, oa_vmem, ob_vmem):
            pltpu.sync_copy(t_hbm.at[ia_vmem.at[0]], oa_vmem)
            pltpu.sync_copy(t_hbm.at[ib_vmem.at[0]], ob_vmem)

        pltpu.emit_pipeline(
            body, grid=(n // SC_WINDOW,),
            in_specs=[pl.BlockSpec((1, SC_WINDOW), index_map=lambda i: (i, 0)),
                      pl.BlockSpec((1, SC_WINDOW), index_map=lambda i: (i, 0))],
            out_specs=[pl.BlockSpec((SC_WINDOW, 128), index_map=lambda i: (i, 0)),
                       pl.BlockSpec((SC_WINDOW, 128), index_map=lambda i: (i, 0))],
            core_axis_name=("core", "subcore"), dimension_semantics=(pltpu.PARALLEL,),
        )(ia_hbm, ib_hbm, oa_hbm, ob_hbm)

    return gather_kernel(table, idx_a, idx_b)


def _dispatch_plan(counts, n, n_tiles):
    cnt = counts[:, 0].astype(jnp.int32)
    blocks = (cnt + EXPERT_TILE - 1) // EXPERT_TILE
    ends = jnp.cumsum(blocks)
    first_tile = ends - blocks
    tile_id = jnp.arange(n_tiles, dtype=jnp.int32)
    tile_expert = jnp.minimum(jnp.sum((tile_id[:, None] >= ends[None, :]).astype(jnp.int32), axis=1), MOE_EXPERTS - 1)
    onehot = (tile_expert[:, None] == jnp.arange(MOE_EXPERTS, dtype=jnp.int32)[None, :]).astype(jnp.int32)
    block_in_expert = tile_id - jnp.sum(onehot * first_tile[None, :], axis=1)
    used = tile_id < ends[-1]
    rows_left = jnp.sum(onehot * cnt[None, :], axis=1) - block_in_expert * EXPERT_TILE
    tile_rows = jnp.where(used, jnp.clip(rows_left, 0, EXPERT_TILE), 0).astype(jnp.int32)
    blocks_per_expert = n // EXPERT_TILE
    tile_block = jnp.where(used, tile_expert * blocks_per_expert + block_in_expert,
                           MOE_EXPERTS * blocks_per_expert).astype(jnp.int32)
    tile_first = (used & (block_in_expert == 0)).astype(jnp.int32)
    return tile_expert, tile_block, tile_rows, tile_first


def _experts_kernel(te_ref, blk_ref, rows_ref, first_ref, xs_ref, wg_ref, wu_ref, wd_ref, y_ref, wgb, wub, wdb):
    i = pl.program_id(0)

    @pl.when(first_ref[i] == 1)
    def _():
        wgb[...] = wg_ref[0, 0].astype(BF16)
        wub[...] = wu_ref[0, 0].astype(BF16)
        wdb[...] = wd_ref[0, 0].astype(BF16)

    @pl.when(rows_ref[i] > 0)
    def _():
        x = _merge_rows(xs_ref, EXPERT_TILE)
        row = lax.broadcasted_iota(jnp.int32, x.shape, 0)
        x = jnp.where(row < rows_ref[i], x, 0.0).astype(BF16)
        act = _silu(_dot(x, wgb[...])) * _dot(x, wub[...])
        _split_rows(y_ref, _dot(act.astype(BF16), wdb[...]), EXPERT_TILE)


def _experts(xs, tile_expert, tile_block, tile_rows, tile_first, w_gate, w_up, w_down, layer):
    n_tiles = tile_expert.shape[0]
    _, _, d, ff = w_gate.shape
    rows_blk = pl.BlockSpec((ROW_SPLIT * EXPERT_TILE, 128), lambda i, te, blk, rows, first: (blk[i], 0))
    return pl.pallas_call(
        _experts_kernel,
        grid_spec=pltpu.PrefetchScalarGridSpec(
            num_scalar_prefetch=4,
            grid=(n_tiles,),
            in_specs=[rows_blk,
                      pl.BlockSpec((1, 1, d, ff), lambda i, te, blk, rows, first: (layer, te[i], 0, 0)),
                      pl.BlockSpec((1, 1, d, ff), lambda i, te, blk, rows, first: (layer, te[i], 0, 0)),
                      pl.BlockSpec((1, 1, ff, d), lambda i, te, blk, rows, first: (layer, te[i], 0, 0))],
            out_specs=rows_blk,
            scratch_shapes=[pltpu.VMEM((d, ff), BF16), pltpu.VMEM((d, ff), BF16), pltpu.VMEM((ff, d), BF16)],
        ),
        out_shape=jax.ShapeDtypeStruct(xs.shape, xs.dtype),
        compiler_params=_cparams("arbitrary"),
        name="moe_experts",
    )(tile_expert, tile_block, tile_rows, tile_first, xs, w_gate, w_up, w_down)


def _combine_kernel(x1_ref, g1_ref, g2_ref, route_ref, gf_ref, o_ref, *, tile, final_norm):
    route = route_ref[...]
    out = (x1_ref[...] + route[:, ROUTE_W1:ROUTE_W1 + 1] * _merge_rows(g1_ref, tile)
           + route[:, ROUTE_W2:ROUTE_W2 + 1] * _merge_rows(g2_ref, tile))
    o_ref[...] = _rms(out, gf_ref[...]) if final_norm else out


def _combine(x1, g1, g2, route, g_final, final_norm):
    n, d = x1.shape
    tile = min(SEQ_TILE, n)
    return pl.pallas_call(
        functools.partial(_combine_kernel, tile=tile, final_norm=final_norm),
        grid=(n // tile,),
        in_specs=[_rows(tile, d), _rows(ROW_SPLIT * tile, 128), _rows(ROW_SPLIT * tile, 128),
                  _rows(tile, ROUTE_LANES), _full((1, d))],
        out_specs=_rows(tile, d),
        out_shape=jax.ShapeDtypeStruct((n, d), F32),
        compiler_params=_cparams("parallel"),
        name="moe_combine",
    )(x1, g1, g2, route, g_final)


def _moe(h2_rows, route, counts, idx1, idx2, x1, w_gate, w_up, w_down, layer, g_final, final_norm):
    n = x1.shape[0]
    n_tiles = (MOE_TOPK * n) // EXPERT_TILE + MOE_EXPERTS
    tile_expert, tile_block, tile_rows, tile_first = _dispatch_plan(counts, n, n_tiles)
    xs = _sc_scatter2(h2_rows, idx1, idx2, ROW_SPLIT * (MOE_EXPERTS * n + EXPERT_TILE))
    ys = _experts(xs, tile_expert, tile_block, tile_rows, tile_first, w_gate, w_up, w_down, layer)
    g1, g2 = _sc_gather2(ys, idx1, idx2)
    return _combine(x1, g1, g2, route, g_final, final_norm)


def kernel(x, positions, norm1_g, w_in, w_out, s5_a_re, s5_a_im, s5_log_dt, s5_b_re, s5_b_im, s5_c_re, s5_c_im, s5_d, s5_w_glu, s5_b_glu, m2_conv_w, m2_conv_b, m2_dt_bias, m2_a_log, m2_d, m2_norm_g, hg_lb_logits, hg_norm_g, norm2_g, moe_w_group, moe_b_group, moe_w_expert, moe_b_expert, moe_w_gate, moe_w_up, moe_w_down, final_norm_g):
    bsz, seqlen, d = x.shape
    assert bsz == 1 and seqlen % SEQ_TILE == 0 and seqlen % EXPERT_TILE == 0
    depth = w_in.shape[0]

    lb_probs = jax.nn.softmax(hg_lb_logits.astype(F32), axis=0)
    lower_bounds = (jnp.cumsum(lb_probs, axis=0) - lb_probs[0]).reshape(depth, 1, GROUP_WIDTH)
    cos_t, sin_t = _rope_tables(positions.reshape(seqlen, 1))

    s5_p = _s5_params(s5_a_re, s5_a_im, s5_log_dt, s5_b_re, s5_b_im, s5_c_re, s5_c_im, s5_d)
    m2_p = _mamba2_params(m2_conv_w, m2_conv_b, m2_dt_bias, m2_a_log, m2_d, m2_norm_g)
    w_route = jnp.zeros((depth, d, ROUTE_LANES), F32)
    w_route = w_route.at[:, :, :MOE_GROUPS].set(moe_w_group)
    w_route = w_route.at[:, :, EXPERT_ROW:EXPERT_ROW + MOE_EXPERTS].set(moe_w_expert)
    wr_packed = jnp.concatenate(_split2(w_route), axis=2)
    b_route = jnp.zeros((depth, 1, ROUTE_LANES), F32)
    b_route = b_route.at[:, 0, :MOE_GROUPS].set(moe_b_group)
    b_route = b_route.at[:, 0, EXPERT_ROW:EXPERT_ROW + MOE_EXPERTS].set(moe_b_expert)
    w_glu_b, w_out_b = s5_w_glu.astype(BF16), w_out.astype(BF16)
    b_glu = s5_b_glu.reshape(depth, 1, GROUP_WIDTH)
    g1, g2 = norm1_g.reshape(depth, 1, d), norm2_g.reshape(depth, 1, d)
    hg_g = hg_norm_g.reshape(depth, 1, GROUP_WIDTH)
    g_final = final_norm_g.reshape(1, d)
    tile = min(SEQ_TILE, seqlen)
    chunk = min(CHUNK, tile)

    xc = x.reshape(seqlen, d)
    for l in range(depth):
        u_b, u_f, p_ret, p_m2, p_hg = _in_proj(xc, g1, _w_prep(w_in, l), l)
        y_s5 = _s5(u_b, u_f, s5_p, l)
        y_ret, y_m2, y_hg = _mixers(
            seqlen,
            _retention_call(p_ret, cos_t, sin_t, tile, min(RET_CHUNK, tile)),
            _mamba2_call(p_m2, m2_p, l, tile, chunk),
            _hgrn2_call(p_hg, lower_bounds, hg_g, l, tile, chunk))
        x1, h2_rows, route, counts, idx1, idx2 = _out_proj(xc, y_s5, y_ret, y_m2, y_hg, w_glu_b, b_glu, w_out_b,
                                                           g2, wr_packed, b_route, l)
        xc = _moe(h2_rows, route, counts, idx1, idx2, x1, moe_w_gate, moe_w_up, moe_w_down, l, g_final,
                  final_norm=(l == depth - 1))
    return xc.reshape(bsz, seqlen, d)
```

```python
import functools
import math

import numpy as np
import jax
import jax.numpy as jnp
from jax import lax
from jax.experimental import pallas as pl
from jax.experimental.pallas import tpu as pltpu
from jax.experimental.pallas import tpu_sc as plsc

F32 = jnp.float32
BF16 = jnp.bfloat16
NORM_EPS = 1e-6

GROUP_WIDTH = 256
HEAD_DIM = 64
N_HEADS = GROUP_WIDTH // HEAD_DIM
S5_GROUPS = 16
S5_CH = 16
S5_STATE = 64
S5_DT_CLAMP = -1e-4
M2_STATE = 128
M2_CONV = 4
M2_CONV_DIM = 768
ROPE_BASE = 10000.0
MOE_GROUPS = 4
MOE_PER_GROUP = 4
MOE_EXPERTS = 16
ROUTE_LANES = 128

SEQ_TILE = 512
CHUNK = 128
RET_CHUNK = 256
CHUNK_UNROLL = 4
S5_CHUNK = 16
MOE_TOPK = 2
EXPERT_TILE = 512
SC_WINDOW = 128
VMEM_LIMIT = 56 * 1024 * 1024


def _cparams(*sem):
    return pltpu.CompilerParams(dimension_semantics=sem, vmem_limit_bytes=VMEM_LIMIT)


def _dot(a, b):
    return jnp.dot(a, b, preferred_element_type=F32)


def _dot_nt(a, b):
    return lax.dot_general(a, b, (((1,), (1,)), ((), ())), preferred_element_type=F32)


def _dot_tn(a, b):
    return lax.dot_general(a, b, (((0,), (0,)), ((), ())), preferred_element_type=F32)


def _split2(x):
    hi = x.astype(BF16)
    return hi, (x - hi.astype(F32)).astype(BF16)


def _split3(x):
    hi = x.astype(BF16)
    r = x - hi.astype(F32)
    mid = r.astype(BF16)
    return hi, mid, (r - mid.astype(F32)).astype(BF16)


def _dot_exact_lhs(m, x):
    hi, mid, lo = _split3(x)
    return _dot(m, hi) + _dot(m, mid) + _dot(m, lo)


def _dot_exact_rhs(x, m):
    hi, lo = _split2(x)
    return _dot(hi, m) + _dot(lo, m)


def _sigmoid(x):
    return 1.0 / (1.0 + jnp.exp(-x))


def _silu(x):
    return x * _sigmoid(x)


def _rms(x, g):
    return x * lax.rsqrt(jnp.mean(x * x, axis=-1, keepdims=True) + NORM_EPS) * g


def _full(shape):
    return pl.BlockSpec(shape, lambda *_: (0,) * len(shape))


def _rows(tile, width):
    return pl.BlockSpec((tile, width), lambda i: (i, 0))


def _pick(shape, layer):
    return pl.BlockSpec((None,) + tuple(shape), lambda *_: (layer,) + (0,) * len(shape))


IN_SEGMENTS = (256, 1024, 1280, 1024)


DT_COL = 9 * GROUP_WIDTH
W_PREP_ROWS = 128


def _w_prep_kernel(w_ref, o_ref):
    o_ref[:, 0:DT_COL] = w_ref[0, :, 0:DT_COL].astype(BF16)
    head = lax.broadcasted_iota(jnp.int32, (W_PREP_ROWS, GROUP_WIDTH), 1) // HEAD_DIM
    rep = jnp.zeros((W_PREP_ROWS, GROUP_WIDTH), F32)
    for h in range(N_HEADS):
        rep = jnp.where(head == h, w_ref[0, :, DT_COL + h:DT_COL + h + 1], rep)
    o_ref[:, DT_COL:DT_COL + GROUP_WIDTH] = rep.astype(BF16)
    o_ref[:, DT_COL + GROUP_WIDTH:] = w_ref[0, :, DT_COL + N_HEADS:].astype(BF16)


def _w_prep(w_in, layer):
    _, d, n_in = w_in.shape
    return pl.pallas_call(
        _w_prep_kernel,
        grid=(d // W_PREP_ROWS,),
        in_specs=[pl.BlockSpec((1, W_PREP_ROWS, n_in), lambda i: (layer, i, 0))],
        out_specs=pl.BlockSpec((W_PREP_ROWS, sum(IN_SEGMENTS)), lambda i: (i, 0)),
        out_shape=jax.ShapeDtypeStruct((d, sum(IN_SEGMENTS)), BF16),
        compiler_params=_cparams("parallel"),
        name="w_in_prep",
    )(w_in)


def _in_proj_kernel(x_ref, g_ref, w_ref, ub_ref, uf_ref, ret_ref, m2_ref, hg_ref, u_tmp, *, tile):
    hb = _rms(x_ref[...], g_ref[...]).astype(BF16)
    c0, c1, c2, c3 = np.cumsum(IN_SEGMENTS)
    ret_ref[...] = _dot(hb, w_ref[:, c0:c1])
    m2_ref[...] = _dot(hb, w_ref[:, c1:c2])
    hg_ref[...] = _dot(hb, w_ref[:, c2:c3])
    u = _dot(hb, w_ref[:, 0:c0])
    for j in range(GROUP_WIDTH // 128):
        u_tmp[j] = u[:, 128 * j:128 * (j + 1)]
    for s in range(S5_CHUNK):
        for j in range(GROUP_WIDTH // 128):
            v = u_tmp[j, pl.ds(s, tile // S5_CHUNK, stride=S5_CHUNK), :]
            lanes = slice(GROUP_WIDTH * s + 128 * j, GROUP_WIDTH * s + 128 * (j + 1))
            uf_ref[:, lanes] = v
            ub_ref[:, lanes] = v.astype(BF16)


def _in_proj(x, g, w, layer):
    n, d = x.shape
    tile = min(SEQ_TILE, n)
    fold = S5_CHUNK * GROUP_WIDTH
    return pl.pallas_call(
        functools.partial(_in_proj_kernel, tile=tile),
        grid=(n // tile,),
        in_specs=[_rows(tile, d), _pick((1, d), layer), _full(w.shape)],
        out_specs=[_rows(tile // S5_CHUNK, fold)] * 2 + [_rows(tile, s) for s in IN_SEGMENTS[1:]],
        out_shape=[jax.ShapeDtypeStruct((n // S5_CHUNK, fold), BF16),
                   jax.ShapeDtypeStruct((n // S5_CHUNK, fold), F32)]
        + [jax.ShapeDtypeStruct((n, s), F32) for s in IN_SEGMENTS[1:]],
        scratch_shapes=[pltpu.VMEM((GROUP_WIDTH // 128, tile, 128), F32)],
        compiler_params=_cparams("parallel"),
        name="in_proj",
    )(x, g, w)


def _rope_kernel(pos_ref, invf_ref, cos_ref, sin_ref):
    ang = pos_ref[...].astype(F32) * invf_ref[...]
    cos_ref[...] = jnp.cos(ang)
    sin_ref[...] = jnp.sin(ang)


def _rope_tables(positions):
    n = positions.shape[0]
    tile = min(SEQ_TILE, n)
    half = HEAD_DIM // 2
    inv_freq = ROPE_BASE ** (-jnp.arange(half, dtype=F32) / half)
    invf = jnp.tile(inv_freq, 128 // half).reshape(1, 128)
    return pl.pallas_call(
        _rope_kernel,
        grid=(n // tile,),
        in_specs=[_rows(tile, 1), _full((1, 128))],
        out_specs=[_rows(tile, 128), _rows(tile, 128)],
        out_shape=[jax.ShapeDtypeStruct((n, 128), F32)] * 2,
        compiler_params=_cparams("parallel"),
        name="rope_tables",
    )(positions, invf)


def _head_mean_matrix():
    h = np.arange(GROUP_WIDTH) // HEAD_DIM
    return jnp.asarray((h[:, None] == h[None, :]) / HEAD_DIM, dtype=BF16)


def _head_block_mask():
    h = np.arange(GROUP_WIDTH) // HEAD_DIM
    return jnp.asarray(h[:, None] == h[None, :], dtype=F32)


HEAD_PAIRS = GROUP_WIDTH // 128


def _lanes(x, j):
    return x[:, 128 * j:128 * (j + 1)]


def _stack_pair(x):
    xb = x.astype(BF16)
    low = lax.broadcasted_iota(jnp.int32, x.shape, 1) < HEAD_DIM
    zero = jnp.zeros_like(xb)
    return jnp.concatenate([jnp.where(low, xb, zero), jnp.where(low, zero, xb)], axis=0)


def _pair_scores(q, kb):
    s = _dot_nt(_stack_pair(q), kb)
    return s[:q.shape[0]], s[q.shape[0]:]


def _pair_apply(a0, a1, v):
    return _dot(jnp.concatenate([a0.astype(BF16), a1.astype(BF16)], axis=1), _stack_pair(v))


def _ret_constants(chunk):
    lg = np.log1p(-np.exp2(-5.0 - np.arange(N_HEADS, dtype=np.float64)))
    idx = np.arange(chunk, dtype=np.float64)
    rel = idx[:, None] - idx[None, :]
    decay = np.where(rel >= 0, np.exp(np.maximum(rel, 0.0)[None] * lg[:, None, None]), 0.0)
    lane_lg = np.repeat(lg, HEAD_DIM)
    xi = np.exp((idx + 1.0)[:, None] * lane_lg[None, :])
    zeta = np.exp((chunk - 1.0 - idx)[:, None] * lane_lg[None, :])
    h = np.arange(GROUP_WIDTH) // HEAD_DIM
    gc = np.where(h[:, None] == h[None, :], np.exp(chunk * lane_lg)[:, None], 0.0)
    f = lambda a: jnp.asarray(a, dtype=F32)
    return f(decay), f(xi), f(zeta), f(gc)


def _ret_kernel(p_ref, cos_ref, sin_ref, dec_ref, xi_ref, zeta_ref, gc_ref, bm_ref, gm_ref,
                o_ref, r_ref, *, chunk, n_chunks):
    lane = lax.broadcasted_iota(jnp.int32, (chunk, GROUP_WIDTH), 1)
    first_half = (lane % HEAD_DIM) < (HEAD_DIM // 2)
    gm = gm_ref[...]

    def rope(t, cos2, sin2):
        rot = jnp.where(first_half, -pltpu.roll(t, GROUP_WIDTH - HEAD_DIM // 2, 1),
                        pltpu.roll(t, HEAD_DIM // 2, 1))
        return t * cos2 + rot * sin2

    def body(c, carry):
        rows = pl.ds(pl.multiple_of(c * chunk, chunk), chunk)
        cs = cos_ref[rows, :]
        sn = sin_ref[rows, :]
        cos2 = jnp.concatenate([cs, cs], axis=1)
        sin2 = jnp.concatenate([sn, sn], axis=1)
        q = rope(p_ref[rows, 0:256], cos2, sin2)
        k = rope(p_ref[rows, 256:512], cos2, sin2) * (HEAD_DIM ** -0.5)
        v = p_ref[rows, 512:768]
        g = p_ref[rows, 768:1024]
        kb = k.astype(BF16)
        inner = []
        for j in range(HEAD_PAIRS):
            s0, s1 = _pair_scores(_lanes(q, j), _lanes(kb, j))
            inner.append(_pair_apply(s0 * dec_ref[2 * j], s1 * dec_ref[2 * j + 1], _lanes(v, j)))
        inner = jnp.concatenate(inner, axis=1)
        r_prev = r_ref[...]
        cross = _dot((q * xi_ref[...]).astype(BF16), r_prev.astype(BF16))
        r_ref[...] = gc_ref[...] * r_prev + bm_ref[...] * _dot_tn(kb, (zeta_ref[...] * v).astype(BF16))
        o = inner + cross
        cen = o - _dot_exact_rhs(o, gm)
        var = _dot_exact_rhs(cen * cen, gm)
        o_ref[rows, :] = (cen * lax.rsqrt(var + NORM_EPS) * _silu(g)).astype(BF16)
        return carry

    lax.fori_loop(0, n_chunks, body, 0, unroll=CHUNK_UNROLL)


def _retention_call(proj, cos_t, sin_t, tile, chunk):
    decay, xi, zeta, gc = _ret_constants(chunk)
    body = functools.partial(_ret_kernel, chunk=chunk, n_chunks=tile // chunk)
    operands = (proj, cos_t, sin_t, decay, xi, zeta, gc, _head_block_mask(), _head_mean_matrix())
    specs = [_rows(tile, 1024), _rows(tile, 128), _rows(tile, 128), _full(decay.shape), _full(xi.shape),
             _full(zeta.shape), _full(gc.shape), _full((256, 256)), _full((256, 256))]
    return body, operands, specs, [pltpu.VMEM((GROUP_WIDTH, GROUP_WIDTH), F32)]


def _tri_matrix(chunk):
    i = np.arange(chunk)
    return jnp.asarray(i[:, None] >= i[None, :], dtype=BF16)


def _m2_kernel(p_ref, cw_ref, cb_ref, dtb_ref, alog_ref, d_ref, ng_ref, tri_ref,
               o_ref, tail_ref, ext_ref, act_ref, st_ref, *, tile, chunk, n_chunks):
    ext_ref[0:8, :] = tail_ref[...]
    ext_ref[8:tile + 8, :] = p_ref[:, 256:1024]
    tail_ref[...] = p_ref[tile - 8:tile, 256:1024]
    conv = cb_ref[...]
    for j in range(M2_CONV):
        lo = 8 - (M2_CONV - 1) + j
        conv = conv + cw_ref[j:j + 1, :] * ext_ref[lo:lo + tile, :]
    act_ref[...] = _silu(conv)
    a_lane = -jnp.exp(alog_ref[...])
    tri = tri_ref[...]
    ti = lax.broadcasted_iota(jnp.int32, (chunk, chunk), 0)
    si = lax.broadcasted_iota(jnp.int32, (chunk, chunk), 1)
    causal = ti >= si
    lane = lax.broadcasted_iota(jnp.int32, (chunk, 128), 1)

    def body(c, carry):
        start = pl.multiple_of(c * chunk, chunk)
        rows = pl.ds(start, chunk)
        xbc = act_ref[rows, :]
        xs = xbc[:, 0:256]
        z = p_ref[rows, 0:256]
        x_dt = p_ref[rows, 1024:1280] + dtb_ref[...]
        dt = jnp.maximum(x_dt, 0.0) + jnp.log1p(jnp.exp(-jnp.abs(x_dt)))
        acum = _dot_exact_lhs(tri, dt * a_lane)
        acum_t = acum.T
        a_last = acum[chunk - 1:chunk, :]
        e_acum = jnp.exp(acum)
        decs = jnp.exp(a_last - acum)
        d_chunk = jnp.exp(a_last)
        xc = xs * dt
        ys = []
        for g in range(2):
            sl = slice(128 * g, 128 * (g + 1))
            bmg = xbc[:, 256 + 128 * g:256 + 128 * (g + 1)].astype(BF16)
            cmg = xbc[:, 512 + 128 * g:512 + 128 * (g + 1)].astype(BF16)
            cb = _dot_nt(cmg, bmg)
            xcg = xc[:, sl]
            yd = jnp.zeros((chunk, 128), F32)
            for hh in range(2):
                col0 = 128 * g + HEAD_DIM * hh
                diff = acum[:, col0:col0 + 1] - acum_t[col0:col0 + 1, :]
                lm = jnp.where(causal, jnp.exp(jnp.where(causal, diff, 0.0)), 0.0)
                xm = jnp.where((lane // HEAD_DIM) == hh, xcg, 0.0).astype(BF16)
                yd = yd + _dot((cb * lm).astype(BF16), xm)
            st = st_ref[:, sl]
            y_off = _dot(cmg, st.astype(BF16)) * e_acum[:, sl]
            st_ref[:, sl] = d_chunk[:, sl] * st + _dot_tn(bmg, (xcg * decs[:, sl]).astype(BF16))
            ys.append(yd + y_off + d_ref[:, sl] * xs[:, sl])
        y = jnp.concatenate(ys, axis=1) * _silu(z)
        o_ref[rows, :] = _rms(y, ng_ref[...]).astype(BF16)
        return carry

    lax.fori_loop(0, n_chunks, body, 0, unroll=CHUNK_UNROLL)


def _mamba2_params(conv_w, conv_b, dt_bias, a_log, d_skip, norm_g):
    depth = conv_w.shape[0]
    lanes = lambda v: jnp.repeat(v, HEAD_DIM, axis=1).reshape(depth, 1, GROUP_WIDTH)
    return (conv_w, conv_b.reshape(depth, 1, -1), lanes(dt_bias), lanes(a_log), lanes(d_skip),
            norm_g.reshape(depth, 1, -1))


def _mamba2_call(proj, params, layer, tile, chunk):
    body = functools.partial(_m2_kernel, tile=tile, chunk=chunk, n_chunks=tile // chunk)
    operands = (proj,) + tuple(params) + (_tri_matrix(chunk),)
    specs = ([_rows(tile, 1280), _pick((M2_CONV, M2_CONV_DIM), layer), _pick((1, M2_CONV_DIM), layer)]
             + [_pick((1, GROUP_WIDTH), layer)] * 4 + [_full((chunk, chunk))])
    scratch = [pltpu.VMEM((8, M2_CONV_DIM), F32), pltpu.VMEM((tile + 8, M2_CONV_DIM), F32),
               pltpu.VMEM((tile, M2_CONV_DIM), F32), pltpu.VMEM((M2_STATE, GROUP_WIDTH), F32)]
    return body, operands, specs, scratch


HG_MATMUL_LEVELS = 3


def _hg_exponent_matrix(chunk):
    levels = HG_MATMUL_LEVELS
    t = np.arange(chunk)[:, None]
    r = np.arange(chunk)[None, :]
    blocks = []
    for lvl in range(levels):
        b = 1 << lvl
        blk = t // b
        odd = (blk % 2) == 1
        q_rows = odd & (r >= blk * b) & (r <= t)
        k_rows = (~odd) & (r > t) & (r <= (blk + 1) * b - 1)
        blocks.append(q_rows | k_rows)
    blocks.append(r <= t)
    return jnp.asarray(np.concatenate(blocks, axis=0), dtype=BF16)


def _hg_kernel(p_ref, lb_ref, ng_ref, gexp_ref, bm_ref, gm_ref, o_ref, st_ref, *, chunk, n_chunks):
    levels = int(math.log2(chunk))
    row = lax.broadcasted_iota(jnp.int32, (chunk, 128), 0)
    odd_rows = [((row >> lvl) & 1) == 1 for lvl in range(levels)]
    row_wide = lax.broadcasted_iota(jnp.int32, (chunk, GROUP_WIDTH), 0)
    odd_rows_wide = [((row_wide >> lvl) & 1) == 1 for lvl in range(levels)]
    ti = lax.broadcasted_iota(jnp.int32, (chunk, chunk), 0)
    si = lax.broadcasted_iota(jnp.int32, (chunk, chunk), 1)
    pair_level = [((ti >> (lvl + 1)) == (si >> (lvl + 1))) & (((ti >> lvl) & 1) == 1) & (((si >> lvl) & 1) == 0)
                  for lvl in range(levels)]
    lb = lb_ref[...]
    gm = gm_ref[...]

    def body(c, carry):
        rows = pl.ds(pl.multiple_of(c * chunk, chunk), chunk)
        q = _silu(p_ref[rows, 0:256])
        forget = lb + (1.0 - lb) * _sigmoid(p_ref[rows, 256:512])
        k = 1.0 - forget
        v = p_ref[rows, 512:768]
        g = p_ref[rows, 768:1024]
        lf_hi, lf_lo = _split2(jnp.log(forget))
        expo = _dot(gexp_ref[...], lf_hi) + _dot(gexp_ref[...], lf_lo)
        bcum = expo[HG_MATMUL_LEVELS * chunk:(HG_MATMUL_LEVELS + 1) * chunk, :]

        def level_log_decay(lvl):
            if lvl < HG_MATMUL_LEVELS:
                return expo[lvl * chunk:(lvl + 1) * chunk, :]
            b = 1 << lvl
            ref = jnp.concatenate([jnp.broadcast_to(bcum[m + b - 1:m + b, :], (2 * b, GROUP_WIDTH))
                                   for m in range(0, chunk, 2 * b)], axis=0)
            return jnp.where(odd_rows_wide[lvl], bcum - ref, ref - bcum)


        log_decay = [level_log_decay(lvl) for lvl in range(levels)]
        intra = []
        for j in range(HEAD_PAIRS):
            qj, kj = _lanes(q, j), _lanes(k, j)
            a0, a1 = (jnp.where(ti == si, s, 0.0) for s in _pair_scores(qj, kj.astype(BF16)))
            for lvl in range(levels):
                w = jnp.exp(_lanes(log_decay[lvl], j)) * jnp.where(odd_rows[lvl], qj, kj)
                s0, s1 = _pair_scores(w, w.astype(BF16))
                a0 = jnp.where(pair_level[lvl], s0, a0)
                a1 = jnp.where(pair_level[lvl], s1, a1)
            intra.append(_pair_apply(a0, a1, _lanes(v, j)))
        intra = jnp.concatenate(intra, axis=1)

        b_last = bcum[chunk - 1:chunk, :]
        suffix = b_last - bcum
        st = st_ref[...]
        cross = _dot_nt((q * jnp.exp(bcum)).astype(BF16), st.astype(BF16))
        st_ref[...] = jnp.exp(b_last) * st + bm_ref[...] * _dot_tn(
            v.astype(BF16), (k * jnp.exp(suffix)).astype(BF16))
        o = intra + cross
        o = o * lax.rsqrt(_dot_exact_rhs(o * o, gm) + NORM_EPS) * ng_ref[...]
        o_ref[rows, :] = (o * _silu(g)).astype(BF16)
        return carry

    lax.fori_loop(0, n_chunks, body, 0, unroll=CHUNK_UNROLL)


def _hgrn2_call(proj, lower_bounds, norm_g, layer, tile, chunk):
    gexp = _hg_exponent_matrix(chunk)
    body = functools.partial(_hg_kernel, chunk=chunk, n_chunks=tile // chunk)
    operands = (proj, lower_bounds, norm_g, gexp, _head_block_mask(), _head_mean_matrix())
    specs = [_rows(tile, 1024), _pick((1, GROUP_WIDTH), layer), _pick((1, GROUP_WIDTH), layer), _full(gexp.shape),
             _full((256, 256)), _full((256, 256))]
    return body, operands, specs, [pltpu.VMEM((GROUP_WIDTH, GROUP_WIDTH), F32)]


def _mixers_kernel(*refs, bodies, n_in, n_scratch):
    n_mix = len(bodies)
    ins, pos = [], 0
    for k in n_in:
        ins.append(refs[pos:pos + k])
        pos += k
    outs = refs[pos:pos + n_mix]
    pos += n_mix
    scratch = []
    for k in n_scratch:
        scratch.append(refs[pos:pos + k])
        pos += k

    @pl.when(pl.program_id(0) == 0)
    def _():
        for group in scratch:
            for ref in group:
                ref[...] = jnp.zeros_like(ref)

    for body, i, o, s in zip(bodies, ins, outs, scratch):
        body(*i, o, *s)


def _mixers(n, *calls):
    tile = min(SEQ_TILE, n)
    bodies = tuple(c[0] for c in calls)
    return pl.pallas_call(
        functools.partial(_mixers_kernel, bodies=bodies, n_in=tuple(len(c[1]) for c in calls),
                          n_scratch=tuple(len(c[3]) for c in calls)),
        grid=(n // tile,),
        in_specs=[s for c in calls for s in c[2]],
        out_specs=[_rows(tile, GROUP_WIDTH)] * len(calls),
        out_shape=[jax.ShapeDtypeStruct((n, GROUP_WIDTH), BF16)] * len(calls),
        scratch_shapes=[s for c in calls for s in c[3]],
        compiler_params=_cparams("arbitrary"),
        name="mixers",
    )(*[a for c in calls for a in c[1]])


S5_LANES = S5_GROUPS * 2 * S5_STATE
S5_TAP_SPLITS = 4


def _s5_rows(a_re, a_im, log_dt):
    are = jnp.minimum(a_re, S5_DT_CLAMP)
    dt = jnp.exp(log_dt)
    lam_re = are * dt
    lam_im = a_im * dt
    mag = jnp.exp(lam_re)
    ab_re = mag * jnp.cos(lam_im)
    ab_im = mag * jnp.sin(lam_im)
    den = are * are + a_im * a_im
    k_re = ((ab_re - 1.0) * are + ab_im * a_im) / den
    k_im = (ab_im * are - (ab_re - 1.0) * a_im) / den
    return lam_re, lam_im, k_re, k_im


def _s5_power(lam_re, lam_im, e):
    m = jnp.exp(e * lam_re)
    return m * jnp.cos(e * lam_im), m * jnp.sin(e * lam_im)


def _s5_state_kernel(u_ref, are_ref, aim_ref, ldt_ref, b1_ref, b2_ref, ca_ref, k_ref, sp_ref,
                     bb1_ref, bb2_ref, inc_ref, *, n_chunks):
    s = pl.program_id(0)
    lam_re, lam_im, k_re, k_im = _s5_rows(are_ref[...], aim_ref[...], ldt_ref[...])

    @pl.when(s == 0)
    def _():
        bb1_ref[...] = k_re * b1_ref[...] + k_im * b2_ref[...]
        bb2_ref[...] = k_re * b2_ref[...] - k_im * b1_ref[...]

    p_re, p_im = _s5_power(lam_re, lam_im, (S5_CHUNK - 1 - s).astype(F32))
    w = (p_re * bb1_ref[...] + p_im * bb2_ref[...]).astype(BF16)
    k_ref[0] = _dot_nt(w, ca_ref[...].astype(BF16)).astype(BF16)
    contrib = _dot(u_ref[...], w)

    @pl.when(s == 0)
    def _():
        inc_ref[...] = contrib

    @pl.when(s > 0)
    def _():
        inc_ref[...] += contrib

    @pl.when(s == S5_CHUNK - 1)
    def _():
        n_steps = max(1, int(math.ceil(math.log2(n_chunks))))
        step = lax.broadcasted_iota(jnp.int32, (16, S5_LANES), 0)
        e = (jnp.left_shift(1, step) * S5_CHUNK).astype(F32)
        a_re_all, a_im_all = _s5_power(lam_re, lam_im, e)
        row = lax.broadcasted_iota(jnp.int32, (n_chunks, 128), 0)
        half = S5_LANES // 2

        def shifted(x, sh):
            return jnp.where(row >= sh, pltpu.roll(x, sh, 0), 0.0)

        for j in range(half // 128):
            re_l, im_l = slice(128 * j, 128 * (j + 1)), slice(half + 128 * j, half + 128 * (j + 1))
            x_re, x_im = inc_ref[:, re_l], inc_ref[:, im_l]
            for k in range(n_steps):
                sh = 1 << k
                if sh >= n_chunks:
                    break
                a_re, a_im = a_re_all[k:k + 1, re_l], a_im_all[k:k + 1, re_l]
                p_re, p_im = shifted(x_re, sh), shifted(x_im, sh)
                x_re, x_im = x_re + a_re * p_re - a_im * p_im, x_im + a_re * p_im + a_im * p_re
            sp_ref[:, re_l] = shifted(x_re, 1).astype(BF16)
            sp_ref[:, im_l] = shifted(x_im, 1).astype(BF16)


def _s5_out_kernel(ub_ref, uf_ref, k_ref, sp_ref, are_ref, aim_ref, ldt_ref, ca_ref, cb_ref, d_ref,
                   y_ref, taps_ref):
    t = pl.program_id(0)
    fold = S5_CHUNK * GROUP_WIDTH

    @pl.when(t == 0)
    def _():
        for j in range(S5_CHUNK):
            taps_ref[GROUP_WIDTH * j:GROUP_WIDTH * (j + 1), :] = k_ref[j]
        taps_ref[fold:, :] = jnp.zeros((fold - GROUP_WIDTH, GROUP_WIDTH), BF16)

    lam_re, lam_im, _, _ = _s5_rows(are_ref[...], aim_ref[...], ldt_ref[...])
    p_re, p_im = _s5_power(lam_re, lam_im, (t + 1).astype(F32))
    w_out = (p_re * ca_ref[...] + p_im * cb_ref[...]).astype(BF16)
    start = pl.multiple_of((S5_CHUNK - 1 - t) * GROUP_WIDTH, GROUP_WIDTH)
    y_ref[...] = _dot_nt(sp_ref[...], w_out) + d_ref[...] * uf_ref[...]
    quarter = S5_CHUNK // S5_TAP_SPLITS
    for part in range(S5_TAP_SPLITS):
        width = (part + 1) * quarter * GROUP_WIDTH

        @pl.when((t >= part * quarter) & (t < (part + 1) * quarter))
        def _():
            y_ref[...] += _dot(ub_ref[:, 0:width], taps_ref[pl.ds(start, width), :])


def _s5_embed(re, im):
    eye = jnp.eye(S5_GROUPS, dtype=F32)
    blocks = [(eye[None, :, None, :, None] * x[:, :, :, None, :]).reshape(-1, S5_GROUPS * S5_CH, S5_LANES // 2)
              for x in (re, im)]
    return jnp.concatenate(blocks, axis=2)


def _s5_params(a_re, a_im, log_dt, b_re, b_im, c_re, c_im, d_skip):
    depth = a_re.shape[0]
    row = lambda v: jnp.tile(v.reshape(depth, 1, S5_LANES // 2), (1, 1, 2))
    bt_re, bt_im = b_re.transpose(0, 1, 3, 2), b_im.transpose(0, 1, 3, 2)
    return dict(are=row(a_re), aim=row(a_im), ldt=row(jnp.repeat(log_dt, S5_STATE, axis=1)),
                b1=_s5_embed(bt_re, bt_im), b2=_s5_embed(-bt_im, bt_re),
                ca=_s5_embed(c_re, -c_im), cb=_s5_embed(-c_im, -c_re),
                d=d_skip.reshape(depth, 1, GROUP_WIDTH))


def _s5(u_b, u_f, p, layer):
    n_chunks, fold = u_b.shape
    row_spec, mat_spec = _pick((1, S5_LANES), layer), _pick((GROUP_WIDTH, S5_LANES), layer)
    col = lambda: pl.BlockSpec((n_chunks, GROUP_WIDTH), lambda s: (0, s))
    taps, s_prev = pl.pallas_call(
        functools.partial(_s5_state_kernel, n_chunks=n_chunks),
        grid=(S5_CHUNK,),
        in_specs=[col()] + [row_spec] * 3 + [mat_spec] * 3,
        out_specs=[pl.BlockSpec((1, GROUP_WIDTH, GROUP_WIDTH), lambda s: (s, 0, 0)),
                   _full((n_chunks, S5_LANES))],
        out_shape=[jax.ShapeDtypeStruct((S5_CHUNK, GROUP_WIDTH, GROUP_WIDTH), BF16),
                   jax.ShapeDtypeStruct((n_chunks, S5_LANES), BF16)],
        scratch_shapes=[pltpu.VMEM((GROUP_WIDTH, S5_LANES), F32), pltpu.VMEM((GROUP_WIDTH, S5_LANES), F32),
                        pltpu.VMEM((n_chunks, S5_LANES), F32)],
        compiler_params=_cparams("arbitrary"),
        name="s5_state",
    )(u_b, p['are'], p['aim'], p['ldt'], p['b1'], p['b2'], p['ca'])
    return pl.pallas_call(
        _s5_out_kernel,
        grid=(S5_CHUNK,),
        in_specs=[_full((n_chunks, fold)), col(), _full(taps.shape), _full(s_prev.shape)]
        + [row_spec] * 3 + [mat_spec] * 2 + [_pick((1, GROUP_WIDTH), layer)],
        out_specs=col(),
        out_shape=jax.ShapeDtypeStruct((n_chunks, fold), F32),
        scratch_shapes=[pltpu.VMEM(((2 * S5_CHUNK - 1) * GROUP_WIDTH, GROUP_WIDTH), BF16)],
        compiler_params=_cparams("arbitrary"),
        name="s5_out",
    )(u_b, u_f, taps, s_prev, p['are'], p['aim'], p['ldt'], p['ca'], p['cb'], p['d'])


EXPERT_ROW = 8


def _route(logits_t):
    tokens = logits_t.shape[1]
    big = jnp.int32(1 << 20)
    neg = jnp.float32(-jnp.inf)
    g_row = lax.broadcasted_iota(jnp.int32, (8, tokens), 0)
    is_group = g_row < MOE_GROUPS
    gl = jnp.where(is_group, logits_t[0:8, :], neg)
    ge = jnp.where(is_group, jnp.exp(gl - jnp.max(gl, axis=0, keepdims=True)), 0.0)
    gp = ge / jnp.sum(ge, axis=0, keepdims=True)
    p_g = jnp.max(gp, axis=0, keepdims=True)
    g_idx = jnp.min(jnp.where(is_group & (gp == p_g), g_row, big), axis=0, keepdims=True)
    e_row = lax.broadcasted_iota(jnp.int32, (MOE_EXPERTS, tokens), 0)
    in_group = (e_row // MOE_PER_GROUP) == g_idx
    el = jnp.where(in_group, logits_t[EXPERT_ROW:EXPERT_ROW + MOE_EXPERTS, :], neg)
    ee = jnp.where(in_group, jnp.exp(el - jnp.max(el, axis=0, keepdims=True)), 0.0)
    ep = ee / jnp.sum(ee, axis=0, keepdims=True)
    p1 = jnp.max(jnp.where(in_group, ep, -1.0), axis=0, keepdims=True)
    i1 = jnp.min(jnp.where(in_group & (ep == p1), e_row, big), axis=0, keepdims=True)
    rest = in_group & (e_row != i1)
    p2 = jnp.max(jnp.where(rest, ep, -1.0), axis=0, keepdims=True)
    i2 = jnp.min(jnp.where(rest & (ep == p2), e_row, big), axis=0, keepdims=True)
    tot = p1 + p2
    return i1, i2, p_g * p1 / tot, p_g * p2 / tot


INDEX_DIGIT_BITS = 6
ROW_SPLIT = 4
ROUTE_E1, ROUTE_E2, ROUTE_R1, ROUTE_R2, ROUTE_W1, ROUTE_W2 = range(6)
HIGH_HALF = 0xFFFF0000


def _split_rows(ref, value, rows):
    half = value.shape[1] // 2
    lo = lax.bitcast_convert_type(value[:, :half].astype(jnp.bfloat16).astype(F32), jnp.uint32)
    hi = lax.bitcast_convert_type(value[:, half:].astype(jnp.bfloat16).astype(F32), jnp.uint32)
    words = lax.bitcast_convert_type((lo >> 16) | (hi & jnp.uint32(HIGH_HALF)), jnp.int32)
    for j in range(ROW_SPLIT):
        ref[pl.ds(j, rows, stride=ROW_SPLIT), :] = words[:, 128 * j:128 * (j + 1)]


def _merge_rows(ref, rows):
    words = jnp.concatenate([ref[pl.ds(j, rows, stride=ROW_SPLIT), :] for j in range(ROW_SPLIT)], axis=1)
    words = lax.bitcast_convert_type(words, jnp.uint32)
    lo = lax.bitcast_convert_type(words << 16, F32)
    hi = lax.bitcast_convert_type(words & jnp.uint32(HIGH_HALF), F32)
    return jnp.concatenate([lo, hi], axis=1)


def _out_kernel(x_ref, s5_ref, ret_ref, m2_ref, hg_ref, wglu_ref, bglu_ref, wo_ref, g2_ref,
                wrh_ref, br_ref, stri_ref, spread_ref, x1_ref, h2_ref, route_ref, cnt_ref, i1_ref, i2_ref,
                s5_tmp, carry_ref, *, tile, n_tokens):
    @pl.when(pl.program_id(0) == 0)
    def _():
        carry_ref[...] = jnp.zeros_like(carry_ref)

    for s in range(S5_CHUNK):
        for j in range(GROUP_WIDTH // 128):
            lanes = slice(GROUP_WIDTH * s + 128 * j, GROUP_WIDTH * s + 128 * (j + 1))
            s5_tmp[j, pl.ds(s, tile // S5_CHUNK, stride=S5_CHUNK), :] = s5_ref[:, lanes]
    y = jnp.concatenate([s5_tmp[j] for j in range(GROUP_WIDTH // 128)], axis=1)
    y = y * (0.5 * (1.0 + jnp.tanh(math.sqrt(2.0 / math.pi) * (y + 0.044715 * (y * y * y)))))
    y = y * _sigmoid(_dot(y.astype(BF16), wglu_ref[...]) + bglu_ref[...])
    acc = x_ref[...] + _dot(y.astype(BF16), wo_ref[0:256, :])
    acc = acc + _dot(ret_ref[...], wo_ref[256:512, :])
    acc = acc + _dot(m2_ref[...], wo_ref[512:768, :])
    acc = acc + _dot(hg_ref[...], wo_ref[768:1024, :])
    x1_ref[...] = acc
    h2 = _rms(acc, g2_ref[...])
    _split_rows(h2_ref, h2, tile)
    hi, lo = _split2(h2)
    hw = _dot(hi, wrh_ref[...])
    logits = (hw[:, :ROUTE_LANES] + hw[:, ROUTE_LANES:] + _dot(lo, wrh_ref[:, :ROUTE_LANES])) + br_ref[...]
    e1, e2, w1, w2 = _route(logits.T)
    e_row = lax.broadcasted_iota(jnp.int32, (MOE_EXPERTS, tile), 0)
    picked = jnp.where((e_row == e1) | (e_row == e2), 1.0, 0.0)
    rank = carry_ref[:, 0:1] + _dot_nt(picked.astype(BF16), stri_ref[...])
    r1 = jnp.sum(jnp.where(e_row == e1, rank, 0.0), axis=0, keepdims=True)
    r2 = jnp.sum(jnp.where(e_row == e2, rank, 0.0), axis=0, keepdims=True)
    carry_ref[...] += jnp.sum(picked, axis=1, keepdims=True)
    cnt_ref[...] = carry_ref[...]
    rec_row = lax.broadcasted_iota(jnp.int32, (ROUTE_LANES, tile), 0)
    rec = jnp.zeros((ROUTE_LANES, tile), F32)
    for col, val in ((ROUTE_E1, e1.astype(F32)), (ROUTE_E2, e2.astype(F32)), (ROUTE_R1, r1), (ROUTE_R2, r2),
                     (ROUTE_W1, w1), (ROUTE_W2, w2)):
        rec = jnp.where(rec_row == col, val, rec)
    route_ref[...] = rec.T
    digit_row = lax.broadcasted_iota(jnp.int32, (8, tile), 0)
    digits = jnp.zeros((8, tile), F32)
    for slot, (e, r) in enumerate(((e1, r1), (e2, r2))):
        pos = e.astype(F32) * float(n_tokens) + r
        for k, shift in enumerate((2 * INDEX_DIGIT_BITS, INDEX_DIGIT_BITS, 0)):
            digit = jnp.floor(pos * (1.0 / (1 << shift)))
            pos = pos - digit * float(1 << shift)
            digits = jnp.where(digit_row == 3 * slot + k, digit, digits)
    sub = lax.broadcasted_iota(jnp.int32, (1, 128 * ROW_SPLIT), 1) % ROW_SPLIT
    for q in range(tile // 128):
        o = _dot(digits[:, 128 * q:128 * (q + 1)].astype(BF16), spread_ref[...])
        for slot, out_ref in enumerate((i1_ref, i2_ref)):
            moved = (float(1 << (2 * INDEX_DIGIT_BITS)) * o[3 * slot:3 * slot + 1]
                     + float(1 << INDEX_DIGIT_BITS) * o[3 * slot + 1:3 * slot + 2] + o[3 * slot + 2:3 * slot + 3])
            idx = moved.astype(jnp.int32) * ROW_SPLIT + sub
            for j in range(ROW_SPLIT):
                row = ROW_SPLIT * q + j
                out_ref[row:row + 1, :] = idx[:, 128 * j:128 * (j + 1)]


def _out_proj(x, y_s5, y_ret, y_m2, y_hg, w_glu, b_glu, w_out, g2, wr_packed, b_route, layer):
    n, d = x.shape
    tile = min(SEQ_TILE, n)
    i = np.arange(tile)
    strict_lower = jnp.asarray(i[:, None] > i[None, :], dtype=BF16)
    assert MOE_EXPERTS * n <= (1 << (3 * INDEX_DIGIT_BITS))
    per_row = 128 // ROW_SPLIT
    out_rows = tile // per_row
    lanes = np.arange(128 * ROW_SPLIT)
    spread = jnp.asarray(np.arange(128)[:, None] == (lanes // ROW_SPLIT)[None, :], dtype=BF16)
    return pl.pallas_call(
        functools.partial(_out_kernel, tile=tile, n_tokens=n),
        grid=(n // tile,),
        in_specs=[_rows(tile, d), _rows(tile // S5_CHUNK, S5_CHUNK * GROUP_WIDTH)] + [_rows(tile, GROUP_WIDTH)] * 3
        + [_pick((256, 256), layer), _pick((1, 256), layer), _pick((d, d), layer), _pick((1, d), layer),
           _pick((d, 2 * ROUTE_LANES), layer), _pick((1, ROUTE_LANES), layer), _full((tile, tile)),
           _full(spread.shape)],
        out_specs=[_rows(tile, d), _rows(ROW_SPLIT * tile, 128), _rows(tile, ROUTE_LANES),
                   _full((MOE_EXPERTS, ROUTE_LANES)), _rows(out_rows, 128), _rows(out_rows, 128)],
        out_shape=[jax.ShapeDtypeStruct((n, d), F32), jax.ShapeDtypeStruct((ROW_SPLIT * n, 128), jnp.int32),
                   jax.ShapeDtypeStruct((n, ROUTE_LANES), F32), jax.ShapeDtypeStruct((MOE_EXPERTS, ROUTE_LANES), F32)]
        + [jax.ShapeDtypeStruct((n // per_row, 128), jnp.int32)] * 2,
        scratch_shapes=[pltpu.VMEM((GROUP_WIDTH // 128, tile, 128), F32),
                        pltpu.VMEM((MOE_EXPERTS, ROUTE_LANES), F32)],
        compiler_params=_cparams("arbitrary"),
        name="out_proj_router",
    )(x, y_s5, y_ret, y_m2, y_hg, w_glu, b_glu, w_out, g2, wr_packed, b_route, strict_lower, spread)


def _sc_mesh():
    return plsc.VectorSubcoreMesh(core_axis_name="core", subcore_axis_name="subcore")


def _sc_scatter2(src, idx_a, idx_b, n_out):
    n = src.shape[0]

    @functools.partial(pl.kernel, out_type=jax.ShapeDtypeStruct((n_out, 128), src.dtype), mesh=_sc_mesh(),
                       scratch_types=[])
    def scatter_kernel(x_hbm, ia_hbm, ib_hbm, o_hbm):
        def body(x_vmem, ia_vmem, ib_vmem):
            pltpu.sync_copy(x_vmem, o_hbm.at[ia_vmem.at[0]])
            pltpu.sync_copy(x_vmem, o_hbm.at[ib_vmem.at[0]])

        pltpu.emit_pipeline(
            body, grid=(n // SC_WINDOW,),
            in_specs=[pl.BlockSpec((SC_WINDOW, 128), index_map=lambda i: (i, 0)),
                      pl.BlockSpec((1, SC_WINDOW), index_map=lambda i: (i, 0)),
                      pl.BlockSpec((1, SC_WINDOW), index_map=lambda i: (i, 0))],
            out_specs=[],
            core_axis_name=("core", "subcore"), dimension_semantics=(pltpu.PARALLEL,),
        )(x_hbm, ia_hbm, ib_hbm)

    return scatter_kernel(src, idx_a, idx_b)


def _sc_gather2(table, idx_a, idx_b):
    n = idx_a.size
    sds = jax.ShapeDtypeStruct((n, 128), table.dtype)

    @functools.partial(pl.kernel, out_type=(sds, sds), mesh=_sc_mesh(), scratch_types=[])
    def gather_kernel(t_hbm, ia_hbm, ib_hbm, oa_hbm, ob_hbm):
        def body(ia_vmem, ib_vmem, oa_vmem, ob_vmem):
            pltpu.sync_copy(t_hbm.at[ia_vmem.at[0]], oa_vmem)
            pltpu.sync_copy(t_hbm.at[ib_vmem.at[0]], ob_vmem)

        pltpu.emit_pipeline(
            body, grid=(n // SC_WINDOW,),
            in_specs=[pl.BlockSpec((1, SC_WINDOW), index_map=lambda i: (i, 0)),
                      pl.BlockSpec((1, SC_WINDOW), index_map=lambda i: (i, 0))],
            out_specs=[pl.BlockSpec((SC_WINDOW, 128), index_map=lambda i: (i, 0)),
                       pl.BlockSpec((SC_WINDOW, 128), index_map=lambda i: (i, 0))],
            core_axis_name=("core", "subcore"), dimension_semantics=(pltpu.PARALLEL,),
        )(ia_hbm, ib_hbm, oa_hbm, ob_hbm)

    return gather_kernel(table, idx_a, idx_b)


def _dispatch_plan(counts, n, n_tiles):
    cnt = counts[:, 0].astype(jnp.int32)
    blocks = (cnt + EXPERT_TILE - 1) // EXPERT_TILE
    ends = jnp.cumsum(blocks)
    first_tile = ends - blocks
    tile_id = jnp.arange(n_tiles, dtype=jnp.int32)
    tile_expert = jnp.minimum(jnp.sum((tile_id[:, None] >= ends[None, :]).astype(jnp.int32), axis=1), MOE_EXPERTS - 1)
    onehot = (tile_expert[:, None] == jnp.arange(MOE_EXPERTS, dtype=jnp.int32)[None, :]).astype(jnp.int32)
    block_in_expert = tile_id - jnp.sum(onehot * first_tile[None, :], axis=1)
    used = tile_id < ends[-1]
    rows_left = jnp.sum(onehot * cnt[None, :], axis=1) - block_in_expert * EXPERT_TILE
    tile_rows = jnp.where(used, jnp.clip(rows_left, 0, EXPERT_TILE), 0).astype(jnp.int32)
    blocks_per_expert = n // EXPERT_TILE
    tile_block = jnp.where(used, tile_expert * blocks_per_expert + block_in_expert,
                           MOE_EXPERTS * blocks_per_expert).astype(jnp.int32)
    tile_first = (used & (block_in_expert == 0)).astype(jnp.int32)
    return tile_expert, tile_block, tile_rows, tile_first


def _experts_kernel(te_ref, blk_ref, rows_ref, first_ref, xs_ref, wg_ref, wu_ref, wd_ref, y_ref, wgb, wub, wdb):
    i = pl.program_id(0)

    @pl.when(first_ref[i] == 1)
    def _():
        wgb[...] = wg_ref[0, 0].astype(BF16)
        wub[...] = wu_ref[0, 0].astype(BF16)
        wdb[...] = wd_ref[0, 0].astype(BF16)

    def run(n_rows):
        x = _merge_rows(xs_ref, n_rows)
        row = lax.broadcasted_iota(jnp.int32, x.shape, 0)
        x = jnp.where(row < rows_ref[i], x, 0.0).astype(BF16)
        act = _silu(_dot(x, wgb[...])) * _dot(x, wub[...])
        _split_rows(y_ref, _dot(act.astype(BF16), wdb[...]), n_rows)

    half = EXPERT_TILE // 2

    @pl.when(rows_ref[i] > half)
    def _():
        run(EXPERT_TILE)

    @pl.when((rows_ref[i] > 0) & (rows_ref[i] <= half))
    def _():
        run(half)


def _experts(xs, tile_expert, tile_block, tile_rows, tile_first, w_gate, w_up, w_down, layer):
    n_tiles = tile_expert.shape[0]
    _, _, d, ff = w_gate.shape
    rows_blk = pl.BlockSpec((ROW_SPLIT * EXPERT_TILE, 128), lambda i, te, blk, rows, first: (blk[i], 0))
    return pl.pallas_call(
        _experts_kernel,
        grid_spec=pltpu.PrefetchScalarGridSpec(
            num_scalar_prefetch=4,
            grid=(n_tiles,),
            in_specs=[rows_blk,
                      pl.BlockSpec((1, 1, d, ff), lambda i, te, blk, rows, first: (layer, te[i], 0, 0)),
                      pl.BlockSpec((1, 1, d, ff), lambda i, te, blk, rows, first: (layer, te[i], 0, 0)),
                      pl.BlockSpec((1, 1, ff, d), lambda i, te, blk, rows, first: (layer, te[i], 0, 0))],
            out_specs=rows_blk,
            scratch_shapes=[pltpu.VMEM((d, ff), BF16), pltpu.VMEM((d, ff), BF16), pltpu.VMEM((ff, d), BF16)],
        ),
        out_shape=jax.ShapeDtypeStruct(xs.shape, xs.dtype),
        compiler_params=_cparams("arbitrary"),
        name="moe_experts",
    )(tile_expert, tile_block, tile_rows, tile_first, xs, w_gate, w_up, w_down)


def _combine_kernel(x1_ref, g1_ref, g2_ref, route_ref, gf_ref, o_ref, *, tile, final_norm):
    route = route_ref[...]
    out = (x1_ref[...] + route[:, ROUTE_W1:ROUTE_W1 + 1] * _merge_rows(g1_ref, tile)
           + route[:, ROUTE_W2:ROUTE_W2 + 1] * _merge_rows(g2_ref, tile))
    o_ref[...] = _rms(out, gf_ref[...]) if final_norm else out


def _combine(x1, g1, g2, route, g_final, final_norm):
    n, d = x1.shape
    tile = min(SEQ_TILE, n)
    return pl.pallas_call(
        functools.partial(_combine_kernel, tile=tile, final_norm=final_norm),
        grid=(n // tile,),
        in_specs=[_rows(tile, d), _rows(ROW_SPLIT * tile, 128), _rows(ROW_SPLIT * tile, 128),
                  _rows(tile, ROUTE_LANES), _full((1, d))],
        out_specs=_rows(tile, d),
        out_shape=jax.ShapeDtypeStruct((n, d), F32),
        compiler_params=_cparams("parallel"),
        name="moe_combine",
    )(x1, g1, g2, route, g_final)


def _moe(h2_rows, route, counts, idx1, idx2, x1, w_gate, w_up, w_down, layer, g_final, final_norm):
    n = x1.shape[0]
    n_tiles = (MOE_TOPK * n) // EXPERT_TILE + MOE_EXPERTS
    tile_expert, tile_block, tile_rows, tile_first = _dispatch_plan(counts, n, n_tiles)
    xs = _sc_scatter2(h2_rows, idx1, idx2, ROW_SPLIT * (MOE_EXPERTS * n + EXPERT_TILE))
    ys = _experts(xs, tile_expert, tile_block, tile_rows, tile_first, w_gate, w_up, w_down, layer)
    g1, g2 = _sc_gather2(ys, idx1, idx2)
    return _combine(x1, g1, g2, route, g_final, final_norm)


def kernel(x, positions, norm1_g, w_in, w_out, s5_a_re, s5_a_im, s5_log_dt, s5_b_re, s5_b_im, s5_c_re, s5_c_im, s5_d, s5_w_glu, s5_b_glu, m2_conv_w, m2_conv_b, m2_dt_bias, m2_a_log, m2_d, m2_norm_g, hg_lb_logits, hg_norm_g, norm2_g, moe_w_group, moe_b_group, moe_w_expert, moe_b_expert, moe_w_gate, moe_w_up, moe_w_down, final_norm_g):
    bsz, seqlen, d = x.shape
    assert bsz == 1 and seqlen % SEQ_TILE == 0 and seqlen % EXPERT_TILE == 0
    depth = w_in.shape[0]

    lb_probs = jax.nn.softmax(hg_lb_logits.astype(F32), axis=0)
    lower_bounds = (jnp.cumsum(lb_probs, axis=0) - lb_probs[0]).reshape(depth, 1, GROUP_WIDTH)
    cos_t, sin_t = _rope_tables(positions.reshape(seqlen, 1))

    s5_p = _s5_params(s5_a_re, s5_a_im, s5_log_dt, s5_b_re, s5_b_im, s5_c_re, s5_c_im, s5_d)
    m2_p = _mamba2_params(m2_conv_w, m2_conv_b, m2_dt_bias, m2_a_log, m2_d, m2_norm_g)
    w_route = jnp.zeros((depth, d, ROUTE_LANES), F32)
    w_route = w_route.at[:, :, :MOE_GROUPS].set(moe_w_group)
    w_route = w_route.at[:, :, EXPERT_ROW:EXPERT_ROW + MOE_EXPERTS].set(moe_w_expert)
    wr_packed = jnp.concatenate(_split2(w_route), axis=2)
    b_route = jnp.zeros((depth, 1, ROUTE_LANES), F32)
    b_route = b_route.at[:, 0, :MOE_GROUPS].set(moe_b_group)
    b_route = b_route.at[:, 0, EXPERT_ROW:EXPERT_ROW + MOE_EXPERTS].set(moe_b_expert)
    w_glu_b, w_out_b = s5_w_glu.astype(BF16), w_out.astype(BF16)
    b_glu = s5_b_glu.reshape(depth, 1, GROUP_WIDTH)
    g1, g2 = norm1_g.reshape(depth, 1, d), norm2_g.reshape(depth, 1, d)
    hg_g = hg_norm_g.reshape(depth, 1, GROUP_WIDTH)
    g_final = final_norm_g.reshape(1, d)
    tile = min(SEQ_TILE, seqlen)
    chunk = min(CHUNK, tile)

    xc = x.reshape(seqlen, d)
    for l in range(depth):
        u_b, u_f, p_ret, p_m2, p_hg = _in_proj(xc, g1, _w_prep(w_in, l), l)
        y_s5 = _s5(u_b, u_f, s5_p, l)
        y_ret, y_m2, y_hg = _mixers(
            seqlen,
            _retention_call(p_ret, cos_t, sin_t, tile, min(RET_CHUNK, tile)),
            _mamba2_call(p_m2, m2_p, l, tile, chunk),
            _hgrn2_call(p_hg, lower_bounds, hg_g, l, tile, chunk))
        x1, h2_rows, route, counts, idx1, idx2 = _out_proj(xc, y_s5, y_ret, y_m2, y_hg, w_glu_b, b_glu, w_out_b,
                                                           g2, wr_packed, b_route, l)
        xc = _moe(h2_rows, route, counts, idx1, idx2, x1, moe_w_gate, moe_w_up, moe_w_down, l, g_final,
                  final_norm=(l == depth - 1))
    return xc.reshape(bsz, seqlen, d)
```

```python
import functools
import math

import numpy as np
import jax
import jax.numpy as jnp
from jax import lax
from jax.experimental import pallas as pl
from jax.experimental.pallas import tpu as pltpu
from jax.experimental.pallas import tpu_sc as plsc

F32 = jnp.float32
BF16 = jnp.bfloat16
NORM_EPS = 1e-6

GROUP_WIDTH = 256
HEAD_DIM = 64
N_HEADS = GROUP_WIDTH // HEAD_DIM
S5_GROUPS = 16
S5_CH = 16
S5_STATE = 64
S5_DT_CLAMP = -1e-4
M2_STATE = 128
M2_CONV = 4
M2_CONV_DIM = 768
ROPE_BASE = 10000.0
MOE_GROUPS = 4
MOE_PER_GROUP = 4
MOE_EXPERTS = 16
ROUTE_LANES = 128

SEQ_TILE = 512
CHUNK = 128
RET_CHUNK = 256
CHUNK_UNROLL = 4
S5_CHUNK = 16
MOE_TOPK = 2
EXPERT_TILE = 512
SC_WINDOW = 128
VMEM_LIMIT = 56 * 1024 * 1024


def _cparams(*sem):
    return pltpu.CompilerParams(dimension_semantics=sem, vmem_limit_bytes=VMEM_LIMIT)


def _dot(a, b):
    return jnp.dot(a, b, preferred_element_type=F32)


def _dot_nt(a, b):
    return lax.dot_general(a, b, (((1,), (1,)), ((), ())), preferred_element_type=F32)


def _dot_tn(a, b):
    return lax.dot_general(a, b, (((0,), (0,)), ((), ())), preferred_element_type=F32)


def _split2(x):
    hi = x.astype(BF16)
    return hi, (x - hi.astype(F32)).astype(BF16)


def _split3(x):
    hi = x.astype(BF16)
    r = x - hi.astype(F32)
    mid = r.astype(BF16)
    return hi, mid, (r - mid.astype(F32)).astype(BF16)


def _dot_exact_lhs(m, x):
    hi, mid, lo = _split3(x)
    return _dot(m, hi) + _dot(m, mid) + _dot(m, lo)


def _dot_exact_rhs(x, m):
    hi, lo = _split2(x)
    return _dot(hi, m) + _dot(lo, m)


def _sigmoid(x):
    return 1.0 / (1.0 + jnp.exp(-x))


def _silu(x):
    return x * _sigmoid(x)


def _rms(x, g):
    return x * lax.rsqrt(jnp.mean(x * x, axis=-1, keepdims=True) + NORM_EPS) * g


def _full(shape):
    return pl.BlockSpec(shape, lambda *_: (0,) * len(shape))


def _rows(tile, width):
    return pl.BlockSpec((tile, width), lambda i: (i, 0))


def _pick(shape, layer):
    return pl.BlockSpec((None,) + tuple(shape), lambda *_: (layer,) + (0,) * len(shape))


IN_SEGMENTS = (256, 1024, 1280, 1024)


DT_COL = 9 * GROUP_WIDTH
W_PREP_ROWS = 128


def _w_prep_kernel(w_ref, o_ref):
    o_ref[:, 0:DT_COL] = w_ref[0, :, 0:DT_COL].astype(BF16)
    head = lax.broadcasted_iota(jnp.int32, (W_PREP_ROWS, GROUP_WIDTH), 1) // HEAD_DIM
    rep = jnp.zeros((W_PREP_ROWS, GROUP_WIDTH), F32)
    for h in range(N_HEADS):
        rep = jnp.where(head == h, w_ref[0, :, DT_COL + h:DT_COL + h + 1], rep)
    o_ref[:, DT_COL:DT_COL + GROUP_WIDTH] = rep.astype(BF16)
    o_ref[:, DT_COL + GROUP_WIDTH:] = w_ref[0, :, DT_COL + N_HEADS:].astype(BF16)


def _w_prep(w_in, layer):
    _, d, n_in = w_in.shape
    return pl.pallas_call(
        _w_prep_kernel,
        grid=(d // W_PREP_ROWS,),
        in_specs=[pl.BlockSpec((1, W_PREP_ROWS, n_in), lambda i: (layer, i, 0))],
        out_specs=pl.BlockSpec((W_PREP_ROWS, sum(IN_SEGMENTS)), lambda i: (i, 0)),
        out_shape=jax.ShapeDtypeStruct((d, sum(IN_SEGMENTS)), BF16),
        compiler_params=_cparams("parallel"),
        name="w_in_prep",
    )(w_in)


def _rope(t, cos2, sin2):
    lane = lax.broadcasted_iota(jnp.int32, t.shape, 1)
    first_half = (lane % HEAD_DIM) < (HEAD_DIM // 2)
    rot = jnp.where(first_half, -pltpu.roll(t, GROUP_WIDTH - HEAD_DIM // 2, 1), pltpu.roll(t, HEAD_DIM // 2, 1))
    return t * cos2 + rot * sin2


def _softplus(x):
    return jnp.maximum(x, 0.0) + jnp.log1p(jnp.exp(-jnp.abs(x)))


def _in_proj_kernel(x_ref, g_ref, w_ref, cos_ref, sin_ref, cw_ref, cb_ref, dtb_ref, lb_ref,
                    ub_ref, uf_ref, ret_ref, m2_ref, hg_ref, u_tmp, tail_ref, ext_ref, *, tile):
    @pl.when(pl.program_id(0) == 0)
    def _():
        tail_ref[...] = jnp.zeros_like(tail_ref)

    hb = _rms(x_ref[...], g_ref[...]).astype(BF16)
    c0, c1, c2, _ = np.cumsum(IN_SEGMENTS)
    w = GROUP_WIDTH

    def piece(base, j):
        return _dot(hb, w_ref[:, base + w * j:base + w * (j + 1)])

    cos2 = jnp.concatenate([cos_ref[...]] * 2, axis=1)
    sin2 = jnp.concatenate([sin_ref[...]] * 2, axis=1)
    ret_ref[:, 0:w] = _rope(piece(c0, 0), cos2, sin2)
    ret_ref[:, w:2 * w] = _rope(piece(c0, 1), cos2, sin2) * (HEAD_DIM ** -0.5)
    ret_ref[:, 2 * w:3 * w] = piece(c0, 2)
    ret_ref[:, 3 * w:4 * w] = _silu(piece(c0, 3))

    m2_ref[:, 0:w] = _silu(piece(c1, 0))
    ext_ref[0:8, :] = tail_ref[...]
    for j in range(M2_CONV_DIM // w):
        cols = slice(w * j, w * (j + 1))
        xbc = piece(c1, 1 + j)
        ext_ref[8:tile + 8, cols] = xbc
        tail_ref[:, cols] = xbc[tile - 8:tile, :]
        conv = cb_ref[:, cols]
        for t in range(M2_CONV):
            lo = 8 - (M2_CONV - 1) + t
            conv = conv + cw_ref[t:t + 1, cols] * ext_ref[lo:lo + tile, cols]
        m2_ref[:, w * (1 + j):w * (2 + j)] = _silu(conv)
    m2_ref[:, w + M2_CONV_DIM:] = _softplus(piece(c1, 1 + M2_CONV_DIM // w) + dtb_ref[...])

    lb = lb_ref[...]
    hg_ref[:, 0:w] = _silu(piece(c2, 0))
    hg_ref[:, w:2 * w] = jnp.log(lb + (1.0 - lb) * _sigmoid(piece(c2, 1)))
    hg_ref[:, 2 * w:3 * w] = piece(c2, 2)
    hg_ref[:, 3 * w:4 * w] = _silu(piece(c2, 3))
    u = _dot(hb, w_ref[:, 0:c0])
    for j in range(GROUP_WIDTH // 128):
        u_tmp[j] = u[:, 128 * j:128 * (j + 1)]
    for s in range(S5_CHUNK):
        for j in range(GROUP_WIDTH // 128):
            v = u_tmp[j, pl.ds(s, tile // S5_CHUNK, stride=S5_CHUNK), :]
            lanes = slice(GROUP_WIDTH * s + 128 * j, GROUP_WIDTH * s + 128 * (j + 1))
            uf_ref[:, lanes] = v
            ub_ref[:, lanes] = v.astype(BF16)


def _in_proj(x, g, w, cos_t, sin_t, conv_w, conv_b, dt_bias, lower_bounds, layer):
    n, d = x.shape
    tile = min(SEQ_TILE, n)
    fold = S5_CHUNK * GROUP_WIDTH
    return pl.pallas_call(
        functools.partial(_in_proj_kernel, tile=tile),
        grid=(n // tile,),
        in_specs=[_rows(tile, d), _pick((1, d), layer), _full(w.shape), _rows(tile, 128), _rows(tile, 128),
                  _pick((M2_CONV, M2_CONV_DIM), layer), _pick((1, M2_CONV_DIM), layer),
                  _pick((1, GROUP_WIDTH), layer), _pick((1, GROUP_WIDTH), layer)],
        out_specs=[_rows(tile // S5_CHUNK, fold)] * 2 + [_rows(tile, s) for s in IN_SEGMENTS[1:]],
        out_shape=[jax.ShapeDtypeStruct((n // S5_CHUNK, fold), BF16),
                   jax.ShapeDtypeStruct((n // S5_CHUNK, fold), F32)]
        + [jax.ShapeDtypeStruct((n, s), F32) for s in IN_SEGMENTS[1:]],
        scratch_shapes=[pltpu.VMEM((GROUP_WIDTH // 128, tile, 128), F32), pltpu.VMEM((8, M2_CONV_DIM), F32),
                        pltpu.VMEM((tile + 8, M2_CONV_DIM), F32)],
        compiler_params=_cparams("arbitrary"),
        name="in_proj",
    )(x, g, w, cos_t, sin_t, conv_w, conv_b, dt_bias, lower_bounds)


def _rope_kernel(pos_ref, invf_ref, cos_ref, sin_ref):
    ang = pos_ref[...].astype(F32) * invf_ref[...]
    cos_ref[...] = jnp.cos(ang)
    sin_ref[...] = jnp.sin(ang)


def _rope_tables(positions):
    n = positions.shape[0]
    tile = min(SEQ_TILE, n)
    half = HEAD_DIM // 2
    inv_freq = ROPE_BASE ** (-jnp.arange(half, dtype=F32) / half)
    invf = jnp.tile(inv_freq, 128 // half).reshape(1, 128)
    return pl.pallas_call(
        _rope_kernel,
        grid=(n // tile,),
        in_specs=[_rows(tile, 1), _full((1, 128))],
        out_specs=[_rows(tile, 128), _rows(tile, 128)],
        out_shape=[jax.ShapeDtypeStruct((n, 128), F32)] * 2,
        compiler_params=_cparams("parallel"),
        name="rope_tables",
    )(positions, invf)


def _head_mean_matrix():
    h = np.arange(GROUP_WIDTH) // HEAD_DIM
    return jnp.asarray((h[:, None] == h[None, :]) / HEAD_DIM, dtype=BF16)


def _head_block_mask():
    h = np.arange(GROUP_WIDTH) // HEAD_DIM
    return jnp.asarray(h[:, None] == h[None, :], dtype=F32)


HEAD_PAIRS = GROUP_WIDTH // 128


def _lanes(x, j):
    return x[:, 128 * j:128 * (j + 1)]


def _stack_pair(x):
    xb = x.astype(BF16)
    low = lax.broadcasted_iota(jnp.int32, x.shape, 1) < HEAD_DIM
    zero = jnp.zeros_like(xb)
    return jnp.concatenate([jnp.where(low, xb, zero), jnp.where(low, zero, xb)], axis=0)


def _pair_scores(q, kb):
    s = _dot_nt(_stack_pair(q), kb)
    return s[:q.shape[0]], s[q.shape[0]:]


def _pair_apply(a0, a1, v):
    return _dot(jnp.concatenate([a0.astype(BF16), a1.astype(BF16)], axis=1), _stack_pair(v))


def _ret_constants(chunk):
    lg = np.log1p(-np.exp2(-5.0 - np.arange(N_HEADS, dtype=np.float64)))
    idx = np.arange(chunk, dtype=np.float64)
    rel = idx[:, None] - idx[None, :]
    decay = np.where(rel >= 0, np.exp(np.maximum(rel, 0.0)[None] * lg[:, None, None]), 0.0)
    lane_lg = np.repeat(lg, HEAD_DIM)
    xi = np.exp((idx + 1.0)[:, None] * lane_lg[None, :])
    zeta = np.exp((chunk - 1.0 - idx)[:, None] * lane_lg[None, :])
    h = np.arange(GROUP_WIDTH) // HEAD_DIM
    gc = np.where(h[:, None] == h[None, :], np.exp(chunk * lane_lg)[:, None], 0.0)
    f = lambda a: jnp.asarray(a, dtype=F32)
    return f(decay), f(xi), f(zeta), f(gc)


def _ret_kernel(p_ref, dec_ref, xi_ref, zeta_ref, gc_ref, bm_ref, gm_ref, o_ref, r_ref, *, chunk, n_chunks):
    gm = gm_ref[...]

    def body(c, carry):
        rows = pl.ds(pl.multiple_of(c * chunk, chunk), chunk)
        q = p_ref[rows, 0:256]
        k = p_ref[rows, 256:512]
        v = p_ref[rows, 512:768]
        gate = p_ref[rows, 768:1024]
        kb = k.astype(BF16)
        inner = []
        for j in range(HEAD_PAIRS):
            s0, s1 = _pair_scores(_lanes(q, j), _lanes(kb, j))
            inner.append(_pair_apply(s0 * dec_ref[2 * j], s1 * dec_ref[2 * j + 1], _lanes(v, j)))
        inner = jnp.concatenate(inner, axis=1)
        r_prev = r_ref[...]
        cross = _dot((q * xi_ref[...]).astype(BF16), r_prev.astype(BF16))
        r_ref[...] = gc_ref[...] * r_prev + bm_ref[...] * _dot_tn(kb, (zeta_ref[...] * v).astype(BF16))
        o = inner + cross
        cen = o - _dot_exact_rhs(o, gm)
        var = _dot_exact_rhs(cen * cen, gm)
        o_ref[rows, :] = (cen * lax.rsqrt(var + NORM_EPS) * gate).astype(BF16)
        return carry

    lax.fori_loop(0, n_chunks, body, 0, unroll=CHUNK_UNROLL)


def _retention_call(proj, tile, chunk):
    decay, xi, zeta, gc = _ret_constants(chunk)
    body = functools.partial(_ret_kernel, chunk=chunk, n_chunks=tile // chunk)
    operands = (proj, decay, xi, zeta, gc, _head_block_mask(), _head_mean_matrix())
    specs = [_rows(tile, 1024), _full(decay.shape), _full(xi.shape),
             _full(zeta.shape), _full(gc.shape), _full((256, 256)), _full((256, 256))]
    return body, operands, specs, [pltpu.VMEM((GROUP_WIDTH, GROUP_WIDTH), F32)]


def _tri_matrix(chunk):
    i = np.arange(chunk)
    return jnp.asarray(i[:, None] >= i[None, :], dtype=BF16)


def _m2_kernel(p_ref, alog_ref, d_ref, ng_ref, tri_ref, o_ref, st_ref, *, chunk, n_chunks):
    a_lane = -jnp.exp(alog_ref[...])
    tri = tri_ref[...]
    ti = lax.broadcasted_iota(jnp.int32, (chunk, chunk), 0)
    si = lax.broadcasted_iota(jnp.int32, (chunk, chunk), 1)
    causal = ti >= si
    lane = lax.broadcasted_iota(jnp.int32, (chunk, 128), 1)

    def body(c, carry):
        start = pl.multiple_of(c * chunk, chunk)
        rows = pl.ds(start, chunk)
        xbc = p_ref[rows, 256:1024]
        xs = xbc[:, 0:256]
        gate = p_ref[rows, 0:256]
        dt = p_ref[rows, 1024:1280]
        acum = _dot_exact_lhs(tri, dt * a_lane)
        acum_t = acum.T
        a_last = acum[chunk - 1:chunk, :]
        e_acum = jnp.exp(acum)
        decs = jnp.exp(a_last - acum)
        d_chunk = jnp.exp(a_last)
        xc = xs * dt
        ys = []
        for g in range(2):
            sl = slice(128 * g, 128 * (g + 1))
            bmg = xbc[:, 256 + 128 * g:256 + 128 * (g + 1)].astype(BF16)
            cmg = xbc[:, 512 + 128 * g:512 + 128 * (g + 1)].astype(BF16)
            cb = _dot_nt(cmg, bmg)
            xcg = xc[:, sl]
            yd = jnp.zeros((chunk, 128), F32)
            for hh in range(2):
                col0 = 128 * g + HEAD_DIM * hh
                diff = acum[:, col0:col0 + 1] - acum_t[col0:col0 + 1, :]
                lm = jnp.where(causal, jnp.exp(jnp.where(causal, diff, 0.0)), 0.0)
                xm = jnp.where((lane // HEAD_DIM) == hh, xcg, 0.0).astype(BF16)
                yd = yd + _dot((cb * lm).astype(BF16), xm)
            st = st_ref[:, sl]
            y_off = _dot(cmg, st.astype(BF16)) * e_acum[:, sl]
            st_ref[:, sl] = d_chunk[:, sl] * st + _dot_tn(bmg, (xcg * decs[:, sl]).astype(BF16))
            ys.append(yd + y_off + d_ref[:, sl] * xs[:, sl])
        y = jnp.concatenate(ys, axis=1) * gate
        o_ref[rows, :] = _rms(y, ng_ref[...]).astype(BF16)
        return carry

    lax.fori_loop(0, n_chunks, body, 0, unroll=CHUNK_UNROLL)


def _head_lanes(v):
    return jnp.repeat(v, HEAD_DIM, axis=1).reshape(v.shape[0], 1, GROUP_WIDTH)


def _mamba2_call(proj, a_log, d_skip, norm_g, layer, tile, chunk):
    body = functools.partial(_m2_kernel, chunk=chunk, n_chunks=tile // chunk)
    operands = (proj, a_log, d_skip, norm_g, _tri_matrix(chunk))
    specs = [_rows(tile, 1280)] + [_pick((1, GROUP_WIDTH), layer)] * 3 + [_full((chunk, chunk))]
    return body, operands, specs, [pltpu.VMEM((M2_STATE, GROUP_WIDTH), F32)]


HG_MATMUL_LEVELS = 3


def _hg_exponent_matrix(chunk):
    levels = HG_MATMUL_LEVELS
    t = np.arange(chunk)[:, None]
    r = np.arange(chunk)[None, :]
    blocks = []
    for lvl in range(levels):
        b = 1 << lvl
        blk = t // b
        odd = (blk % 2) == 1
        q_rows = odd & (r >= blk * b) & (r <= t)
        k_rows = (~odd) & (r > t) & (r <= (blk + 1) * b - 1)
        blocks.append(q_rows | k_rows)
    blocks.append(r <= t)
    return jnp.asarray(np.concatenate(blocks, axis=0), dtype=BF16)


def _hg_kernel(p_ref, ng_ref, gexp_ref, bm_ref, gm_ref, o_ref, st_ref, *, chunk, n_chunks):
    levels = int(math.log2(chunk))
    row = lax.broadcasted_iota(jnp.int32, (chunk, 128), 0)
    odd_rows = [((row >> lvl) & 1) == 1 for lvl in range(levels)]
    row_wide = lax.broadcasted_iota(jnp.int32, (chunk, GROUP_WIDTH), 0)
    odd_rows_wide = [((row_wide >> lvl) & 1) == 1 for lvl in range(levels)]
    ti = lax.broadcasted_iota(jnp.int32, (chunk, chunk), 0)
    si = lax.broadcasted_iota(jnp.int32, (chunk, chunk), 1)
    pair_level = [((ti >> (lvl + 1)) == (si >> (lvl + 1))) & (((ti >> lvl) & 1) == 1) & (((si >> lvl) & 1) == 0)
                  for lvl in range(levels)]
    gm = gm_ref[...]

    def body(c, carry):
        rows = pl.ds(pl.multiple_of(c * chunk, chunk), chunk)
        q = p_ref[rows, 0:256]
        log_forget = p_ref[rows, 256:512]
        k = 1.0 - jnp.exp(log_forget)
        v = p_ref[rows, 512:768]
        gate = p_ref[rows, 768:1024]
        lf_hi, lf_lo = _split2(log_forget)
        expo = _dot(gexp_ref[...], lf_hi) + _dot(gexp_ref[...], lf_lo)
        bcum = expo[HG_MATMUL_LEVELS * chunk:(HG_MATMUL_LEVELS + 1) * chunk, :]

        def level_log_decay(lvl):
            if lvl < HG_MATMUL_LEVELS:
                return expo[lvl * chunk:(lvl + 1) * chunk, :]
            b = 1 << lvl
            ref = jnp.concatenate([jnp.broadcast_to(bcum[m + b - 1:m + b, :], (2 * b, GROUP_WIDTH))
                                   for m in range(0, chunk, 2 * b)], axis=0)
            return jnp.where(odd_rows_wide[lvl], bcum - ref, ref - bcum)


        log_decay = [level_log_decay(lvl) for lvl in range(levels)]
        intra = []
        for j in range(HEAD_PAIRS):
            qj, kj = _lanes(q, j), _lanes(k, j)
            a0, a1 = (jnp.where(ti == si, s, 0.0) for s in _pair_scores(qj, kj.astype(BF16)))
            for lvl in range(levels):
                w = jnp.exp(_lanes(log_decay[lvl], j)) * jnp.where(odd_rows[lvl], qj, kj)
                s0, s1 = _pair_scores(w, w.astype(BF16))
                a0 = jnp.where(pair_level[lvl], s0, a0)
                a1 = jnp.where(pair_level[lvl], s1, a1)
            intra.append(_pair_apply(a0, a1, _lanes(v, j)))
        intra = jnp.concatenate(intra, axis=1)

        b_last = bcum[chunk - 1:chunk, :]
        suffix = b_last - bcum
        st = st_ref[...]
        cross = _dot_nt((q * jnp.exp(bcum)).astype(BF16), st.astype(BF16))
        st_ref[...] = jnp.exp(b_last) * st + bm_ref[...] * _dot_tn(
            v.astype(BF16), (k * jnp.exp(suffix)).astype(BF16))
        o = intra + cross
        o = o * lax.rsqrt(_dot_exact_rhs(o * o, gm) + NORM_EPS) * ng_ref[...]
        o_ref[rows, :] = (o * gate).astype(BF16)
        return carry

    lax.fori_loop(0, n_chunks, body, 0, unroll=CHUNK_UNROLL)


def _hgrn2_call(proj, norm_g, layer, tile, chunk):
    gexp = _hg_exponent_matrix(chunk)
    body = functools.partial(_hg_kernel, chunk=chunk, n_chunks=tile // chunk)
    operands = (proj, norm_g, gexp, _head_block_mask(), _head_mean_matrix())
    specs = [_rows(tile, 1024), _pick((1, GROUP_WIDTH), layer), _full(gexp.shape),
             _full((256, 256)), _full((256, 256))]
    return body, operands, specs, [pltpu.VMEM((GROUP_WIDTH, GROUP_WIDTH), F32)]


def _mixers_kernel(*refs, bodies, n_in, n_scratch):
    n_mix = len(bodies)
    ins, pos = [], 0
    for k in n_in:
        ins.append(refs[pos:pos + k])
        pos += k
    outs = refs[pos:pos + n_mix]
    pos += n_mix
    scratch = []
    for k in n_scratch:
        scratch.append(refs[pos:pos + k])
        pos += k

    @pl.when(pl.program_id(0) == 0)
    def _():
        for group in scratch:
            for ref in group:
                ref[...] = jnp.zeros_like(ref)

    for body, i, o, s in zip(bodies, ins, outs, scratch):
        body(*i, o, *s)


def _mixers(n, *calls):
    tile = min(SEQ_TILE, n)
    bodies = tuple(c[0] for c in calls)
    return pl.pallas_call(
        functools.partial(_mixers_kernel, bodies=bodies, n_in=tuple(len(c[1]) for c in calls),
                          n_scratch=tuple(len(c[3]) for c in calls)),
        grid=(n // tile,),
        in_specs=[s for c in calls for s in c[2]],
        out_specs=[_rows(tile, GROUP_WIDTH)] * len(calls),
        out_shape=[jax.ShapeDtypeStruct((n, GROUP_WIDTH), BF16)] * len(calls),
        scratch_shapes=[s for c in calls for s in c[3]],
        compiler_params=_cparams("arbitrary"),
        name="mixers",
    )(*[a for c in calls for a in c[1]])


S5_LANES = S5_GROUPS * 2 * S5_STATE
S5_TAP_SPLITS = 4


def _s5_rows(a_re, a_im, log_dt):
    are = jnp.minimum(a_re, S5_DT_CLAMP)
    dt = jnp.exp(log_dt)
    lam_re = are * dt
    lam_im = a_im * dt
    mag = jnp.exp(lam_re)
    ab_re = mag * jnp.cos(lam_im)
    ab_im = mag * jnp.sin(lam_im)
    den = are * are + a_im * a_im
    k_re = ((ab_re - 1.0) * are + ab_im * a_im) / den
    k_im = (ab_im * are - (ab_re - 1.0) * a_im) / den
    return lam_re, lam_im, k_re, k_im


def _s5_power(lam_re, lam_im, e):
    m = jnp.exp(e * lam_re)
    return m * jnp.cos(e * lam_im), m * jnp.sin(e * lam_im)


def _s5_state_kernel(u_ref, are_ref, aim_ref, ldt_ref, b1_ref, b2_ref, ca_ref, k_ref, sp_ref,
                     bb1_ref, bb2_ref, inc_ref, *, n_chunks):
    s = pl.program_id(0)
    lam_re, lam_im, k_re, k_im = _s5_rows(are_ref[...], aim_ref[...], ldt_ref[...])

    @pl.when(s == 0)
    def _():
        bb1_ref[...] = k_re * b1_ref[...] + k_im * b2_ref[...]
        bb2_ref[...] = k_re * b2_ref[...] - k_im * b1_ref[...]

    p_re, p_im = _s5_power(lam_re, lam_im, (S5_CHUNK - 1 - s).astype(F32))
    w = (p_re * bb1_ref[...] + p_im * bb2_ref[...]).astype(BF16)
    k_ref[0] = _dot_nt(w, ca_ref[...].astype(BF16)).astype(BF16)
    contrib = _dot(u_ref[...], w)

    @pl.when(s == 0)
    def _():
        inc_ref[...] = contrib

    @pl.when(s > 0)
    def _():
        inc_ref[...] += contrib

    @pl.when(s == S5_CHUNK - 1)
    def _():
        n_steps = max(1, int(math.ceil(math.log2(n_chunks))))
        step = lax.broadcasted_iota(jnp.int32, (16, S5_LANES), 0)
        e = (jnp.left_shift(1, step) * S5_CHUNK).astype(F32)
        a_re_all, a_im_all = _s5_power(lam_re, lam_im, e)
        row = lax.broadcasted_iota(jnp.int32, (n_chunks, 128), 0)
        half = S5_LANES // 2

        def shifted(x, sh):
            return jnp.where(row >= sh, pltpu.roll(x, sh, 0), 0.0)

        for j in range(half // 128):
            re_l, im_l = slice(128 * j, 128 * (j + 1)), slice(half + 128 * j, half + 128 * (j + 1))
            x_re, x_im = inc_ref[:, re_l], inc_ref[:, im_l]
            for k in range(n_steps):
                sh = 1 << k
                if sh >= n_chunks:
                    break
                a_re, a_im = a_re_all[k:k + 1, re_l], a_im_all[k:k + 1, re_l]
                p_re, p_im = shifted(x_re, sh), shifted(x_im, sh)
                x_re, x_im = x_re + a_re * p_re - a_im * p_im, x_im + a_re * p_im + a_im * p_re
            sp_ref[:, re_l] = shifted(x_re, 1).astype(BF16)
            sp_ref[:, im_l] = shifted(x_im, 1).astype(BF16)


def _s5_out_kernel(ub_ref, uf_ref, k_ref, sp_ref, are_ref, aim_ref, ldt_ref, ca_ref, cb_ref, d_ref,
                   y_ref, taps_ref):
    t = pl.program_id(0)
    fold = S5_CHUNK * GROUP_WIDTH

    @pl.when(t == 0)
    def _():
        for j in range(S5_CHUNK):
            taps_ref[GROUP_WIDTH * j:GROUP_WIDTH * (j + 1), :] = k_ref[j]
        taps_ref[fold:, :] = jnp.zeros((fold - GROUP_WIDTH, GROUP_WIDTH), BF16)

    lam_re, lam_im, _, _ = _s5_rows(are_ref[...], aim_ref[...], ldt_ref[...])
    p_re, p_im = _s5_power(lam_re, lam_im, (t + 1).astype(F32))
    w_out = (p_re * ca_ref[...] + p_im * cb_ref[...]).astype(BF16)
    start = pl.multiple_of((S5_CHUNK - 1 - t) * GROUP_WIDTH, GROUP_WIDTH)
    y_ref[...] = _dot_nt(sp_ref[...], w_out) + d_ref[...] * uf_ref[...]
    quarter = S5_CHUNK // S5_TAP_SPLITS
    for part in range(S5_TAP_SPLITS):
        width = (part + 1) * quarter * GROUP_WIDTH

        @pl.when((t >= part * quarter) & (t < (part + 1) * quarter))
        def _():
            y_ref[...] += _dot(ub_ref[:, 0:width], taps_ref[pl.ds(start, width), :])


def _s5_embed(re, im):
    eye = jnp.eye(S5_GROUPS, dtype=F32)
    blocks = [(eye[None, :, None, :, None] * x[:, :, :, None, :]).reshape(-1, S5_GROUPS * S5_CH, S5_LANES // 2)
              for x in (re, im)]
    return jnp.concatenate(blocks, axis=2)


def _s5_params(a_re, a_im, log_dt, b_re, b_im, c_re, c_im, d_skip):
    depth = a_re.shape[0]
    row = lambda v: jnp.tile(v.reshape(depth, 1, S5_LANES // 2), (1, 1, 2))
    bt_re, bt_im = b_re.transpose(0, 1, 3, 2), b_im.transpose(0, 1, 3, 2)
    return dict(are=row(a_re), aim=row(a_im), ldt=row(jnp.repeat(log_dt, S5_STATE, axis=1)),
                b1=_s5_embed(bt_re, bt_im), b2=_s5_embed(-bt_im, bt_re),
                ca=_s5_embed(c_re, -c_im), cb=_s5_embed(-c_im, -c_re),
                d=d_skip.reshape(depth, 1, GROUP_WIDTH))


def _s5(u_b, u_f, p, layer):
    n_chunks, fold = u_b.shape
    row_spec, mat_spec = _pick((1, S5_LANES), layer), _pick((GROUP_WIDTH, S5_LANES), layer)
    col = lambda: pl.BlockSpec((n_chunks, GROUP_WIDTH), lambda s: (0, s))
    taps, s_prev = pl.pallas_call(
        functools.partial(_s5_state_kernel, n_chunks=n_chunks),
        grid=(S5_CHUNK,),
        in_specs=[col()] + [row_spec] * 3 + [mat_spec] * 3,
        out_specs=[pl.BlockSpec((1, GROUP_WIDTH, GROUP_WIDTH), lambda s: (s, 0, 0)),
                   _full((n_chunks, S5_LANES))],
        out_shape=[jax.ShapeDtypeStruct((S5_CHUNK, GROUP_WIDTH, GROUP_WIDTH), BF16),
                   jax.ShapeDtypeStruct((n_chunks, S5_LANES), BF16)],
        scratch_shapes=[pltpu.VMEM((GROUP_WIDTH, S5_LANES), F32), pltpu.VMEM((GROUP_WIDTH, S5_LANES), F32),
                        pltpu.VMEM((n_chunks, S5_LANES), F32)],
        compiler_params=_cparams("arbitrary"),
        name="s5_state",
    )(u_b, p['are'], p['aim'], p['ldt'], p['b1'], p['b2'], p['ca'])
    return pl.pallas_call(
        _s5_out_kernel,
        grid=(S5_CHUNK,),
        in_specs=[_full((n_chunks, fold)), col(), _full(taps.shape), _full(s_prev.shape)]
        + [row_spec] * 3 + [mat_spec] * 2 + [_pick((1, GROUP_WIDTH), layer)],
        out_specs=col(),
        out_shape=jax.ShapeDtypeStruct((n_chunks, fold), F32),
        scratch_shapes=[pltpu.VMEM(((2 * S5_CHUNK - 1) * GROUP_WIDTH, GROUP_WIDTH), BF16)],
        compiler_params=_cparams("arbitrary"),
        name="s5_out",
    )(u_b, u_f, taps, s_prev, p['are'], p['aim'], p['ldt'], p['ca'], p['cb'], p['d'])


EXPERT_ROW = 8


def _route(logits_t):
    tokens = logits_t.shape[1]
    big = jnp.int32(1 << 20)
    neg = jnp.float32(-jnp.inf)
    g_row = lax.broadcasted_iota(jnp.int32, (8, tokens), 0)
    is_group = g_row < MOE_GROUPS
    gl = jnp.where(is_group, logits_t[0:8, :], neg)
    ge = jnp.where(is_group, jnp.exp(gl - jnp.max(gl, axis=0, keepdims=True)), 0.0)
    gp = ge / jnp.sum(ge, axis=0, keepdims=True)
    p_g = jnp.max(gp, axis=0, keepdims=True)
    g_idx = jnp.min(jnp.where(is_group & (gp == p_g), g_row, big), axis=0, keepdims=True)
    e_row = lax.broadcasted_iota(jnp.int32, (MOE_EXPERTS, tokens), 0)
    in_group = (e_row // MOE_PER_GROUP) == g_idx
    el = jnp.where(in_group, logits_t[EXPERT_ROW:EXPERT_ROW + MOE_EXPERTS, :], neg)
    ee = jnp.where(in_group, jnp.exp(el - jnp.max(el, axis=0, keepdims=True)), 0.0)
    ep = ee / jnp.sum(ee, axis=0, keepdims=True)
    p1 = jnp.max(jnp.where(in_group, ep, -1.0), axis=0, keepdims=True)
    i1 = jnp.min(jnp.where(in_group & (ep == p1), e_row, big), axis=0, keepdims=True)
    rest = in_group & (e_row != i1)
    p2 = jnp.max(jnp.where(rest, ep, -1.0), axis=0, keepdims=True)
    i2 = jnp.min(jnp.where(rest & (ep == p2), e_row, big), axis=0, keepdims=True)
    tot = p1 + p2
    return i1, i2, p_g * p1 / tot, p_g * p2 / tot


INDEX_DIGIT_BITS = 6
ROW_SPLIT = 4
ROUTE_E1, ROUTE_E2, ROUTE_R1, ROUTE_R2, ROUTE_W1, ROUTE_W2 = range(6)
HIGH_HALF = 0xFFFF0000


def _split_rows(ref, value, rows):
    half = value.shape[1] // 2
    lo = lax.bitcast_convert_type(value[:, :half].astype(jnp.bfloat16).astype(F32), jnp.uint32)
    hi = lax.bitcast_convert_type(value[:, half:].astype(jnp.bfloat16).astype(F32), jnp.uint32)
    words = lax.bitcast_convert_type((lo >> 16) | (hi & jnp.uint32(HIGH_HALF)), jnp.int32)
    for j in range(ROW_SPLIT):
        ref[pl.ds(j, rows, stride=ROW_SPLIT), :] = words[:, 128 * j:128 * (j + 1)]


def _merge_rows(ref, rows):
    words = jnp.concatenate([ref[pl.ds(j, rows, stride=ROW_SPLIT), :] for j in range(ROW_SPLIT)], axis=1)
    words = lax.bitcast_convert_type(words, jnp.uint32)
    lo = lax.bitcast_convert_type(words << 16, F32)
    hi = lax.bitcast_convert_type(words & jnp.uint32(HIGH_HALF), F32)
    return jnp.concatenate([lo, hi], axis=1)


def _out_kernel(x_ref, s5_ref, ret_ref, m2_ref, hg_ref, wglu_ref, bglu_ref, wo_ref, g2_ref,
                wrh_ref, br_ref, stri_ref, spread_ref, x1_ref, h2_ref, route_ref, cnt_ref, i1_ref, i2_ref,
                s5_tmp, carry_ref, *, tile, n_tokens):
    @pl.when(pl.program_id(0) == 0)
    def _():
        carry_ref[...] = jnp.zeros_like(carry_ref)

    for s in range(S5_CHUNK):
        for j in range(GROUP_WIDTH // 128):
            lanes = slice(GROUP_WIDTH * s + 128 * j, GROUP_WIDTH * s + 128 * (j + 1))
            s5_tmp[j, pl.ds(s, tile // S5_CHUNK, stride=S5_CHUNK), :] = s5_ref[:, lanes]
    y = jnp.concatenate([s5_tmp[j] for j in range(GROUP_WIDTH // 128)], axis=1)
    y = y * (0.5 * (1.0 + jnp.tanh(math.sqrt(2.0 / math.pi) * (y + 0.044715 * (y * y * y)))))
    y = y * _sigmoid(_dot(y.astype(BF16), wglu_ref[...]) + bglu_ref[...])
    acc = x_ref[...] + _dot(y.astype(BF16), wo_ref[0:256, :])
    acc = acc + _dot(ret_ref[...], wo_ref[256:512, :])
    acc = acc + _dot(m2_ref[...], wo_ref[512:768, :])
    acc = acc + _dot(hg_ref[...], wo_ref[768:1024, :])
    x1_ref[...] = acc
    h2 = _rms(acc, g2_ref[...])
    _split_rows(h2_ref, h2, tile)
    hi, lo = _split2(h2)
    hw = _dot(hi, wrh_ref[...])
    logits = (hw[:, :ROUTE_LANES] + hw[:, ROUTE_LANES:] + _dot(lo, wrh_ref[:, :ROUTE_LANES])) + br_ref[...]
    e1, e2, w1, w2 = _route(logits.T)
    e_row = lax.broadcasted_iota(jnp.int32, (MOE_EXPERTS, tile), 0)
    picked = jnp.where((e_row == e1) | (e_row == e2), 1.0, 0.0)
    rank = carry_ref[:, 0:1] + _dot_nt(picked.astype(BF16), stri_ref[...])
    r1 = jnp.sum(jnp.where(e_row == e1, rank, 0.0), axis=0, keepdims=True)
    r2 = jnp.sum(jnp.where(e_row == e2, rank, 0.0), axis=0, keepdims=True)
    carry_ref[...] += jnp.sum(picked, axis=1, keepdims=True)
    cnt_ref[...] = carry_ref[...]
    rec_row = lax.broadcasted_iota(jnp.int32, (ROUTE_LANES, tile), 0)
    rec = jnp.zeros((ROUTE_LANES, tile), F32)
    for col, val in ((ROUTE_E1, e1.astype(F32)), (ROUTE_E2, e2.astype(F32)), (ROUTE_R1, r1), (ROUTE_R2, r2),
                     (ROUTE_W1, w1), (ROUTE_W2, w2)):
        rec = jnp.where(rec_row == col, val, rec)
    route_ref[...] = rec.T
    digit_row = lax.broadcasted_iota(jnp.int32, (8, tile), 0)
    digits = jnp.zeros((8, tile), F32)
    for slot, (e, r) in enumerate(((e1, r1), (e2, r2))):
        pos = e.astype(F32) * float(n_tokens) + r
        for k, shift in enumerate((2 * INDEX_DIGIT_BITS, INDEX_DIGIT_BITS, 0)):
            digit = jnp.floor(pos * (1.0 / (1 << shift)))
            pos = pos - digit * float(1 << shift)
            digits = jnp.where(digit_row == 3 * slot + k, digit, digits)
    sub = lax.broadcasted_iota(jnp.int32, (1, 128 * ROW_SPLIT), 1) % ROW_SPLIT
    for q in range(tile // 128):
        o = _dot(digits[:, 128 * q:128 * (q + 1)].astype(BF16), spread_ref[...])
        for slot, out_ref in enumerate((i1_ref, i2_ref)):
            moved = (float(1 << (2 * INDEX_DIGIT_BITS)) * o[3 * slot:3 * slot + 1]
                     + float(1 << INDEX_DIGIT_BITS) * o[3 * slot + 1:3 * slot + 2] + o[3 * slot + 2:3 * slot + 3])
            idx = moved.astype(jnp.int32) * ROW_SPLIT + sub
            for j in range(ROW_SPLIT):
                row = ROW_SPLIT * q + j
                out_ref[row:row + 1, :] = idx[:, 128 * j:128 * (j + 1)]


def _out_proj(x, y_s5, y_ret, y_m2, y_hg, w_glu, b_glu, w_out, g2, wr_packed, b_route, layer):
    n, d = x.shape
    tile = min(SEQ_TILE, n)
    i = np.arange(tile)
    strict_lower = jnp.asarray(i[:, None] > i[None, :], dtype=BF16)
    assert MOE_EXPERTS * n <= (1 << (3 * INDEX_DIGIT_BITS))
    per_row = 128 // ROW_SPLIT
    out_rows = tile // per_row
    lanes = np.arange(128 * ROW_SPLIT)
    spread = jnp.asarray(np.arange(128)[:, None] == (lanes // ROW_SPLIT)[None, :], dtype=BF16)
    return pl.pallas_call(
        functools.partial(_out_kernel, tile=tile, n_tokens=n),
        grid=(n // tile,),
        in_specs=[_rows(tile, d), _rows(tile // S5_CHUNK, S5_CHUNK * GROUP_WIDTH)] + [_rows(tile, GROUP_WIDTH)] * 3
        + [_pick((256, 256), layer), _pick((1, 256), layer), _pick((d, d), layer), _pick((1, d), layer),
           _pick((d, 2 * ROUTE_LANES), layer), _pick((1, ROUTE_LANES), layer), _full((tile, tile)),
           _full(spread.shape)],
        out_specs=[_rows(tile, d), _rows(ROW_SPLIT * tile, 128), _rows(tile, ROUTE_LANES),
                   _full((MOE_EXPERTS, ROUTE_LANES)), _rows(out_rows, 128), _rows(out_rows, 128)],
        out_shape=[jax.ShapeDtypeStruct((n, d), F32), jax.ShapeDtypeStruct((ROW_SPLIT * n, 128), jnp.int32),
                   jax.ShapeDtypeStruct((n, ROUTE_LANES), F32), jax.ShapeDtypeStruct((MOE_EXPERTS, ROUTE_LANES), F32)]
        + [jax.ShapeDtypeStruct((n // per_row, 128), jnp.int32)] * 2,
        scratch_shapes=[pltpu.VMEM((GROUP_WIDTH // 128, tile, 128), F32),
                        pltpu.VMEM((MOE_EXPERTS, ROUTE_LANES), F32)],
        compiler_params=_cparams("arbitrary"),
        name="out_proj_router",
    )(x, y_s5, y_ret, y_m2, y_hg, w_glu, b_glu, w_out, g2, wr_packed, b_route, strict_lower, spread)


def _sc_mesh():
    return plsc.VectorSubcoreMesh(core_axis_name="core", subcore_axis_name="subcore")


def _sc_scatter2(src, idx_a, idx_b, n_out):
    n = src.shape[0]

    @functools.partial(pl.kernel, out_type=jax.ShapeDtypeStruct((n_out, 128), src.dtype), mesh=_sc_mesh(),
                       scratch_types=[])
    def scatter_kernel(x_hbm, ia_hbm, ib_hbm, o_hbm):
        def body(x_vmem, ia_vmem, ib_vmem):
            pltpu.sync_copy(x_vmem, o_hbm.at[ia_vmem.at[0]])
            pltpu.sync_copy(x_vmem, o_hbm.at[ib_vmem.at[0]])

        pltpu.emit_pipeline(
            body, grid=(n // SC_WINDOW,),
            in_specs=[pl.BlockSpec((SC_WINDOW, 128), index_map=lambda i: (i, 0)),
                      pl.BlockSpec((1, SC_WINDOW), index_map=lambda i: (i, 0)),
                      pl.BlockSpec((1, SC_WINDOW), index_map=lambda i: (i, 0))],
            out_specs=[],
            core_axis_name=("core", "subcore"), dimension_semantics=(pltpu.PARALLEL,),
        )(x_hbm, ia_hbm, ib_hbm)

    return scatter_kernel(src, idx_a, idx_b)


def _sc_gather2(table, idx_a, idx_b):
    n = idx_a.size
    sds = jax.ShapeDtypeStruct((n, 128), table.dtype)

    @functools.partial(pl.kernel, out_type=(sds, sds), mesh=_sc_mesh(), scratch_types=[])
    def gather_kernel(t_hbm, ia_hbm, ib_hbm, oa_hbm, ob_hbm):
        def body(ia_vmem, ib_vmem, oa_vmem, ob_vmem):
            pltpu.sync_copy(t_hbm.at[ia_vmem.at[0]], oa_vmem)
            pltpu.sync_copy(t_hbm.at[ib_vmem.at[0]], ob_vmem)

        pltpu.emit_pipeline(
            body, grid=(n // SC_WINDOW,),
            in_specs=[pl.BlockSpec((1, SC_WINDOW), index_map=lambda i: (i, 0)),
                      pl.BlockSpec((1, SC_WINDOW), index_map=lambda i: (i, 0))],
            out_specs=[pl.BlockSpec((SC_WINDOW, 128), index_map=lambda i: (i, 0)),
                       pl.BlockSpec((SC_WINDOW, 128), index_map=lambda i: (i, 0))],
            core_axis_name=("core", "subcore"), dimension_semantics=(pltpu.PARALLEL,),
        )(ia_hbm, ib_hbm, oa_hbm, ob_hbm)

    return gather_kernel(table, idx_a, idx_b)


def _dispatch_plan(counts, n, n_tiles):
    cnt = counts[:, 0].astype(jnp.int32)
    blocks = (cnt + EXPERT_TILE - 1) // EXPERT_TILE
    ends = jnp.cumsum(blocks)
    first_tile = ends - blocks
    tile_id = jnp.arange(n_tiles, dtype=jnp.int32)
    tile_expert = jnp.minimum(jnp.sum((tile_id[:, None] >= ends[None, :]).astype(jnp.int32), axis=1), MOE_EXPERTS - 1)
    onehot = (tile_expert[:, None] == jnp.arange(MOE_EXPERTS, dtype=jnp.int32)[None, :]).astype(jnp.int32)
    block_in_expert = tile_id - jnp.sum(onehot * first_tile[None, :], axis=1)
    used = tile_id < ends[-1]
    rows_left = jnp.sum(onehot * cnt[None, :], axis=1) - block_in_expert * EXPERT_TILE
    tile_rows = jnp.where(used, jnp.clip(rows_left, 0, EXPERT_TILE), 0).astype(jnp.int32)
    blocks_per_expert = n // EXPERT_TILE
    tile_block = jnp.where(used, tile_expert * blocks_per_expert + block_in_expert,
                           MOE_EXPERTS * blocks_per_expert).astype(jnp.int32)
    tile_first = (used & (block_in_expert == 0)).astype(jnp.int32)
    return tile_expert, tile_block, tile_rows, tile_first


def _experts_kernel(te_ref, blk_ref, rows_ref, first_ref, xs_ref, wg_ref, wu_ref, wd_ref, y_ref, wgb, wub, wdb):
    i = pl.program_id(0)

    @pl.when(first_ref[i] == 1)
    def _():
        wgb[...] = wg_ref[0, 0].astype(BF16)
        wub[...] = wu_ref[0, 0].astype(BF16)
        wdb[...] = wd_ref[0, 0].astype(BF16)

    def run(n_rows):
        x = _merge_rows(xs_ref, n_rows)
        row = lax.broadcasted_iota(jnp.int32, x.shape, 0)
        x = jnp.where(row < rows_ref[i], x, 0.0).astype(BF16)
        act = _silu(_dot(x, wgb[...])) * _dot(x, wub[...])
        _split_rows(y_ref, _dot(act.astype(BF16), wdb[...]), n_rows)

    half = EXPERT_TILE // 2

    @pl.when(rows_ref[i] > half)
    def _():
        run(EXPERT_TILE)

    @pl.when((rows_ref[i] > 0) & (rows_ref[i] <= half))
    def _():
        run(half)


def _experts(xs, tile_expert, tile_block, tile_rows, tile_first, w_gate, w_up, w_down, layer):
    n_tiles = tile_expert.shape[0]
    _, _, d, ff = w_gate.shape
    rows_blk = pl.BlockSpec((ROW_SPLIT * EXPERT_TILE, 128), lambda i, te, blk, rows, first: (blk[i], 0))
    return pl.pallas_call(
        _experts_kernel,
        grid_spec=pltpu.PrefetchScalarGridSpec(
            num_scalar_prefetch=4,
            grid=(n_tiles,),
            in_specs=[rows_blk,
                      pl.BlockSpec((1, 1, d, ff), lambda i, te, blk, rows, first: (layer, te[i], 0, 0)),
                      pl.BlockSpec((1, 1, d, ff), lambda i, te, blk, rows, first: (layer, te[i], 0, 0)),
                      pl.BlockSpec((1, 1, ff, d), lambda i, te, blk, rows, first: (layer, te[i], 0, 0))],
            out_specs=rows_blk,
            scratch_shapes=[pltpu.VMEM((d, ff), BF16), pltpu.VMEM((d, ff), BF16), pltpu.VMEM((ff, d), BF16)],
        ),
        out_shape=jax.ShapeDtypeStruct(xs.shape, xs.dtype),
        compiler_params=_cparams("arbitrary"),
        name="moe_experts",
    )(tile_expert, tile_block, tile_rows, tile_first, xs, w_gate, w_up, w_down)


def _combine_kernel(x1_ref, g1_ref, g2_ref, route_ref, gf_ref, o_ref, *, tile, final_norm):
    route = route_ref[...]
    out = (x1_ref[...] + route[:, ROUTE_W1:ROUTE_W1 + 1] * _merge_rows(g1_ref, tile)
           + route[:, ROUTE_W2:ROUTE_W2 + 1] * _merge_rows(g2_ref, tile))
    o_ref[...] = _rms(out, gf_ref[...]) if final_norm else out


def _combine(x1, g1, g2, route, g_final, final_norm):
    n, d = x1.shape
    tile = min(SEQ_TILE, n)
    return pl.pallas_call(
        functools.partial(_combine_kernel, tile=tile, final_norm=final_norm),
        grid=(n // tile,),
        in_specs=[_rows(tile, d), _rows(ROW_SPLIT * tile, 128), _rows(ROW_SPLIT * tile, 128),
                  _rows(tile, ROUTE_LANES), _full((1, d))],
        out_specs=_rows(tile, d),
        out_shape=jax.ShapeDtypeStruct((n, d), F32),
        compiler_params=_cparams("parallel"),
        name="moe_combine",
    )(x1, g1, g2, route, g_final)


def _moe(h2_rows, route, counts, idx1, idx2, x1, w_gate, w_up, w_down, layer, g_final, final_norm):
    n = x1.shape[0]
    n_tiles = (MOE_TOPK * n) // EXPERT_TILE + MOE_EXPERTS
    tile_expert, tile_block, tile_rows, tile_first = _dispatch_plan(counts, n, n_tiles)
    xs = _sc_scatter2(h2_rows, idx1, idx2, ROW_SPLIT * (MOE_EXPERTS * n + EXPERT_TILE))
    ys = _experts(xs, tile_expert, tile_block, tile_rows, tile_first, w_gate, w_up, w_down, layer)
    g1, g2 = _sc_gather2(ys, idx1, idx2)
    return _combine(x1, g1, g2, route, g_final, final_norm)


def kernel(x, positions, norm1_g, w_in, w_out, s5_a_re, s5_a_im, s5_log_dt, s5_b_re, s5_b_im, s5_c_re, s5_c_im, s5_d, s5_w_glu, s5_b_glu, m2_conv_w, m2_conv_b, m2_dt_bias, m2_a_log, m2_d, m2_norm_g, hg_lb_logits, hg_norm_g, norm2_g, moe_w_group, moe_b_group, moe_w_expert, moe_b_expert, moe_w_gate, moe_w_up, moe_w_down, final_norm_g):
    bsz, seqlen, d = x.shape
    assert bsz == 1 and seqlen % SEQ_TILE == 0 and seqlen % EXPERT_TILE == 0
    depth = w_in.shape[0]

    lb_probs = jax.nn.softmax(hg_lb_logits.astype(F32), axis=0)
    lower_bounds = (jnp.cumsum(lb_probs, axis=0) - lb_probs[0]).reshape(depth, 1, GROUP_WIDTH)
    cos_t, sin_t = _rope_tables(positions.reshape(seqlen, 1))

    s5_p = _s5_params(s5_a_re, s5_a_im, s5_log_dt, s5_b_re, s5_b_im, s5_c_re, s5_c_im, s5_d)
    m2_conv_b3 = m2_conv_b.reshape(depth, 1, M2_CONV_DIM)
    m2_dtb, m2_alog, m2_dl = _head_lanes(m2_dt_bias), _head_lanes(m2_a_log), _head_lanes(m2_d)
    m2_g = m2_norm_g.reshape(depth, 1, GROUP_WIDTH)
    w_route = jnp.zeros((depth, d, ROUTE_LANES), F32)
    w_route = w_route.at[:, :, :MOE_GROUPS].set(moe_w_group)
    w_route = w_route.at[:, :, EXPERT_ROW:EXPERT_ROW + MOE_EXPERTS].set(moe_w_expert)
    wr_packed = jnp.concatenate(_split2(w_route), axis=2)
    b_route = jnp.zeros((depth, 1, ROUTE_LANES), F32)
    b_route = b_route.at[:, 0, :MOE_GROUPS].set(moe_b_group)
    b_route = b_route.at[:, 0, EXPERT_ROW:EXPERT_ROW + MOE_EXPERTS].set(moe_b_expert)
    w_glu_b, w_out_b = s5_w_glu.astype(BF16), w_out.astype(BF16)
    b_glu = s5_b_glu.reshape(depth, 1, GROUP_WIDTH)
    g1, g2 = norm1_g.reshape(depth, 1, d), norm2_g.reshape(depth, 1, d)
    hg_g = hg_norm_g.reshape(depth, 1, GROUP_WIDTH)
    g_final = final_norm_g.reshape(1, d)
    tile = min(SEQ_TILE, seqlen)
    chunk = min(CHUNK, tile)

    xc = x.reshape(seqlen, d)
    for l in range(depth):
        u_b, u_f, p_ret, p_m2, p_hg = _in_proj(xc, g1, _w_prep(w_in, l), cos_t, sin_t, m2_conv_w, m2_conv_b3,
                                               m2_dtb, lower_bounds, l)
        y_s5 = _s5(u_b, u_f, s5_p, l)
        y_ret, y_m2, y_hg = _mixers(
            seqlen,
            _retention_call(p_ret, tile, min(RET_CHUNK, tile)),
            _mamba2_call(p_m2, m2_alog, m2_dl, m2_g, l, tile, chunk),
            _hgrn2_call(p_hg, hg_g, l, tile, chunk))
        x1, h2_rows, route, counts, idx1, idx2 = _out_proj(xc, y_s5, y_ret, y_m2, y_hg, w_glu_b, b_glu, w_out_b,
                                                           g2, wr_packed, b_route, l)
        xc = _moe(h2_rows, route, counts, idx1, idx2, x1, moe_w_gate, moe_w_up, moe_w_down, l, g_final,
                  final_norm=(l == depth - 1))
    return xc.reshape(bsz, seqlen, d)
```

```python
import functools
import math

import numpy as np
import jax
import jax.numpy as jnp
from jax import lax
from jax.experimental import pallas as pl
from jax.experimental.pallas import tpu as pltpu
from jax.experimental.pallas import tpu_sc as plsc

F32 = jnp.float32
BF16 = jnp.bfloat16
NORM_EPS = 1e-6

GROUP_WIDTH = 256
HEAD_DIM = 64
N_HEADS = GROUP_WIDTH // HEAD_DIM
S5_GROUPS = 16
S5_CH = 16
S5_STATE = 64
S5_DT_CLAMP = -1e-4
M2_STATE = 128
M2_CONV = 4
M2_CONV_DIM = 768
ROPE_BASE = 10000.0
MOE_GROUPS = 4
MOE_PER_GROUP = 4
MOE_EXPERTS = 16
ROUTE_LANES = 128

SEQ_TILE = 512
CHUNK = 128
RET_CHUNK = 256
CHUNK_UNROLL = 4
S5_CHUNK = 16
MOE_TOPK = 2
EXPERT_TILE = 512
SC_WINDOW = 128
VMEM_LIMIT = 56 * 1024 * 1024


def _cparams(*sem):
    return pltpu.CompilerParams(dimension_semantics=sem, vmem_limit_bytes=VMEM_LIMIT)


def _dot(a, b):
    return jnp.dot(a, b, preferred_element_type=F32)


def _dot_nt(a, b):
    return lax.dot_general(a, b, (((1,), (1,)), ((), ())), preferred_element_type=F32)


def _dot_tn(a, b):
    return lax.dot_general(a, b, (((0,), (0,)), ((), ())), preferred_element_type=F32)


def _split2(x):
    hi = x.astype(BF16)
    return hi, (x - hi.astype(F32)).astype(BF16)


def _split3(x):
    hi = x.astype(BF16)
    r = x - hi.astype(F32)
    mid = r.astype(BF16)
    return hi, mid, (r - mid.astype(F32)).astype(BF16)


def _dot_exact_lhs(m, x):
    hi, mid, lo = _split3(x)
    return _dot(m, hi) + _dot(m, mid) + _dot(m, lo)


def _dot_exact_rhs(x, m):
    hi, lo = _split2(x)
    return _dot(hi, m) + _dot(lo, m)


def _sigmoid(x):
    return 1.0 / (1.0 + jnp.exp(-x))


def _silu(x):
    return x * _sigmoid(x)


def _rms(x, g):
    return x * lax.rsqrt(jnp.mean(x * x, axis=-1, keepdims=True) + NORM_EPS) * g


def _full(shape):
    return pl.BlockSpec(shape, lambda *_: (0,) * len(shape))


def _rows(tile, width):
    return pl.BlockSpec((tile, width), lambda i: (i, 0))


def _pick(shape, layer):
    return pl.BlockSpec((None,) + tuple(shape), lambda *_: (layer,) + (0,) * len(shape))


IN_SEGMENTS = (256, 1024, 1280, 1024)


DT_COL = 9 * GROUP_WIDTH
W_PREP_ROWS = 128


def _w_prep_kernel(w_ref, o_ref):
    o_ref[:, 0:DT_COL] = w_ref[0, :, 0:DT_COL].astype(BF16)
    head = lax.broadcasted_iota(jnp.int32, (W_PREP_ROWS, GROUP_WIDTH), 1) // HEAD_DIM
    rep = jnp.zeros((W_PREP_ROWS, GROUP_WIDTH), F32)
    for h in range(N_HEADS):
        rep = jnp.where(head == h, w_ref[0, :, DT_COL + h:DT_COL + h + 1], rep)
    o_ref[:, DT_COL:DT_COL + GROUP_WIDTH] = rep.astype(BF16)
    o_ref[:, DT_COL + GROUP_WIDTH:] = w_ref[0, :, DT_COL + N_HEADS:].astype(BF16)


def _w_prep(w_in, layer):
    _, d, n_in = w_in.shape
    return pl.pallas_call(
        _w_prep_kernel,
        grid=(d // W_PREP_ROWS,),
        in_specs=[pl.BlockSpec((1, W_PREP_ROWS, n_in), lambda i: (layer, i, 0))],
        out_specs=pl.BlockSpec((W_PREP_ROWS, sum(IN_SEGMENTS)), lambda i: (i, 0)),
        out_shape=jax.ShapeDtypeStruct((d, sum(IN_SEGMENTS)), BF16),
        compiler_params=_cparams("parallel"),
        name="w_in_prep",
    )(w_in)


def _in_proj_kernel(*refs, tile, moe_pending):
    if moe_pending:
        x_ref, g1_ref, g2_ref, route_ref, g_ref, w_ref, xo_ref, ub_ref, uf_ref, ret_ref, m2_ref, hg_ref, u_tmp = refs
        x = _moe_combine(x_ref[...], g1_ref, g2_ref, route_ref[...], tile)
        xo_ref[...] = x
    else:
        x_ref, g_ref, w_ref, ub_ref, uf_ref, ret_ref, m2_ref, hg_ref, u_tmp = refs
        x = x_ref[...]
    hb = _rms(x, g_ref[...]).astype(BF16)
    c0, c1, c2, c3 = np.cumsum(IN_SEGMENTS)
    ret_ref[...] = _dot(hb, w_ref[:, c0:c1])
    m2_ref[...] = _dot(hb, w_ref[:, c1:c2])
    hg_ref[...] = _dot(hb, w_ref[:, c2:c3])
    u = _dot(hb, w_ref[:, 0:c0])
    for j in range(GROUP_WIDTH // 128):
        u_tmp[j] = u[:, 128 * j:128 * (j + 1)]
    for s in range(S5_CHUNK):
        for j in range(GROUP_WIDTH // 128):
            v = u_tmp[j, pl.ds(s, tile // S5_CHUNK, stride=S5_CHUNK), :]
            lanes = slice(GROUP_WIDTH * s + 128 * j, GROUP_WIDTH * s + 128 * (j + 1))
            uf_ref[:, lanes] = v
            ub_ref[:, lanes] = v.astype(BF16)


def _in_proj(x, pending, g, w, layer):
    n, d = x.shape
    tile = min(SEQ_TILE, n)
    fold = S5_CHUNK * GROUP_WIDTH
    moe_specs = [_rows(ROW_SPLIT * tile, 128), _rows(ROW_SPLIT * tile, 128), _rows(tile, ROUTE_LANES)]
    return pl.pallas_call(
        functools.partial(_in_proj_kernel, tile=tile, moe_pending=pending is not None),
        grid=(n // tile,),
        in_specs=[_rows(tile, d)] + (moe_specs if pending else []) + [_pick((1, d), layer), _full(w.shape)],
        out_specs=([_rows(tile, d)] if pending else []) + [_rows(tile // S5_CHUNK, fold)] * 2
        + [_rows(tile, s) for s in IN_SEGMENTS[1:]],
        out_shape=([jax.ShapeDtypeStruct((n, d), F32)] if pending else [])
        + [jax.ShapeDtypeStruct((n // S5_CHUNK, fold), BF16), jax.ShapeDtypeStruct((n // S5_CHUNK, fold), F32)]
        + [jax.ShapeDtypeStruct((n, s), F32) for s in IN_SEGMENTS[1:]],
        scratch_shapes=[pltpu.VMEM((GROUP_WIDTH // 128, tile, 128), F32)],
        compiler_params=_cparams("parallel"),
        name="in_proj",
    )(x, *(pending or ()), g, w)


def _rope_kernel(pos_ref, invf_ref, cos_ref, sin_ref):
    ang = pos_ref[...].astype(F32) * invf_ref[...]
    cos_ref[...] = jnp.cos(ang)
    sin_ref[...] = jnp.sin(ang)


def _rope_tables(positions):
    n = positions.shape[0]
    tile = min(SEQ_TILE, n)
    half = HEAD_DIM // 2
    inv_freq = ROPE_BASE ** (-jnp.arange(half, dtype=F32) / half)
    invf = jnp.tile(inv_freq, 128 // half).reshape(1, 128)
    return pl.pallas_call(
        _rope_kernel,
        grid=(n // tile,),
        in_specs=[_rows(tile, 1), _full((1, 128))],
        out_specs=[_rows(tile, 128), _rows(tile, 128)],
        out_shape=[jax.ShapeDtypeStruct((n, 128), F32)] * 2,
        compiler_params=_cparams("parallel"),
        name="rope_tables",
    )(positions, invf)


def _head_mean_matrix():
    h = np.arange(GROUP_WIDTH) // HEAD_DIM
    return jnp.asarray((h[:, None] == h[None, :]) / HEAD_DIM, dtype=BF16)


def _head_block_mask():
    h = np.arange(GROUP_WIDTH) // HEAD_DIM
    return jnp.asarray(h[:, None] == h[None, :], dtype=F32)


HEAD_PAIRS = GROUP_WIDTH // 128


def _lanes(x, j):
    return x[:, 128 * j:128 * (j + 1)]


def _stack_pair(x):
    xb = x.astype(BF16)
    low = lax.broadcasted_iota(jnp.int32, x.shape, 1) < HEAD_DIM
    zero = jnp.zeros_like(xb)
    return jnp.concatenate([jnp.where(low, xb, zero), jnp.where(low, zero, xb)], axis=0)


def _pair_scores(q, kb):
    s = _dot_nt(_stack_pair(q), kb)
    return s[:q.shape[0]], s[q.shape[0]:]


def _pair_apply(a0, a1, v):
    return _dot(jnp.concatenate([a0.astype(BF16), a1.astype(BF16)], axis=1), _stack_pair(v))


def _ret_constants(chunk):
    lg = np.log1p(-np.exp2(-5.0 - np.arange(N_HEADS, dtype=np.float64)))
    idx = np.arange(chunk, dtype=np.float64)
    rel = idx[:, None] - idx[None, :]
    decay = np.where(rel >= 0, np.exp(np.maximum(rel, 0.0)[None] * lg[:, None, None]), 0.0)
    lane_lg = np.repeat(lg, HEAD_DIM)
    xi = np.exp((idx + 1.0)[:, None] * lane_lg[None, :])
    zeta = np.exp((chunk - 1.0 - idx)[:, None] * lane_lg[None, :])
    h = np.arange(GROUP_WIDTH) // HEAD_DIM
    gc = np.where(h[:, None] == h[None, :], np.exp(chunk * lane_lg)[:, None], 0.0)
    f = lambda a: jnp.asarray(a, dtype=F32)
    return f(decay), f(xi), f(zeta), f(gc)


def _ret_kernel(p_ref, cos_ref, sin_ref, dec_ref, xi_ref, zeta_ref, gc_ref, bm_ref, gm_ref,
                o_ref, r_ref, *, chunk, n_chunks):
    lane = lax.broadcasted_iota(jnp.int32, (chunk, GROUP_WIDTH), 1)
    first_half = (lane % HEAD_DIM) < (HEAD_DIM // 2)
    gm = gm_ref[...]

    def rope(t, cos2, sin2):
        rot = jnp.where(first_half, -pltpu.roll(t, GROUP_WIDTH - HEAD_DIM // 2, 1),
                        pltpu.roll(t, HEAD_DIM // 2, 1))
        return t * cos2 + rot * sin2

    def body(c, carry):
        rows = pl.ds(pl.multiple_of(c * chunk, chunk), chunk)
        cs = cos_ref[rows, :]
        sn = sin_ref[rows, :]
        cos2 = jnp.concatenate([cs, cs], axis=1)
        sin2 = jnp.concatenate([sn, sn], axis=1)
        q = rope(p_ref[rows, 0:256], cos2, sin2)
        k = rope(p_ref[rows, 256:512], cos2, sin2) * (HEAD_DIM ** -0.5)
        v = p_ref[rows, 512:768]
        g = p_ref[rows, 768:1024]
        kb = k.astype(BF16)
        inner = []
        for j in range(HEAD_PAIRS):
            s0, s1 = _pair_scores(_lanes(q, j), _lanes(kb, j))
            inner.append(_pair_apply(s0 * dec_ref[2 * j], s1 * dec_ref[2 * j + 1], _lanes(v, j)))
        inner = jnp.concatenate(inner, axis=1)
        r_prev = r_ref[...]
        cross = _dot((q * xi_ref[...]).astype(BF16), r_prev.astype(BF16))
        r_ref[...] = gc_ref[...] * r_prev + bm_ref[...] * _dot_tn(kb, (zeta_ref[...] * v).astype(BF16))
        o = inner + cross
        cen = o - _dot_exact_rhs(o, gm)
        var = _dot_exact_rhs(cen * cen, gm)
        o_ref[rows, :] = (cen * lax.rsqrt(var + NORM_EPS) * _silu(g)).astype(BF16)
        return carry

    lax.fori_loop(0, n_chunks, body, 0, unroll=CHUNK_UNROLL)


def _retention_call(proj, cos_t, sin_t, tile, chunk):
    decay, xi, zeta, gc = _ret_constants(chunk)
    body = functools.partial(_ret_kernel, chunk=chunk, n_chunks=tile // chunk)
    operands = (proj, cos_t, sin_t, decay, xi, zeta, gc, _head_block_mask(), _head_mean_matrix())
    specs = [_rows(tile, 1024), _rows(tile, 128), _rows(tile, 128), _full(decay.shape), _full(xi.shape),
             _full(zeta.shape), _full(gc.shape), _full((256, 256)), _full((256, 256))]
    return body, operands, specs, [pltpu.VMEM((GROUP_WIDTH, GROUP_WIDTH), F32)]


def _tri_matrix(chunk):
    i = np.arange(chunk)
    return jnp.asarray(i[:, None] >= i[None, :], dtype=BF16)


def _m2_kernel(p_ref, cw_ref, cb_ref, dtb_ref, alog_ref, d_ref, ng_ref, tri_ref,
               o_ref, tail_ref, ext_ref, act_ref, st_ref, *, tile, chunk, n_chunks):
    ext_ref[0:8, :] = tail_ref[...]
    ext_ref[8:tile + 8, :] = p_ref[:, 256:1024]
    tail_ref[...] = p_ref[tile - 8:tile, 256:1024]
    conv = cb_ref[...]
    for j in range(M2_CONV):
        lo = 8 - (M2_CONV - 1) + j
        conv = conv + cw_ref[j:j + 1, :] * ext_ref[lo:lo + tile, :]
    act_ref[...] = _silu(conv)
    a_lane = -jnp.exp(alog_ref[...])
    tri = tri_ref[...]
    ti = lax.broadcasted_iota(jnp.int32, (chunk, chunk), 0)
    si = lax.broadcasted_iota(jnp.int32, (chunk, chunk), 1)
    causal = ti >= si
    lane = lax.broadcasted_iota(jnp.int32, (chunk, 128), 1)

    def body(c, carry):
        start = pl.multiple_of(c * chunk, chunk)
        rows = pl.ds(start, chunk)
        xbc = act_ref[rows, :]
        xs = xbc[:, 0:256]
        z = p_ref[rows, 0:256]
        x_dt = p_ref[rows, 1024:1280] + dtb_ref[...]
        dt = jnp.maximum(x_dt, 0.0) + jnp.log1p(jnp.exp(-jnp.abs(x_dt)))
        acum = _dot_exact_lhs(tri, dt * a_lane)
        acum_t = acum.T
        a_last = acum[chunk - 1:chunk, :]
        e_acum = jnp.exp(acum)
        decs = jnp.exp(a_last - acum)
        d_chunk = jnp.exp(a_last)
        xc = xs * dt
        ys = []
        for g in range(2):
            sl = slice(128 * g, 128 * (g + 1))
            bmg = xbc[:, 256 + 128 * g:256 + 128 * (g + 1)].astype(BF16)
            cmg = xbc[:, 512 + 128 * g:512 + 128 * (g + 1)].astype(BF16)
            cb = _dot_nt(cmg, bmg)
            xcg = xc[:, sl]
            yd = jnp.zeros((chunk, 128), F32)
            for hh in range(2):
                col0 = 128 * g + HEAD_DIM * hh
                diff = acum[:, col0:col0 + 1] - acum_t[col0:col0 + 1, :]
                lm = jnp.where(causal, jnp.exp(jnp.where(causal, diff, 0.0)), 0.0)
                xm = jnp.where((lane // HEAD_DIM) == hh, xcg, 0.0).astype(BF16)
                yd = yd + _dot((cb * lm).astype(BF16), xm)
            st = st_ref[:, sl]
            y_off = _dot(cmg, st.astype(BF16)) * e_acum[:, sl]
            st_ref[:, sl] = d_chunk[:, sl] * st + _dot_tn(bmg, (xcg * decs[:, sl]).astype(BF16))
            ys.append(yd + y_off + d_ref[:, sl] * xs[:, sl])
        y = jnp.concatenate(ys, axis=1) * _silu(z)
        o_ref[rows, :] = _rms(y, ng_ref[...]).astype(BF16)
        return carry

    lax.fori_loop(0, n_chunks, body, 0, unroll=CHUNK_UNROLL)


def _mamba2_params(conv_w, conv_b, dt_bias, a_log, d_skip, norm_g):
    depth = conv_w.shape[0]
    lanes = lambda v: jnp.repeat(v, HEAD_DIM, axis=1).reshape(depth, 1, GROUP_WIDTH)
    return (conv_w, conv_b.reshape(depth, 1, -1), lanes(dt_bias), lanes(a_log), lanes(d_skip),
            norm_g.reshape(depth, 1, -1))


def _mamba2_call(proj, params, layer, tile, chunk):
    body = functools.partial(_m2_kernel, tile=tile, chunk=chunk, n_chunks=tile // chunk)
    operands = (proj,) + tuple(params) + (_tri_matrix(chunk),)
    specs = ([_rows(tile, 1280), _pick((M2_CONV, M2_CONV_DIM), layer), _pick((1, M2_CONV_DIM), layer)]
             + [_pick((1, GROUP_WIDTH), layer)] * 4 + [_full((chunk, chunk))])
    scratch = [pltpu.VMEM((8, M2_CONV_DIM), F32), pltpu.VMEM((tile + 8, M2_CONV_DIM), F32),
               pltpu.VMEM((tile, M2_CONV_DIM), F32), pltpu.VMEM((M2_STATE, GROUP_WIDTH), F32)]
    return body, operands, specs, scratch


HG_MATMUL_LEVELS = 3


def _hg_exponent_matrix(chunk):
    levels = HG_MATMUL_LEVELS
    t = np.arange(chunk)[:, None]
    r = np.arange(chunk)[None, :]
    blocks = []
    for lvl in range(levels):
        b = 1 << lvl
        blk = t // b
        odd = (blk % 2) == 1
        q_rows = odd & (r >= blk * b) & (r <= t)
        k_rows = (~odd) & (r > t) & (r <= (blk + 1) * b - 1)
        blocks.append(q_rows | k_rows)
    blocks.append(r <= t)
    return jnp.asarray(np.concatenate(blocks, axis=0), dtype=BF16)


def _hg_kernel(p_ref, lb_ref, ng_ref, gexp_ref, bm_ref, gm_ref, o_ref, st_ref, *, chunk, n_chunks):
    levels = int(math.log2(chunk))
    row = lax.broadcasted_iota(jnp.int32, (chunk, 128), 0)
    odd_rows = [((row >> lvl) & 1) == 1 for lvl in range(levels)]
    row_wide = lax.broadcasted_iota(jnp.int32, (chunk, GROUP_WIDTH), 0)
    odd_rows_wide = [((row_wide >> lvl) & 1) == 1 for lvl in range(levels)]
    ti = lax.broadcasted_iota(jnp.int32, (chunk, chunk), 0)
    si = lax.broadcasted_iota(jnp.int32, (chunk, chunk), 1)
    pair_level = [((ti >> (lvl + 1)) == (si >> (lvl + 1))) & (((ti >> lvl) & 1) == 1) & (((si >> lvl) & 1) == 0)
                  for lvl in range(levels)]
    lb = lb_ref[...]
    gm = gm_ref[...]

    def body(c, carry):
        rows = pl.ds(pl.multiple_of(c * chunk, chunk), chunk)
        q = _silu(p_ref[rows, 0:256])
        forget = lb + (1.0 - lb) * _sigmoid(p_ref[rows, 256:512])
        k = 1.0 - forget
        v = p_ref[rows, 512:768]
        g = p_ref[rows, 768:1024]
        lf_hi, lf_lo = _split2(jnp.log(forget))
        expo = _dot(gexp_ref[...], lf_hi) + _dot(gexp_ref[...], lf_lo)
        bcum = expo[HG_MATMUL_LEVELS * chunk:(HG_MATMUL_LEVELS + 1) * chunk, :]

        def level_log_decay(lvl):
            if lvl < HG_MATMUL_LEVELS:
                return expo[lvl * chunk:(lvl + 1) * chunk, :]
            b = 1 << lvl
            ref = jnp.concatenate([jnp.broadcast_to(bcum[m + b - 1:m + b, :], (2 * b, GROUP_WIDTH))
                                   for m in range(0, chunk, 2 * b)], axis=0)
            return jnp.where(odd_rows_wide[lvl], bcum - ref, ref - bcum)


        log_decay = [level_log_decay(lvl) for lvl in range(levels)]
        intra = []
        for j in range(HEAD_PAIRS):
            qj, kj = _lanes(q, j), _lanes(k, j)
            a0, a1 = (jnp.where(ti == si, s, 0.0) for s in _pair_scores(qj, kj.astype(BF16)))
            for lvl in range(levels):
                w = jnp.exp(_lanes(log_decay[lvl], j)) * jnp.where(odd_rows[lvl], qj, kj)
                s0, s1 = _pair_scores(w, w.astype(BF16))
                a0 = jnp.where(pair_level[lvl], s0, a0)
                a1 = jnp.where(pair_level[lvl], s1, a1)
            intra.append(_pair_apply(a0, a1, _lanes(v, j)))
        intra = jnp.concatenate(intra, axis=1)

        b_last = bcum[chunk - 1:chunk, :]
        suffix = b_last - bcum
        st = st_ref[...]
        cross = _dot_nt((q * jnp.exp(bcum)).astype(BF16), st.astype(BF16))
        st_ref[...] = jnp.exp(b_last) * st + bm_ref[...] * _dot_tn(
            v.astype(BF16), (k * jnp.exp(suffix)).astype(BF16))
        o = intra + cross
        o = o * lax.rsqrt(_dot_exact_rhs(o * o, gm) + NORM_EPS) * ng_ref[...]
        o_ref[rows, :] = (o * _silu(g)).astype(BF16)
        return carry

    lax.fori_loop(0, n_chunks, body, 0, unroll=CHUNK_UNROLL)


def _hgrn2_call(proj, lower_bounds, norm_g, layer, tile, chunk):
    gexp = _hg_exponent_matrix(chunk)
    body = functools.partial(_hg_kernel, chunk=chunk, n_chunks=tile // chunk)
    operands = (proj, lower_bounds, norm_g, gexp, _head_block_mask(), _head_mean_matrix())
    specs = [_rows(tile, 1024), _pick((1, GROUP_WIDTH), layer), _pick((1, GROUP_WIDTH), layer), _full(gexp.shape),
             _full((256, 256)), _full((256, 256))]
    return body, operands, specs, [pltpu.VMEM((GROUP_WIDTH, GROUP_WIDTH), F32)]


def _mixers_kernel(*refs, bodies, n_in, n_scratch):
    n_mix = len(bodies)
    ins, pos = [], 0
    for k in n_in:
        ins.append(refs[pos:pos + k])
        pos += k
    outs = refs[pos:pos + n_mix]
    pos += n_mix
    scratch = []
    for k in n_scratch:
        scratch.append(refs[pos:pos + k])
        pos += k

    @pl.when(pl.program_id(0) == 0)
    def _():
        for group in scratch:
            for ref in group:
                ref[...] = jnp.zeros_like(ref)

    for body, i, o, s in zip(bodies, ins, outs, scratch):
        body(*i, o, *s)


def _mixers(n, *calls):
    tile = min(SEQ_TILE, n)
    bodies = tuple(c[0] for c in calls)
    return pl.pallas_call(
        functools.partial(_mixers_kernel, bodies=bodies, n_in=tuple(len(c[1]) for c in calls),
                          n_scratch=tuple(len(c[3]) for c in calls)),
        grid=(n // tile,),
        in_specs=[s for c in calls for s in c[2]],
        out_specs=[_rows(tile, GROUP_WIDTH)] * len(calls),
        out_shape=[jax.ShapeDtypeStruct((n, GROUP_WIDTH), BF16)] * len(calls),
        scratch_shapes=[s for c in calls for s in c[3]],
        compiler_params=_cparams("arbitrary"),
        name="mixers",
    )(*[a for c in calls for a in c[1]])


S5_LANES = S5_GROUPS * 2 * S5_STATE
S5_TAP_SPLITS = 4


def _s5_rows(a_re, a_im, log_dt):
    are = jnp.minimum(a_re, S5_DT_CLAMP)
    dt = jnp.exp(log_dt)
    lam_re = are * dt
    lam_im = a_im * dt
    mag = jnp.exp(lam_re)
    ab_re = mag * jnp.cos(lam_im)
    ab_im = mag * jnp.sin(lam_im)
    den = are * are + a_im * a_im
    k_re = ((ab_re - 1.0) * are + ab_im * a_im) / den
    k_im = (ab_im * are - (ab_re - 1.0) * a_im) / den
    return lam_re, lam_im, k_re, k_im


def _s5_power(lam_re, lam_im, e):
    m = jnp.exp(e * lam_re)
    return m * jnp.cos(e * lam_im), m * jnp.sin(e * lam_im)


def _s5_state_kernel(u_ref, are_ref, aim_ref, ldt_ref, b1_ref, b2_ref, ca_ref, k_ref, sp_ref,
                     bb1_ref, bb2_ref, inc_ref, *, n_chunks):
    s = pl.program_id(0)
    lam_re, lam_im, k_re, k_im = _s5_rows(are_ref[...], aim_ref[...], ldt_ref[...])

    @pl.when(s == 0)
    def _():
        bb1_ref[...] = k_re * b1_ref[...] + k_im * b2_ref[...]
        bb2_ref[...] = k_re * b2_ref[...] - k_im * b1_ref[...]

    p_re, p_im = _s5_power(lam_re, lam_im, (S5_CHUNK - 1 - s).astype(F32))
    w = (p_re * bb1_ref[...] + p_im * bb2_ref[...]).astype(BF16)
    k_ref[0] = _dot_nt(w, ca_ref[...].astype(BF16)).astype(BF16)
    contrib = _dot(u_ref[...], w)

    @pl.when(s == 0)
    def _():
        inc_ref[...] = contrib

    @pl.when(s > 0)
    def _():
        inc_ref[...] += contrib

    @pl.when(s == S5_CHUNK - 1)
    def _():
        n_steps = max(1, int(math.ceil(math.log2(n_chunks))))
        step = lax.broadcasted_iota(jnp.int32, (16, S5_LANES), 0)
        e = (jnp.left_shift(1, step) * S5_CHUNK).astype(F32)
        a_re_all, a_im_all = _s5_power(lam_re, lam_im, e)
        row = lax.broadcasted_iota(jnp.int32, (n_chunks, 128), 0)
        half = S5_LANES // 2

        def shifted(x, sh):
            return jnp.where(row >= sh, pltpu.roll(x, sh, 0), 0.0)

        for j in range(half // 128):
            re_l, im_l = slice(128 * j, 128 * (j + 1)), slice(half + 128 * j, half + 128 * (j + 1))
            x_re, x_im = inc_ref[:, re_l], inc_ref[:, im_l]
            for k in range(n_steps):
                sh = 1 << k
                if sh >= n_chunks:
                    break
                a_re, a_im = a_re_all[k:k + 1, re_l], a_im_all[k:k + 1, re_l]
                p_re, p_im = shifted(x_re, sh), shifted(x_im, sh)
                x_re, x_im = x_re + a_re * p_re - a_im * p_im, x_im + a_re * p_im + a_im * p_re
            sp_ref[:, re_l] = shifted(x_re, 1).astype(BF16)
            sp_ref[:, im_l] = shifted(x_im, 1).astype(BF16)


def _s5_out_kernel(ub_ref, uf_ref, k_ref, sp_ref, are_ref, aim_ref, ldt_ref, ca_ref, cb_ref, d_ref,
                   y_ref, taps_ref):
    t = pl.program_id(0)
    fold = S5_CHUNK * GROUP_WIDTH

    @pl.when(t == 0)
    def _():
        for j in range(S5_CHUNK):
            taps_ref[GROUP_WIDTH * j:GROUP_WIDTH * (j + 1), :] = k_ref[j]
        taps_ref[fold:, :] = jnp.zeros((fold - GROUP_WIDTH, GROUP_WIDTH), BF16)

    lam_re, lam_im, _, _ = _s5_rows(are_ref[...], aim_ref[...], ldt_ref[...])
    p_re, p_im = _s5_power(lam_re, lam_im, (t + 1).astype(F32))
    w_out = (p_re * ca_ref[...] + p_im * cb_ref[...]).astype(BF16)
    start = pl.multiple_of((S5_CHUNK - 1 - t) * GROUP_WIDTH, GROUP_WIDTH)
    y_ref[...] = _dot_nt(sp_ref[...], w_out) + d_ref[...] * uf_ref[...]
    quarter = S5_CHUNK // S5_TAP_SPLITS
    for part in range(S5_TAP_SPLITS):
        width = (part + 1) * quarter * GROUP_WIDTH

        @pl.when((t >= part * quarter) & (t < (part + 1) * quarter))
        def _():
            y_ref[...] += _dot(ub_ref[:, 0:width], taps_ref[pl.ds(start, width), :])


def _s5_embed(re, im):
    eye = jnp.eye(S5_GROUPS, dtype=F32)
    blocks = [(eye[None, :, None, :, None] * x[:, :, :, None, :]).reshape(-1, S5_GROUPS * S5_CH, S5_LANES // 2)
              for x in (re, im)]
    return jnp.concatenate(blocks, axis=2)


def _s5_params(a_re, a_im, log_dt, b_re, b_im, c_re, c_im, d_skip):
    depth = a_re.shape[0]
    row = lambda v: jnp.tile(v.reshape(depth, 1, S5_LANES // 2), (1, 1, 2))
    bt_re, bt_im = b_re.transpose(0, 1, 3, 2), b_im.transpose(0, 1, 3, 2)
    return dict(are=row(a_re), aim=row(a_im), ldt=row(jnp.repeat(log_dt, S5_STATE, axis=1)),
                b1=_s5_embed(bt_re, bt_im), b2=_s5_embed(-bt_im, bt_re),
                ca=_s5_embed(c_re, -c_im), cb=_s5_embed(-c_im, -c_re),
                d=d_skip.reshape(depth, 1, GROUP_WIDTH))


def _s5(u_b, u_f, p, layer):
    n_chunks, fold = u_b.shape
    row_spec, mat_spec = _pick((1, S5_LANES), layer), _pick((GROUP_WIDTH, S5_LANES), layer)
    col = lambda: pl.BlockSpec((n_chunks, GROUP_WIDTH), lambda s: (0, s))
    taps, s_prev = pl.pallas_call(
        functools.partial(_s5_state_kernel, n_chunks=n_chunks),
        grid=(S5_CHUNK,),
        in_specs=[col()] + [row_spec] * 3 + [mat_spec] * 3,
        out_specs=[pl.BlockSpec((1, GROUP_WIDTH, GROUP_WIDTH), lambda s: (s, 0, 0)),
                   _full((n_chunks, S5_LANES))],
        out_shape=[jax.ShapeDtypeStruct((S5_CHUNK, GROUP_WIDTH, GROUP_WIDTH), BF16),
                   jax.ShapeDtypeStruct((n_chunks, S5_LANES), BF16)],
        scratch_shapes=[pltpu.VMEM((GROUP_WIDTH, S5_LANES), F32), pltpu.VMEM((GROUP_WIDTH, S5_LANES), F32),
                        pltpu.VMEM((n_chunks, S5_LANES), F32)],
        compiler_params=_cparams("arbitrary"),
        name="s5_state",
    )(u_b, p['are'], p['aim'], p['ldt'], p['b1'], p['b2'], p['ca'])
    return pl.pallas_call(
        _s5_out_kernel,
        grid=(S5_CHUNK,),
        in_specs=[_full((n_chunks, fold)), col(), _full(taps.shape), _full(s_prev.shape)]
        + [row_spec] * 3 + [mat_spec] * 2 + [_pick((1, GROUP_WIDTH), layer)],
        out_specs=col(),
        out_shape=jax.ShapeDtypeStruct((n_chunks, fold), F32),
        scratch_shapes=[pltpu.VMEM(((2 * S5_CHUNK - 1) * GROUP_WIDTH, GROUP_WIDTH), BF16)],
        compiler_params=_cparams("arbitrary"),
        name="s5_out",
    )(u_b, u_f, taps, s_prev, p['are'], p['aim'], p['ldt'], p['ca'], p['cb'], p['d'])


EXPERT_ROW = 8


def _route(logits_t):
    tokens = logits_t.shape[1]
    big = jnp.int32(1 << 20)
    neg = jnp.float32(-jnp.inf)
    g_row = lax.broadcasted_iota(jnp.int32, (8, tokens), 0)
    is_group = g_row < MOE_GROUPS
    gl = jnp.where(is_group, logits_t[0:8, :], neg)
    ge = jnp.where(is_group, jnp.exp(gl - jnp.max(gl, axis=0, keepdims=True)), 0.0)
    gp = ge / jnp.sum(ge, axis=0, keepdims=True)
    p_g = jnp.max(gp, axis=0, keepdims=True)
    g_idx = jnp.min(jnp.where(is_group & (gp == p_g), g_row, big), axis=0, keepdims=True)
    e_row = lax.broadcasted_iota(jnp.int32, (MOE_EXPERTS, tokens), 0)
    in_group = (e_row // MOE_PER_GROUP) == g_idx
    el = jnp.where(in_group, logits_t[EXPERT_ROW:EXPERT_ROW + MOE_EXPERTS, :], neg)
    ee = jnp.where(in_group, jnp.exp(el - jnp.max(el, axis=0, keepdims=True)), 0.0)
    ep = ee / jnp.sum(ee, axis=0, keepdims=True)
    p1 = jnp.max(jnp.where(in_group, ep, -1.0), axis=0, keepdims=True)
    i1 = jnp.min(jnp.where(in_group & (ep == p1), e_row, big), axis=0, keepdims=True)
    rest = in_group & (e_row != i1)
    p2 = jnp.max(jnp.where(rest, ep, -1.0), axis=0, keepdims=True)
    i2 = jnp.min(jnp.where(rest & (ep == p2), e_row, big), axis=0, keepdims=True)
    tot = p1 + p2
    return i1, i2, p_g * p1 / tot, p_g * p2 / tot


INDEX_DIGIT_BITS = 6
ROW_SPLIT = 4
ROUTE_E1, ROUTE_E2, ROUTE_R1, ROUTE_R2, ROUTE_W1, ROUTE_W2 = range(6)
HIGH_HALF = 0xFFFF0000


def _split_rows(ref, value, rows):
    half = value.shape[1] // 2
    lo = lax.bitcast_convert_type(value[:, :half].astype(jnp.bfloat16).astype(F32), jnp.uint32)
    hi = lax.bitcast_convert_type(value[:, half:].astype(jnp.bfloat16).astype(F32), jnp.uint32)
    words = lax.bitcast_convert_type((lo >> 16) | (hi & jnp.uint32(HIGH_HALF)), jnp.int32)
    for j in range(ROW_SPLIT):
        ref[pl.ds(j, rows, stride=ROW_SPLIT), :] = words[:, 128 * j:128 * (j + 1)]


def _merge_rows(ref, rows):
    words = jnp.concatenate([ref[pl.ds(j, rows, stride=ROW_SPLIT), :] for j in range(ROW_SPLIT)], axis=1)
    words = lax.bitcast_convert_type(words, jnp.uint32)
    lo = lax.bitcast_convert_type(words << 16, F32)
    hi = lax.bitcast_convert_type(words & jnp.uint32(HIGH_HALF), F32)
    return jnp.concatenate([lo, hi], axis=1)


def _moe_combine(x1, g1_ref, g2_ref, route, rows):
    return (x1 + route[:, ROUTE_W1:ROUTE_W1 + 1] * _merge_rows(g1_ref, rows)
            + route[:, ROUTE_W2:ROUTE_W2 + 1] * _merge_rows(g2_ref, rows))


def _out_kernel(x_ref, s5_ref, ret_ref, m2_ref, hg_ref, wglu_ref, bglu_ref, wo_ref, g2_ref,
                wrh_ref, br_ref, stri_ref, spread_ref, x1_ref, h2_ref, route_ref, cnt_ref, i1_ref, i2_ref,
                s5_tmp, carry_ref, *, tile, n_tokens):
    @pl.when(pl.program_id(0) == 0)
    def _():
        carry_ref[...] = jnp.zeros_like(carry_ref)

    for s in range(S5_CHUNK):
        for j in range(GROUP_WIDTH // 128):
            lanes = slice(GROUP_WIDTH * s + 128 * j, GROUP_WIDTH * s + 128 * (j + 1))
            s5_tmp[j, pl.ds(s, tile // S5_CHUNK, stride=S5_CHUNK), :] = s5_ref[:, lanes]
    y = jnp.concatenate([s5_tmp[j] for j in range(GROUP_WIDTH // 128)], axis=1)
    y = y * (0.5 * (1.0 + jnp.tanh(math.sqrt(2.0 / math.pi) * (y + 0.044715 * (y * y * y)))))
    y = y * _sigmoid(_dot(y.astype(BF16), wglu_ref[...]) + bglu_ref[...])
    acc = x_ref[...] + _dot(y.astype(BF16), wo_ref[0:256, :])
    acc = acc + _dot(ret_ref[...], wo_ref[256:512, :])
    acc = acc + _dot(m2_ref[...], wo_ref[512:768, :])
    acc = acc + _dot(hg_ref[...], wo_ref[768:1024, :])
    x1_ref[...] = acc
    h2 = _rms(acc, g2_ref[...])
    _split_rows(h2_ref, h2, tile)
    hi, lo = _split2(h2)
    hw = _dot(hi, wrh_ref[...])
    logits = (hw[:, :ROUTE_LANES] + hw[:, ROUTE_LANES:] + _dot(lo, wrh_ref[:, :ROUTE_LANES])) + br_ref[...]
    e1, e2, w1, w2 = _route(logits.T)
    e_row = lax.broadcasted_iota(jnp.int32, (MOE_EXPERTS, tile), 0)
    picked = jnp.where((e_row == e1) | (e_row == e2), 1.0, 0.0)
    rank = carry_ref[:, 0:1] + _dot_nt(picked.astype(BF16), stri_ref[...])
    r1 = jnp.sum(jnp.where(e_row == e1, rank, 0.0), axis=0, keepdims=True)
    r2 = jnp.sum(jnp.where(e_row == e2, rank, 0.0), axis=0, keepdims=True)
    carry_ref[...] += jnp.sum(picked, axis=1, keepdims=True)
    cnt_ref[...] = carry_ref[...]
    rec_row = lax.broadcasted_iota(jnp.int32, (ROUTE_LANES, tile), 0)
    rec = jnp.zeros((ROUTE_LANES, tile), F32)
    for col, val in ((ROUTE_E1, e1.astype(F32)), (ROUTE_E2, e2.astype(F32)), (ROUTE_R1, r1), (ROUTE_R2, r2),
                     (ROUTE_W1, w1), (ROUTE_W2, w2)):
        rec = jnp.where(rec_row == col, val, rec)
    route_ref[...] = rec.T
    digit_row = lax.broadcasted_iota(jnp.int32, (8, tile), 0)
    digits = jnp.zeros((8, tile), F32)
    for slot, (e, r) in enumerate(((e1, r1), (e2, r2))):
        pos = e.astype(F32) * float(n_tokens) + r
        for k, shift in enumerate((2 * INDEX_DIGIT_BITS, INDEX_DIGIT_BITS, 0)):
            digit = jnp.floor(pos * (1.0 / (1 << shift)))
            pos = pos - digit * float(1 << shift)
            digits = jnp.where(digit_row == 3 * slot + k, digit, digits)
    sub = lax.broadcasted_iota(jnp.int32, (1, 128 * ROW_SPLIT), 1) % ROW_SPLIT
    for q in range(tile // 128):
        o = _dot(digits[:, 128 * q:128 * (q + 1)].astype(BF16), spread_ref[...])
        for slot, out_ref in enumerate((i1_ref, i2_ref)):
            moved = (float(1 << (2 * INDEX_DIGIT_BITS)) * o[3 * slot:3 * slot + 1]
                     + float(1 << INDEX_DIGIT_BITS) * o[3 * slot + 1:3 * slot + 2] + o[3 * slot + 2:3 * slot + 3])
            idx = moved.astype(jnp.int32) * ROW_SPLIT + sub
            for j in range(ROW_SPLIT):
                row = ROW_SPLIT * q + j
                out_ref[row:row + 1, :] = idx[:, 128 * j:128 * (j + 1)]


def _out_proj(x, y_s5, y_ret, y_m2, y_hg, w_glu, b_glu, w_out, g2, wr_packed, b_route, layer):
    n, d = x.shape
    tile = min(SEQ_TILE, n)
    i = np.arange(tile)
    strict_lower = jnp.asarray(i[:, None] > i[None, :], dtype=BF16)
    assert MOE_EXPERTS * n <= (1 << (3 * INDEX_DIGIT_BITS))
    per_row = 128 // ROW_SPLIT
    out_rows = tile // per_row
    lanes = np.arange(128 * ROW_SPLIT)
    spread = jnp.asarray(np.arange(128)[:, None] == (lanes // ROW_SPLIT)[None, :], dtype=BF16)
    return pl.pallas_call(
        functools.partial(_out_kernel, tile=tile, n_tokens=n),
        grid=(n // tile,),
        in_specs=[_rows(tile, d), _rows(tile // S5_CHUNK, S5_CHUNK * GROUP_WIDTH)] + [_rows(tile, GROUP_WIDTH)] * 3
        + [_pick((256, 256), layer), _pick((1, 256), layer), _pick((d, d), layer), _pick((1, d), layer),
           _pick((d, 2 * ROUTE_LANES), layer), _pick((1, ROUTE_LANES), layer), _full((tile, tile)),
           _full(spread.shape)],
        out_specs=[_rows(tile, d), _rows(ROW_SPLIT * tile, 128), _rows(tile, ROUTE_LANES),
                   _full((MOE_EXPERTS, ROUTE_LANES)), _rows(out_rows, 128), _rows(out_rows, 128)],
        out_shape=[jax.ShapeDtypeStruct((n, d), F32), jax.ShapeDtypeStruct((ROW_SPLIT * n, 128), jnp.int32),
                   jax.ShapeDtypeStruct((n, ROUTE_LANES), F32), jax.ShapeDtypeStruct((MOE_EXPERTS, ROUTE_LANES), F32)]
        + [jax.ShapeDtypeStruct((n // per_row, 128), jnp.int32)] * 2,
        scratch_shapes=[pltpu.VMEM((GROUP_WIDTH // 128, tile, 128), F32),
                        pltpu.VMEM((MOE_EXPERTS, ROUTE_LANES), F32)],
        compiler_params=_cparams("arbitrary"),
        name="out_proj_router",
    )(x, y_s5, y_ret, y_m2, y_hg, w_glu, b_glu, w_out, g2, wr_packed, b_route, strict_lower, spread)


def _sc_mesh():
    return plsc.VectorSubcoreMesh(core_axis_name="core", subcore_axis_name="subcore")


def _sc_scatter2(src, idx_a, idx_b, n_out):
    n = src.shape[0]

    @functools.partial(pl.kernel, out_type=jax.ShapeDtypeStruct((n_out, 128), src.dtype), mesh=_sc_mesh(),
                       scratch_types=[])
    def scatter_kernel(x_hbm, ia_hbm, ib_hbm, o_hbm):
        def body(x_vmem, ia_vmem, ib_vmem):
            pltpu.sync_copy(x_vmem, o_hbm.at[ia_vmem.at[0]])
            pltpu.sync_copy(x_vmem, o_hbm.at[ib_vmem.at[0]])

        pltpu.emit_pipeline(
            body, grid=(n // SC_WINDOW,),
            in_specs=[pl.BlockSpec((SC_WINDOW, 128), index_map=lambda i: (i, 0)),
                      pl.BlockSpec((1, SC_WINDOW), index_map=lambda i: (i, 0)),
                      pl.BlockSpec((1, SC_WINDOW), index_map=lambda i: (i, 0))],
            out_specs=[],
            core_axis_name=("core", "subcore"), dimension_semantics=(pltpu.PARALLEL,),
        )(x_hbm, ia_hbm, ib_hbm)

    return scatter_kernel(src, idx_a, idx_b)


def _sc_gather2(table, idx_a, idx_b):
    n = idx_a.size
    sds = jax.ShapeDtypeStruct((n, 128), table.dtype)

    @functools.partial(pl.kernel, out_type=(sds, sds), mesh=_sc_mesh(), scratch_types=[])
    def gather_kernel(t_hbm, ia_hbm, ib_hbm, oa_hbm, ob_hbm):
        def body(ia_vmem, ib_vmem, oa_vmem, ob_vmem):
            pltpu.sync_copy(t_hbm.at[ia_vmem.at[0]], oa_vmem)
            pltpu.sync_copy(t_hbm.at[ib_vmem.at[0]], ob_vmem)

        pltpu.emit_pipeline(
            body, grid=(n // SC_WINDOW,),
            in_specs=[pl.BlockSpec((1, SC_WINDOW), index_map=lambda i: (i, 0)),
                      pl.BlockSpec((1, SC_WINDOW), index_map=lambda i: (i, 0))],
            out_specs=[pl.BlockSpec((SC_WINDOW, 128), index_map=lambda i: (i, 0)),
                       pl.BlockSpec((SC_WINDOW, 128), index_map=lambda i: (i, 0))],
            core_axis_name=("core", "subcore"), dimension_semantics=(pltpu.PARALLEL,),
        )(ia_hbm, ib_hbm, oa_hbm, ob_hbm)

    return gather_kernel(table, idx_a, idx_b)


def _dispatch_plan(counts, n, n_tiles):
    cnt = counts[:, 0].astype(jnp.int32)
    blocks = (cnt + EXPERT_TILE - 1) // EXPERT_TILE
    ends = jnp.cumsum(blocks)
    first_tile = ends - blocks
    tile_id = jnp.arange(n_tiles, dtype=jnp.int32)
    tile_expert = jnp.minimum(jnp.sum((tile_id[:, None] >= ends[None, :]).astype(jnp.int32), axis=1), MOE_EXPERTS - 1)
    onehot = (tile_expert[:, None] == jnp.arange(MOE_EXPERTS, dtype=jnp.int32)[None, :]).astype(jnp.int32)
    block_in_expert = tile_id - jnp.sum(onehot * first_tile[None, :], axis=1)
    used = tile_id < ends[-1]
    rows_left = jnp.sum(onehot * cnt[None, :], axis=1) - block_in_expert * EXPERT_TILE
    tile_rows = jnp.where(used, jnp.clip(rows_left, 0, EXPERT_TILE), 0).astype(jnp.int32)
    blocks_per_expert = n // EXPERT_TILE
    tile_block = jnp.where(used, tile_expert * blocks_per_expert + block_in_expert,
                           MOE_EXPERTS * blocks_per_expert).astype(jnp.int32)
    tile_first = (used & (block_in_expert == 0)).astype(jnp.int32)
    return tile_expert, tile_block, tile_rows, tile_first


def _experts_kernel(te_ref, blk_ref, rows_ref, first_ref, xs_ref, wg_ref, wu_ref, wd_ref, y_ref, wgb, wub, wdb):
    i = pl.program_id(0)

    @pl.when(first_ref[i] == 1)
    def _():
        wgb[...] = wg_ref[0, 0].astype(BF16)
        wub[...] = wu_ref[0, 0].astype(BF16)
        wdb[...] = wd_ref[0, 0].astype(BF16)

    def run(n_rows):
        x = _merge_rows(xs_ref, n_rows)
        row = lax.broadcasted_iota(jnp.int32, x.shape, 0)
        x = jnp.where(row < rows_ref[i], x, 0.0).astype(BF16)
        act = _silu(_dot(x, wgb[...])) * _dot(x, wub[...])
        _split_rows(y_ref, _dot(act.astype(BF16), wdb[...]), n_rows)

    half = EXPERT_TILE // 2

    @pl.when(rows_ref[i] > half)
    def _():
        run(EXPERT_TILE)

    @pl.when((rows_ref[i] > 0) & (rows_ref[i] <= half))
    def _():
        run(half)


def _experts(xs, tile_expert, tile_block, tile_rows, tile_first, w_gate, w_up, w_down, layer):
    n_tiles = tile_expert.shape[0]
    _, _, d, ff = w_gate.shape
    rows_blk = pl.BlockSpec((ROW_SPLIT * EXPERT_TILE, 128), lambda i, te, blk, rows, first: (blk[i], 0))
    return pl.pallas_call(
        _experts_kernel,
        grid_spec=pltpu.PrefetchScalarGridSpec(
            num_scalar_prefetch=4,
            grid=(n_tiles,),
            in_specs=[rows_blk,
                      pl.BlockSpec((1, 1, d, ff), lambda i, te, blk, rows, first: (layer, te[i], 0, 0)),
                      pl.BlockSpec((1, 1, d, ff), lambda i, te, blk, rows, first: (layer, te[i], 0, 0)),
                      pl.BlockSpec((1, 1, ff, d), lambda i, te, blk, rows, first: (layer, te[i], 0, 0))],
            out_specs=rows_blk,
            scratch_shapes=[pltpu.VMEM((d, ff), BF16), pltpu.VMEM((d, ff), BF16), pltpu.VMEM((ff, d), BF16)],
        ),
        out_shape=jax.ShapeDtypeStruct(xs.shape, xs.dtype),
        compiler_params=_cparams("arbitrary"),
        name="moe_experts",
    )(tile_expert, tile_block, tile_rows, tile_first, xs, w_gate, w_up, w_down)


def _combine_kernel(x1_ref, g1_ref, g2_ref, route_ref, gf_ref, o_ref, *, tile):
    o_ref[...] = _rms(_moe_combine(x1_ref[...], g1_ref, g2_ref, route_ref[...], tile), gf_ref[...])


def _combine(x1, g1, g2, route, g_final):
    n, d = x1.shape
    tile = min(SEQ_TILE, n)
    return pl.pallas_call(
        functools.partial(_combine_kernel, tile=tile),
        grid=(n // tile,),
        in_specs=[_rows(tile, d), _rows(ROW_SPLIT * tile, 128), _rows(ROW_SPLIT * tile, 128),
                  _rows(tile, ROUTE_LANES), _full((1, d))],
        out_specs=_rows(tile, d),
        out_shape=jax.ShapeDtypeStruct((n, d), F32),
        compiler_params=_cparams("parallel"),
        name="moe_combine",
    )(x1, g1, g2, route, g_final)


def _moe(h2_rows, counts, idx1, idx2, w_gate, w_up, w_down, layer):
    n = h2_rows.shape[0] // ROW_SPLIT
    n_tiles = (MOE_TOPK * n) // EXPERT_TILE + MOE_EXPERTS
    tile_expert, tile_block, tile_rows, tile_first = _dispatch_plan(counts, n, n_tiles)
    xs = _sc_scatter2(h2_rows, idx1, idx2, ROW_SPLIT * (MOE_EXPERTS * n + EXPERT_TILE))
    ys = _experts(xs, tile_expert, tile_block, tile_rows, tile_first, w_gate, w_up, w_down, layer)
    return _sc_gather2(ys, idx1, idx2)


def kernel(x, positions, norm1_g, w_in, w_out, s5_a_re, s5_a_im, s5_log_dt, s5_b_re, s5_b_im, s5_c_re, s5_c_im, s5_d, s5_w_glu, s5_b_glu, m2_conv_w, m2_conv_b, m2_dt_bias, m2_a_log, m2_d, m2_norm_g, hg_lb_logits, hg_norm_g, norm2_g, moe_w_group, moe_b_group, moe_w_expert, moe_b_expert, moe_w_gate, moe_w_up, moe_w_down, final_norm_g):
    bsz, seqlen, d = x.shape
    assert bsz == 1 and seqlen % SEQ_TILE == 0 and seqlen % EXPERT_TILE == 0
    depth = w_in.shape[0]

    lb_probs = jax.nn.softmax(hg_lb_logits.astype(F32), axis=0)
    lower_bounds = (jnp.cumsum(lb_probs, axis=0) - lb_probs[0]).reshape(depth, 1, GROUP_WIDTH)
    cos_t, sin_t = _rope_tables(positions.reshape(seqlen, 1))

    s5_p = _s5_params(s5_a_re, s5_a_im, s5_log_dt, s5_b_re, s5_b_im, s5_c_re, s5_c_im, s5_d)
    m2_p = _mamba2_params(m2_conv_w, m2_conv_b, m2_dt_bias, m2_a_log, m2_d, m2_norm_g)
    w_route = jnp.zeros((depth, d, ROUTE_LANES), F32)
    w_route = w_route.at[:, :, :MOE_GROUPS].set(moe_w_group)
    w_route = w_route.at[:, :, EXPERT_ROW:EXPERT_ROW + MOE_EXPERTS].set(moe_w_expert)
    wr_packed = jnp.concatenate(_split2(w_route), axis=2)
    b_route = jnp.zeros((depth, 1, ROUTE_LANES), F32)
    b_route = b_route.at[:, 0, :MOE_GROUPS].set(moe_b_group)
    b_route = b_route.at[:, 0, EXPERT_ROW:EXPERT_ROW + MOE_EXPERTS].set(moe_b_expert)
    w_glu_b, w_out_b = s5_w_glu.astype(BF16), w_out.astype(BF16)
    b_glu = s5_b_glu.reshape(depth, 1, GROUP_WIDTH)
    g1, g2 = norm1_g.reshape(depth, 1, d), norm2_g.reshape(depth, 1, d)
    hg_g = hg_norm_g.reshape(depth, 1, GROUP_WIDTH)
    g_final = final_norm_g.reshape(1, d)
    tile = min(SEQ_TILE, seqlen)
    chunk = min(CHUNK, tile)

    xc, pending = x.reshape(seqlen, d), None
    for l in range(depth):
        outs = _in_proj(xc, pending, g1, _w_prep(w_in, l), l)
        if pending is not None:
            xc, outs = outs[0], outs[1:]
        u_b, u_f, p_ret, p_m2, p_hg = outs
        y_s5 = _s5(u_b, u_f, s5_p, l)
        y_ret, y_m2, y_hg = _mixers(
            seqlen,
            _retention_call(p_ret, cos_t, sin_t, tile, min(RET_CHUNK, tile)),
            _mamba2_call(p_m2, m2_p, l, tile, chunk),
            _hgrn2_call(p_hg, lower_bounds, hg_g, l, tile, chunk))
        x1, h2_rows, route, counts, idx1, idx2 = _out_proj(xc, y_s5, y_ret, y_m2, y_hg, w_glu_b, b_glu, w_out_b,
                                                           g2, wr_packed, b_route, l)
        rows1, rows2 = _moe(h2_rows, counts, idx1, idx2, moe_w_gate, moe_w_up, moe_w_down, l)
        xc, pending = x1, (rows1, rows2, route)
    xc = _combine(xc, *pending, g_final)
    return xc.reshape(bsz, seqlen, d)
```

```python
import functools
import math

import numpy as np
import jax
import jax.numpy as jnp
from jax import lax
from jax.experimental import pallas as pl
from jax.experimental.pallas import tpu as pltpu
from jax.experimental.pallas import tpu_sc as plsc

F32 = jnp.float32
BF16 = jnp.bfloat16
NORM_EPS = 1e-6

GROUP_WIDTH = 256
HEAD_DIM = 64
N_HEADS = GROUP_WIDTH // HEAD_DIM
S5_GROUPS = 16
S5_CH = 16
S5_STATE = 64
S5_DT_CLAMP = -1e-4
M2_STATE = 128
M2_CONV = 4
M2_CONV_DIM = 768
ROPE_BASE = 10000.0
MOE_GROUPS = 4
MOE_PER_GROUP = 4
MOE_EXPERTS = 16
ROUTE_LANES = 128

SEQ_TILE = 512
CHUNK = 128
RET_CHUNK = 256
CHUNK_UNROLL = 4
S5_CHUNK = 16
MOE_TOPK = 2
EXPERT_TILE = 512
SC_WINDOW = 128
VMEM_LIMIT = 56 * 1024 * 1024


def _cparams(*sem):
    return pltpu.CompilerParams(dimension_semantics=sem, vmem_limit_bytes=VMEM_LIMIT)


def _dot(a, b):
    return jnp.dot(a, b, preferred_element_type=F32)


def _dot_nt(a, b):
    return lax.dot_general(a, b, (((1,), (1,)), ((), ())), preferred_element_type=F32)


def _dot_tn(a, b):
    return lax.dot_general(a, b, (((0,), (0,)), ((), ())), preferred_element_type=F32)


def _split2(x):
    hi = x.astype(BF16)
    return hi, (x - hi.astype(F32)).astype(BF16)


def _split3(x):
    hi = x.astype(BF16)
    r = x - hi.astype(F32)
    mid = r.astype(BF16)
    return hi, mid, (r - mid.astype(F32)).astype(BF16)


def _dot_exact_lhs(m, x):
    hi, mid, lo = _split3(x)
    return _dot(m, hi) + _dot(m, mid) + _dot(m, lo)


def _dot_exact_rhs(x, m):
    hi, lo = _split2(x)
    return _dot(hi, m) + _dot(lo, m)


def _sigmoid(x):
    return 1.0 / (1.0 + jnp.exp(-x))


def _silu(x):
    return x * _sigmoid(x)


def _rms(x, g):
    return x * lax.rsqrt(jnp.mean(x * x, axis=-1, keepdims=True) + NORM_EPS) * g


def _full(shape):
    return pl.BlockSpec(shape, lambda *_: (0,) * len(shape))


def _rows(tile, width):
    return pl.BlockSpec((tile, width), lambda i: (i, 0))


def _pick(shape, layer):
    return pl.BlockSpec((None,) + tuple(shape), lambda *_: (layer,) + (0,) * len(shape))


IN_SEGMENTS = (256, 1024, 1280, 1024)


DT_COL = 9 * GROUP_WIDTH
W_PREP_ROWS = 128


def _w_prep_kernel(w_ref, o_ref):
    o_ref[:, 0:DT_COL] = w_ref[0, :, 0:DT_COL].astype(BF16)
    head = lax.broadcasted_iota(jnp.int32, (W_PREP_ROWS, GROUP_WIDTH), 1) // HEAD_DIM
    rep = jnp.zeros((W_PREP_ROWS, GROUP_WIDTH), F32)
    for h in range(N_HEADS):
        rep = jnp.where(head == h, w_ref[0, :, DT_COL + h:DT_COL + h + 1], rep)
    o_ref[:, DT_COL:DT_COL + GROUP_WIDTH] = rep.astype(BF16)
    o_ref[:, DT_COL + GROUP_WIDTH:] = w_ref[0, :, DT_COL + N_HEADS:].astype(BF16)


def _w_prep(w_in, layer):
    _, d, n_in = w_in.shape
    return pl.pallas_call(
        _w_prep_kernel,
        grid=(d // W_PREP_ROWS,),
        in_specs=[pl.BlockSpec((1, W_PREP_ROWS, n_in), lambda i: (layer, i, 0))],
        out_specs=pl.BlockSpec((W_PREP_ROWS, sum(IN_SEGMENTS)), lambda i: (i, 0)),
        out_shape=jax.ShapeDtypeStruct((d, sum(IN_SEGMENTS)), BF16),
        compiler_params=_cparams("parallel"),
        name="w_in_prep",
    )(w_in)


def _in_proj_kernel(*refs, tile, moe_pending):
    if moe_pending:
        x_ref, g1_ref, g2_ref, route_ref, g_ref, w_ref, xo_ref, ub_ref, uf_ref, ret_ref, m2_ref, hg_ref, u_tmp = refs
        x = _moe_combine(x_ref[...], g1_ref, g2_ref, route_ref[...], tile)
        xo_ref[...] = x
    else:
        x_ref, g_ref, w_ref, ub_ref, uf_ref, ret_ref, m2_ref, hg_ref, u_tmp = refs
        x = x_ref[...]
    hb = _rms(x, g_ref[...]).astype(BF16)
    c0, c1, c2, c3 = np.cumsum(IN_SEGMENTS)
    ret_ref[...] = _dot(hb, w_ref[:, c0:c1])
    m2_ref[...] = _dot(hb, w_ref[:, c1:c2])
    hg_ref[...] = _dot(hb, w_ref[:, c2:c3])
    u = _dot(hb, w_ref[:, 0:c0])
    for j in range(GROUP_WIDTH // 128):
        u_tmp[j] = u[:, 128 * j:128 * (j + 1)]
    for s in range(S5_CHUNK):
        for j in range(GROUP_WIDTH // 128):
            v = u_tmp[j, pl.ds(s, tile // S5_CHUNK, stride=S5_CHUNK), :]
            lanes = slice(GROUP_WIDTH * s + 128 * j, GROUP_WIDTH * s + 128 * (j + 1))
            uf_ref[:, lanes] = v
            ub_ref[:, lanes] = v.astype(BF16)


def _in_proj(x, pending, g, w, layer):
    n, d = x.shape
    tile = min(SEQ_TILE, n)
    fold = S5_CHUNK * GROUP_WIDTH
    moe_specs = [_rows(ROW_SPLIT * tile, 128), _rows(ROW_SPLIT * tile, 128), _rows(tile, ROUTE_LANES)]
    return pl.pallas_call(
        functools.partial(_in_proj_kernel, tile=tile, moe_pending=pending is not None),
        grid=(n // tile,),
        in_specs=[_rows(tile, d)] + (moe_specs if pending else []) + [_pick((1, d), layer), _full(w.shape)],
        out_specs=([_rows(tile, d)] if pending else []) + [_rows(tile // S5_CHUNK, fold)] * 2
        + [_rows(tile, s) for s in IN_SEGMENTS[1:]],
        out_shape=([jax.ShapeDtypeStruct((n, d), F32)] if pending else [])
        + [jax.ShapeDtypeStruct((n // S5_CHUNK, fold), BF16), jax.ShapeDtypeStruct((n // S5_CHUNK, fold), F32)]
        + [jax.ShapeDtypeStruct((n, s), F32) for s in IN_SEGMENTS[1:]],
        scratch_shapes=[pltpu.VMEM((GROUP_WIDTH // 128, tile, 128), F32)],
        compiler_params=_cparams("parallel"),
        name="in_proj",
    )(x, *(pending or ()), g, w)


def _rope_kernel(pos_ref, invf_ref, cos_ref, sin_ref):
    ang = pos_ref[...].astype(F32) * invf_ref[...]
    cos_ref[...] = jnp.cos(ang)
    sin_ref[...] = jnp.sin(ang)


def _rope_tables(positions):
    n = positions.shape[0]
    tile = min(SEQ_TILE, n)
    half = HEAD_DIM // 2
    inv_freq = ROPE_BASE ** (-jnp.arange(half, dtype=F32) / half)
    invf = jnp.tile(inv_freq, 128 // half).reshape(1, 128)
    return pl.pallas_call(
        _rope_kernel,
        grid=(n // tile,),
        in_specs=[_rows(tile, 1), _full((1, 128))],
        out_specs=[_rows(tile, 128), _rows(tile, 128)],
        out_shape=[jax.ShapeDtypeStruct((n, 128), F32)] * 2,
        compiler_params=_cparams("parallel"),
        name="rope_tables",
    )(positions, invf)


def _head_mean_matrix():
    h = np.arange(GROUP_WIDTH) // HEAD_DIM
    return jnp.asarray((h[:, None] == h[None, :]) / HEAD_DIM, dtype=BF16)


def _head_block_mask():
    h = np.arange(GROUP_WIDTH) // HEAD_DIM
    return jnp.asarray(h[:, None] == h[None, :], dtype=F32)


HEAD_PAIRS = GROUP_WIDTH // 128


def _lanes(x, j):
    return x[:, 128 * j:128 * (j + 1)]


def _stack_pair(x):
    xb = x.astype(BF16)
    low = lax.broadcasted_iota(jnp.int32, x.shape, 1) < HEAD_DIM
    zero = jnp.zeros_like(xb)
    return jnp.concatenate([jnp.where(low, xb, zero), jnp.where(low, zero, xb)], axis=0)


def _pair_scores(q, kb):
    s = _dot_nt(_stack_pair(q), kb)
    return s[:q.shape[0]], s[q.shape[0]:]


def _pair_apply(a0, a1, v):
    return _dot(jnp.concatenate([a0.astype(BF16), a1.astype(BF16)], axis=1), _stack_pair(v))


def _ret_constants(chunk):
    lg = np.log1p(-np.exp2(-5.0 - np.arange(N_HEADS, dtype=np.float64)))
    idx = np.arange(chunk, dtype=np.float64)
    rel = idx[:, None] - idx[None, :]
    decay = np.where(rel >= 0, np.exp(np.maximum(rel, 0.0)[None] * lg[:, None, None]), 0.0)
    lane_lg = np.repeat(lg, HEAD_DIM)
    xi = np.exp((idx + 1.0)[:, None] * lane_lg[None, :])
    zeta = np.exp((chunk - 1.0 - idx)[:, None] * lane_lg[None, :])
    h = np.arange(GROUP_WIDTH) // HEAD_DIM
    gc = np.where(h[:, None] == h[None, :], np.exp(chunk * lane_lg)[:, None], 0.0)
    f = lambda a: jnp.asarray(a, dtype=F32)
    return f(decay), f(xi), f(zeta), f(gc)


def _ret_kernel(p_ref, cos_ref, sin_ref, dec_ref, xi_ref, zeta_ref, gc_ref, bm_ref, gm_ref,
                o_ref, r_ref, *, chunk, n_chunks):
    lane = lax.broadcasted_iota(jnp.int32, (chunk, GROUP_WIDTH), 1)
    first_half = (lane % HEAD_DIM) < (HEAD_DIM // 2)
    gm = gm_ref[...]

    def rope(t, cos2, sin2):
        rot = jnp.where(first_half, -pltpu.roll(t, GROUP_WIDTH - HEAD_DIM // 2, 1),
                        pltpu.roll(t, HEAD_DIM // 2, 1))
        return t * cos2 + rot * sin2

    def body(c, carry):
        rows = pl.ds(pl.multiple_of(c * chunk, chunk), chunk)
        cs = cos_ref[rows, :]
        sn = sin_ref[rows, :]
        cos2 = jnp.concatenate([cs, cs], axis=1)
        sin2 = jnp.concatenate([sn, sn], axis=1)
        q = rope(p_ref[rows, 0:256], cos2, sin2)
        k = rope(p_ref[rows, 256:512], cos2, sin2) * (HEAD_DIM ** -0.5)
        v = p_ref[rows, 512:768]
        g = p_ref[rows, 768:1024]
        kb = k.astype(BF16)
        inner = []
        for j in range(HEAD_PAIRS):
            s0, s1 = _pair_scores(_lanes(q, j), _lanes(kb, j))
            inner.append(_pair_apply(s0 * dec_ref[2 * j], s1 * dec_ref[2 * j + 1], _lanes(v, j)))
        inner = jnp.concatenate(inner, axis=1)
        r_prev = r_ref[...]
        cross = _dot((q * xi_ref[...]).astype(BF16), r_prev.astype(BF16))
        r_ref[...] = gc_ref[...] * r_prev + bm_ref[...] * _dot_tn(kb, (zeta_ref[...] * v).astype(BF16))
        o = inner + cross
        cen = o - _dot_exact_rhs(o, gm)
        var = _dot_exact_rhs(cen * cen, gm)
        o_ref[rows, :] = (cen * lax.rsqrt(var + NORM_EPS) * _silu(g)).astype(BF16)
        return carry

    lax.fori_loop(0, n_chunks, body, 0, unroll=CHUNK_UNROLL)


def _retention_call(proj, cos_t, sin_t, tile, chunk):
    decay, xi, zeta, gc = _ret_constants(chunk)
    body = functools.partial(_ret_kernel, chunk=chunk, n_chunks=tile // chunk)
    operands = (proj, cos_t, sin_t, decay, xi, zeta, gc, _head_block_mask(), _head_mean_matrix())
    specs = [_rows(tile, 1024), _rows(tile, 128), _rows(tile, 128), _full(decay.shape), _full(xi.shape),
             _full(zeta.shape), _full(gc.shape), _full((256, 256)), _full((256, 256))]
    return body, operands, specs, [pltpu.VMEM((GROUP_WIDTH, GROUP_WIDTH), F32)]


def _tri_matrix(chunk):
    i = np.arange(chunk)
    return jnp.asarray(i[:, None] >= i[None, :], dtype=BF16)


def _m2_kernel(p_ref, cw_ref, cb_ref, dtb_ref, alog_ref, d_ref, ng_ref, tri_ref,
               o_ref, tail_ref, ext_ref, act_ref, st_ref, *, tile, chunk, n_chunks):
    ext_ref[0:8, :] = tail_ref[...]
    ext_ref[8:tile + 8, :] = p_ref[:, 256:1024]
    tail_ref[...] = p_ref[tile - 8:tile, 256:1024]
    conv = cb_ref[...]
    for j in range(M2_CONV):
        lo = 8 - (M2_CONV - 1) + j
        conv = conv + cw_ref[j:j + 1, :] * ext_ref[lo:lo + tile, :]
    act_ref[...] = _silu(conv)
    a_lane = -jnp.exp(alog_ref[...])
    tri = tri_ref[...]
    ti = lax.broadcasted_iota(jnp.int32, (chunk, chunk), 0)
    si = lax.broadcasted_iota(jnp.int32, (chunk, chunk), 1)
    causal = ti >= si
    lane = lax.broadcasted_iota(jnp.int32, (chunk, 128), 1)

    def body(c, carry):
        start = pl.multiple_of(c * chunk, chunk)
        rows = pl.ds(start, chunk)
        xbc = act_ref[rows, :]
        xs = xbc[:, 0:256]
        z = p_ref[rows, 0:256]
        x_dt = p_ref[rows, 1024:1280] + dtb_ref[...]
        dt = jnp.maximum(x_dt, 0.0) + jnp.log1p(jnp.exp(-jnp.abs(x_dt)))
        acum = _dot_exact_lhs(tri, dt * a_lane)
        acum_t = acum.T
        a_last = acum[chunk - 1:chunk, :]
        e_acum = jnp.exp(acum)
        decs = jnp.exp(a_last - acum)
        d_chunk = jnp.exp(a_last)
        xc = xs * dt
        ys = []
        for g in range(2):
            sl = slice(128 * g, 128 * (g + 1))
            bmg = xbc[:, 256 + 128 * g:256 + 128 * (g + 1)].astype(BF16)
            cmg = xbc[:, 512 + 128 * g:512 + 128 * (g + 1)].astype(BF16)
            cb = _dot_nt(cmg, bmg)
            xcg = xc[:, sl]
            yd = jnp.zeros((chunk, 128), F32)
            for hh in range(2):
                col0 = 128 * g + HEAD_DIM * hh
                diff = acum[:, col0:col0 + 1] - acum_t[col0:col0 + 1, :]
                lm = jnp.where(causal, jnp.exp(jnp.where(causal, diff, 0.0)), 0.0)
                xm = jnp.where((lane // HEAD_DIM) == hh, xcg, 0.0).astype(BF16)
                yd = yd + _dot((cb * lm).astype(BF16), xm)
            st = st_ref[:, sl]
            y_off = _dot(cmg, st.astype(BF16)) * e_acum[:, sl]
            st_ref[:, sl] = d_chunk[:, sl] * st + _dot_tn(bmg, (xcg * decs[:, sl]).astype(BF16))
            ys.append(yd + y_off + d_ref[:, sl] * xs[:, sl])
        y = jnp.concatenate(ys, axis=1) * _silu(z)
        o_ref[rows, :] = _rms(y, ng_ref[...]).astype(BF16)
        return carry

    lax.fori_loop(0, n_chunks, body, 0, unroll=CHUNK_UNROLL)


def _mamba2_params(conv_w, conv_b, dt_bias, a_log, d_skip, norm_g):
    depth = conv_w.shape[0]
    lanes = lambda v: jnp.repeat(v, HEAD_DIM, axis=1).reshape(depth, 1, GROUP_WIDTH)
    return (conv_w, conv_b.reshape(depth, 1, -1), lanes(dt_bias), lanes(a_log), lanes(d_skip),
            norm_g.reshape(depth, 1, -1))


def _mamba2_call(proj, params, layer, tile, chunk):
    body = functools.partial(_m2_kernel, tile=tile, chunk=chunk, n_chunks=tile // chunk)
    operands = (proj,) + tuple(params) + (_tri_matrix(chunk),)
    specs = ([_rows(tile, 1280), _pick((M2_CONV, M2_CONV_DIM), layer), _pick((1, M2_CONV_DIM), layer)]
             + [_pick((1, GROUP_WIDTH), layer)] * 4 + [_full((chunk, chunk))])
    scratch = [pltpu.VMEM((8, M2_CONV_DIM), F32), pltpu.VMEM((tile + 8, M2_CONV_DIM), F32),
               pltpu.VMEM((tile, M2_CONV_DIM), F32), pltpu.VMEM((M2_STATE, GROUP_WIDTH), F32)]
    return body, operands, specs, scratch


HG_MATMUL_LEVELS = 3


def _hg_exponent_matrix(chunk):
    levels = HG_MATMUL_LEVELS
    t = np.arange(chunk)[:, None]
    r = np.arange(chunk)[None, :]
    blocks = []
    for lvl in range(levels):
        b = 1 << lvl
        blk = t // b
        odd = (blk % 2) == 1
        q_rows = odd & (r >= blk * b) & (r <= t)
        k_rows = (~odd) & (r > t) & (r <= (blk + 1) * b - 1)
        blocks.append(q_rows | k_rows)
    blocks.append(r <= t)
    return jnp.asarray(np.concatenate(blocks, axis=0), dtype=BF16)


def _hg_kernel(p_ref, lb_ref, ng_ref, gexp_ref, bm_ref, gm_ref, o_ref, st_ref, *, chunk, n_chunks):
    levels = int(math.log2(chunk))
    row = lax.broadcasted_iota(jnp.int32, (chunk, 128), 0)
    odd_rows = [((row >> lvl) & 1) == 1 for lvl in range(levels)]
    row_wide = lax.broadcasted_iota(jnp.int32, (chunk, GROUP_WIDTH), 0)
    odd_rows_wide = [((row_wide >> lvl) & 1) == 1 for lvl in range(levels)]
    ti = lax.broadcasted_iota(jnp.int32, (chunk, chunk), 0)
    si = lax.broadcasted_iota(jnp.int32, (chunk, chunk), 1)
    pair_level = [((ti >> (lvl + 1)) == (si >> (lvl + 1))) & (((ti >> lvl) & 1) == 1) & (((si >> lvl) & 1) == 0)
                  for lvl in range(levels)]
    lb = lb_ref[...]
    gm = gm_ref[...]

    def body(c, carry):
        rows = pl.ds(pl.multiple_of(c * chunk, chunk), chunk)
        q = _silu(p_ref[rows, 0:256])
        forget = lb + (1.0 - lb) * _sigmoid(p_ref[rows, 256:512])
        k = 1.0 - forget
        v = p_ref[rows, 512:768]
        g = p_ref[rows, 768:1024]
        lf_hi, lf_lo = _split2(jnp.log(forget))
        expo = _dot(gexp_ref[...], lf_hi) + _dot(gexp_ref[...], lf_lo)
        bcum = expo[HG_MATMUL_LEVELS * chunk:(HG_MATMUL_LEVELS + 1) * chunk, :]

        def level_log_decay(lvl):
            if lvl < HG_MATMUL_LEVELS:
                return expo[lvl * chunk:(lvl + 1) * chunk, :]
            b = 1 << lvl
            ref = jnp.concatenate([jnp.broadcast_to(bcum[m + b - 1:m + b, :], (2 * b, GROUP_WIDTH))
                                   for m in range(0, chunk, 2 * b)], axis=0)
            return jnp.where(odd_rows_wide[lvl], bcum - ref, ref - bcum)


        log_decay = [level_log_decay(lvl) for lvl in range(levels)]
        intra = []
        for j in range(HEAD_PAIRS):
            qj, kj = _lanes(q, j), _lanes(k, j)
            a0, a1 = (jnp.where(ti == si, s, 0.0) for s in _pair_scores(qj, kj.astype(BF16)))
            for lvl in range(levels):
                w = jnp.exp(_lanes(log_decay[lvl], j)) * jnp.where(odd_rows[lvl], qj, kj)
                s0, s1 = _pair_scores(w, w.astype(BF16))
                a0 = jnp.where(pair_level[lvl], s0, a0)
                a1 = jnp.where(pair_level[lvl], s1, a1)
            intra.append(_pair_apply(a0, a1, _lanes(v, j)))
        intra = jnp.concatenate(intra, axis=1)

        b_last = bcum[chunk - 1:chunk, :]
        suffix = b_last - bcum
        st = st_ref[...]
        cross = _dot_nt((q * jnp.exp(bcum)).astype(BF16), st.astype(BF16))
        st_ref[...] = jnp.exp(b_last) * st + bm_ref[...] * _dot_tn(
            v.astype(BF16), (k * jnp.exp(suffix)).astype(BF16))
        o = intra + cross
        o = o * lax.rsqrt(_dot_exact_rhs(o * o, gm) + NORM_EPS) * ng_ref[...]
        o_ref[rows, :] = (o * _silu(g)).astype(BF16)
        return carry

    lax.fori_loop(0, n_chunks, body, 0, unroll=CHUNK_UNROLL)


def _hgrn2_call(proj, lower_bounds, norm_g, layer, tile, chunk):
    gexp = _hg_exponent_matrix(chunk)
    body = functools.partial(_hg_kernel, chunk=chunk, n_chunks=tile // chunk)
    operands = (proj, lower_bounds, norm_g, gexp, _head_block_mask(), _head_mean_matrix())
    specs = [_rows(tile, 1024), _pick((1, GROUP_WIDTH), layer), _pick((1, GROUP_WIDTH), layer), _full(gexp.shape),
             _full((256, 256)), _full((256, 256))]
    return body, operands, specs, [pltpu.VMEM((GROUP_WIDTH, GROUP_WIDTH), F32)]


def _mixers_kernel(*refs, bodies, n_in, n_scratch):
    n_mix = len(bodies)
    ins, pos = [], 0
    for k in n_in:
        ins.append(refs[pos:pos + k])
        pos += k
    outs = refs[pos:pos + n_mix]
    pos += n_mix
    scratch = []
    for k in n_scratch:
        scratch.append(refs[pos:pos + k])
        pos += k

    @pl.when(pl.program_id(0) == 0)
    def _():
        for group in scratch:
            for ref in group:
                ref[...] = jnp.zeros_like(ref)

    for body, i, o, s in zip(bodies, ins, outs, scratch):
        body(*i, o, *s)


def _mixers(n, *calls):
    tile = min(SEQ_TILE, n)
    bodies = tuple(c[0] for c in calls)
    return pl.pallas_call(
        functools.partial(_mixers_kernel, bodies=bodies, n_in=tuple(len(c[1]) for c in calls),
                          n_scratch=tuple(len(c[3]) for c in calls)),
        grid=(n // tile,),
        in_specs=[s for c in calls for s in c[2]],
        out_specs=[_rows(tile, GROUP_WIDTH)] * len(calls),
        out_shape=[jax.ShapeDtypeStruct((n, GROUP_WIDTH), BF16)] * len(calls),
        scratch_shapes=[s for c in calls for s in c[3]],
        compiler_params=_cparams("arbitrary"),
        name="mixers",
    )(*[a for c in calls for a in c[1]])


S5_LANES = S5_GROUPS * 2 * S5_STATE
S5_TAP_SPLITS = 4
S5_STATE_GROUP = 4


def _s5_rows(a_re, a_im, log_dt):
    are = jnp.minimum(a_re, S5_DT_CLAMP)
    dt = jnp.exp(log_dt)
    lam_re = are * dt
    lam_im = a_im * dt
    mag = jnp.exp(lam_re)
    ab_re = mag * jnp.cos(lam_im)
    ab_im = mag * jnp.sin(lam_im)
    den = are * are + a_im * a_im
    k_re = ((ab_re - 1.0) * are + ab_im * a_im) / den
    k_im = (ab_im * are - (ab_re - 1.0) * a_im) / den
    return lam_re, lam_im, k_re, k_im


def _s5_power(lam_re, lam_im, e):
    m = jnp.exp(e * lam_re)
    return m * jnp.cos(e * lam_im), m * jnp.sin(e * lam_im)


def _s5_state_kernel(u_ref, are_ref, aim_ref, ldt_ref, b1_ref, b2_ref, ca_ref, k_ref, sp_ref,
                     bb1_ref, bb2_ref, inc_ref, w_ref, *, n_chunks):
    g = pl.program_id(0)
    lam_re, lam_im, k_re, k_im = _s5_rows(are_ref[...], aim_ref[...], ldt_ref[...])

    @pl.when(g == 0)
    def _():
        bb1_ref[...] = k_re * b1_ref[...] + k_im * b2_ref[...]
        bb2_ref[...] = k_re * b2_ref[...] - k_im * b1_ref[...]

    ca = ca_ref[...].astype(BF16)
    for j in range(S5_STATE_GROUP):
        s = g * S5_STATE_GROUP + j
        p_re, p_im = _s5_power(lam_re, lam_im, (S5_CHUNK - 1 - s).astype(F32))
        w = (p_re * bb1_ref[...] + p_im * bb2_ref[...]).astype(BF16)
        k_ref[j] = _dot_nt(w, ca).astype(BF16)
        w_ref[GROUP_WIDTH * j:GROUP_WIDTH * (j + 1), :] = w
    contrib = _dot(u_ref[...], w_ref[...])

    @pl.when(g == 0)
    def _():
        inc_ref[...] = contrib

    @pl.when(g > 0)
    def _():
        inc_ref[...] += contrib

    @pl.when(g == S5_CHUNK // S5_STATE_GROUP - 1)
    def _():
        n_steps = max(1, int(math.ceil(math.log2(n_chunks))))
        step = lax.broadcasted_iota(jnp.int32, (16, S5_LANES), 0)
        e = (jnp.left_shift(1, step) * S5_CHUNK).astype(F32)
        a_re_all, a_im_all = _s5_power(lam_re, lam_im, e)
        row = lax.broadcasted_iota(jnp.int32, (n_chunks, 128), 0)
        half = S5_LANES // 2

        def shifted(x, sh):
            return jnp.where(row >= sh, pltpu.roll(x, sh, 0), 0.0)

        for j in range(half // 128):
            re_l, im_l = slice(128 * j, 128 * (j + 1)), slice(half + 128 * j, half + 128 * (j + 1))
            x_re, x_im = inc_ref[:, re_l], inc_ref[:, im_l]
            for k in range(n_steps):
                sh = 1 << k
                if sh >= n_chunks:
                    break
                a_re, a_im = a_re_all[k:k + 1, re_l], a_im_all[k:k + 1, re_l]
                p_re, p_im = shifted(x_re, sh), shifted(x_im, sh)
                x_re, x_im = x_re + a_re * p_re - a_im * p_im, x_im + a_re * p_im + a_im * p_re
            sp_ref[:, re_l] = shifted(x_re, 1).astype(BF16)
            sp_ref[:, im_l] = shifted(x_im, 1).astype(BF16)


def _s5_out_kernel(ub_ref, uf_ref, k_ref, sp_ref, are_ref, aim_ref, ldt_ref, ca_ref, cb_ref, d_ref,
                   y_ref, taps_ref):
    t = pl.program_id(0)
    fold = S5_CHUNK * GROUP_WIDTH

    @pl.when(t == 0)
    def _():
        for j in range(S5_CHUNK):
            taps_ref[GROUP_WIDTH * j:GROUP_WIDTH * (j + 1), :] = k_ref[j]
        taps_ref[fold:, :] = jnp.zeros((fold - GROUP_WIDTH, GROUP_WIDTH), BF16)

    lam_re, lam_im, _, _ = _s5_rows(are_ref[...], aim_ref[...], ldt_ref[...])
    p_re, p_im = _s5_power(lam_re, lam_im, (t + 1).astype(F32))
    w_out = (p_re * ca_ref[...] + p_im * cb_ref[...]).astype(BF16)
    start = pl.multiple_of((S5_CHUNK - 1 - t) * GROUP_WIDTH, GROUP_WIDTH)
    y_ref[...] = _dot_nt(sp_ref[...], w_out) + d_ref[...] * uf_ref[...]
    quarter = S5_CHUNK // S5_TAP_SPLITS
    for part in range(S5_TAP_SPLITS):
        width = (part + 1) * quarter * GROUP_WIDTH

        @pl.when((t >= part * quarter) & (t < (part + 1) * quarter))
        def _():
            y_ref[...] += _dot(ub_ref[:, 0:width], taps_ref[pl.ds(start, width), :])


def _s5_embed(re, im):
    eye = jnp.eye(S5_GROUPS, dtype=F32)
    blocks = [(eye[None, :, None, :, None] * x[:, :, :, None, :]).reshape(-1, S5_GROUPS * S5_CH, S5_LANES // 2)
              for x in (re, im)]
    return jnp.concatenate(blocks, axis=2)


def _s5_params(a_re, a_im, log_dt, b_re, b_im, c_re, c_im, d_skip):
    depth = a_re.shape[0]
    row = lambda v: jnp.tile(v.reshape(depth, 1, S5_LANES // 2), (1, 1, 2))
    bt_re, bt_im = b_re.transpose(0, 1, 3, 2), b_im.transpose(0, 1, 3, 2)
    return dict(are=row(a_re), aim=row(a_im), ldt=row(jnp.repeat(log_dt, S5_STATE, axis=1)),
                b1=_s5_embed(bt_re, bt_im), b2=_s5_embed(-bt_im, bt_re),
                ca=_s5_embed(c_re, -c_im), cb=_s5_embed(-c_im, -c_re),
                d=d_skip.reshape(depth, 1, GROUP_WIDTH))


def _s5(u_b, u_f, p, layer):
    n_chunks, fold = u_b.shape
    row_spec, mat_spec = _pick((1, S5_LANES), layer), _pick((GROUP_WIDTH, S5_LANES), layer)
    col = lambda: pl.BlockSpec((n_chunks, GROUP_WIDTH), lambda s: (0, s))
    group_cols = S5_STATE_GROUP * GROUP_WIDTH
    taps, s_prev = pl.pallas_call(
        functools.partial(_s5_state_kernel, n_chunks=n_chunks),
        grid=(S5_CHUNK // S5_STATE_GROUP,),
        in_specs=[pl.BlockSpec((n_chunks, group_cols), lambda g: (0, g))] + [row_spec] * 3 + [mat_spec] * 3,
        out_specs=[pl.BlockSpec((S5_STATE_GROUP, GROUP_WIDTH, GROUP_WIDTH), lambda g: (g, 0, 0)),
                   _full((n_chunks, S5_LANES))],
        out_shape=[jax.ShapeDtypeStruct((S5_CHUNK, GROUP_WIDTH, GROUP_WIDTH), BF16),
                   jax.ShapeDtypeStruct((n_chunks, S5_LANES), BF16)],
        scratch_shapes=[pltpu.VMEM((GROUP_WIDTH, S5_LANES), F32), pltpu.VMEM((GROUP_WIDTH, S5_LANES), F32),
                        pltpu.VMEM((n_chunks, S5_LANES), F32), pltpu.VMEM((group_cols, S5_LANES), BF16)],
        compiler_params=_cparams("arbitrary"),
        name="s5_state",
    )(u_b, p['are'], p['aim'], p['ldt'], p['b1'], p['b2'], p['ca'])
    return pl.pallas_call(
        _s5_out_kernel,
        grid=(S5_CHUNK,),
        in_specs=[_full((n_chunks, fold)), col(), _full(taps.shape), _full(s_prev.shape)]
        + [row_spec] * 3 + [mat_spec] * 2 + [_pick((1, GROUP_WIDTH), layer)],
        out_specs=col(),
        out_shape=jax.ShapeDtypeStruct((n_chunks, fold), F32),
        scratch_shapes=[pltpu.VMEM(((2 * S5_CHUNK - 1) * GROUP_WIDTH, GROUP_WIDTH), BF16)],
        compiler_params=_cparams("arbitrary"),
        name="s5_out",
    )(u_b, u_f, taps, s_prev, p['are'], p['aim'], p['ldt'], p['ca'], p['cb'], p['d'])


EXPERT_ROW = 8


def _route(logits_t):
    tokens = logits_t.shape[1]
    big = jnp.int32(1 << 20)
    neg = jnp.float32(-jnp.inf)
    g_row = lax.broadcasted_iota(jnp.int32, (8, tokens), 0)
    is_group = g_row < MOE_GROUPS
    gl = jnp.where(is_group, logits_t[0:8, :], neg)
    ge = jnp.where(is_group, jnp.exp(gl - jnp.max(gl, axis=0, keepdims=True)), 0.0)
    gp = ge / jnp.sum(ge, axis=0, keepdims=True)
    p_g = jnp.max(gp, axis=0, keepdims=True)
    g_idx = jnp.min(jnp.where(is_group & (gp == p_g), g_row, big), axis=0, keepdims=True)
    e_row = lax.broadcasted_iota(jnp.int32, (MOE_EXPERTS, tokens), 0)
    in_group = (e_row // MOE_PER_GROUP) == g_idx
    el = jnp.where(in_group, logits_t[EXPERT_ROW:EXPERT_ROW + MOE_EXPERTS, :], neg)
    ee = jnp.where(in_group, jnp.exp(el - jnp.max(el, axis=0, keepdims=True)), 0.0)
    ep = ee / jnp.sum(ee, axis=0, keepdims=True)
    p1 = jnp.max(jnp.where(in_group, ep, -1.0), axis=0, keepdims=True)
    i1 = jnp.min(jnp.where(in_group & (ep == p1), e_row, big), axis=0, keepdims=True)
    rest = in_group & (e_row != i1)
    p2 = jnp.max(jnp.where(rest, ep, -1.0), axis=0, keepdims=True)
    i2 = jnp.min(jnp.where(rest & (ep == p2), e_row, big), axis=0, keepdims=True)
    tot = p1 + p2
    return i1, i2, p_g * p1 / tot, p_g * p2 / tot


INDEX_DIGIT_BITS = 6
ROW_SPLIT = 4
ROUTE_E1, ROUTE_E2, ROUTE_R1, ROUTE_R2, ROUTE_W1, ROUTE_W2 = range(6)
HIGH_HALF = 0xFFFF0000


def _split_rows(ref, value, rows):
    half = value.shape[1] // 2
    lo = lax.bitcast_convert_type(value[:, :half].astype(jnp.bfloat16).astype(F32), jnp.uint32)
    hi = lax.bitcast_convert_type(value[:, half:].astype(jnp.bfloat16).astype(F32), jnp.uint32)
    words = lax.bitcast_convert_type((lo >> 16) | (hi & jnp.uint32(HIGH_HALF)), jnp.int32)
    for j in range(ROW_SPLIT):
        ref[pl.ds(j, rows, stride=ROW_SPLIT), :] = words[:, 128 * j:128 * (j + 1)]


def _merge_rows(ref, rows):
    words = jnp.concatenate([ref[pl.ds(j, rows, stride=ROW_SPLIT), :] for j in range(ROW_SPLIT)], axis=1)
    words = lax.bitcast_convert_type(words, jnp.uint32)
    lo = lax.bitcast_convert_type(words << 16, F32)
    hi = lax.bitcast_convert_type(words & jnp.uint32(HIGH_HALF), F32)
    return jnp.concatenate([lo, hi], axis=1)


def _moe_combine(x1, g1_ref, g2_ref, route, rows):
    return (x1 + route[:, ROUTE_W1:ROUTE_W1 + 1] * _merge_rows(g1_ref, rows)
            + route[:, ROUTE_W2:ROUTE_W2 + 1] * _merge_rows(g2_ref, rows))


def _out_kernel(x_ref, s5_ref, ret_ref, m2_ref, hg_ref, wglu_ref, bglu_ref, wo_ref, g2_ref,
                wrh_ref, br_ref, stri_ref, spread_ref, x1_ref, h2_ref, route_ref, cnt_ref, i1_ref, i2_ref,
                s5_tmp, carry_ref, *, tile, n_tokens):
    @pl.when(pl.program_id(0) == 0)
    def _():
        carry_ref[...] = jnp.zeros_like(carry_ref)

    for s in range(S5_CHUNK):
        for j in range(GROUP_WIDTH // 128):
            lanes = slice(GROUP_WIDTH * s + 128 * j, GROUP_WIDTH * s + 128 * (j + 1))
            s5_tmp[j, pl.ds(s, tile // S5_CHUNK, stride=S5_CHUNK), :] = s5_ref[:, lanes]
    y = jnp.concatenate([s5_tmp[j] for j in range(GROUP_WIDTH // 128)], axis=1)
    y = y * (0.5 * (1.0 + jnp.tanh(math.sqrt(2.0 / math.pi) * (y + 0.044715 * (y * y * y)))))
    y = y * _sigmoid(_dot(y.astype(BF16), wglu_ref[...]) + bglu_ref[...])
    acc = x_ref[...] + _dot(y.astype(BF16), wo_ref[0:256, :])
    acc = acc + _dot(ret_ref[...], wo_ref[256:512, :])
    acc = acc + _dot(m2_ref[...], wo_ref[512:768, :])
    acc = acc + _dot(hg_ref[...], wo_ref[768:1024, :])
    x1_ref[...] = acc
    h2 = _rms(acc, g2_ref[...])
    _split_rows(h2_ref, h2, tile)
    hi, lo = _split2(h2)
    hw = _dot(hi, wrh_ref[...])
    logits = (hw[:, :ROUTE_LANES] + hw[:, ROUTE_LANES:] + _dot(lo, wrh_ref[:, :ROUTE_LANES])) + br_ref[...]
    e1, e2, w1, w2 = _route(logits.T)
    e_row = lax.broadcasted_iota(jnp.int32, (MOE_EXPERTS, tile), 0)
    picked = jnp.where((e_row == e1) | (e_row == e2), 1.0, 0.0)
    rank = carry_ref[:, 0:1] + _dot_nt(picked.astype(BF16), stri_ref[...])
    r1 = jnp.sum(jnp.where(e_row == e1, rank, 0.0), axis=0, keepdims=True)
    r2 = jnp.sum(jnp.where(e_row == e2, rank, 0.0), axis=0, keepdims=True)
    carry_ref[...] += jnp.sum(picked, axis=1, keepdims=True)
    cnt_ref[...] = carry_ref[...]
    rec_row = lax.broadcasted_iota(jnp.int32, (ROUTE_LANES, tile), 0)
    rec = jnp.zeros((ROUTE_LANES, tile), F32)
    for col, val in ((ROUTE_E1, e1.astype(F32)), (ROUTE_E2, e2.astype(F32)), (ROUTE_R1, r1), (ROUTE_R2, r2),
                     (ROUTE_W1, w1), (ROUTE_W2, w2)):
        rec = jnp.where(rec_row == col, val, rec)
    route_ref[...] = rec.T
    digit_row = lax.broadcasted_iota(jnp.int32, (8, tile), 0)
    digits = jnp.zeros((8, tile), F32)
    for slot, (e, r) in enumerate(((e1, r1), (e2, r2))):
        pos = e.astype(F32) * float(n_tokens) + r
        for k, shift in enumerate((2 * INDEX_DIGIT_BITS, INDEX_DIGIT_BITS, 0)):
            digit = jnp.floor(pos * (1.0 / (1 << shift)))
            pos = pos - digit * float(1 << shift)
            digits = jnp.where(digit_row == 3 * slot + k, digit, digits)
    sub = lax.broadcasted_iota(jnp.int32, (1, 128 * ROW_SPLIT), 1) % ROW_SPLIT
    for q in range(tile // 128):
        o = _dot(digits[:, 128 * q:128 * (q + 1)].astype(BF16), spread_ref[...])
        for slot, out_ref in enumerate((i1_ref, i2_ref)):
            moved = (float(1 << (2 * INDEX_DIGIT_BITS)) * o[3 * slot:3 * slot + 1]
                     + float(1 << INDEX_DIGIT_BITS) * o[3 * slot + 1:3 * slot + 2] + o[3 * slot + 2:3 * slot + 3])
            idx = moved.astype(jnp.int32) * ROW_SPLIT + sub
            for j in range(ROW_SPLIT):
                row = ROW_SPLIT * q + j
                out_ref[row:row + 1, :] = idx[:, 128 * j:128 * (j + 1)]


def _out_proj(x, y_s5, y_ret, y_m2, y_hg, w_glu, b_glu, w_out, g2, wr_packed, b_route, layer):
    n, d = x.shape
    tile = min(SEQ_TILE, n)
    i = np.arange(tile)
    strict_lower = jnp.asarray(i[:, None] > i[None, :], dtype=BF16)
    assert MOE_EXPERTS * n <= (1 << (3 * INDEX_DIGIT_BITS))
    per_row = 128 // ROW_SPLIT
    out_rows = tile // per_row
    lanes = np.arange(128 * ROW_SPLIT)
    spread = jnp.asarray(np.arange(128)[:, None] == (lanes // ROW_SPLIT)[None, :], dtype=BF16)
    return pl.pallas_call(
        functools.partial(_out_kernel, tile=tile, n_tokens=n),
        grid=(n // tile,),
        in_specs=[_rows(tile, d), _rows(tile // S5_CHUNK, S5_CHUNK * GROUP_WIDTH)] + [_rows(tile, GROUP_WIDTH)] * 3
        + [_pick((256, 256), layer), _pick((1, 256), layer), _pick((d, d), layer), _pick((1, d), layer),
           _pick((d, 2 * ROUTE_LANES), layer), _pick((1, ROUTE_LANES), layer), _full((tile, tile)),
           _full(spread.shape)],
        out_specs=[_rows(tile, d), _rows(ROW_SPLIT * tile, 128), _rows(tile, ROUTE_LANES),
                   _full((MOE_EXPERTS, ROUTE_LANES)), _rows(out_rows, 128), _rows(out_rows, 128)],
        out_shape=[jax.ShapeDtypeStruct((n, d), F32), jax.ShapeDtypeStruct((ROW_SPLIT * n, 128), jnp.int32),
                   jax.ShapeDtypeStruct((n, ROUTE_LANES), F32), jax.ShapeDtypeStruct((MOE_EXPERTS, ROUTE_LANES), F32)]
        + [jax.ShapeDtypeStruct((n // per_row, 128), jnp.int32)] * 2,
        scratch_shapes=[pltpu.VMEM((GROUP_WIDTH // 128, tile, 128), F32),
                        pltpu.VMEM((MOE_EXPERTS, ROUTE_LANES), F32)],
        compiler_params=_cparams("arbitrary"),
        name="out_proj_router",
    )(x, y_s5, y_ret, y_m2, y_hg, w_glu, b_glu, w_out, g2, wr_packed, b_route, strict_lower, spread)


def _sc_mesh():
    return plsc.VectorSubcoreMesh(core_axis_name="core", subcore_axis_name="subcore")


def _sc_scatter2(src, idx_a, idx_b, n_out):
    n = src.shape[0]

    @functools.partial(pl.kernel, out_type=jax.ShapeDtypeStruct((n_out, 128), src.dtype), mesh=_sc_mesh(),
                       scratch_types=[])
    def scatter_kernel(x_hbm, ia_hbm, ib_hbm, o_hbm):
        def body(x_vmem, ia_vmem, ib_vmem):
            pltpu.sync_copy(x_vmem, o_hbm.at[ia_vmem.at[0]])
            pltpu.sync_copy(x_vmem, o_hbm.at[ib_vmem.at[0]])

        pltpu.emit_pipeline(
            body, grid=(n // SC_WINDOW,),
            in_specs=[pl.BlockSpec((SC_WINDOW, 128), index_map=lambda i: (i, 0)),
                      pl.BlockSpec((1, SC_WINDOW), index_map=lambda i: (i, 0)),
                      pl.BlockSpec((1, SC_WINDOW), index_map=lambda i: (i, 0))],
            out_specs=[],
            core_axis_name=("core", "subcore"), dimension_semantics=(pltpu.PARALLEL,),
        )(x_hbm, ia_hbm, ib_hbm)

    return scatter_kernel(src, idx_a, idx_b)


def _sc_gather2(table, idx_a, idx_b):
    n = idx_a.size
    sds = jax.ShapeDtypeStruct((n, 128), table.dtype)

    @functools.partial(pl.kernel, out_type=(sds, sds), mesh=_sc_mesh(), scratch_types=[])
    def gather_kernel(t_hbm, ia_hbm, ib_hbm, oa_hbm, ob_hbm):
        def body(ia_vmem, ib_vmem, oa_vmem, ob_vmem):
            pltpu.sync_copy(t_hbm.at[ia_vmem.at[0]], oa_vmem)
            pltpu.sync_copy(t_hbm.at[ib_vmem.at[0]], ob_vmem)

        pltpu.emit_pipeline(
            body, grid=(n // SC_WINDOW,),
            in_specs=[pl.BlockSpec((1, SC_WINDOW), index_map=lambda i: (i, 0)),
                      pl.BlockSpec((1, SC_WINDOW), index_map=lambda i: (i, 0))],
            out_specs=[pl.BlockSpec((SC_WINDOW, 128), index_map=lambda i: (i, 0)),
                       pl.BlockSpec((SC_WINDOW, 128), index_map=lambda i: (i, 0))],
            core_axis_name=("core", "subcore"), dimension_semantics=(pltpu.PARALLEL,),
        )(ia_hbm, ib_hbm, oa_hbm, ob_hbm)

    return gather_kernel(table, idx_a, idx_b)


def _dispatch_plan(counts, n, n_tiles):
    cnt = counts[:, 0].astype(jnp.int32)
    blocks = (cnt + EXPERT_TILE - 1) // EXPERT_TILE
    ends = jnp.cumsum(blocks)
    first_tile = ends - blocks
    tile_id = jnp.arange(n_tiles, dtype=jnp.int32)
    tile_expert = jnp.minimum(jnp.sum((tile_id[:, None] >= ends[None, :]).astype(jnp.int32), axis=1), MOE_EXPERTS - 1)
    onehot = (tile_expert[:, None] == jnp.arange(MOE_EXPERTS, dtype=jnp.int32)[None, :]).astype(jnp.int32)
    block_in_expert = tile_id - jnp.sum(onehot * first_tile[None, :], axis=1)
    used = tile_id < ends[-1]
    rows_left = jnp.sum(onehot * cnt[None, :], axis=1) - block_in_expert * EXPERT_TILE
    tile_rows = jnp.where(used, jnp.clip(rows_left, 0, EXPERT_TILE), 0).astype(jnp.int32)
    blocks_per_expert = n // EXPERT_TILE
    tile_block = jnp.where(used, tile_expert * blocks_per_expert + block_in_expert,
                           MOE_EXPERTS * blocks_per_expert).astype(jnp.int32)
    tile_first = (used & (block_in_expert == 0)).astype(jnp.int32)
    return tile_expert, tile_block, tile_rows, tile_first


def _experts_kernel(te_ref, blk_ref, rows_ref, first_ref, xs_ref, wg_ref, wu_ref, wd_ref, y_ref, wgb, wub, wdb):
    i = pl.program_id(0)

    @pl.when(first_ref[i] == 1)
    def _():
        wgb[...] = wg_ref[0, 0].astype(BF16)
        wub[...] = wu_ref[0, 0].astype(BF16)
        wdb[...] = wd_ref[0, 0].astype(BF16)

    def run(n_rows):
        x = _merge_rows(xs_ref, n_rows)
        row = lax.broadcasted_iota(jnp.int32, x.shape, 0)
        x = jnp.where(row < rows_ref[i], x, 0.0).astype(BF16)
        act = _silu(_dot(x, wgb[...])) * _dot(x, wub[...])
        _split_rows(y_ref, _dot(act.astype(BF16), wdb[...]), n_rows)

    half = EXPERT_TILE // 2

    @pl.when(rows_ref[i] > half)
    def _():
        run(EXPERT_TILE)

    @pl.when((rows_ref[i] > 0) & (rows_ref[i] <= half))
    def _():
        run(half)


def _experts(xs, tile_expert, tile_block, tile_rows, tile_first, w_gate, w_up, w_down, layer):
    n_tiles = tile_expert.shape[0]
    _, _, d, ff = w_gate.shape
    rows_blk = pl.BlockSpec((ROW_SPLIT * EXPERT_TILE, 128), lambda i, te, blk, rows, first: (blk[i], 0))
    return pl.pallas_call(
        _experts_kernel,
        grid_spec=pltpu.PrefetchScalarGridSpec(
            num_scalar_prefetch=4,
            grid=(n_tiles,),
            in_specs=[rows_blk,
                      pl.BlockSpec((1, 1, d, ff), lambda i, te, blk, rows, first: (layer, te[i], 0, 0)),
                      pl.BlockSpec((1, 1, d, ff), lambda i, te, blk, rows, first: (layer, te[i], 0, 0)),
                      pl.BlockSpec((1, 1, ff, d), lambda i, te, blk, rows, first: (layer, te[i], 0, 0))],
            out_specs=rows_blk,
            scratch_shapes=[pltpu.VMEM((d, ff), BF16), pltpu.VMEM((d, ff), BF16), pltpu.VMEM((ff, d), BF16)],
        ),
        out_shape=jax.ShapeDtypeStruct(xs.shape, xs.dtype),
        compiler_params=_cparams("arbitrary"),
        name="moe_experts",
    )(tile_expert, tile_block, tile_rows, tile_first, xs, w_gate, w_up, w_down)


def _combine_kernel(x1_ref, g1_ref, g2_ref, route_ref, gf_ref, o_ref, *, tile):
    o_ref[...] = _rms(_moe_combine(x1_ref[...], g1_ref, g2_ref, route_ref[...], tile), gf_ref[...])


def _combine(x1, g1, g2, route, g_final):
    n, d = x1.shape
    tile = min(SEQ_TILE, n)
    return pl.pallas_call(
        functools.partial(_combine_kernel, tile=tile),
        grid=(n // tile,),
        in_specs=[_rows(tile, d), _rows(ROW_SPLIT * tile, 128), _rows(ROW_SPLIT * tile, 128),
                  _rows(tile, ROUTE_LANES), _full((1, d))],
        out_specs=_rows(tile, d),
        out_shape=jax.ShapeDtypeStruct((n, d), F32),
        compiler_params=_cparams("parallel"),
        name="moe_combine",
    )(x1, g1, g2, route, g_final)


def _moe(h2_rows, counts, idx1, idx2, w_gate, w_up, w_down, layer):
    n = h2_rows.shape[0] // ROW_SPLIT
    n_tiles = (MOE_TOPK * n) // EXPERT_TILE + MOE_EXPERTS
    tile_expert, tile_block, tile_rows, tile_first = _dispatch_plan(counts, n, n_tiles)
    xs = _sc_scatter2(h2_rows, idx1, idx2, ROW_SPLIT * (MOE_EXPERTS * n + EXPERT_TILE))
    ys = _experts(xs, tile_expert, tile_block, tile_rows, tile_first, w_gate, w_up, w_down, layer)
    return _sc_gather2(ys, idx1, idx2)


def kernel(x, positions, norm1_g, w_in, w_out, s5_a_re, s5_a_im, s5_log_dt, s5_b_re, s5_b_im, s5_c_re, s5_c_im, s5_d, s5_w_glu, s5_b_glu, m2_conv_w, m2_conv_b, m2_dt_bias, m2_a_log, m2_d, m2_norm_g, hg_lb_logits, hg_norm_g, norm2_g, moe_w_group, moe_b_group, moe_w_expert, moe_b_expert, moe_w_gate, moe_w_up, moe_w_down, final_norm_g):
    bsz, seqlen, d = x.shape
    assert bsz == 1 and seqlen % SEQ_TILE == 0 and seqlen % EXPERT_TILE == 0
    depth = w_in.shape[0]

    lb_probs = jax.nn.softmax(hg_lb_logits.astype(F32), axis=0)
    lower_bounds = (jnp.cumsum(lb_probs, axis=0) - lb_probs[0]).reshape(depth, 1, GROUP_WIDTH)
    cos_t, sin_t = _rope_tables(positions.reshape(seqlen, 1))

    s5_p = _s5_params(s5_a_re, s5_a_im, s5_log_dt, s5_b_re, s5_b_im, s5_c_re, s5_c_im, s5_d)
    m2_p = _mamba2_params(m2_conv_w, m2_conv_b, m2_dt_bias, m2_a_log, m2_d, m2_norm_g)
    w_route = jnp.zeros((depth, d, ROUTE_LANES), F32)
    w_route = w_route.at[:, :, :MOE_GROUPS].set(moe_w_group)
    w_route = w_route.at[:, :, EXPERT_ROW:EXPERT_ROW + MOE_EXPERTS].set(moe_w_expert)
    wr_packed = jnp.concatenate(_split2(w_route), axis=2)
    b_route = jnp.zeros((depth, 1, ROUTE_LANES), F32)
    b_route = b_route.at[:, 0, :MOE_GROUPS].set(moe_b_group)
    b_route = b_route.at[:, 0, EXPERT_ROW:EXPERT_ROW + MOE_EXPERTS].set(moe_b_expert)
    w_glu_b, w_out_b = s5_w_glu.astype(BF16), w_out.astype(BF16)
    b_glu = s5_b_glu.reshape(depth, 1, GROUP_WIDTH)
    g1, g2 = norm1_g.reshape(depth, 1, d), norm2_g.reshape(depth, 1, d)
    hg_g = hg_norm_g.reshape(depth, 1, GROUP_WIDTH)
    g_final = final_norm_g.reshape(1, d)
    tile = min(SEQ_TILE, seqlen)
    chunk = min(CHUNK, tile)

    xc, pending = x.reshape(seqlen, d), None
    for l in range(depth):
        outs = _in_proj(xc, pending, g1, _w_prep(w_in, l), l)
        if pending is not None:
            xc, outs = outs[0], outs[1:]
        u_b, u_f, p_ret, p_m2, p_hg = outs
        y_s5 = _s5(u_b, u_f, s5_p, l)
        y_ret, y_m2, y_hg = _mixers(
            seqlen,
            _retention_call(p_ret, cos_t, sin_t, tile, min(RET_CHUNK, tile)),
            _mamba2_call(p_m2, m2_p, l, tile, chunk),
            _hgrn2_call(p_hg, lower_bounds, hg_g, l, tile, chunk))
        x1, h2_rows, route, counts, idx1, idx2 = _out_proj(xc, y_s5, y_ret, y_m2, y_hg, w_glu_b, b_glu, w_out_b,
                                                           g2, wr_packed, b_route, l)
        rows1, rows2 = _moe(h2_rows, counts, idx1, idx2, moe_w_gate, moe_w_up, moe_w_down, l)
        xc, pending = x1, (rows1, rows2, route)
    xc = _combine(xc, *pending, g_final)
    return xc.reshape(bsz, seqlen, d)
```

```python
import functools
import math

import numpy as np
import jax
import jax.numpy as jnp
from jax import lax
from jax.experimental import pallas as pl
from jax.experimental.pallas import tpu as pltpu
from jax.experimental.pallas import tpu_sc as plsc

F32 = jnp.float32
BF16 = jnp.bfloat16
NORM_EPS = 1e-6

GROUP_WIDTH = 256
HEAD_DIM = 64
N_HEADS = GROUP_WIDTH // HEAD_DIM
S5_GROUPS = 16
S5_CH = 16
S5_STATE = 64
S5_DT_CLAMP = -1e-4
M2_STATE = 128
M2_CONV = 4
M2_CONV_DIM = 768
ROPE_BASE = 10000.0
MOE_GROUPS = 4
MOE_PER_GROUP = 4
MOE_EXPERTS = 16
ROUTE_LANES = 128

SEQ_TILE = 512
CHUNK = 128
RET_CHUNK = 256
CHUNK_UNROLL = 4
S5_CHUNK = 16
MOE_TOPK = 2
EXPERT_TILE = 512
SC_WINDOW = 128
VMEM_LIMIT = 56 * 1024 * 1024


def _cparams(*sem):
    return pltpu.CompilerParams(dimension_semantics=sem, vmem_limit_bytes=VMEM_LIMIT)


def _dot(a, b):
    return jnp.dot(a, b, preferred_element_type=F32)


def _dot_nt(a, b):
    return lax.dot_general(a, b, (((1,), (1,)), ((), ())), preferred_element_type=F32)


def _dot_tn(a, b):
    return lax.dot_general(a, b, (((0,), (0,)), ((), ())), preferred_element_type=F32)


def _split2(x):
    hi = x.astype(BF16)
    return hi, (x - hi.astype(F32)).astype(BF16)


def _split3(x):
    hi = x.astype(BF16)
    r = x - hi.astype(F32)
    mid = r.astype(BF16)
    return hi, mid, (r - mid.astype(F32)).astype(BF16)


def _dot_exact_lhs(m, x):
    hi, mid, lo = _split3(x)
    return _dot(m, hi) + _dot(m, mid) + _dot(m, lo)


def _dot_exact_rhs(x, m):
    hi, lo = _split2(x)
    return _dot(hi, m) + _dot(lo, m)


def _sigmoid(x):
    return 1.0 / (1.0 + jnp.exp(-x))


def _silu(x):
    return x * _sigmoid(x)


def _rms(x, g):
    return x * lax.rsqrt(jnp.mean(x * x, axis=-1, keepdims=True) + NORM_EPS) * g


def _full(shape):
    return pl.BlockSpec(shape, lambda *_: (0,) * len(shape))


def _rows(tile, width):
    return pl.BlockSpec((tile, width), lambda i: (i, 0))


def _pick(shape, layer):
    return pl.BlockSpec((None,) + tuple(shape), lambda *_: (layer,) + (0,) * len(shape))


IN_SEGMENTS = (256, 1024, 1280, 1024)


DT_COL = 9 * GROUP_WIDTH
W_PREP_ROWS = 128


def _w_prep_kernel(w_ref, o_ref):
    o_ref[:, 0:DT_COL] = w_ref[0, :, 0:DT_COL].astype(BF16)
    head = lax.broadcasted_iota(jnp.int32, (W_PREP_ROWS, GROUP_WIDTH), 1) // HEAD_DIM
    rep = jnp.zeros((W_PREP_ROWS, GROUP_WIDTH), F32)
    for h in range(N_HEADS):
        rep = jnp.where(head == h, w_ref[0, :, DT_COL + h:DT_COL + h + 1], rep)
    o_ref[:, DT_COL:DT_COL + GROUP_WIDTH] = rep.astype(BF16)
    o_ref[:, DT_COL + GROUP_WIDTH:] = w_ref[0, :, DT_COL + N_HEADS:].astype(BF16)


def _w_prep(w_in, layer):
    _, d, n_in = w_in.shape
    return pl.pallas_call(
        _w_prep_kernel,
        grid=(d // W_PREP_ROWS,),
        in_specs=[pl.BlockSpec((1, W_PREP_ROWS, n_in), lambda i: (layer, i, 0))],
        out_specs=pl.BlockSpec((W_PREP_ROWS, sum(IN_SEGMENTS)), lambda i: (i, 0)),
        out_shape=jax.ShapeDtypeStruct((d, sum(IN_SEGMENTS)), BF16),
        compiler_params=_cparams("parallel"),
        name="w_in_prep",
    )(w_in)


def _in_proj_kernel(*refs, tile, moe_pending):
    if moe_pending:
        x_ref, g1_ref, g2_ref, route_ref, g_ref, w_ref, xo_ref, ub_ref, uf_ref, ret_ref, m2_ref, hg_ref, u_tmp = refs
        x = _moe_combine(x_ref[...], g1_ref, g2_ref, route_ref[...], tile)
        xo_ref[...] = x
    else:
        x_ref, g_ref, w_ref, ub_ref, uf_ref, ret_ref, m2_ref, hg_ref, u_tmp = refs
        x = x_ref[...]
    hb = _rms(x, g_ref[...]).astype(BF16)
    c0, c1, c2, c3 = np.cumsum(IN_SEGMENTS)
    ret_ref[...] = _dot(hb, w_ref[:, c0:c1])
    m2_ref[...] = _dot(hb, w_ref[:, c1:c2])
    hg_ref[...] = _dot(hb, w_ref[:, c2:c3])
    u = _dot(hb, w_ref[:, 0:c0])
    for j in range(GROUP_WIDTH // 128):
        u_tmp[j] = u[:, 128 * j:128 * (j + 1)]
    for s in range(S5_CHUNK):
        for j in range(GROUP_WIDTH // 128):
            v = u_tmp[j, pl.ds(s, tile // S5_CHUNK, stride=S5_CHUNK), :]
            lanes = slice(GROUP_WIDTH * s + 128 * j, GROUP_WIDTH * s + 128 * (j + 1))
            uf_ref[:, lanes] = v
            ub_ref[:, lanes] = v.astype(BF16)


def _in_proj(x, pending, g, w, layer):
    n, d = x.shape
    tile = min(SEQ_TILE, n)
    fold = S5_CHUNK * GROUP_WIDTH
    moe_specs = [_rows(ROW_SPLIT * tile, 128), _rows(ROW_SPLIT * tile, 128), _rows(tile, ROUTE_LANES)]
    return pl.pallas_call(
        functools.partial(_in_proj_kernel, tile=tile, moe_pending=pending is not None),
        grid=(n // tile,),
        in_specs=[_rows(tile, d)] + (moe_specs if pending else []) + [_pick((1, d), layer), _full(w.shape)],
        out_specs=([_rows(tile, d)] if pending else []) + [_rows(tile // S5_CHUNK, fold)] * 2
        + [_rows(tile, s) for s in IN_SEGMENTS[1:]],
        out_shape=([jax.ShapeDtypeStruct((n, d), F32)] if pending else [])
        + [jax.ShapeDtypeStruct((n // S5_CHUNK, fold), BF16), jax.ShapeDtypeStruct((n // S5_CHUNK, fold), F32)]
        + [jax.ShapeDtypeStruct((n, s), F32) for s in IN_SEGMENTS[1:]],
        scratch_shapes=[pltpu.VMEM((GROUP_WIDTH // 128, tile, 128), F32)],
        compiler_params=_cparams("parallel"),
        name="in_proj",
    )(x, *(pending or ()), g, w)


def _rope_kernel(pos_ref, invf_ref, cos_ref, sin_ref):
    ang = pos_ref[...].astype(F32) * invf_ref[...]
    cos_ref[...] = jnp.cos(ang)
    sin_ref[...] = jnp.sin(ang)


def _rope_tables(positions):
    n = positions.shape[0]
    tile = min(SEQ_TILE, n)
    half = HEAD_DIM // 2
    inv_freq = ROPE_BASE ** (-jnp.arange(half, dtype=F32) / half)
    invf = jnp.tile(inv_freq, 128 // half).reshape(1, 128)
    return pl.pallas_call(
        _rope_kernel,
        grid=(n // tile,),
        in_specs=[_rows(tile, 1), _full((1, 128))],
        out_specs=[_rows(tile, 128), _rows(tile, 128)],
        out_shape=[jax.ShapeDtypeStruct((n, 128), F32)] * 2,
        compiler_params=_cparams("parallel"),
        name="rope_tables",
    )(positions, invf)


def _head_mean_matrix():
    h = np.arange(GROUP_WIDTH) // HEAD_DIM
    return jnp.asarray((h[:, None] == h[None, :]) / HEAD_DIM, dtype=BF16)


def _head_block_mask():
    h = np.arange(GROUP_WIDTH) // HEAD_DIM
    return jnp.asarray(h[:, None] == h[None, :], dtype=F32)


HEAD_PAIRS = GROUP_WIDTH // 128


def _lanes(x, j):
    return x[:, 128 * j:128 * (j + 1)]


def _stack_pair(x):
    xb = x.astype(BF16)
    low = lax.broadcasted_iota(jnp.int32, x.shape, 1) < HEAD_DIM
    zero = jnp.zeros_like(xb)
    return jnp.concatenate([jnp.where(low, xb, zero), jnp.where(low, zero, xb)], axis=0)


def _pair_scores(q, kb):
    s = _dot_nt(_stack_pair(q), kb)
    return s[:q.shape[0]], s[q.shape[0]:]


def _pair_apply(a0, a1, v):
    return _dot(jnp.concatenate([a0.astype(BF16), a1.astype(BF16)], axis=1), _stack_pair(v))


def _ret_constants(chunk):
    lg = np.log1p(-np.exp2(-5.0 - np.arange(N_HEADS, dtype=np.float64)))
    idx = np.arange(chunk, dtype=np.float64)
    rel = idx[:, None] - idx[None, :]
    decay = np.where(rel >= 0, np.exp(np.maximum(rel, 0.0)[None] * lg[:, None, None]), 0.0)
    lane_lg = np.repeat(lg, HEAD_DIM)
    xi = np.exp((idx + 1.0)[:, None] * lane_lg[None, :])
    zeta = np.exp((chunk - 1.0 - idx)[:, None] * lane_lg[None, :])
    h = np.arange(GROUP_WIDTH) // HEAD_DIM
    gc = np.where(h[:, None] == h[None, :], np.exp(chunk * lane_lg)[:, None], 0.0)
    f = lambda a: jnp.asarray(a, dtype=F32)
    return f(decay), f(xi), f(zeta), f(gc)


def _ret_kernel(p_ref, cos_ref, sin_ref, dec_ref, xi_ref, zeta_ref, gc_ref, bm_ref, gm_ref,
                o_ref, r_ref, *, chunk, n_chunks):
    lane = lax.broadcasted_iota(jnp.int32, (chunk, GROUP_WIDTH), 1)
    first_half = (lane % HEAD_DIM) < (HEAD_DIM // 2)
    gm = gm_ref[...]

    def rope(t, cos2, sin2):
        rot = jnp.where(first_half, -pltpu.roll(t, GROUP_WIDTH - HEAD_DIM // 2, 1),
                        pltpu.roll(t, HEAD_DIM // 2, 1))
        return t * cos2 + rot * sin2

    def body(c, carry):
        rows = pl.ds(pl.multiple_of(c * chunk, chunk), chunk)
        cs = cos_ref[rows, :]
        sn = sin_ref[rows, :]
        cos2 = jnp.concatenate([cs, cs], axis=1)
        sin2 = jnp.concatenate([sn, sn], axis=1)
        q = rope(p_ref[rows, 0:256], cos2, sin2)
        k = rope(p_ref[rows, 256:512], cos2, sin2) * (HEAD_DIM ** -0.5)
        v = p_ref[rows, 512:768]
        g = p_ref[rows, 768:1024]
        kb = k.astype(BF16)
        inner = []
        for j in range(HEAD_PAIRS):
            s0, s1 = _pair_scores(_lanes(q, j), _lanes(kb, j))
            inner.append(_pair_apply(s0 * dec_ref[2 * j], s1 * dec_ref[2 * j + 1], _lanes(v, j)))
        inner = jnp.concatenate(inner, axis=1)
        r_prev = r_ref[...]
        cross = _dot((q * xi_ref[...]).astype(BF16), r_prev.astype(BF16))
        r_ref[...] = gc_ref[...] * r_prev + bm_ref[...] * _dot_tn(kb, (zeta_ref[...] * v).astype(BF16))
        o = inner + cross
        cen = o - _dot_exact_rhs(o, gm)
        var = _dot_exact_rhs(cen * cen, gm)
        o_ref[rows, :] = (cen * lax.rsqrt(var + NORM_EPS) * _silu(g)).astype(BF16)
        return carry

    lax.fori_loop(0, n_chunks, body, 0, unroll=CHUNK_UNROLL)


def _retention_call(proj, cos_t, sin_t, tile, chunk):
    decay, xi, zeta, gc = _ret_constants(chunk)
    body = functools.partial(_ret_kernel, chunk=chunk, n_chunks=tile // chunk)
    operands = (proj, cos_t, sin_t, decay, xi, zeta, gc, _head_block_mask(), _head_mean_matrix())
    specs = [_rows(tile, 1024), _rows(tile, 128), _rows(tile, 128), _full(decay.shape), _full(xi.shape),
             _full(zeta.shape), _full(gc.shape), _full((256, 256)), _full((256, 256))]
    return body, operands, specs, [pltpu.VMEM((GROUP_WIDTH, GROUP_WIDTH), F32)]


def _tri_matrix(chunk):
    i = np.arange(chunk)
    return jnp.asarray(i[:, None] >= i[None, :], dtype=BF16)


def _m2_kernel(p_ref, cw_ref, cb_ref, dtb_ref, alog_ref, d_ref, ng_ref, tri_ref,
               o_ref, tail_ref, ext_ref, act_ref, st_ref, *, tile, chunk, n_chunks):
    ext_ref[0:8, :] = tail_ref[...]
    ext_ref[8:tile + 8, :] = p_ref[:, 256:1024]
    tail_ref[...] = p_ref[tile - 8:tile, 256:1024]
    conv = cb_ref[...]
    for j in range(M2_CONV):
        lo = 8 - (M2_CONV - 1) + j
        conv = conv + cw_ref[j:j + 1, :] * ext_ref[lo:lo + tile, :]
    act_ref[...] = _silu(conv)
    a_lane = -jnp.exp(alog_ref[...])
    tri = tri_ref[...]
    ti = lax.broadcasted_iota(jnp.int32, (chunk, chunk), 0)
    si = lax.broadcasted_iota(jnp.int32, (chunk, chunk), 1)
    causal = ti >= si
    lane = lax.broadcasted_iota(jnp.int32, (chunk, 128), 1)

    def body(c, carry):
        start = pl.multiple_of(c * chunk, chunk)
        rows = pl.ds(start, chunk)
        xbc = act_ref[rows, :]
        xs = xbc[:, 0:256]
        z = p_ref[rows, 0:256]
        x_dt = p_ref[rows, 1024:1280] + dtb_ref[...]
        dt = jnp.maximum(x_dt, 0.0) + jnp.log1p(jnp.exp(-jnp.abs(x_dt)))
        acum = _dot_exact_lhs(tri, dt * a_lane)
        acum_t = acum.T
        a_last = acum[chunk - 1:chunk, :]
        e_acum = jnp.exp(acum)
        decs = jnp.exp(a_last - acum)
        d_chunk = jnp.exp(a_last)
        xc = xs * dt
        ys = []
        for g in range(2):
            sl = slice(128 * g, 128 * (g + 1))
            bmg = xbc[:, 256 + 128 * g:256 + 128 * (g + 1)].astype(BF16)
            cmg = xbc[:, 512 + 128 * g:512 + 128 * (g + 1)].astype(BF16)
            cb = _dot_nt(cmg, bmg)
            xcg = xc[:, sl]
            yd = jnp.zeros((chunk, 128), F32)
            for hh in range(2):
                col0 = 128 * g + HEAD_DIM * hh
                diff = acum[:, col0:col0 + 1] - acum_t[col0:col0 + 1, :]
                lm = jnp.where(causal, jnp.exp(jnp.where(causal, diff, 0.0)), 0.0)
                xm = jnp.where((lane // HEAD_DIM) == hh, xcg, 0.0).astype(BF16)
                yd = yd + _dot((cb * lm).astype(BF16), xm)
            st = st_ref[:, sl]
            y_off = _dot(cmg, st.astype(BF16)) * e_acum[:, sl]
            st_ref[:, sl] = d_chunk[:, sl] * st + _dot_tn(bmg, (xcg * decs[:, sl]).astype(BF16))
            ys.append(yd + y_off + d_ref[:, sl] * xs[:, sl])
        y = jnp.concatenate(ys, axis=1) * _silu(z)
        o_ref[rows, :] = _rms(y, ng_ref[...]).astype(BF16)
        return carry

    lax.fori_loop(0, n_chunks, body, 0, unroll=CHUNK_UNROLL)


def _mamba2_params(conv_w, conv_b, dt_bias, a_log, d_skip, norm_g):
    depth = conv_w.shape[0]
    lanes = lambda v: jnp.repeat(v, HEAD_DIM, axis=1).reshape(depth, 1, GROUP_WIDTH)
    return (conv_w, conv_b.reshape(depth, 1, -1), lanes(dt_bias), lanes(a_log), lanes(d_skip),
            norm_g.reshape(depth, 1, -1))


def _mamba2_call(proj, params, layer, tile, chunk):
    body = functools.partial(_m2_kernel, tile=tile, chunk=chunk, n_chunks=tile // chunk)
    operands = (proj,) + tuple(params) + (_tri_matrix(chunk),)
    specs = ([_rows(tile, 1280), _pick((M2_CONV, M2_CONV_DIM), layer), _pick((1, M2_CONV_DIM), layer)]
             + [_pick((1, GROUP_WIDTH), layer)] * 4 + [_full((chunk, chunk))])
    scratch = [pltpu.VMEM((8, M2_CONV_DIM), F32), pltpu.VMEM((tile + 8, M2_CONV_DIM), F32),
               pltpu.VMEM((tile, M2_CONV_DIM), F32), pltpu.VMEM((M2_STATE, GROUP_WIDTH), F32)]
    return body, operands, specs, scratch


HG_MATMUL_LEVELS = 3


def _hg_exponent_matrix(chunk):
    levels = HG_MATMUL_LEVELS
    t = np.arange(chunk)[:, None]
    r = np.arange(chunk)[None, :]
    blocks = []
    for lvl in range(levels):
        b = 1 << lvl
        blk = t // b
        odd = (blk % 2) == 1
        q_rows = odd & (r >= blk * b) & (r <= t)
        k_rows = (~odd) & (r > t) & (r <= (blk + 1) * b - 1)
        blocks.append(q_rows | k_rows)
    blocks.append(r <= t)
    return jnp.asarray(np.concatenate(blocks, axis=0), dtype=BF16)


def _hg_kernel(p_ref, lb_ref, ng_ref, gexp_ref, bm_ref, gm_ref, o_ref, st_ref, *, chunk, n_chunks):
    levels = int(math.log2(chunk))
    row = lax.broadcasted_iota(jnp.int32, (chunk, 128), 0)
    odd_rows = [((row >> lvl) & 1) == 1 for lvl in range(levels)]
    row_wide = lax.broadcasted_iota(jnp.int32, (chunk, GROUP_WIDTH), 0)
    odd_rows_wide = [((row_wide >> lvl) & 1) == 1 for lvl in range(levels)]
    ti = lax.broadcasted_iota(jnp.int32, (chunk, chunk), 0)
    si = lax.broadcasted_iota(jnp.int32, (chunk, chunk), 1)
    pair_level = [((ti >> (lvl + 1)) == (si >> (lvl + 1))) & (((ti >> lvl) & 1) == 1) & (((si >> lvl) & 1) == 0)
                  for lvl in range(levels)]
    lb = lb_ref[...]
    gm = gm_ref[...]

    def body(c, carry):
        rows = pl.ds(pl.multiple_of(c * chunk, chunk), chunk)
        q = _silu(p_ref[rows, 0:256])
        forget = lb + (1.0 - lb) * _sigmoid(p_ref[rows, 256:512])
        k = 1.0 - forget
        v = p_ref[rows, 512:768]
        g = p_ref[rows, 768:1024]
        lf_hi, lf_lo = _split2(jnp.log(forget))
        expo = _dot(gexp_ref[...], lf_hi) + _dot(gexp_ref[...], lf_lo)
        bcum = expo[HG_MATMUL_LEVELS * chunk:(HG_MATMUL_LEVELS + 1) * chunk, :]

        def level_log_decay(lvl):
            if lvl < HG_MATMUL_LEVELS:
                return expo[lvl * chunk:(lvl + 1) * chunk, :]
            b = 1 << lvl
            ref = jnp.concatenate([jnp.broadcast_to(bcum[m + b - 1:m + b, :], (2 * b, GROUP_WIDTH))
                                   for m in range(0, chunk, 2 * b)], axis=0)
            return jnp.where(odd_rows_wide[lvl], bcum - ref, ref - bcum)


        log_decay = [level_log_decay(lvl) for lvl in range(levels)]
        intra = []
        for j in range(HEAD_PAIRS):
            qj, kj = _lanes(q, j), _lanes(k, j)
            a0, a1 = (jnp.where(ti == si, s, 0.0) for s in _pair_scores(qj, kj.astype(BF16)))
            for lvl in range(levels):
                w = jnp.exp(_lanes(log_decay[lvl], j)) * jnp.where(odd_rows[lvl], qj, kj)
                s0, s1 = _pair_scores(w, w.astype(BF16))
                a0 = jnp.where(pair_level[lvl], s0, a0)
                a1 = jnp.where(pair_level[lvl], s1, a1)
            intra.append(_pair_apply(a0, a1, _lanes(v, j)))
        intra = jnp.concatenate(intra, axis=1)

        b_last = bcum[chunk - 1:chunk, :]
        suffix = b_last - bcum
        st = st_ref[...]
        cross = _dot_nt((q * jnp.exp(bcum)).astype(BF16), st.astype(BF16))
        st_ref[...] = jnp.exp(b_last) * st + bm_ref[...] * _dot_tn(
            v.astype(BF16), (k * jnp.exp(suffix)).astype(BF16))
        o = intra + cross
        o = o * lax.rsqrt(_dot_exact_rhs(o * o, gm) + NORM_EPS) * ng_ref[...]
        o_ref[rows, :] = (o * _silu(g)).astype(BF16)
        return carry

    lax.fori_loop(0, n_chunks, body, 0, unroll=CHUNK_UNROLL)


def _hgrn2_call(proj, lower_bounds, norm_g, layer, tile, chunk):
    gexp = _hg_exponent_matrix(chunk)
    body = functools.partial(_hg_kernel, chunk=chunk, n_chunks=tile // chunk)
    operands = (proj, lower_bounds, norm_g, gexp, _head_block_mask(), _head_mean_matrix())
    specs = [_rows(tile, 1024), _pick((1, GROUP_WIDTH), layer), _pick((1, GROUP_WIDTH), layer), _full(gexp.shape),
             _full((256, 256)), _full((256, 256))]
    return body, operands, specs, [pltpu.VMEM((GROUP_WIDTH, GROUP_WIDTH), F32)]


def _mixers_kernel(*refs, bodies, n_in, n_scratch):
    n_mix = len(bodies)
    ins, pos = [], 0
    for k in n_in:
        ins.append(refs[pos:pos + k])
        pos += k
    outs = refs[pos:pos + n_mix]
    pos += n_mix
    scratch = []
    for k in n_scratch:
        scratch.append(refs[pos:pos + k])
        pos += k

    @pl.when(pl.program_id(0) == 0)
    def _():
        for group in scratch:
            for ref in group:
                ref[...] = jnp.zeros_like(ref)

    for body, i, o, s in zip(bodies, ins, outs, scratch):
        body(*i, o, *s)


def _mixers(n, *calls):
    tile = min(SEQ_TILE, n)
    bodies = tuple(c[0] for c in calls)
    return pl.pallas_call(
        functools.partial(_mixers_kernel, bodies=bodies, n_in=tuple(len(c[1]) for c in calls),
                          n_scratch=tuple(len(c[3]) for c in calls)),
        grid=(n // tile,),
        in_specs=[s for c in calls for s in c[2]],
        out_specs=[_rows(tile, GROUP_WIDTH)] * len(calls),
        out_shape=[jax.ShapeDtypeStruct((n, GROUP_WIDTH), BF16)] * len(calls),
        scratch_shapes=[s for c in calls for s in c[3]],
        compiler_params=_cparams("arbitrary"),
        name="mixers",
    )(*[a for c in calls for a in c[1]])


S5_LANES = S5_GROUPS * 2 * S5_STATE
S5_TAP_SPLITS = 4
S5_STATE_GROUP = 4


def _s5_rows(a_re, a_im, log_dt):
    are = jnp.minimum(a_re, S5_DT_CLAMP)
    dt = jnp.exp(log_dt)
    lam_re = are * dt
    lam_im = a_im * dt
    mag = jnp.exp(lam_re)
    ab_re = mag * jnp.cos(lam_im)
    ab_im = mag * jnp.sin(lam_im)
    den = are * are + a_im * a_im
    k_re = ((ab_re - 1.0) * are + ab_im * a_im) / den
    k_im = (ab_im * are - (ab_re - 1.0) * a_im) / den
    return lam_re, lam_im, k_re, k_im


def _s5_power(lam_re, lam_im, e):
    m = jnp.exp(e * lam_re)
    return m * jnp.cos(e * lam_im), m * jnp.sin(e * lam_im)


def _s5_state_kernel(u_ref, are_ref, aim_ref, ldt_ref, b1_ref, b2_ref, ca_ref, k_ref, sp_ref,
                     bb1_ref, bb2_ref, inc_ref, w_ref, *, n_chunks):
    g = pl.program_id(0)
    lam_re, lam_im, k_re, k_im = _s5_rows(are_ref[...], aim_ref[...], ldt_ref[...])

    @pl.when(g == 0)
    def _():
        bb1_ref[...] = k_re * b1_ref[...] + k_im * b2_ref[...]
        bb2_ref[...] = k_re * b2_ref[...] - k_im * b1_ref[...]

    ca = ca_ref[...].astype(BF16)
    for j in range(S5_STATE_GROUP):
        s = g * S5_STATE_GROUP + j
        p_re, p_im = _s5_power(lam_re, lam_im, (S5_CHUNK - 1 - s).astype(F32))
        w = (p_re * bb1_ref[...] + p_im * bb2_ref[...]).astype(BF16)
        k_ref[j] = _dot_nt(w, ca).astype(BF16)
        w_ref[GROUP_WIDTH * j:GROUP_WIDTH * (j + 1), :] = w
    contrib = _dot(u_ref[...], w_ref[...])

    @pl.when(g == 0)
    def _():
        inc_ref[...] = contrib

    @pl.when(g > 0)
    def _():
        inc_ref[...] += contrib

    @pl.when(g == S5_CHUNK // S5_STATE_GROUP - 1)
    def _():
        half = S5_LANES // 2
        group = 8
        step = lax.broadcasted_iota(jnp.int32, (group, S5_LANES), 0)
        a_re_all, a_im_all = _s5_power(lam_re, lam_im, ((step + 1) * S5_CHUNK).astype(F32))
        row = lax.broadcasted_iota(jnp.int32, (n_chunks, 128), 0)

        def shifted(x, sh, within):
            keep = ((row % group) >= sh) if within else (row >= sh)
            return jnp.where(keep, pltpu.roll(x, sh, 0), 0.0)

        for j in range(half // 128):
            re_l, im_l = slice(128 * j, 128 * (j + 1)), slice(half + 128 * j, half + 128 * (j + 1))
            x_re, x_im = inc_ref[:, re_l], inc_ref[:, im_l]
            for sh in (1, 2, 4):
                a_re, a_im = a_re_all[sh - 1:sh, re_l], a_im_all[sh - 1:sh, re_l]
                p_re, p_im = shifted(x_re, sh, True), shifted(x_im, sh, True)
                x_re, x_im = x_re + a_re * p_re - a_im * p_im, x_im + a_re * p_im + a_im * p_re
            inc_ref[:, re_l] = x_re
            inc_ref[:, im_l] = x_im

        g_re, g_im = a_re_all[:, 0:half], a_im_all[:, 0:half]

        def carry_in(v, carry):
            c_re, c_im = carry
            rows = pl.ds(pl.multiple_of(v * group, group), group)
            x_re = inc_ref[rows, 0:half] + g_re * c_re - g_im * c_im
            x_im = inc_ref[rows, half:] + g_re * c_im + g_im * c_re
            inc_ref[rows, 0:half] = x_re
            inc_ref[rows, half:] = x_im
            return x_re[group - 1:group, :], x_im[group - 1:group, :]

        zero = jnp.zeros((1, half), F32)
        lax.fori_loop(0, n_chunks // group, carry_in, (zero, zero))
        for j in range(S5_LANES // 128):
            lanes = slice(128 * j, 128 * (j + 1))
            sp_ref[:, lanes] = shifted(inc_ref[:, lanes], 1, False).astype(BF16)


def _s5_out_kernel(ub_ref, uf_ref, k_ref, sp_ref, are_ref, aim_ref, ldt_ref, ca_ref, cb_ref, d_ref,
                   y_ref, taps_ref):
    t = pl.program_id(0)
    fold = S5_CHUNK * GROUP_WIDTH

    @pl.when(t == 0)
    def _():
        for j in range(S5_CHUNK):
            taps_ref[GROUP_WIDTH * j:GROUP_WIDTH * (j + 1), :] = k_ref[j]
        taps_ref[fold:, :] = jnp.zeros((fold - GROUP_WIDTH, GROUP_WIDTH), BF16)

    lam_re, lam_im, _, _ = _s5_rows(are_ref[...], aim_ref[...], ldt_ref[...])
    p_re, p_im = _s5_power(lam_re, lam_im, (t + 1).astype(F32))
    w_out = (p_re * ca_ref[...] + p_im * cb_ref[...]).astype(BF16)
    start = pl.multiple_of((S5_CHUNK - 1 - t) * GROUP_WIDTH, GROUP_WIDTH)
    y_ref[...] = _dot_nt(sp_ref[...], w_out) + d_ref[...] * uf_ref[...]
    quarter = S5_CHUNK // S5_TAP_SPLITS
    for part in range(S5_TAP_SPLITS):
        width = (part + 1) * quarter * GROUP_WIDTH

        @pl.when((t >= part * quarter) & (t < (part + 1) * quarter))
        def _():
            y_ref[...] += _dot(ub_ref[:, 0:width], taps_ref[pl.ds(start, width), :])


def _s5_embed(re, im):
    eye = jnp.eye(S5_GROUPS, dtype=F32)
    blocks = [(eye[None, :, None, :, None] * x[:, :, :, None, :]).reshape(-1, S5_GROUPS * S5_CH, S5_LANES // 2)
              for x in (re, im)]
    return jnp.concatenate(blocks, axis=2)


def _s5_params(a_re, a_im, log_dt, b_re, b_im, c_re, c_im, d_skip):
    depth = a_re.shape[0]
    row = lambda v: jnp.tile(v.reshape(depth, 1, S5_LANES // 2), (1, 1, 2))
    bt_re, bt_im = b_re.transpose(0, 1, 3, 2), b_im.transpose(0, 1, 3, 2)
    return dict(are=row(a_re), aim=row(a_im), ldt=row(jnp.repeat(log_dt, S5_STATE, axis=1)),
                b1=_s5_embed(bt_re, bt_im), b2=_s5_embed(-bt_im, bt_re),
                ca=_s5_embed(c_re, -c_im), cb=_s5_embed(-c_im, -c_re),
                d=d_skip.reshape(depth, 1, GROUP_WIDTH))


def _s5(u_b, u_f, p, layer):
    n_chunks, fold = u_b.shape
    row_spec, mat_spec = _pick((1, S5_LANES), layer), _pick((GROUP_WIDTH, S5_LANES), layer)
    col = lambda: pl.BlockSpec((n_chunks, GROUP_WIDTH), lambda s: (0, s))
    group_cols = S5_STATE_GROUP * GROUP_WIDTH
    taps, s_prev = pl.pallas_call(
        functools.partial(_s5_state_kernel, n_chunks=n_chunks),
        grid=(S5_CHUNK // S5_STATE_GROUP,),
        in_specs=[pl.BlockSpec((n_chunks, group_cols), lambda g: (0, g))] + [row_spec] * 3 + [mat_spec] * 3,
        out_specs=[pl.BlockSpec((S5_STATE_GROUP, GROUP_WIDTH, GROUP_WIDTH), lambda g: (g, 0, 0)),
                   _full((n_chunks, S5_LANES))],
        out_shape=[jax.ShapeDtypeStruct((S5_CHUNK, GROUP_WIDTH, GROUP_WIDTH), BF16),
                   jax.ShapeDtypeStruct((n_chunks, S5_LANES), BF16)],
        scratch_shapes=[pltpu.VMEM((GROUP_WIDTH, S5_LANES), F32), pltpu.VMEM((GROUP_WIDTH, S5_LANES), F32),
                        pltpu.VMEM((n_chunks, S5_LANES), F32), pltpu.VMEM((group_cols, S5_LANES), BF16)],
        compiler_params=_cparams("arbitrary"),
        name="s5_state",
    )(u_b, p['are'], p['aim'], p['ldt'], p['b1'], p['b2'], p['ca'])
    return pl.pallas_call(
        _s5_out_kernel,
        grid=(S5_CHUNK,),
        in_specs=[_full((n_chunks, fold)), col(), _full(taps.shape), _full(s_prev.shape)]
        + [row_spec] * 3 + [mat_spec] * 2 + [_pick((1, GROUP_WIDTH), layer)],
        out_specs=col(),
        out_shape=jax.ShapeDtypeStruct((n_chunks, fold), F32),
        scratch_shapes=[pltpu.VMEM(((2 * S5_CHUNK - 1) * GROUP_WIDTH, GROUP_WIDTH), BF16)],
        compiler_params=_cparams("arbitrary"),
        name="s5_out",
    )(u_b, u_f, taps, s_prev, p['are'], p['aim'], p['ldt'], p['ca'], p['cb'], p['d'])


EXPERT_ROW = 8


def _route(logits_t):
    tokens = logits_t.shape[1]
    big = jnp.int32(1 << 20)
    neg = jnp.float32(-jnp.inf)
    g_row = lax.broadcasted_iota(jnp.int32, (8, tokens), 0)
    is_group = g_row < MOE_GROUPS
    gl = jnp.where(is_group, logits_t[0:8, :], neg)
    ge = jnp.where(is_group, jnp.exp(gl - jnp.max(gl, axis=0, keepdims=True)), 0.0)
    gp = ge / jnp.sum(ge, axis=0, keepdims=True)
    p_g = jnp.max(gp, axis=0, keepdims=True)
    g_idx = jnp.min(jnp.where(is_group & (gp == p_g), g_row, big), axis=0, keepdims=True)
    e_row = lax.broadcasted_iota(jnp.int32, (MOE_EXPERTS, tokens), 0)
    in_group = (e_row // MOE_PER_GROUP) == g_idx
    el = jnp.where(in_group, logits_t[EXPERT_ROW:EXPERT_ROW + MOE_EXPERTS, :], neg)
    ee = jnp.where(in_group, jnp.exp(el - jnp.max(el, axis=0, keepdims=True)), 0.0)
    ep = ee / jnp.sum(ee, axis=0, keepdims=True)
    p1 = jnp.max(jnp.where(in_group, ep, -1.0), axis=0, keepdims=True)
    i1 = jnp.min(jnp.where(in_group & (ep == p1), e_row, big), axis=0, keepdims=True)
    rest = in_group & (e_row != i1)
    p2 = jnp.max(jnp.where(rest, ep, -1.0), axis=0, keepdims=True)
    i2 = jnp.min(jnp.where(rest & (ep == p2), e_row, big), axis=0, keepdims=True)
    tot = p1 + p2
    return i1, i2, p_g * p1 / tot, p_g * p2 / tot


INDEX_DIGIT_BITS = 6
ROW_SPLIT = 4
ROUTE_E1, ROUTE_E2, ROUTE_R1, ROUTE_R2, ROUTE_W1, ROUTE_W2 = range(6)
HIGH_HALF = 0xFFFF0000


def _split_rows(ref, value, rows):
    half = value.shape[1] // 2
    lo = lax.bitcast_convert_type(value[:, :half].astype(jnp.bfloat16).astype(F32), jnp.uint32)
    hi = lax.bitcast_convert_type(value[:, half:].astype(jnp.bfloat16).astype(F32), jnp.uint32)
    words = lax.bitcast_convert_type((lo >> 16) | (hi & jnp.uint32(HIGH_HALF)), jnp.int32)
    for j in range(ROW_SPLIT):
        ref[pl.ds(j, rows, stride=ROW_SPLIT), :] = words[:, 128 * j:128 * (j + 1)]


def _merge_rows(ref, rows):
    words = jnp.concatenate([ref[pl.ds(j, rows, stride=ROW_SPLIT), :] for j in range(ROW_SPLIT)], axis=1)
    words = lax.bitcast_convert_type(words, jnp.uint32)
    lo = lax.bitcast_convert_type(words << 16, F32)
    hi = lax.bitcast_convert_type(words & jnp.uint32(HIGH_HALF), F32)
    return jnp.concatenate([lo, hi], axis=1)


def _moe_combine(x1, g1_ref, g2_ref, route, rows):
    return (x1 + route[:, ROUTE_W1:ROUTE_W1 + 1] * _merge_rows(g1_ref, rows)
            + route[:, ROUTE_W2:ROUTE_W2 + 1] * _merge_rows(g2_ref, rows))


def _out_kernel(x_ref, s5_ref, ret_ref, m2_ref, hg_ref, wglu_ref, bglu_ref, wo_ref, g2_ref,
                wrh_ref, br_ref, stri_ref, spread_ref, x1_ref, h2_ref, route_ref, cnt_ref, i1_ref, i2_ref,
                s5_tmp, carry_ref, *, tile, n_tokens):
    @pl.when(pl.program_id(0) == 0)
    def _():
        carry_ref[...] = jnp.zeros_like(carry_ref)

    for s in range(S5_CHUNK):
        for j in range(GROUP_WIDTH // 128):
            lanes = slice(GROUP_WIDTH * s + 128 * j, GROUP_WIDTH * s + 128 * (j + 1))
            s5_tmp[j, pl.ds(s, tile // S5_CHUNK, stride=S5_CHUNK), :] = s5_ref[:, lanes]
    y = jnp.concatenate([s5_tmp[j] for j in range(GROUP_WIDTH // 128)], axis=1)
    y = y * (0.5 * (1.0 + jnp.tanh(math.sqrt(2.0 / math.pi) * (y + 0.044715 * (y * y * y)))))
    y = y * _sigmoid(_dot(y.astype(BF16), wglu_ref[...]) + bglu_ref[...])
    acc = x_ref[...] + _dot(y.astype(BF16), wo_ref[0:256, :])
    acc = acc + _dot(ret_ref[...], wo_ref[256:512, :])
    acc = acc + _dot(m2_ref[...], wo_ref[512:768, :])
    acc = acc + _dot(hg_ref[...], wo_ref[768:1024, :])
    x1_ref[...] = acc
    h2 = _rms(acc, g2_ref[...])
    _split_rows(h2_ref, h2, tile)
    hi, lo = _split2(h2)
    hw = _dot(hi, wrh_ref[...])
    logits = (hw[:, :ROUTE_LANES] + hw[:, ROUTE_LANES:] + _dot(lo, wrh_ref[:, :ROUTE_LANES])) + br_ref[...]
    e1, e2, w1, w2 = _route(logits.T)
    e_row = lax.broadcasted_iota(jnp.int32, (MOE_EXPERTS, tile), 0)
    picked = jnp.where((e_row == e1) | (e_row == e2), 1.0, 0.0)
    rank = carry_ref[:, 0:1] + _dot_nt(picked.astype(BF16), stri_ref[...])
    r1 = jnp.sum(jnp.where(e_row == e1, rank, 0.0), axis=0, keepdims=True)
    r2 = jnp.sum(jnp.where(e_row == e2, rank, 0.0), axis=0, keepdims=True)
    carry_ref[...] += jnp.sum(picked, axis=1, keepdims=True)
    cnt_ref[...] = carry_ref[...]
    rec_row = lax.broadcasted_iota(jnp.int32, (ROUTE_LANES, tile), 0)
    rec = jnp.zeros((ROUTE_LANES, tile), F32)
    for col, val in ((ROUTE_E1, e1.astype(F32)), (ROUTE_E2, e2.astype(F32)), (ROUTE_R1, r1), (ROUTE_R2, r2),
                     (ROUTE_W1, w1), (ROUTE_W2, w2)):
        rec = jnp.where(rec_row == col, val, rec)
    route_ref[...] = rec.T
    digit_row = lax.broadcasted_iota(jnp.int32, (8, tile), 0)
    digits = jnp.zeros((8, tile), F32)
    for slot, (e, r) in enumerate(((e1, r1), (e2, r2))):
        pos = e.astype(F32) * float(n_tokens) + r
        for k, shift in enumerate((2 * INDEX_DIGIT_BITS, INDEX_DIGIT_BITS, 0)):
            digit = jnp.floor(pos * (1.0 / (1 << shift)))
            pos = pos - digit * float(1 << shift)
            digits = jnp.where(digit_row == 3 * slot + k, digit, digits)
    sub = lax.broadcasted_iota(jnp.int32, (1, 128 * ROW_SPLIT), 1) % ROW_SPLIT
    for q in range(tile // 128):
        o = _dot(digits[:, 128 * q:128 * (q + 1)].astype(BF16), spread_ref[...])
        for slot, out_ref in enumerate((i1_ref, i2_ref)):
            moved = (float(1 << (2 * INDEX_DIGIT_BITS)) * o[3 * slot:3 * slot + 1]
                     + float(1 << INDEX_DIGIT_BITS) * o[3 * slot + 1:3 * slot + 2] + o[3 * slot + 2:3 * slot + 3])
            idx = moved.astype(jnp.int32) * ROW_SPLIT + sub
            for j in range(ROW_SPLIT):
                row = ROW_SPLIT * q + j
                out_ref[row:row + 1, :] = idx[:, 128 * j:128 * (j + 1)]


def _out_proj(x, y_s5, y_ret, y_m2, y_hg, w_glu, b_glu, w_out, g2, wr_packed, b_route, layer):
    n, d = x.shape
    tile = min(SEQ_TILE, n)
    i = np.arange(tile)
    strict_lower = jnp.asarray(i[:, None] > i[None, :], dtype=BF16)
    assert MOE_EXPERTS * n <= (1 << (3 * INDEX_DIGIT_BITS))
    per_row = 128 // ROW_SPLIT
    out_rows = tile // per_row
    lanes = np.arange(128 * ROW_SPLIT)
    spread = jnp.asarray(np.arange(128)[:, None] == (lanes // ROW_SPLIT)[None, :], dtype=BF16)
    return pl.pallas_call(
        functools.partial(_out_kernel, tile=tile, n_tokens=n),
        grid=(n // tile,),
        in_specs=[_rows(tile, d), _rows(tile // S5_CHUNK, S5_CHUNK * GROUP_WIDTH)] + [_rows(tile, GROUP_WIDTH)] * 3
        + [_pick((256, 256), layer), _pick((1, 256), layer), _pick((d, d), layer), _pick((1, d), layer),
           _pick((d, 2 * ROUTE_LANES), layer), _pick((1, ROUTE_LANES), layer), _full((tile, tile)),
           _full(spread.shape)],
        out_specs=[_rows(tile, d), _rows(ROW_SPLIT * tile, 128), _rows(tile, ROUTE_LANES),
                   _full((MOE_EXPERTS, ROUTE_LANES)), _rows(out_rows, 128), _rows(out_rows, 128)],
        out_shape=[jax.ShapeDtypeStruct((n, d), F32), jax.ShapeDtypeStruct((ROW_SPLIT * n, 128), jnp.int32),
                   jax.ShapeDtypeStruct((n, ROUTE_LANES), F32), jax.ShapeDtypeStruct((MOE_EXPERTS, ROUTE_LANES), F32)]
        + [jax.ShapeDtypeStruct((n // per_row, 128), jnp.int32)] * 2,
        scratch_shapes=[pltpu.VMEM((GROUP_WIDTH // 128, tile, 128), F32),
                        pltpu.VMEM((MOE_EXPERTS, ROUTE_LANES), F32)],
        compiler_params=_cparams("arbitrary"),
        name="out_proj_router",
    )(x, y_s5, y_ret, y_m2, y_hg, w_glu, b_glu, w_out, g2, wr_packed, b_route, strict_lower, spread)


def _sc_mesh():
    return plsc.VectorSubcoreMesh(core_axis_name="core", subcore_axis_name="subcore")


def _sc_scatter2(src, idx_a, idx_b, n_out):
    n = src.shape[0]

    @functools.partial(pl.kernel, out_type=jax.ShapeDtypeStruct((n_out, 128), src.dtype), mesh=_sc_mesh(),
                       scratch_types=[])
    def scatter_kernel(x_hbm, ia_hbm, ib_hbm, o_hbm):
        def body(x_vmem, ia_vmem, ib_vmem):
            pltpu.sync_copy(x_vmem, o_hbm.at[ia_vmem.at[0]])
            pltpu.sync_copy(x_vmem, o_hbm.at[ib_vmem.at[0]])

        pltpu.emit_pipeline(
            body, grid=(n // SC_WINDOW,),
            in_specs=[pl.BlockSpec((SC_WINDOW, 128), index_map=lambda i: (i, 0)),
                      pl.BlockSpec((1, SC_WINDOW), index_map=lambda i: (i, 0)),
                      pl.BlockSpec((1, SC_WINDOW), index_map=lambda i: (i, 0))],
            out_specs=[],
            core_axis_name=("core", "subcore"), dimension_semantics=(pltpu.PARALLEL,),
        )(x_hbm, ia_hbm, ib_hbm)

    return scatter_kernel(src, idx_a, idx_b)


def _sc_gather2(table, idx_a, idx_b):
    n = idx_a.size
    sds = jax.ShapeDtypeStruct((n, 128), table.dtype)

    @functools.partial(pl.kernel, out_type=(sds, sds), mesh=_sc_mesh(), scratch_types=[])
    def gather_kernel(t_hbm, ia_hbm, ib_hbm, oa_hbm, ob_hbm):
        def body(ia_vmem, ib_vmem, oa_vmem, ob_vmem):
            pltpu.sync_copy(t_hbm.at[ia_vmem.at[0]], oa_vmem)
            pltpu.sync_copy(t_hbm.at[ib_vmem.at[0]], ob_vmem)

        pltpu.emit_pipeline(
            body, grid=(n // SC_WINDOW,),
            in_specs=[pl.BlockSpec((1, SC_WINDOW), index_map=lambda i: (i, 0)),
                      pl.BlockSpec((1, SC_WINDOW), index_map=lambda i: (i, 0))],
            out_specs=[pl.BlockSpec((SC_WINDOW, 128), index_map=lambda i: (i, 0)),
                       pl.BlockSpec((SC_WINDOW, 128), index_map=lambda i: (i, 0))],
            core_axis_name=("core", "subcore"), dimension_semantics=(pltpu.PARALLEL,),
        )(ia_hbm, ib_hbm, oa_hbm, ob_hbm)

    return gather_kernel(table, idx_a, idx_b)


def _dispatch_plan(counts, n, n_tiles):
    cnt = counts[:, 0].astype(jnp.int32)
    blocks = (cnt + EXPERT_TILE - 1) // EXPERT_TILE
    ends = jnp.cumsum(blocks)
    first_tile = ends - blocks
    tile_id = jnp.arange(n_tiles, dtype=jnp.int32)
    tile_expert = jnp.minimum(jnp.sum((tile_id[:, None] >= ends[None, :]).astype(jnp.int32), axis=1), MOE_EXPERTS - 1)
    onehot = (tile_expert[:, None] == jnp.arange(MOE_EXPERTS, dtype=jnp.int32)[None, :]).astype(jnp.int32)
    block_in_expert = tile_id - jnp.sum(onehot * first_tile[None, :], axis=1)
    used = tile_id < ends[-1]
    rows_left = jnp.sum(onehot * cnt[None, :], axis=1) - block_in_expert * EXPERT_TILE
    tile_rows = jnp.where(used, jnp.clip(rows_left, 0, EXPERT_TILE), 0).astype(jnp.int32)
    blocks_per_expert = n // EXPERT_TILE
    tile_block = jnp.where(used, tile_expert * blocks_per_expert + block_in_expert,
                           MOE_EXPERTS * blocks_per_expert).astype(jnp.int32)
    tile_first = (used & (block_in_expert == 0)).astype(jnp.int32)
    return tile_expert, tile_block, tile_rows, tile_first


def _experts_kernel(te_ref, blk_ref, rows_ref, first_ref, xs_ref, wg_ref, wu_ref, wd_ref, y_ref, wgb, wub, wdb):
    i = pl.program_id(0)

    @pl.when(first_ref[i] == 1)
    def _():
        wgb[...] = wg_ref[0, 0].astype(BF16)
        wub[...] = wu_ref[0, 0].astype(BF16)
        wdb[...] = wd_ref[0, 0].astype(BF16)

    def run(n_rows):
        x = _merge_rows(xs_ref, n_rows)
        row = lax.broadcasted_iota(jnp.int32, x.shape, 0)
        x = jnp.where(row < rows_ref[i], x, 0.0).astype(BF16)
        act = _silu(_dot(x, wgb[...])) * _dot(x, wub[...])
        _split_rows(y_ref, _dot(act.astype(BF16), wdb[...]), n_rows)

    half = EXPERT_TILE // 2

    @pl.when(rows_ref[i] > half)
    def _():
        run(EXPERT_TILE)

    @pl.when((rows_ref[i] > 0) & (rows_ref[i] <= half))
    def _():
        run(half)


def _experts(xs, tile_expert, tile_block, tile_rows, tile_first, w_gate, w_up, w_down, layer):
    n_tiles = tile_expert.shape[0]
    _, _, d, ff = w_gate.shape
    rows_blk = pl.BlockSpec((ROW_SPLIT * EXPERT_TILE, 128), lambda i, te, blk, rows, first: (blk[i], 0))
    return pl.pallas_call(
        _experts_kernel,
        grid_spec=pltpu.PrefetchScalarGridSpec(
            num_scalar_prefetch=4,
            grid=(n_tiles,),
            in_specs=[rows_blk,
                      pl.BlockSpec((1, 1, d, ff), lambda i, te, blk, rows, first: (layer, te[i], 0, 0)),
                      pl.BlockSpec((1, 1, d, ff), lambda i, te, blk, rows, first: (layer, te[i], 0, 0)),
                      pl.BlockSpec((1, 1, ff, d), lambda i, te, blk, rows, first: (layer, te[i], 0, 0))],
            out_specs=rows_blk,
            scratch_shapes=[pltpu.VMEM((d, ff), BF16), pltpu.VMEM((d, ff), BF16), pltpu.VMEM((ff, d), BF16)],
        ),
        out_shape=jax.ShapeDtypeStruct(xs.shape, xs.dtype),
        compiler_params=_cparams("arbitrary"),
        name="moe_experts",
    )(tile_expert, tile_block, tile_rows, tile_first, xs, w_gate, w_up, w_down)


def _combine_kernel(x1_ref, g1_ref, g2_ref, route_ref, gf_ref, o_ref, *, tile):
    o_ref[...] = _rms(_moe_combine(x1_ref[...], g1_ref, g2_ref, route_ref[...], tile), gf_ref[...])


def _combine(x1, g1, g2, route, g_final):
    n, d = x1.shape
    tile = min(SEQ_TILE, n)
    return pl.pallas_call(
        functools.partial(_combine_kernel, tile=tile),
        grid=(n // tile,),
        in_specs=[_rows(tile, d), _rows(ROW_SPLIT * tile, 128), _rows(ROW_SPLIT * tile, 128),
                  _rows(tile, ROUTE_LANES), _full((1, d))],
        out_specs=_rows(tile, d),
        out_shape=jax.ShapeDtypeStruct((n, d), F32),
        compiler_params=_cparams("parallel"),
        name="moe_combine",
    )(x1, g1, g2, route, g_final)


def _moe(h2_rows, counts, idx1, idx2, w_gate, w_up, w_down, layer):
    n = h2_rows.shape[0] // ROW_SPLIT
    n_tiles = (MOE_TOPK * n) // EXPERT_TILE + MOE_EXPERTS
    tile_expert, tile_block, tile_rows, tile_first = _dispatch_plan(counts, n, n_tiles)
    xs = _sc_scatter2(h2_rows, idx1, idx2, ROW_SPLIT * (MOE_EXPERTS * n + EXPERT_TILE))
    ys = _experts(xs, tile_expert, tile_block, tile_rows, tile_first, w_gate, w_up, w_down, layer)
    return _sc_gather2(ys, idx1, idx2)


def kernel(x, positions, norm1_g, w_in, w_out, s5_a_re, s5_a_im, s5_log_dt, s5_b_re, s5_b_im, s5_c_re, s5_c_im, s5_d, s5_w_glu, s5_b_glu, m2_conv_w, m2_conv_b, m2_dt_bias, m2_a_log, m2_d, m2_norm_g, hg_lb_logits, hg_norm_g, norm2_g, moe_w_group, moe_b_group, moe_w_expert, moe_b_expert, moe_w_gate, moe_w_up, moe_w_down, final_norm_g):
    bsz, seqlen, d = x.shape
    assert bsz == 1 and seqlen % SEQ_TILE == 0 and seqlen % EXPERT_TILE == 0
    depth = w_in.shape[0]

    lb_probs = jax.nn.softmax(hg_lb_logits.astype(F32), axis=0)
    lower_bounds = (jnp.cumsum(lb_probs, axis=0) - lb_probs[0]).reshape(depth, 1, GROUP_WIDTH)
    cos_t, sin_t = _rope_tables(positions.reshape(seqlen, 1))

    s5_p = _s5_params(s5_a_re, s5_a_im, s5_log_dt, s5_b_re, s5_b_im, s5_c_re, s5_c_im, s5_d)
    m2_p = _mamba2_params(m2_conv_w, m2_conv_b, m2_dt_bias, m2_a_log, m2_d, m2_norm_g)
    w_route = jnp.zeros((depth, d, ROUTE_LANES), F32)
    w_route = w_route.at[:, :, :MOE_GROUPS].set(moe_w_group)
    w_route = w_route.at[:, :, EXPERT_ROW:EXPERT_ROW + MOE_EXPERTS].set(moe_w_expert)
    wr_packed = jnp.concatenate(_split2(w_route), axis=2)
    b_route = jnp.zeros((depth, 1, ROUTE_LANES), F32)
    b_route = b_route.at[:, 0, :MOE_GROUPS].set(moe_b_group)
    b_route = b_route.at[:, 0, EXPERT_ROW:EXPERT_ROW + MOE_EXPERTS].set(moe_b_expert)
    w_glu_b, w_out_b = s5_w_glu.astype(BF16), w_out.astype(BF16)
    b_glu = s5_b_glu.reshape(depth, 1, GROUP_WIDTH)
    g1, g2 = norm1_g.reshape(depth, 1, d), norm2_g.reshape(depth, 1, d)
    hg_g = hg_norm_g.reshape(depth, 1, GROUP_WIDTH)
    g_final = final_norm_g.reshape(1, d)
    tile = min(SEQ_TILE, seqlen)
    chunk = min(CHUNK, tile)

    xc, pending = x.reshape(seqlen, d), None
    for l in range(depth):
        outs = _in_proj(xc, pending, g1, _w_prep(w_in, l), l)
        if pending is not None:
            xc, outs = outs[0], outs[1:]
        u_b, u_f, p_ret, p_m2, p_hg = outs
        y_s5 = _s5(u_b, u_f, s5_p, l)
        y_ret, y_m2, y_hg = _mixers(
            seqlen,
            _retention_call(p_ret, cos_t, sin_t, tile, min(RET_CHUNK, tile)),
            _mamba2_call(p_m2, m2_p, l, tile, chunk),
            _hgrn2_call(p_hg, lower_bounds, hg_g, l, tile, chunk))
        x1, h2_rows, route, counts, idx1, idx2 = _out_proj(xc, y_s5, y_ret, y_m2, y_hg, w_glu_b, b_glu, w_out_b,
                                                           g2, wr_packed, b_route, l)
        rows1, rows2 = _moe(h2_rows, counts, idx1, idx2, moe_w_gate, moe_w_up, moe_w_down, l)
        xc, pending = x1, (rows1, rows2, route)
    xc = _combine(xc, *pending, g_final)
    return xc.reshape(bsz, seqlen, d)
```

```python
import functools
import math

import numpy as np
import jax
import jax.numpy as jnp
from jax import lax
from jax.experimental import pallas as pl
from jax.experimental.pallas import tpu as pltpu
from jax.experimental.pallas import tpu_sc as plsc

F32 = jnp.float32
BF16 = jnp.bfloat16
NORM_EPS = 1e-6

GROUP_WIDTH = 256
HEAD_DIM = 64
N_HEADS = GROUP_WIDTH // HEAD_DIM
S5_GROUPS = 16
S5_CH = 16
S5_STATE = 64
S5_DT_CLAMP = -1e-4
M2_STATE = 128
M2_CONV = 4
M2_CONV_DIM = 768
ROPE_BASE = 10000.0
MOE_GROUPS = 4
MOE_PER_GROUP = 4
MOE_EXPERTS = 16
ROUTE_LANES = 128

SEQ_TILE = 512
CHUNK = 128
RET_CHUNK = 256
CHUNK_UNROLL = 4
S5_CHUNK = 16
MOE_TOPK = 2
EXPERT_TILE = 512
SC_WINDOW = 128
VMEM_LIMIT = 56 * 1024 * 1024


def _cparams(*sem):
    return pltpu.CompilerParams(dimension_semantics=sem, vmem_limit_bytes=VMEM_LIMIT)


def _dot(a, b):
    return jnp.dot(a, b, preferred_element_type=F32)


def _dot_nt(a, b):
    return lax.dot_general(a, b, (((1,), (1,)), ((), ())), preferred_element_type=F32)


def _dot_tn(a, b):
    return lax.dot_general(a, b, (((0,), (0,)), ((), ())), preferred_element_type=F32)


def _split2(x):
    hi = x.astype(BF16)
    return hi, (x - hi.astype(F32)).astype(BF16)


def _split3(x):
    hi = x.astype(BF16)
    r = x - hi.astype(F32)
    mid = r.astype(BF16)
    return hi, mid, (r - mid.astype(F32)).astype(BF16)


def _dot_exact_lhs(m, x):
    hi, mid, lo = _split3(x)
    return _dot(m, hi) + _dot(m, mid) + _dot(m, lo)


def _dot_exact_rhs(x, m):
    hi, lo = _split2(x)
    return _dot(hi, m) + _dot(lo, m)


def _sigmoid(x):
    return 1.0 / (1.0 + jnp.exp(-x))


def _silu(x):
    return x * _sigmoid(x)


def _rms(x, g):
    return x * lax.rsqrt(jnp.mean(x * x, axis=-1, keepdims=True) + NORM_EPS) * g


def _full(shape):
    return pl.BlockSpec(shape, lambda *_: (0,) * len(shape))


def _rows(tile, width):
    return pl.BlockSpec((tile, width), lambda i: (i, 0))


def _pick(shape, layer):
    return pl.BlockSpec((None,) + tuple(shape), lambda *_: (layer,) + (0,) * len(shape))


IN_SEGMENTS = (256, 1024, 1280, 1024)


DT_COL = 9 * GROUP_WIDTH
W_PREP_ROWS = 128


def _w_prep_kernel(w_ref, o_ref):
    o_ref[:, 0:DT_COL] = w_ref[0, :, 0:DT_COL].astype(BF16)
    head = lax.broadcasted_iota(jnp.int32, (W_PREP_ROWS, GROUP_WIDTH), 1) // HEAD_DIM
    rep = jnp.zeros((W_PREP_ROWS, GROUP_WIDTH), F32)
    for h in range(N_HEADS):
        rep = jnp.where(head == h, w_ref[0, :, DT_COL + h:DT_COL + h + 1], rep)
    o_ref[:, DT_COL:DT_COL + GROUP_WIDTH] = rep.astype(BF16)
    o_ref[:, DT_COL + GROUP_WIDTH:] = w_ref[0, :, DT_COL + N_HEADS:].astype(BF16)


def _w_prep(w_in, layer):
    _, d, n_in = w_in.shape
    return pl.pallas_call(
        _w_prep_kernel,
        grid=(d // W_PREP_ROWS,),
        in_specs=[pl.BlockSpec((1, W_PREP_ROWS, n_in), lambda i: (layer, i, 0))],
        out_specs=pl.BlockSpec((W_PREP_ROWS, sum(IN_SEGMENTS)), lambda i: (i, 0)),
        out_shape=jax.ShapeDtypeStruct((d, sum(IN_SEGMENTS)), BF16),
        compiler_params=_cparams("parallel"),
        name="w_in_prep",
    )(w_in)


def _in_proj_kernel(*refs, tile, moe_pending):
    if moe_pending:
        x_ref, g1_ref, g2_ref, route_ref, g_ref, w_ref, xo_ref, ub_ref, uf_ref, ret_ref, m2_ref, hg_ref, u_tmp = refs
        x = _moe_combine(x_ref[...], g1_ref, g2_ref, route_ref[...], tile)
        xo_ref[...] = x
    else:
        x_ref, g_ref, w_ref, ub_ref, uf_ref, ret_ref, m2_ref, hg_ref, u_tmp = refs
        x = x_ref[...]
    hb = _rms(x, g_ref[...]).astype(BF16)
    c0, c1, c2, c3 = np.cumsum(IN_SEGMENTS)
    ret_ref[...] = _dot(hb, w_ref[:, c0:c1])
    m2_ref[...] = _dot(hb, w_ref[:, c1:c2])
    hg_ref[...] = _dot(hb, w_ref[:, c2:c3])
    u = _dot(hb, w_ref[:, 0:c0])
    for j in range(GROUP_WIDTH // 128):
        u_tmp[j] = u[:, 128 * j:128 * (j + 1)]
    for s in range(S5_CHUNK):
        for j in range(GROUP_WIDTH // 128):
            v = u_tmp[j, pl.ds(s, tile // S5_CHUNK, stride=S5_CHUNK), :]
            lanes = slice(GROUP_WIDTH * s + 128 * j, GROUP_WIDTH * s + 128 * (j + 1))
            uf_ref[:, lanes] = v
            ub_ref[:, lanes] = v.astype(BF16)


def _in_proj(x, pending, g, w, layer):
    n, d = x.shape
    tile = min(SEQ_TILE, n)
    fold = S5_CHUNK * GROUP_WIDTH
    moe_specs = [_rows(ROW_SPLIT * tile, 128), _rows(ROW_SPLIT * tile, 128), _rows(tile, ROUTE_LANES)]
    return pl.pallas_call(
        functools.partial(_in_proj_kernel, tile=tile, moe_pending=pending is not None),
        grid=(n // tile,),
        in_specs=[_rows(tile, d)] + (moe_specs if pending else []) + [_pick((1, d), layer), _full(w.shape)],
        out_specs=([_rows(tile, d)] if pending else []) + [_rows(tile // S5_CHUNK, fold)] * 2
        + [_rows(tile, s) for s in IN_SEGMENTS[1:]],
        out_shape=([jax.ShapeDtypeStruct((n, d), F32)] if pending else [])
        + [jax.ShapeDtypeStruct((n // S5_CHUNK, fold), BF16), jax.ShapeDtypeStruct((n // S5_CHUNK, fold), F32)]
        + [jax.ShapeDtypeStruct((n, s), F32) for s in IN_SEGMENTS[1:]],
        scratch_shapes=[pltpu.VMEM((GROUP_WIDTH // 128, tile, 128), F32)],
        compiler_params=_cparams("parallel"),
        name="in_proj",
    )(x, *(pending or ()), g, w)


def _rope_kernel(pos_ref, invf_ref, cos_ref, sin_ref):
    ang = pos_ref[...].astype(F32) * invf_ref[...]
    cos_ref[...] = jnp.cos(ang)
    sin_ref[...] = jnp.sin(ang)


def _rope_tables(positions):
    n = positions.shape[0]
    tile = min(SEQ_TILE, n)
    half = HEAD_DIM // 2
    inv_freq = ROPE_BASE ** (-jnp.arange(half, dtype=F32) / half)
    invf = jnp.tile(inv_freq, 128 // half).reshape(1, 128)
    return pl.pallas_call(
        _rope_kernel,
        grid=(n // tile,),
        in_specs=[_rows(tile, 1), _full((1, 128))],
        out_specs=[_rows(tile, 128), _rows(tile, 128)],
        out_shape=[jax.ShapeDtypeStruct((n, 128), F32)] * 2,
        compiler_params=_cparams("parallel"),
        name="rope_tables",
    )(positions, invf)


def _head_mean_matrix():
    h = np.arange(GROUP_WIDTH) // HEAD_DIM
    return jnp.asarray((h[:, None] == h[None, :]) / HEAD_DIM, dtype=BF16)


def _head_block_mask():
    h = np.arange(GROUP_WIDTH) // HEAD_DIM
    return jnp.asarray(h[:, None] == h[None, :], dtype=F32)


HEAD_PAIRS = GROUP_WIDTH // 128


def _lanes(x, j):
    return x[:, 128 * j:128 * (j + 1)]


def _stack_pair(x):
    xb = x.astype(BF16)
    low = lax.broadcasted_iota(jnp.int32, x.shape, 1) < HEAD_DIM
    zero = jnp.zeros_like(xb)
    return jnp.concatenate([jnp.where(low, xb, zero), jnp.where(low, zero, xb)], axis=0)


def _pair_scores(q, kb):
    s = _dot_nt(_stack_pair(q), kb)
    return s[:q.shape[0]], s[q.shape[0]:]


def _pair_apply(a0, a1, v):
    return _dot(jnp.concatenate([a0.astype(BF16), a1.astype(BF16)], axis=1), _stack_pair(v))


def _ret_constants(chunk):
    lg = np.log1p(-np.exp2(-5.0 - np.arange(N_HEADS, dtype=np.float64)))
    idx = np.arange(chunk, dtype=np.float64)
    rel = idx[:, None] - idx[None, :]
    decay = np.where(rel >= 0, np.exp(np.maximum(rel, 0.0)[None] * lg[:, None, None]), 0.0)
    lane_lg = np.repeat(lg, HEAD_DIM)
    xi = np.exp((idx + 1.0)[:, None] * lane_lg[None, :])
    zeta = np.exp((chunk - 1.0 - idx)[:, None] * lane_lg[None, :])
    h = np.arange(GROUP_WIDTH) // HEAD_DIM
    gc = np.where(h[:, None] == h[None, :], np.exp(chunk * lane_lg)[:, None], 0.0)
    f = lambda a: jnp.asarray(a, dtype=F32)
    return f(decay), f(xi), f(zeta), f(gc)


def _ret_kernel(p_ref, cos_ref, sin_ref, dec_ref, xi_ref, zeta_ref, gc_ref, bm_ref, gm_ref,
                o_ref, r_ref, *, chunk, n_chunks):
    lane = lax.broadcasted_iota(jnp.int32, (chunk, GROUP_WIDTH), 1)
    first_half = (lane % HEAD_DIM) < (HEAD_DIM // 2)
    gm = gm_ref[...]

    def rope(t, cos2, sin2):
        rot = jnp.where(first_half, -pltpu.roll(t, GROUP_WIDTH - HEAD_DIM // 2, 1),
                        pltpu.roll(t, HEAD_DIM // 2, 1))
        return t * cos2 + rot * sin2

    def body(c, carry):
        rows = pl.ds(pl.multiple_of(c * chunk, chunk), chunk)
        cs = cos_ref[rows, :]
        sn = sin_ref[rows, :]
        cos2 = jnp.concatenate([cs, cs], axis=1)
        sin2 = jnp.concatenate([sn, sn], axis=1)
        q = rope(p_ref[rows, 0:256], cos2, sin2)
        k = rope(p_ref[rows, 256:512], cos2, sin2) * (HEAD_DIM ** -0.5)
        v = p_ref[rows, 512:768]
        g = p_ref[rows, 768:1024]
        kb = k.astype(BF16)
        inner = []
        for j in range(HEAD_PAIRS):
            s0, s1 = _pair_scores(_lanes(q, j), _lanes(kb, j))
            inner.append(_pair_apply(s0 * dec_ref[2 * j], s1 * dec_ref[2 * j + 1], _lanes(v, j)))
        inner = jnp.concatenate(inner, axis=1)
        r_prev = r_ref[...]
        cross = _dot((q * xi_ref[...]).astype(BF16), r_prev.astype(BF16))
        r_ref[...] = gc_ref[...] * r_prev + bm_ref[...] * _dot_tn(kb, (zeta_ref[...] * v).astype(BF16))
        o = inner + cross
        cen = o - _dot_exact_rhs(o, gm)
        var = _dot_exact_rhs(cen * cen, gm)
        o_ref[rows, :] = (cen * lax.rsqrt(var + NORM_EPS) * _silu(g)).astype(BF16)
        return carry

    lax.fori_loop(0, n_chunks, body, 0, unroll=CHUNK_UNROLL)


def _retention_call(proj, cos_t, sin_t, tile, chunk):
    decay, xi, zeta, gc = _ret_constants(chunk)
    body = functools.partial(_ret_kernel, chunk=chunk, n_chunks=tile // chunk)
    operands = (proj, cos_t, sin_t, decay, xi, zeta, gc, _head_block_mask(), _head_mean_matrix())
    specs = [_rows(tile, 1024), _rows(tile, 128), _rows(tile, 128), _full(decay.shape), _full(xi.shape),
             _full(zeta.shape), _full(gc.shape), _full((256, 256)), _full((256, 256))]
    return body, operands, specs, [pltpu.VMEM((GROUP_WIDTH, GROUP_WIDTH), F32)]


def _tri_matrix(chunk):
    i = np.arange(chunk)
    return jnp.asarray(i[:, None] >= i[None, :], dtype=BF16)


def _m2_kernel(p_ref, cw_ref, cb_ref, dtb_ref, alog_ref, d_ref, ng_ref, tri_ref,
               o_ref, tail_ref, ext_ref, act_ref, st_ref, *, tile, chunk, n_chunks):
    ext_ref[0:8, :] = tail_ref[...]
    ext_ref[8:tile + 8, :] = p_ref[:, 256:1024]
    tail_ref[...] = p_ref[tile - 8:tile, 256:1024]
    conv = cb_ref[...]
    for j in range(M2_CONV):
        lo = 8 - (M2_CONV - 1) + j
        conv = conv + cw_ref[j:j + 1, :] * ext_ref[lo:lo + tile, :]
    act_ref[...] = _silu(conv)
    a_lane = -jnp.exp(alog_ref[...])
    tri = tri_ref[...]
    ti = lax.broadcasted_iota(jnp.int32, (chunk, chunk), 0)
    si = lax.broadcasted_iota(jnp.int32, (chunk, chunk), 1)
    causal = ti >= si
    lane = lax.broadcasted_iota(jnp.int32, (chunk, 128), 1)

    def body(c, carry):
        start = pl.multiple_of(c * chunk, chunk)
        rows = pl.ds(start, chunk)
        xbc = act_ref[rows, :]
        xs = xbc[:, 0:256]
        z = p_ref[rows, 0:256]
        x_dt = p_ref[rows, 1024:1280] + dtb_ref[...]
        dt = jnp.maximum(x_dt, 0.0) + jnp.log1p(jnp.exp(-jnp.abs(x_dt)))
        acum = _dot_exact_lhs(tri, dt * a_lane)
        acum_t = acum.T
        a_last = acum[chunk - 1:chunk, :]
        e_acum = jnp.exp(acum)
        decs = jnp.exp(a_last - acum)
        d_chunk = jnp.exp(a_last)
        xc = xs * dt
        ys = []
        for g in range(2):
            sl = slice(128 * g, 128 * (g + 1))
            bmg = xbc[:, 256 + 128 * g:256 + 128 * (g + 1)].astype(BF16)
            cmg = xbc[:, 512 + 128 * g:512 + 128 * (g + 1)].astype(BF16)
            cb = _dot_nt(cmg, bmg)
            xcg = xc[:, sl]
            yd = jnp.zeros((chunk, 128), F32)
            for hh in range(2):
                col0 = 128 * g + HEAD_DIM * hh
                diff = acum[:, col0:col0 + 1] - acum_t[col0:col0 + 1, :]
                lm = jnp.where(causal, jnp.exp(jnp.where(causal, diff, 0.0)), 0.0)
                xm = jnp.where((lane // HEAD_DIM) == hh, xcg, 0.0).astype(BF16)
                yd = yd + _dot((cb * lm).astype(BF16), xm)
            st = st_ref[:, sl]
            y_off = _dot(cmg, st.astype(BF16)) * e_acum[:, sl]
            st_ref[:, sl] = d_chunk[:, sl] * st + _dot_tn(bmg, (xcg * decs[:, sl]).astype(BF16))
            ys.append(yd + y_off + d_ref[:, sl] * xs[:, sl])
        y = jnp.concatenate(ys, axis=1) * _silu(z)
        o_ref[rows, :] = _rms(y, ng_ref[...]).astype(BF16)
        return carry

    lax.fori_loop(0, n_chunks, body, 0, unroll=CHUNK_UNROLL)


def _mamba2_params(conv_w, conv_b, dt_bias, a_log, d_skip, norm_g):
    depth = conv_w.shape[0]
    lanes = lambda v: jnp.repeat(v, HEAD_DIM, axis=1).reshape(depth, 1, GROUP_WIDTH)
    return (conv_w, conv_b.reshape(depth, 1, -1), lanes(dt_bias), lanes(a_log), lanes(d_skip),
            norm_g.reshape(depth, 1, -1))


def _mamba2_call(proj, params, layer, tile, chunk):
    body = functools.partial(_m2_kernel, tile=tile, chunk=chunk, n_chunks=tile // chunk)
    operands = (proj,) + tuple(params) + (_tri_matrix(chunk),)
    specs = ([_rows(tile, 1280), _pick((M2_CONV, M2_CONV_DIM), layer), _pick((1, M2_CONV_DIM), layer)]
             + [_pick((1, GROUP_WIDTH), layer)] * 4 + [_full((chunk, chunk))])
    scratch = [pltpu.VMEM((8, M2_CONV_DIM), F32), pltpu.VMEM((tile + 8, M2_CONV_DIM), F32),
               pltpu.VMEM((tile, M2_CONV_DIM), F32), pltpu.VMEM((M2_STATE, GROUP_WIDTH), F32)]
    return body, operands, specs, scratch


HG_MATMUL_LEVELS = 2


def _hg_exponent_matrix(chunk):
    levels = HG_MATMUL_LEVELS
    t = np.arange(chunk)[:, None]
    r = np.arange(chunk)[None, :]
    blocks = []
    for lvl in range(levels):
        b = 1 << lvl
        blk = t // b
        odd = (blk % 2) == 1
        q_rows = odd & (r >= blk * b) & (r <= t)
        k_rows = (~odd) & (r > t) & (r <= (blk + 1) * b - 1)
        blocks.append(q_rows | k_rows)
    blocks.append(r <= t)
    return jnp.asarray(np.concatenate(blocks, axis=0), dtype=BF16)


def _hg_kernel(p_ref, lb_ref, ng_ref, gexp_ref, bm_ref, gm_ref, o_ref, st_ref, *, chunk, n_chunks):
    levels = int(math.log2(chunk))
    row = lax.broadcasted_iota(jnp.int32, (chunk, 128), 0)
    odd_rows = [((row >> lvl) & 1) == 1 for lvl in range(levels)]
    row_wide = lax.broadcasted_iota(jnp.int32, (chunk, GROUP_WIDTH), 0)
    odd_rows_wide = [((row_wide >> lvl) & 1) == 1 for lvl in range(levels)]
    ti = lax.broadcasted_iota(jnp.int32, (chunk, chunk), 0)
    si = lax.broadcasted_iota(jnp.int32, (chunk, chunk), 1)
    pair_level = [((ti >> (lvl + 1)) == (si >> (lvl + 1))) & (((ti >> lvl) & 1) == 1) & (((si >> lvl) & 1) == 0)
                  for lvl in range(levels)]
    lb = lb_ref[...]
    gm = gm_ref[...]

    def body(c, carry):
        rows = pl.ds(pl.multiple_of(c * chunk, chunk), chunk)
        q = _silu(p_ref[rows, 0:256])
        forget = lb + (1.0 - lb) * _sigmoid(p_ref[rows, 256:512])
        k = 1.0 - forget
        v = p_ref[rows, 512:768]
        g = p_ref[rows, 768:1024]
        lf_hi, lf_lo = _split2(jnp.log(forget))
        expo = _dot(gexp_ref[...], lf_hi) + _dot(gexp_ref[...], lf_lo)
        bcum = expo[HG_MATMUL_LEVELS * chunk:(HG_MATMUL_LEVELS + 1) * chunk, :]

        def level_log_decay(lvl):
            if lvl < HG_MATMUL_LEVELS:
                return expo[lvl * chunk:(lvl + 1) * chunk, :]
            b = 1 << lvl
            ref = jnp.concatenate([jnp.broadcast_to(bcum[m + b - 1:m + b, :], (2 * b, GROUP_WIDTH))
                                   for m in range(0, chunk, 2 * b)], axis=0)
            return jnp.where(odd_rows_wide[lvl], bcum - ref, ref - bcum)


        log_decay = [level_log_decay(lvl) for lvl in range(levels)]
        intra = []
        for j in range(HEAD_PAIRS):
            qj, kj = _lanes(q, j), _lanes(k, j)
            a0, a1 = (jnp.where(ti == si, s, 0.0) for s in _pair_scores(qj, kj.astype(BF16)))
            for lvl in range(levels):
                w = jnp.exp(_lanes(log_decay[lvl], j)) * jnp.where(odd_rows[lvl], qj, kj)
                s0, s1 = _pair_scores(w, w.astype(BF16))
                a0 = jnp.where(pair_level[lvl], s0, a0)
                a1 = jnp.where(pair_level[lvl], s1, a1)
            intra.append(_pair_apply(a0, a1, _lanes(v, j)))
        intra = jnp.concatenate(intra, axis=1)

        b_last = bcum[chunk - 1:chunk, :]
        suffix = b_last - bcum
        st = st_ref[...]
        cross = _dot_nt((q * jnp.exp(bcum)).astype(BF16), st.astype(BF16))
        st_ref[...] = jnp.exp(b_last) * st + bm_ref[...] * _dot_tn(
            v.astype(BF16), (k * jnp.exp(suffix)).astype(BF16))
        o = intra + cross
        o = o * lax.rsqrt(_dot_exact_rhs(o * o, gm) + NORM_EPS) * ng_ref[...]
        o_ref[rows, :] = (o * _silu(g)).astype(BF16)
        return carry

    lax.fori_loop(0, n_chunks, body, 0, unroll=CHUNK_UNROLL)


def _hgrn2_call(proj, lower_bounds, norm_g, layer, tile, chunk):
    gexp = _hg_exponent_matrix(chunk)
    body = functools.partial(_hg_kernel, chunk=chunk, n_chunks=tile // chunk)
    operands = (proj, lower_bounds, norm_g, gexp, _head_block_mask(), _head_mean_matrix())
    specs = [_rows(tile, 1024), _pick((1, GROUP_WIDTH), layer), _pick((1, GROUP_WIDTH), layer), _full(gexp.shape),
             _full((256, 256)), _full((256, 256))]
    return body, operands, specs, [pltpu.VMEM((GROUP_WIDTH, GROUP_WIDTH), F32)]


def _mixers_kernel(*refs, bodies, n_in, n_scratch):
    n_mix = len(bodies)
    ins, pos = [], 0
    for k in n_in:
        ins.append(refs[pos:pos + k])
        pos += k
    outs = refs[pos:pos + n_mix]
    pos += n_mix
    scratch = []
    for k in n_scratch:
        scratch.append(refs[pos:pos + k])
        pos += k

    @pl.when(pl.program_id(0) == 0)
    def _():
        for group in scratch:
            for ref in group:
                ref[...] = jnp.zeros_like(ref)

    for body, i, o, s in zip(bodies, ins, outs, scratch):
        body(*i, o, *s)


def _mixers(n, *calls):
    tile = min(SEQ_TILE, n)
    bodies = tuple(c[0] for c in calls)
    return pl.pallas_call(
        functools.partial(_mixers_kernel, bodies=bodies, n_in=tuple(len(c[1]) for c in calls),
                          n_scratch=tuple(len(c[3]) for c in calls)),
        grid=(n // tile,),
        in_specs=[s for c in calls for s in c[2]],
        out_specs=[_rows(tile, GROUP_WIDTH)] * len(calls),
        out_shape=[jax.ShapeDtypeStruct((n, GROUP_WIDTH), BF16)] * len(calls),
        scratch_shapes=[s for c in calls for s in c[3]],
        compiler_params=_cparams("arbitrary"),
        name="mixers",
    )(*[a for c in calls for a in c[1]])


S5_LANES = S5_GROUPS * 2 * S5_STATE
S5_TAP_SPLITS = 4
S5_STATE_GROUP = 4


def _s5_rows(a_re, a_im, log_dt):
    are = jnp.minimum(a_re, S5_DT_CLAMP)
    dt = jnp.exp(log_dt)
    lam_re = are * dt
    lam_im = a_im * dt
    mag = jnp.exp(lam_re)
    ab_re = mag * jnp.cos(lam_im)
    ab_im = mag * jnp.sin(lam_im)
    den = are * are + a_im * a_im
    k_re = ((ab_re - 1.0) * are + ab_im * a_im) / den
    k_im = (ab_im * are - (ab_re - 1.0) * a_im) / den
    return lam_re, lam_im, k_re, k_im


def _s5_power(lam_re, lam_im, e):
    m = jnp.exp(e * lam_re)
    return m * jnp.cos(e * lam_im), m * jnp.sin(e * lam_im)


def _s5_state_kernel(u_ref, are_ref, aim_ref, ldt_ref, b1_ref, b2_ref, ca_ref, k_ref, sp_ref,
                     bb1_ref, bb2_ref, inc_ref, w_ref, *, n_chunks):
    g = pl.program_id(0)
    lam_re, lam_im, k_re, k_im = _s5_rows(are_ref[...], aim_ref[...], ldt_ref[...])

    @pl.when(g == 0)
    def _():
        bb1_ref[...] = k_re * b1_ref[...] + k_im * b2_ref[...]
        bb2_ref[...] = k_re * b2_ref[...] - k_im * b1_ref[...]

    ca = ca_ref[...].astype(BF16)
    for j in range(S5_STATE_GROUP):
        s = g * S5_STATE_GROUP + j
        p_re, p_im = _s5_power(lam_re, lam_im, (S5_CHUNK - 1 - s).astype(F32))
        w = (p_re * bb1_ref[...] + p_im * bb2_ref[...]).astype(BF16)
        k_ref[j] = _dot_nt(w, ca).astype(BF16)
        w_ref[GROUP_WIDTH * j:GROUP_WIDTH * (j + 1), :] = w
    contrib = _dot(u_ref[...], w_ref[...])

    @pl.when(g == 0)
    def _():
        inc_ref[...] = contrib

    @pl.when(g > 0)
    def _():
        inc_ref[...] += contrib

    @pl.when(g == S5_CHUNK // S5_STATE_GROUP - 1)
    def _():
        half = S5_LANES // 2
        group = 8
        step = lax.broadcasted_iota(jnp.int32, (group, S5_LANES), 0)
        a_re_all, a_im_all = _s5_power(lam_re, lam_im, ((step + 1) * S5_CHUNK).astype(F32))
        row = lax.broadcasted_iota(jnp.int32, (n_chunks, 128), 0)

        def shifted(x, sh, within):
            keep = ((row % group) >= sh) if within else (row >= sh)
            return jnp.where(keep, pltpu.roll(x, sh, 0), 0.0)

        for j in range(half // 128):
            re_l, im_l = slice(128 * j, 128 * (j + 1)), slice(half + 128 * j, half + 128 * (j + 1))
            x_re, x_im = inc_ref[:, re_l], inc_ref[:, im_l]
            for sh in (1, 2, 4):
                a_re, a_im = a_re_all[sh - 1:sh, re_l], a_im_all[sh - 1:sh, re_l]
                p_re, p_im = shifted(x_re, sh, True), shifted(x_im, sh, True)
                x_re, x_im = x_re + a_re * p_re - a_im * p_im, x_im + a_re * p_im + a_im * p_re
            inc_ref[:, re_l] = x_re
            inc_ref[:, im_l] = x_im

        g_re, g_im = a_re_all[:, 0:half], a_im_all[:, 0:half]

        def carry_in(v, carry):
            c_re, c_im = carry
            rows = pl.ds(pl.multiple_of(v * group, group), group)
            x_re = inc_ref[rows, 0:half] + g_re * c_re - g_im * c_im
            x_im = inc_ref[rows, half:] + g_re * c_im + g_im * c_re
            inc_ref[rows, 0:half] = x_re
            inc_ref[rows, half:] = x_im
            return x_re[group - 1:group, :], x_im[group - 1:group, :]

        zero = jnp.zeros((1, half), F32)
        lax.fori_loop(0, n_chunks // group, carry_in, (zero, zero))
        for j in range(S5_LANES // 128):
            lanes = slice(128 * j, 128 * (j + 1))
            sp_ref[:, lanes] = shifted(inc_ref[:, lanes], 1, False).astype(BF16)


def _s5_out_kernel(ub_ref, uf_ref, k_ref, sp_ref, are_ref, aim_ref, ldt_ref, ca_ref, cb_ref, d_ref,
                   y_ref, taps_ref):
    t = pl.program_id(0)
    fold = S5_CHUNK * GROUP_WIDTH

    @pl.when(t == 0)
    def _():
        for j in range(S5_CHUNK):
            taps_ref[GROUP_WIDTH * j:GROUP_WIDTH * (j + 1), :] = k_ref[j]
        taps_ref[fold:, :] = jnp.zeros((fold - GROUP_WIDTH, GROUP_WIDTH), BF16)

    lam_re, lam_im, _, _ = _s5_rows(are_ref[...], aim_ref[...], ldt_ref[...])
    p_re, p_im = _s5_power(lam_re, lam_im, (t + 1).astype(F32))
    w_out = (p_re * ca_ref[...] + p_im * cb_ref[...]).astype(BF16)
    start = pl.multiple_of((S5_CHUNK - 1 - t) * GROUP_WIDTH, GROUP_WIDTH)
    y_ref[...] = _dot_nt(sp_ref[...], w_out) + d_ref[...] * uf_ref[...]
    quarter = S5_CHUNK // S5_TAP_SPLITS
    for part in range(S5_TAP_SPLITS):
        width = (part + 1) * quarter * GROUP_WIDTH

        @pl.when((t >= part * quarter) & (t < (part + 1) * quarter))
        def _():
            y_ref[...] += _dot(ub_ref[:, 0:width], taps_ref[pl.ds(start, width), :])


def _s5_embed(re, im):
    eye = jnp.eye(S5_GROUPS, dtype=F32)
    blocks = [(eye[None, :, None, :, None] * x[:, :, :, None, :]).reshape(-1, S5_GROUPS * S5_CH, S5_LANES // 2)
              for x in (re, im)]
    return jnp.concatenate(blocks, axis=2)


def _s5_params(a_re, a_im, log_dt, b_re, b_im, c_re, c_im, d_skip):
    depth = a_re.shape[0]
    row = lambda v: jnp.tile(v.reshape(depth, 1, S5_LANES // 2), (1, 1, 2))
    bt_re, bt_im = b_re.transpose(0, 1, 3, 2), b_im.transpose(0, 1, 3, 2)
    return dict(are=row(a_re), aim=row(a_im), ldt=row(jnp.repeat(log_dt, S5_STATE, axis=1)),
                b1=_s5_embed(bt_re, bt_im), b2=_s5_embed(-bt_im, bt_re),
                ca=_s5_embed(c_re, -c_im), cb=_s5_embed(-c_im, -c_re),
                d=d_skip.reshape(depth, 1, GROUP_WIDTH))


def _s5(u_b, u_f, p, layer):
    n_chunks, fold = u_b.shape
    row_spec, mat_spec = _pick((1, S5_LANES), layer), _pick((GROUP_WIDTH, S5_LANES), layer)
    col = lambda: pl.BlockSpec((n_chunks, GROUP_WIDTH), lambda s: (0, s))
    group_cols = S5_STATE_GROUP * GROUP_WIDTH
    taps, s_prev = pl.pallas_call(
        functools.partial(_s5_state_kernel, n_chunks=n_chunks),
        grid=(S5_CHUNK // S5_STATE_GROUP,),
        in_specs=[pl.BlockSpec((n_chunks, group_cols), lambda g: (0, g))] + [row_spec] * 3 + [mat_spec] * 3,
        out_specs=[pl.BlockSpec((S5_STATE_GROUP, GROUP_WIDTH, GROUP_WIDTH), lambda g: (g, 0, 0)),
                   _full((n_chunks, S5_LANES))],
        out_shape=[jax.ShapeDtypeStruct((S5_CHUNK, GROUP_WIDTH, GROUP_WIDTH), BF16),
                   jax.ShapeDtypeStruct((n_chunks, S5_LANES), BF16)],
        scratch_shapes=[pltpu.VMEM((GROUP_WIDTH, S5_LANES), F32), pltpu.VMEM((GROUP_WIDTH, S5_LANES), F32),
                        pltpu.VMEM((n_chunks, S5_LANES), F32), pltpu.VMEM((group_cols, S5_LANES), BF16)],
        compiler_params=_cparams("arbitrary"),
        name="s5_state",
    )(u_b, p['are'], p['aim'], p['ldt'], p['b1'], p['b2'], p['ca'])
    return pl.pallas_call(
        _s5_out_kernel,
        grid=(S5_CHUNK,),
        in_specs=[_full((n_chunks, fold)), col(), _full(taps.shape), _full(s_prev.shape)]
        + [row_spec] * 3 + [mat_spec] * 2 + [_pick((1, GROUP_WIDTH), layer)],
        out_specs=col(),
        out_shape=jax.ShapeDtypeStruct((n_chunks, fold), F32),
        scratch_shapes=[pltpu.VMEM(((2 * S5_CHUNK - 1) * GROUP_WIDTH, GROUP_WIDTH), BF16)],
        compiler_params=_cparams("arbitrary"),
        name="s5_out",
    )(u_b, u_f, taps, s_prev, p['are'], p['aim'], p['ldt'], p['ca'], p['cb'], p['d'])


EXPERT_ROW = 8


def _route(logits_t):
    tokens = logits_t.shape[1]
    big = jnp.int32(1 << 20)
    neg = jnp.float32(-jnp.inf)
    g_row = lax.broadcasted_iota(jnp.int32, (8, tokens), 0)
    is_group = g_row < MOE_GROUPS
    gl = jnp.where(is_group, logits_t[0:8, :], neg)
    ge = jnp.where(is_group, jnp.exp(gl - jnp.max(gl, axis=0, keepdims=True)), 0.0)
    gp = ge / jnp.sum(ge, axis=0, keepdims=True)
    p_g = jnp.max(gp, axis=0, keepdims=True)
    g_idx = jnp.min(jnp.where(is_group & (gp == p_g), g_row, big), axis=0, keepdims=True)
    e_row = lax.broadcasted_iota(jnp.int32, (MOE_EXPERTS, tokens), 0)
    in_group = (e_row // MOE_PER_GROUP) == g_idx
    el = jnp.where(in_group, logits_t[EXPERT_ROW:EXPERT_ROW + MOE_EXPERTS, :], neg)
    ee = jnp.where(in_group, jnp.exp(el - jnp.max(el, axis=0, keepdims=True)), 0.0)
    ep = ee / jnp.sum(ee, axis=0, keepdims=True)
    p1 = jnp.max(jnp.where(in_group, ep, -1.0), axis=0, keepdims=True)
    i1 = jnp.min(jnp.where(in_group & (ep == p1), e_row, big), axis=0, keepdims=True)
    rest = in_group & (e_row != i1)
    p2 = jnp.max(jnp.where(rest, ep, -1.0), axis=0, keepdims=True)
    i2 = jnp.min(jnp.where(rest & (ep == p2), e_row, big), axis=0, keepdims=True)
    tot = p1 + p2
    return i1, i2, p_g * p1 / tot, p_g * p2 / tot


INDEX_DIGIT_BITS = 6
ROW_SPLIT = 4
ROUTE_E1, ROUTE_E2, ROUTE_R1, ROUTE_R2, ROUTE_W1, ROUTE_W2 = range(6)
HIGH_HALF = 0xFFFF0000


def _split_rows(ref, value, rows):
    half = value.shape[1] // 2
    lo = lax.bitcast_convert_type(value[:, :half].astype(jnp.bfloat16).astype(F32), jnp.uint32)
    hi = lax.bitcast_convert_type(value[:, half:].astype(jnp.bfloat16).astype(F32), jnp.uint32)
    words = lax.bitcast_convert_type((lo >> 16) | (hi & jnp.uint32(HIGH_HALF)), jnp.int32)
    for j in range(ROW_SPLIT):
        ref[pl.ds(j, rows, stride=ROW_SPLIT), :] = words[:, 128 * j:128 * (j + 1)]


def _merge_rows(ref, rows):
    words = jnp.concatenate([ref[pl.ds(j, rows, stride=ROW_SPLIT), :] for j in range(ROW_SPLIT)], axis=1)
    words = lax.bitcast_convert_type(words, jnp.uint32)
    lo = lax.bitcast_convert_type(words << 16, F32)
    hi = lax.bitcast_convert_type(words & jnp.uint32(HIGH_HALF), F32)
    return jnp.concatenate([lo, hi], axis=1)


def _moe_combine(x1, g1_ref, g2_ref, route, rows):
    return (x1 + route[:, ROUTE_W1:ROUTE_W1 + 1] * _merge_rows(g1_ref, rows)
            + route[:, ROUTE_W2:ROUTE_W2 + 1] * _merge_rows(g2_ref, rows))


def _out_kernel(x_ref, s5_ref, ret_ref, m2_ref, hg_ref, wglu_ref, bglu_ref, wo_ref, g2_ref,
                wrh_ref, br_ref, stri_ref, spread_ref, x1_ref, h2_ref, route_ref, cnt_ref, i1_ref, i2_ref,
                s5_tmp, carry_ref, *, tile, n_tokens):
    @pl.when(pl.program_id(0) == 0)
    def _():
        carry_ref[...] = jnp.zeros_like(carry_ref)

    for s in range(S5_CHUNK):
        for j in range(GROUP_WIDTH // 128):
            lanes = slice(GROUP_WIDTH * s + 128 * j, GROUP_WIDTH * s + 128 * (j + 1))
            s5_tmp[j, pl.ds(s, tile // S5_CHUNK, stride=S5_CHUNK), :] = s5_ref[:, lanes]
    y = jnp.concatenate([s5_tmp[j] for j in range(GROUP_WIDTH // 128)], axis=1)
    y = y * (0.5 * (1.0 + jnp.tanh(math.sqrt(2.0 / math.pi) * (y + 0.044715 * (y * y * y)))))
    y = y * _sigmoid(_dot(y.astype(BF16), wglu_ref[...]) + bglu_ref[...])
    acc = x_ref[...] + _dot(y.astype(BF16), wo_ref[0:256, :])
    acc = acc + _dot(ret_ref[...], wo_ref[256:512, :])
    acc = acc + _dot(m2_ref[...], wo_ref[512:768, :])
    acc = acc + _dot(hg_ref[...], wo_ref[768:1024, :])
    x1_ref[...] = acc
    h2 = _rms(acc, g2_ref[...])
    _split_rows(h2_ref, h2, tile)
    hi, lo = _split2(h2)
    hw = _dot(hi, wrh_ref[...])
    logits = (hw[:, :ROUTE_LANES] + hw[:, ROUTE_LANES:] + _dot(lo, wrh_ref[:, :ROUTE_LANES])) + br_ref[...]
    e1, e2, w1, w2 = _route(logits.T)
    e_row = lax.broadcasted_iota(jnp.int32, (MOE_EXPERTS, tile), 0)
    picked = jnp.where((e_row == e1) | (e_row == e2), 1.0, 0.0)
    rank = carry_ref[:, 0:1] + _dot_nt(picked.astype(BF16), stri_ref[...])
    r1 = jnp.sum(jnp.where(e_row == e1, rank, 0.0), axis=0, keepdims=True)
    r2 = jnp.sum(jnp.where(e_row == e2, rank, 0.0), axis=0, keepdims=True)
    carry_ref[...] += jnp.sum(picked, axis=1, keepdims=True)
    cnt_ref[...] = carry_ref[...]
    rec_row = lax.broadcasted_iota(jnp.int32, (ROUTE_LANES, tile), 0)
    rec = jnp.zeros((ROUTE_LANES, tile), F32)
    for col, val in ((ROUTE_E1, e1.astype(F32)), (ROUTE_E2, e2.astype(F32)), (ROUTE_R1, r1), (ROUTE_R2, r2),
                     (ROUTE_W1, w1), (ROUTE_W2, w2)):
        rec = jnp.where(rec_row == col, val, rec)
    route_ref[...] = rec.T
    digit_row = lax.broadcasted_iota(jnp.int32, (8, tile), 0)
    digits = jnp.zeros((8, tile), F32)
    for slot, (e, r) in enumerate(((e1, r1), (e2, r2))):
        pos = e.astype(F32) * float(n_tokens) + r
        for k, shift in enumerate((2 * INDEX_DIGIT_BITS, INDEX_DIGIT_BITS, 0)):
            digit = jnp.floor(pos * (1.0 / (1 << shift)))
            pos = pos - digit * float(1 << shift)
            digits = jnp.where(digit_row == 3 * slot + k, digit, digits)
    sub = lax.broadcasted_iota(jnp.int32, (1, 128 * ROW_SPLIT), 1) % ROW_SPLIT
    for q in range(tile // 128):
        o = _dot(digits[:, 128 * q:128 * (q + 1)].astype(BF16), spread_ref[...])
        for slot, out_ref in enumerate((i1_ref, i2_ref)):
            moved = (float(1 << (2 * INDEX_DIGIT_BITS)) * o[3 * slot:3 * slot + 1]
                     + float(1 << INDEX_DIGIT_BITS) * o[3 * slot + 1:3 * slot + 2] + o[3 * slot + 2:3 * slot + 3])
            idx = moved.astype(jnp.int32) * ROW_SPLIT + sub
            for j in range(ROW_SPLIT):
                row = ROW_SPLIT * q + j
                out_ref[row:row + 1, :] = idx[:, 128 * j:128 * (j + 1)]


def _out_proj(x, y_s5, y_ret, y_m2, y_hg, w_glu, b_glu, w_out, g2, wr_packed, b_route, layer):
    n, d = x.shape
    tile = min(SEQ_TILE, n)
    i = np.arange(tile)
    strict_lower = jnp.asarray(i[:, None] > i[None, :], dtype=BF16)
    assert MOE_EXPERTS * n <= (1 << (3 * INDEX_DIGIT_BITS))
    per_row = 128 // ROW_SPLIT
    out_rows = tile // per_row
    lanes = np.arange(128 * ROW_SPLIT)
    spread = jnp.asarray(np.arange(128)[:, None] == (lanes // ROW_SPLIT)[None, :], dtype=BF16)
    return pl.pallas_call(
        functools.partial(_out_kernel, tile=tile, n_tokens=n),
        grid=(n // tile,),
        in_specs=[_rows(tile, d), _rows(tile // S5_CHUNK, S5_CHUNK * GROUP_WIDTH)] + [_rows(tile, GROUP_WIDTH)] * 3
        + [_pick((256, 256), layer), _pick((1, 256), layer), _pick((d, d), layer), _pick((1, d), layer),
           _pick((d, 2 * ROUTE_LANES), layer), _pick((1, ROUTE_LANES), layer), _full((tile, tile)),
           _full(spread.shape)],
        out_specs=[_rows(tile, d), _rows(ROW_SPLIT * tile, 128), _rows(tile, ROUTE_LANES),
                   _full((MOE_EXPERTS, ROUTE_LANES)), _rows(out_rows, 128), _rows(out_rows, 128)],
        out_shape=[jax.ShapeDtypeStruct((n, d), F32), jax.ShapeDtypeStruct((ROW_SPLIT * n, 128), jnp.int32),
                   jax.ShapeDtypeStruct((n, ROUTE_LANES), F32), jax.ShapeDtypeStruct((MOE_EXPERTS, ROUTE_LANES), F32)]
        + [jax.ShapeDtypeStruct((n // per_row, 128), jnp.int32)] * 2,
        scratch_shapes=[pltpu.VMEM((GROUP_WIDTH // 128, tile, 128), F32),
                        pltpu.VMEM((MOE_EXPERTS, ROUTE_LANES), F32)],
        compiler_params=_cparams("arbitrary"),
        name="out_proj_router",
    )(x, y_s5, y_ret, y_m2, y_hg, w_glu, b_glu, w_out, g2, wr_packed, b_route, strict_lower, spread)


def _sc_mesh():
    return plsc.VectorSubcoreMesh(core_axis_name="core", subcore_axis_name="subcore")


def _sc_scatter2(src, idx_a, idx_b, n_out):
    n = src.shape[0]

    @functools.partial(pl.kernel, out_type=jax.ShapeDtypeStruct((n_out, 128), src.dtype), mesh=_sc_mesh(),
                       scratch_types=[])
    def scatter_kernel(x_hbm, ia_hbm, ib_hbm, o_hbm):
        def body(x_vmem, ia_vmem, ib_vmem):
            pltpu.sync_copy(x_vmem, o_hbm.at[ia_vmem.at[0]])
            pltpu.sync_copy(x_vmem, o_hbm.at[ib_vmem.at[0]])

        pltpu.emit_pipeline(
            body, grid=(n // SC_WINDOW,),
            in_specs=[pl.BlockSpec((SC_WINDOW, 128), index_map=lambda i: (i, 0)),
                      pl.BlockSpec((1, SC_WINDOW), index_map=lambda i: (i, 0)),
                      pl.BlockSpec((1, SC_WINDOW), index_map=lambda i: (i, 0))],
            out_specs=[],
            core_axis_name=("core", "subcore"), dimension_semantics=(pltpu.PARALLEL,),
        )(x_hbm, ia_hbm, ib_hbm)

    return scatter_kernel(src, idx_a, idx_b)


def _sc_gather2(table, idx_a, idx_b):
    n = idx_a.size
    sds = jax.ShapeDtypeStruct((n, 128), table.dtype)

    @functools.partial(pl.kernel, out_type=(sds, sds), mesh=_sc_mesh(), scratch_types=[])
    def gather_kernel(t_hbm, ia_hbm, ib_hbm, oa_hbm, ob_hbm):
        def body(ia_vmem, ib_vmem, oa_vmem, ob_vmem):
            pltpu.sync_copy(t_hbm.at[ia_vmem.at[0]], oa_vmem)
            pltpu.sync_copy(t_hbm.at[ib_vmem.at[0]], ob_vmem)

        pltpu.emit_pipeline(
            body, grid=(n // SC_WINDOW,),
            in_specs=[pl.BlockSpec((1, SC_WINDOW), index_map=lambda i: (i, 0)),
                      pl.BlockSpec((1, SC_WINDOW), index_map=lambda i: (i, 0))],
            out_specs=[pl.BlockSpec((SC_WINDOW, 128), index_map=lambda i: (i, 0)),
                       pl.BlockSpec((SC_WINDOW, 128), index_map=lambda i: (i, 0))],
            core_axis_name=("core", "subcore"), dimension_semantics=(pltpu.PARALLEL,),
        )(ia_hbm, ib_hbm, oa_hbm, ob_hbm)

    return gather_kernel(table, idx_a, idx_b)


def _dispatch_plan(counts, n, n_tiles):
    cnt = counts[:, 0].astype(jnp.int32)
    blocks = (cnt + EXPERT_TILE - 1) // EXPERT_TILE
    ends = jnp.cumsum(blocks)
    first_tile = ends - blocks
    tile_id = jnp.arange(n_tiles, dtype=jnp.int32)
    tile_expert = jnp.minimum(jnp.sum((tile_id[:, None] >= ends[None, :]).astype(jnp.int32), axis=1), MOE_EXPERTS - 1)
    onehot = (tile_expert[:, None] == jnp.arange(MOE_EXPERTS, dtype=jnp.int32)[None, :]).astype(jnp.int32)
    block_in_expert = tile_id - jnp.sum(onehot * first_tile[None, :], axis=1)
    used = tile_id < ends[-1]
    rows_left = jnp.sum(onehot * cnt[None, :], axis=1) - block_in_expert * EXPERT_TILE
    tile_rows = jnp.where(used, jnp.clip(rows_left, 0, EXPERT_TILE), 0).astype(jnp.int32)
    blocks_per_expert = n // EXPERT_TILE
    tile_block = jnp.where(used, tile_expert * blocks_per_expert + block_in_expert,
                           MOE_EXPERTS * blocks_per_expert).astype(jnp.int32)
    tile_first = (used & (block_in_expert == 0)).astype(jnp.int32)
    return tile_expert, tile_block, tile_rows, tile_first


def _experts_kernel(te_ref, blk_ref, rows_ref, first_ref, xs_ref, wg_ref, wu_ref, wd_ref, y_ref, wgb, wub, wdb):
    i = pl.program_id(0)

    @pl.when(first_ref[i] == 1)
    def _():
        wgb[...] = wg_ref[0, 0].astype(BF16)
        wub[...] = wu_ref[0, 0].astype(BF16)
        wdb[...] = wd_ref[0, 0].astype(BF16)

    def run(n_rows):
        x = _merge_rows(xs_ref, n_rows)
        row = lax.broadcasted_iota(jnp.int32, x.shape, 0)
        x = jnp.where(row < rows_ref[i], x, 0.0).astype(BF16)
        act = _silu(_dot(x, wgb[...])) * _dot(x, wub[...])
        _split_rows(y_ref, _dot(act.astype(BF16), wdb[...]), n_rows)

    half = EXPERT_TILE // 2

    @pl.when(rows_ref[i] > half)
    def _():
        run(EXPERT_TILE)

    @pl.when((rows_ref[i] > 0) & (rows_ref[i] <= half))
    def _():
        run(half)


def _experts(xs, tile_expert, tile_block, tile_rows, tile_first, w_gate, w_up, w_down, layer):
    n_tiles = tile_expert.shape[0]
    _, _, d, ff = w_gate.shape
    rows_blk = pl.BlockSpec((ROW_SPLIT * EXPERT_TILE, 128), lambda i, te, blk, rows, first: (blk[i], 0))
    return pl.pallas_call(
        _experts_kernel,
        grid_spec=pltpu.PrefetchScalarGridSpec(
            num_scalar_prefetch=4,
            grid=(n_tiles,),
            in_specs=[rows_blk,
                      pl.BlockSpec((1, 1, d, ff), lambda i, te, blk, rows, first: (layer, te[i], 0, 0)),
                      pl.BlockSpec((1, 1, d, ff), lambda i, te, blk, rows, first: (layer, te[i], 0, 0)),
                      pl.BlockSpec((1, 1, ff, d), lambda i, te, blk, rows, first: (layer, te[i], 0, 0))],
            out_specs=rows_blk,
            scratch_shapes=[pltpu.VMEM((d, ff), BF16), pltpu.VMEM((d, ff), BF16), pltpu.VMEM((ff, d), BF16)],
        ),
        out_shape=jax.ShapeDtypeStruct(xs.shape, xs.dtype),
        compiler_params=_cparams("arbitrary"),
        name="moe_experts",
    )(tile_expert, tile_block, tile_rows, tile_first, xs, w_gate, w_up, w_down)


def _combine_kernel(x1_ref, g1_ref, g2_ref, route_ref, gf_ref, o_ref, *, tile):
    o_ref[...] = _rms(_moe_combine(x1_ref[...], g1_ref, g2_ref, route_ref[...], tile), gf_ref[...])


def _combine(x1, g1, g2, route, g_final):
    n, d = x1.shape
    tile = min(SEQ_TILE, n)
    return pl.pallas_call(
        functools.partial(_combine_kernel, tile=tile),
        grid=(n // tile,),
        in_specs=[_rows(tile, d), _rows(ROW_SPLIT * tile, 128), _rows(ROW_SPLIT * tile, 128),
                  _rows(tile, ROUTE_LANES), _full((1, d))],
        out_specs=_rows(tile, d),
        out_shape=jax.ShapeDtypeStruct((n, d), F32),
        compiler_params=_cparams("parallel"),
        name="moe_combine",
    )(x1, g1, g2, route, g_final)


def _moe(h2_rows, counts, idx1, idx2, w_gate, w_up, w_down, layer):
    n = h2_rows.shape[0] // ROW_SPLIT
    n_tiles = (MOE_TOPK * n) // EXPERT_TILE + MOE_EXPERTS
    tile_expert, tile_block, tile_rows, tile_first = _dispatch_plan(counts, n, n_tiles)
    xs = _sc_scatter2(h2_rows, idx1, idx2, ROW_SPLIT * (MOE_EXPERTS * n + EXPERT_TILE))
    ys = _experts(xs, tile_expert, tile_block, tile_rows, tile_first, w_gate, w_up, w_down, layer)
    return _sc_gather2(ys, idx1, idx2)


def kernel(x, positions, norm1_g, w_in, w_out, s5_a_re, s5_a_im, s5_log_dt, s5_b_re, s5_b_im, s5_c_re, s5_c_im, s5_d, s5_w_glu, s5_b_glu, m2_conv_w, m2_conv_b, m2_dt_bias, m2_a_log, m2_d, m2_norm_g, hg_lb_logits, hg_norm_g, norm2_g, moe_w_group, moe_b_group, moe_w_expert, moe_b_expert, moe_w_gate, moe_w_up, moe_w_down, final_norm_g):
    bsz, seqlen, d = x.shape
    assert bsz == 1 and seqlen % SEQ_TILE == 0 and seqlen % EXPERT_TILE == 0
    depth = w_in.shape[0]

    lb_probs = jax.nn.softmax(hg_lb_logits.astype(F32), axis=0)
    lower_bounds = (jnp.cumsum(lb_probs, axis=0) - lb_probs[0]).reshape(depth, 1, GROUP_WIDTH)
    cos_t, sin_t = _rope_tables(positions.reshape(seqlen, 1))

    s5_p = _s5_params(s5_a_re, s5_a_im, s5_log_dt, s5_b_re, s5_b_im, s5_c_re, s5_c_im, s5_d)
    m2_p = _mamba2_params(m2_conv_w, m2_conv_b, m2_dt_bias, m2_a_log, m2_d, m2_norm_g)
    w_route = jnp.zeros((depth, d, ROUTE_LANES), F32)
    w_route = w_route.at[:, :, :MOE_GROUPS].set(moe_w_group)
    w_route = w_route.at[:, :, EXPERT_ROW:EXPERT_ROW + MOE_EXPERTS].set(moe_w_expert)
    wr_packed = jnp.concatenate(_split2(w_route), axis=2)
    b_route = jnp.zeros((depth, 1, ROUTE_LANES), F32)
    b_route = b_route.at[:, 0, :MOE_GROUPS].set(moe_b_group)
    b_route = b_route.at[:, 0, EXPERT_ROW:EXPERT_ROW + MOE_EXPERTS].set(moe_b_expert)
    w_glu_b, w_out_b = s5_w_glu.astype(BF16), w_out.astype(BF16)
    b_glu = s5_b_glu.reshape(depth, 1, GROUP_WIDTH)
    g1, g2 = norm1_g.reshape(depth, 1, d), norm2_g.reshape(depth, 1, d)
    hg_g = hg_norm_g.reshape(depth, 1, GROUP_WIDTH)
    g_final = final_norm_g.reshape(1, d)
    tile = min(SEQ_TILE, seqlen)
    chunk = min(CHUNK, tile)

    xc, pending = x.reshape(seqlen, d), None
    for l in range(depth):
        outs = _in_proj(xc, pending, g1, _w_prep(w_in, l), l)
        if pending is not None:
            xc, outs = outs[0], outs[1:]
        u_b, u_f, p_ret, p_m2, p_hg = outs
        y_s5 = _s5(u_b, u_f, s5_p, l)
        y_ret, y_m2, y_hg = _mixers(
            seqlen,
            _retention_call(p_ret, cos_t, sin_t, tile, min(RET_CHUNK, tile)),
            _mamba2_call(p_m2, m2_p, l, tile, chunk),
            _hgrn2_call(p_hg, lower_bounds, hg_g, l, tile, chunk))
        x1, h2_rows, route, counts, idx1, idx2 = _out_proj(xc, y_s5, y_ret, y_m2, y_hg, w_glu_b, b_glu, w_out_b,
                                                           g2, wr_packed, b_route, l)
        rows1, rows2 = _moe(h2_rows, counts, idx1, idx2, moe_w_gate, moe_w_up, moe_w_down, l)
        xc, pending = x1, (rows1, rows2, route)
    xc = _combine(xc, *pending, g_final)
    return xc.reshape(bsz, seqlen, d)
```
